```python
import jax, jax.numpy as jnp
from jax import lax
import numpy as np

D_MODEL = 1024
BATCH = 16
SEQ = 2048
DEPTH = 4

CHUNK = 64
D_CONV = 512
CONV_HEADS = 8
D_POOL = 512
POOL_WINDOWS = (2, 4, 8, 16)
N_POOL = len(POOL_WINDOWS)
POOL_GROUP = D_POOL // N_POOL
D_MIX = D_CONV + D_POOL
D_IN = 2 * D_CONV + D_POOL
CONV_K = 31
D_FF = 2816
FFN_CONV_K = 3
N_MOD = 6
EPS = 1e-6

kernel_name = "hybrid_conformer_pool_streaming_trunk"


def rms_norm(x, g):
    xf = x.astype(jnp.float32)
    y = xf * lax.rsqrt(jnp.mean(xf * xf, axis=-1, keepdims=True) + EPS)
    return (y * g.astype(jnp.float32)).astype(x.dtype)


def layer_norm(x, g, b):
    xf = x.astype(jnp.float32)
    mu = jnp.mean(xf, axis=-1, keepdims=True)
    xc = xf - mu
    var = jnp.mean(xc * xc, axis=-1, keepdims=True)
    y = xc * lax.rsqrt(var + EPS)
    return (y * g.astype(jnp.float32) + b.astype(jnp.float32)).astype(x.dtype)


def causal_dwconv(x, w, b):
    k = w.shape[0]
    ch = x.shape[-1]
    y = lax.conv_general_dilated(
        x, w[:, None, :].astype(x.dtype), window_strides=(1,), padding=[(k - 1, 0)],
        dimension_numbers=("NWC", "WIO", "NWC"), feature_group_count=ch)
    return y + b.astype(x.dtype)


def conformer_conv_mixer(u_val, u_gate, conv_w, conv_b, ln_g, ln_b):
    a = u_val * jax.nn.sigmoid(u_gate)
    a = causal_dwconv(a, conv_w, conv_b)
    a = layer_norm(a, ln_g, ln_b)
    return jax.nn.silu(a)


def multiscale_pool_mixer(h, pool_w, pool_scale):
    s = h.shape[1]
    hf = h.astype(jnp.float32)
    cs = lax.cumsum(hf, axis=1)
    t = jnp.arange(s)
    outs = []
    for g, w in enumerate(POOL_WINDOWS):
        sl = slice(g * POOL_GROUP, (g + 1) * POOL_GROUP)
        csg = cs[..., sl]
        prev = jnp.pad(csg, ((0, 0), (w, 0), (0, 0)))[:, :s]
        cnt = jnp.minimum(t + 1, w).astype(jnp.float32)[None, :, None]
        d = ((csg - prev) / cnt - hf[..., sl]).astype(h.dtype)
        outs.append(jnp.einsum("bsc,cd->bsd", d, pool_w[g]))
    return jnp.concatenate(outs, axis=-1) * pool_scale


def _fwd_setup_inputs(seed: int = 0) -> dict:
    key = jax.random.key(seed)
    ks = jax.random.split(key, 24)
    f32 = jnp.float32

    def nrm(k, shape, scale):
        return jax.random.normal(k, shape, f32) * scale

    L = DEPTH
    return {
        "x": nrm(ks[0], (BATCH, SEQ, D_MODEL), 1.0),
        "c": nrm(ks[1], (BATCH, D_MODEL), 1.0),
        "ada_w": nrm(ks[2], (L, D_MODEL, N_MOD * D_MODEL), 0.1 * D_MODEL ** -0.5),
        "ada_b": nrm(ks[3], (L, N_MOD * D_MODEL), 0.01),
        "pre_mix_g": 1.0 + nrm(ks[4], (L, D_MODEL), 0.05),
        "post_mix_g": 1.0 + nrm(ks[5], (L, D_MODEL), 0.05),
        "w_in": nrm(ks[6], (L, D_MODEL, D_IN), D_MODEL ** -0.5),
        "conv_w": nrm(ks[7], (L, CONV_K, D_CONV), CONV_K ** -0.5),
        "conv_b": nrm(ks[8], (L, D_CONV), 0.01),
        "conv_ln_g": 1.0 + nrm(ks[9], (L, D_CONV), 0.05),
        "conv_ln_b": nrm(ks[10], (L, D_CONV), 0.01),
        "pool_w": nrm(ks[11], (L, N_POOL, POOL_GROUP, POOL_GROUP), POOL_GROUP ** -0.5),
        "pool_scale": 1.0 + nrm(ks[12], (L, D_POOL), 0.1),
        "w_out": nrm(ks[13], (L, D_MIX, D_MODEL), D_MIX ** -0.5),
        "pre_ffn_g": 1.0 + nrm(ks[14], (L, D_MODEL), 0.05),
        "post_ffn_g": 1.0 + nrm(ks[15], (L, D_MODEL), 0.05),
        "ffn_up": nrm(ks[16], (L, D_MODEL, 2 * D_FF), D_MODEL ** -0.5),
        "ffn_conv_w": nrm(ks[17], (L, FFN_CONV_K, 2 * D_FF), FFN_CONV_K ** -0.5),
        "ffn_conv_b": nrm(ks[18], (L, 2 * D_FF), 0.01),
        "ffn_down": nrm(ks[19], (L, D_FF, D_MODEL), D_FF ** -0.5),
    }


def _fwd_reference(x, c, ada_w, ada_b, pre_mix_g, post_mix_g, w_in, conv_w, conv_b, conv_ln_g,
              conv_ln_b, pool_w, pool_scale, w_out, pre_ffn_g, post_ffn_g, ffn_up,
              ffn_conv_w, ffn_conv_b, ffn_down):
    c_act = jax.nn.silu(c)
    for l in range(DEPTH):
        mod = c_act @ ada_w[l] + ada_b[l]
        sh1, sc1, gt1, sh2, sc2, gt2 = [m[:, None, :] for m in jnp.split(mod, N_MOD, axis=-1)]

        h = rms_norm(x, pre_mix_g[l]) * (1.0 + sc1) + sh1
        u = jnp.einsum("bsd,de->bse", h, w_in[l])
        a = conformer_conv_mixer(u[..., :D_CONV], u[..., D_CONV:2 * D_CONV],
                                 conv_w[l], conv_b[l], conv_ln_g[l], conv_ln_b[l])
        p = multiscale_pool_mixer(u[..., 2 * D_CONV:], pool_w[l], pool_scale[l])
        o = jnp.einsum("bsm,md->bsd", jnp.concatenate([a, p], axis=-1), w_out[l])
        x = x + (1.0 + gt1) * rms_norm(o, post_mix_g[l])

        h = rms_norm(x, pre_ffn_g[l]) * (1.0 + sc2) + sh2
        u = jnp.einsum("bsd,df->bsf", h, ffn_up[l])
        u = causal_dwconv(u, ffn_conv_w[l], ffn_conv_b[l])
        hid = jax.nn.silu(u[..., D_FF:]) * u[..., :D_FF]
        o = jnp.einsum("bsf,fd->bsd", hid, ffn_down[l])
        x = x + (1.0 + gt2) * rms_norm(o, post_ffn_g[l])
    return x


import jax as _jax
import jax.numpy as _jnp

TWIN_FORMAT = 'train_step'
FWD_PARAMS = ['x', 'c', 'ada_w', 'ada_b', 'pre_mix_g', 'post_mix_g', 'w_in', 'conv_w', 'conv_b', 'conv_ln_g', 'conv_ln_b', 'pool_w', 'pool_scale', 'w_out', 'pre_ffn_g', 'post_ffn_g', 'ffn_up', 'ffn_conv_w', 'ffn_conv_b', 'ffn_down']
TWIN_WEIGHTS = ['ada_w', 'ada_b', 'pre_mix_g', 'post_mix_g', 'w_in', 'conv_w', 'conv_b', 'conv_ln_g', 'conv_ln_b', 'pool_w', 'pool_scale', 'w_out', 'pre_ffn_g', 'post_ffn_g', 'ffn_up', 'ffn_conv_w', 'ffn_conv_b', 'ffn_down']
TWIN_DIFF_INPUT = 'x'
TWIN_INPUTS = ['x', 'c', 'ada_w', 'ada_b', 'pre_mix_g', 'post_mix_g', 'w_in', 'conv_w', 'conv_b', 'conv_ln_g', 'conv_ln_b', 'pool_w', 'pool_scale', 'w_out', 'pre_ffn_g', 'post_ffn_g', 'ffn_up', 'ffn_conv_w', 'ffn_conv_b', 'ffn_down', 'loss_target', 'm_ada_w', 'm_ada_b', 'm_pre_mix_g', 'm_post_mix_g', 'm_w_in', 'm_conv_w', 'm_conv_b', 'm_conv_ln_g', 'm_conv_ln_b', 'm_pool_w', 'm_pool_scale', 'm_w_out', 'm_pre_ffn_g', 'm_post_ffn_g', 'm_ffn_up', 'm_ffn_conv_w', 'm_ffn_conv_b', 'm_ffn_down', 'v_ada_w', 'v_ada_b', 'v_pre_mix_g', 'v_post_mix_g', 'v_w_in', 'v_conv_w', 'v_conv_b', 'v_conv_ln_g', 'v_conv_ln_b', 'v_pool_w', 'v_pool_scale', 'v_w_out', 'v_pre_ffn_g', 'v_post_ffn_g', 'v_ffn_up', 'v_ffn_conv_w', 'v_ffn_conv_b', 'v_ffn_down']
TWIN_OUTPUTS = ['loss', 'grad_x', 'grad_ada_w', 'grad_ada_b', 'grad_pre_mix_g', 'grad_post_mix_g', 'grad_w_in', 'grad_conv_w', 'grad_conv_b', 'grad_conv_ln_g', 'grad_conv_ln_b', 'grad_pool_w', 'grad_pool_scale', 'grad_w_out', 'grad_pre_ffn_g', 'grad_post_ffn_g', 'grad_ffn_up', 'grad_ffn_conv_w', 'grad_ffn_conv_b', 'grad_ffn_down', 'delta_ada_w', 'delta_ada_b', 'delta_pre_mix_g', 'delta_post_mix_g', 'delta_w_in', 'delta_conv_w', 'delta_conv_b', 'delta_conv_ln_g', 'delta_conv_ln_b', 'delta_pool_w', 'delta_pool_scale', 'delta_w_out', 'delta_pre_ffn_g', 'delta_post_ffn_g', 'delta_ffn_up', 'delta_ffn_conv_w', 'delta_ffn_conv_b', 'delta_ffn_down', 'new_m_ada_w', 'new_m_ada_b', 'new_m_pre_mix_g', 'new_m_post_mix_g', 'new_m_w_in', 'new_m_conv_w', 'new_m_conv_b', 'new_m_conv_ln_g', 'new_m_conv_ln_b', 'new_m_pool_w', 'new_m_pool_scale', 'new_m_w_out', 'new_m_pre_ffn_g', 'new_m_post_ffn_g', 'new_m_ffn_up', 'new_m_ffn_conv_w', 'new_m_ffn_conv_b', 'new_m_ffn_down', 'new_v_ada_w', 'new_v_ada_b', 'new_v_pre_mix_g', 'new_v_post_mix_g', 'new_v_w_in', 'new_v_conv_w', 'new_v_conv_b', 'new_v_conv_ln_g', 'new_v_conv_ln_b', 'new_v_pool_w', 'new_v_pool_scale', 'new_v_w_out', 'new_v_pre_ffn_g', 'new_v_post_ffn_g', 'new_v_ffn_up', 'new_v_ffn_conv_w', 'new_v_ffn_conv_b', 'new_v_ffn_down']
TWIN_LEAF_KINDS = {'loss': 'loss', 'grad_x': 'grad_x', 'grad_ada_w': 'grad_w', 'grad_ada_b': 'grad_w', 'grad_pre_mix_g': 'grad_w', 'grad_post_mix_g': 'grad_w', 'grad_w_in': 'grad_w', 'grad_conv_w': 'grad_w', 'grad_conv_b': 'grad_w', 'grad_conv_ln_g': 'grad_w', 'grad_conv_ln_b': 'grad_w', 'grad_pool_w': 'grad_w', 'grad_pool_scale': 'grad_w', 'grad_w_out': 'grad_w', 'grad_pre_ffn_g': 'grad_w', 'grad_post_ffn_g': 'grad_w', 'grad_ffn_up': 'grad_w', 'grad_ffn_conv_w': 'grad_w', 'grad_ffn_conv_b': 'grad_w', 'grad_ffn_down': 'grad_w', 'delta_ada_w': 'delta_w', 'delta_ada_b': 'delta_w', 'delta_pre_mix_g': 'delta_w', 'delta_post_mix_g': 'delta_w', 'delta_w_in': 'delta_w', 'delta_conv_w': 'delta_w', 'delta_conv_b': 'delta_w', 'delta_conv_ln_g': 'delta_w', 'delta_conv_ln_b': 'delta_w', 'delta_pool_w': 'delta_w', 'delta_pool_scale': 'delta_w', 'delta_w_out': 'delta_w', 'delta_pre_ffn_g': 'delta_w', 'delta_post_ffn_g': 'delta_w', 'delta_ffn_up': 'delta_w', 'delta_ffn_conv_w': 'delta_w', 'delta_ffn_conv_b': 'delta_w', 'delta_ffn_down': 'delta_w', 'new_m_ada_w': 'new_m', 'new_m_ada_b': 'new_m', 'new_m_pre_mix_g': 'new_m', 'new_m_post_mix_g': 'new_m', 'new_m_w_in': 'new_m', 'new_m_conv_w': 'new_m', 'new_m_conv_b': 'new_m', 'new_m_conv_ln_g': 'new_m', 'new_m_conv_ln_b': 'new_m', 'new_m_pool_w': 'new_m', 'new_m_pool_scale': 'new_m', 'new_m_w_out': 'new_m', 'new_m_pre_ffn_g': 'new_m', 'new_m_post_ffn_g': 'new_m', 'new_m_ffn_up': 'new_m', 'new_m_ffn_conv_w': 'new_m', 'new_m_ffn_conv_b': 'new_m', 'new_m_ffn_down': 'new_m', 'new_v_ada_w': 'new_v', 'new_v_ada_b': 'new_v', 'new_v_pre_mix_g': 'new_v', 'new_v_post_mix_g': 'new_v', 'new_v_w_in': 'new_v', 'new_v_conv_w': 'new_v', 'new_v_conv_b': 'new_v', 'new_v_conv_ln_g': 'new_v', 'new_v_conv_ln_b': 'new_v', 'new_v_pool_w': 'new_v', 'new_v_pool_scale': 'new_v', 'new_v_w_out': 'new_v', 'new_v_pre_ffn_g': 'new_v', 'new_v_post_ffn_g': 'new_v', 'new_v_ffn_up': 'new_v', 'new_v_ffn_conv_w': 'new_v', 'new_v_ffn_conv_b': 'new_v', 'new_v_ffn_down': 'new_v'}


def _forward(args):
    return _fwd_reference(*[args[k] for k in FWD_PARAMS])


def _output_shape():
    out = _jax.eval_shape(lambda: _forward(_fwd_setup_inputs(0)))
    return out.shape, out.dtype

N_MICROBATCH = 1
ADAM_LR = 0.001
ADAM_B1 = 0.9
ADAM_B2 = 0.999
ADAM_EPS = 1e-08
ADAM_WD = 0.01
ADAM_STEP = 10
PER_EXAMPLE_BATCH_AXIS = {'x': 0, 'c': 0, 'loss_target': 0}
SHARED_INPUTS = []
_WEIGHT_DTYPES = {'ada_w': _jnp.float32, 'ada_b': _jnp.float32, 'pre_mix_g': _jnp.float32, 'post_mix_g': _jnp.float32, 'w_in': _jnp.float32, 'conv_w': _jnp.float32, 'conv_b': _jnp.float32, 'conv_ln_g': _jnp.float32, 'conv_ln_b': _jnp.float32, 'pool_w': _jnp.float32, 'pool_scale': _jnp.float32, 'w_out': _jnp.float32, 'pre_ffn_g': _jnp.float32, 'post_ffn_g': _jnp.float32, 'ffn_up': _jnp.float32, 'ffn_conv_w': _jnp.float32, 'ffn_conv_b': _jnp.float32, 'ffn_down': _jnp.float32}
MOMENT_SCALE = {'ada_w': 6.259283e+00, 'ada_b': 1.836057e+01, 'pre_mix_g': 1.796035e+00, 'post_mix_g': 3.217155e+01, 'w_in': 1.438061e+00, 'conv_w': 1.178491e+00, 'conv_b': 1.148543e+01, 'conv_ln_g': 4.539512e+00, 'conv_ln_b': 7.605756e+00, 'pool_w': 2.388269e+00, 'pool_scale': 2.585781e+00, 'w_out': 2.603076e+00, 'pre_ffn_g': 1.332170e+00, 'post_ffn_g': 3.187325e+01, 'ffn_up': 5.969755e-01, 'ffn_conv_w': 6.511376e-01, 'ffn_conv_b': 2.088443e+00, 'ffn_down': 1.162089e+00}


def _to_microbatches(a, axis):
    t = _jnp.moveaxis(a, axis, 0)
    t = t.reshape((N_MICROBATCH, t.shape[0] // N_MICROBATCH) + t.shape[1:])
    return _jnp.moveaxis(t, 1, axis + 1)


def setup_inputs(seed: int = 0) -> dict:
    inp = _fwd_setup_inputs(seed)
    key = _jax.random.fold_in(_jax.random.key(seed), 7919)
    shape, _ = _output_shape()
    out = dict(inp)
    out["loss_target"] = _jax.random.normal(_jax.random.fold_in(key, 0), shape, _jnp.float32)
    for i, name in enumerate(TWIN_WEIGHTS):
        w = inp[name].astype(_jnp.float32)
        if MOMENT_SCALE is None:
            s = _jnp.sqrt(_jnp.mean(_jnp.square(w)) + 1e-30)
        else:
            s = MOMENT_SCALE[name]
        km, kv = _jax.random.split(_jax.random.fold_in(key, i + 1))
        out[name] = w
        out["m_" + name] = s * _jax.random.normal(km, w.shape, _jnp.float32)
        out["v_" + name] = (s * s) * _jax.random.uniform(kv, w.shape, _jnp.float32, 0.5, 1.5)
    if N_MICROBATCH > 1:
        for name, axis in PER_EXAMPLE_BATCH_AXIS.items():
            out[name] = _to_microbatches(out[name], axis)
    return {'x': out['x'], 'c': out['c'], 'ada_w': out['ada_w'], 'ada_b': out['ada_b'], 'pre_mix_g': out['pre_mix_g'], 'post_mix_g': out['post_mix_g'], 'w_in': out['w_in'], 'conv_w': out['conv_w'], 'conv_b': out['conv_b'], 'conv_ln_g': out['conv_ln_g'], 'conv_ln_b': out['conv_ln_b'], 'pool_w': out['pool_w'], 'pool_scale': out['pool_scale'], 'w_out': out['w_out'], 'pre_ffn_g': out['pre_ffn_g'], 'post_ffn_g': out['post_ffn_g'], 'ffn_up': out['ffn_up'], 'ffn_conv_w': out['ffn_conv_w'], 'ffn_conv_b': out['ffn_conv_b'], 'ffn_down': out['ffn_down'], 'loss_target': out['loss_target'], 'm_ada_w': out['m_ada_w'], 'm_ada_b': out['m_ada_b'], 'm_pre_mix_g': out['m_pre_mix_g'], 'm_post_mix_g': out['m_post_mix_g'], 'm_w_in': out['m_w_in'], 'm_conv_w': out['m_conv_w'], 'm_conv_b': out['m_conv_b'], 'm_conv_ln_g': out['m_conv_ln_g'], 'm_conv_ln_b': out['m_conv_ln_b'], 'm_pool_w': out['m_pool_w'], 'm_pool_scale': out['m_pool_scale'], 'm_w_out': out['m_w_out'], 'm_pre_ffn_g': out['m_pre_ffn_g'], 'm_post_ffn_g': out['m_post_ffn_g'], 'm_ffn_up': out['m_ffn_up'], 'm_ffn_conv_w': out['m_ffn_conv_w'], 'm_ffn_conv_b': out['m_ffn_conv_b'], 'm_ffn_down': out['m_ffn_down'], 'v_ada_w': out['v_ada_w'], 'v_ada_b': out['v_ada_b'], 'v_pre_mix_g': out['v_pre_mix_g'], 'v_post_mix_g': out['v_post_mix_g'], 'v_w_in': out['v_w_in'], 'v_conv_w': out['v_conv_w'], 'v_conv_b': out['v_conv_b'], 'v_conv_ln_g': out['v_conv_ln_g'], 'v_conv_ln_b': out['v_conv_ln_b'], 'v_pool_w': out['v_pool_w'], 'v_pool_scale': out['v_pool_scale'], 'v_w_out': out['v_w_out'], 'v_pre_ffn_g': out['v_pre_ffn_g'], 'v_post_ffn_g': out['v_post_ffn_g'], 'v_ffn_up': out['v_ffn_up'], 'v_ffn_conv_w': out['v_ffn_conv_w'], 'v_ffn_conv_b': out['v_ffn_conv_b'], 'v_ffn_down': out['v_ffn_down']}


def _loss(weights, diff, rest, loss_target):
    with _jax.named_scope("forward"):
        args = {**rest, TWIN_DIFF_INPUT: diff, **{k: w.astype(_WEIGHT_DTYPES[k]) for k, w in weights.items()}}
        y = _forward(args)
    with _jax.named_scope("loss_head"):
        err = _jnp.square(y.astype(_jnp.float32) - loss_target)
        return 0.5 * _jnp.sum(_jnp.mean(err, axis=-1)) if err.ndim else 0.5 * err


def _adamw(w, g, m, v):
    m = ADAM_B1 * m + (1.0 - ADAM_B1) * g
    v = ADAM_B2 * v + (1.0 - ADAM_B2) * _jnp.square(g)
    m_hat = m / (1.0 - ADAM_B1 ** ADAM_STEP)
    v_hat = v / (1.0 - ADAM_B2 ** ADAM_STEP)
    delta = -ADAM_LR * (m_hat / (_jnp.sqrt(v_hat) + ADAM_EPS) + ADAM_WD * w)
    return delta, m, v


def reference(x, c, ada_w, ada_b, pre_mix_g, post_mix_g, w_in, conv_w, conv_b, conv_ln_g, conv_ln_b, pool_w, pool_scale, w_out, pre_ffn_g, post_ffn_g, ffn_up, ffn_conv_w, ffn_conv_b, ffn_down, loss_target, m_ada_w, m_ada_b, m_pre_mix_g, m_post_mix_g, m_w_in, m_conv_w, m_conv_b, m_conv_ln_g, m_conv_ln_b, m_pool_w, m_pool_scale, m_w_out, m_pre_ffn_g, m_post_ffn_g, m_ffn_up, m_ffn_conv_w, m_ffn_conv_b, m_ffn_down, v_ada_w, v_ada_b, v_pre_mix_g, v_post_mix_g, v_w_in, v_conv_w, v_conv_b, v_conv_ln_g, v_conv_ln_b, v_pool_w, v_pool_scale, v_w_out, v_pre_ffn_g, v_post_ffn_g, v_ffn_up, v_ffn_conv_w, v_ffn_conv_b, v_ffn_down):
    given = dict(x=x, c=c, ada_w=ada_w, ada_b=ada_b, pre_mix_g=pre_mix_g, post_mix_g=post_mix_g, w_in=w_in, conv_w=conv_w, conv_b=conv_b, conv_ln_g=conv_ln_g, conv_ln_b=conv_ln_b, pool_w=pool_w, pool_scale=pool_scale, w_out=w_out, pre_ffn_g=pre_ffn_g, post_ffn_g=post_ffn_g, ffn_up=ffn_up, ffn_conv_w=ffn_conv_w, ffn_conv_b=ffn_conv_b, ffn_down=ffn_down, loss_target=loss_target, m_ada_w=m_ada_w, m_ada_b=m_ada_b, m_pre_mix_g=m_pre_mix_g, m_post_mix_g=m_post_mix_g, m_w_in=m_w_in, m_conv_w=m_conv_w, m_conv_b=m_conv_b, m_conv_ln_g=m_conv_ln_g, m_conv_ln_b=m_conv_ln_b, m_pool_w=m_pool_w, m_pool_scale=m_pool_scale, m_w_out=m_w_out, m_pre_ffn_g=m_pre_ffn_g, m_post_ffn_g=m_post_ffn_g, m_ffn_up=m_ffn_up, m_ffn_conv_w=m_ffn_conv_w, m_ffn_conv_b=m_ffn_conv_b, m_ffn_down=m_ffn_down, v_ada_w=v_ada_w, v_ada_b=v_ada_b, v_pre_mix_g=v_pre_mix_g, v_post_mix_g=v_post_mix_g, v_w_in=v_w_in, v_conv_w=v_conv_w, v_conv_b=v_conv_b, v_conv_ln_g=v_conv_ln_g, v_conv_ln_b=v_conv_ln_b, v_pool_w=v_pool_w, v_pool_scale=v_pool_scale, v_w_out=v_w_out, v_pre_ffn_g=v_pre_ffn_g, v_post_ffn_g=v_post_ffn_g, v_ffn_up=v_ffn_up, v_ffn_conv_w=v_ffn_conv_w, v_ffn_conv_b=v_ffn_conv_b, v_ffn_down=v_ffn_down)
    weights = {n: given[n] for n in TWIN_WEIGHTS}
    shared = {n: given[n] for n in SHARED_INPUTS}
    per_example = {n: given[n] for n in ['x', 'c']}
    grad_fn = _jax.value_and_grad(_loss, argnums=(0, 1))

    def one_microbatch(ex, loss_target):
        ex = dict(ex)
        diff = ex.pop(TWIN_DIFF_INPUT)
        return grad_fn(weights, diff, {**shared, **ex}, loss_target)

    if N_MICROBATCH == 1:
        loss, (grad_w, grad_x) = one_microbatch(per_example, given["loss_target"])
    else:
        def body(carry, xs):
            loss_sum, grad_sum = carry
            l_k, (gw_k, gx_k) = one_microbatch(xs[0], xs[1])
            with _jax.named_scope("update"):
                return (loss_sum + l_k, _jax.tree.map(_jnp.add, grad_sum, gw_k)), gx_k

        init = (_jnp.zeros((), _jnp.float32), _jax.tree.map(_jnp.zeros_like, weights))
        (loss, grad_w), grad_x = _jax.lax.scan(body, init, (per_example, given["loss_target"]))
    with _jax.named_scope("update"):
        delta_w, new_m, new_v = {}, {}, {}
        for n in TWIN_WEIGHTS:
            delta_w[n], new_m[n], new_v[n] = _adamw(weights[n], grad_w[n], given["m_" + n], given["v_" + n])
    return (loss, grad_x, *[grad_w[n] for n in TWIN_WEIGHTS], *[delta_w[n] for n in TWIN_WEIGHTS],
            *[new_m[n] for n in TWIN_WEIGHTS], *[new_v[n] for n in TWIN_WEIGHTS])
```

```python
import functools

import jax
import jax.numpy as jnp
from jax import lax
from jax.experimental import pallas as pl
from jax.experimental.pallas import tpu as pltpu

N_DEV = 8
EPS = 1e-6
POOL_WINDOWS = (2, 4, 8, 16)
POOL_GROUP = 128
CONV_HALO = 32
FFN_HALO = 8
FFN_HALO_IN = 16
LANES = 128
ROW_CHUNK = 64
MXU_DTYPE = jnp.bfloat16
VMEM_LIMIT = 60 * 1024 * 1024

ADAM_LR = 0.001
ADAM_B1 = 0.9
ADAM_B2 = 0.999
ADAM_EPS = 1e-08
ADAM_WD = 0.01
ADAM_STEP = 10

MESH = pl.DeviceIdType.MESH
F32 = jnp.float32


def _mm(a, b):
    return jnp.dot(a.astype(MXU_DTYPE), b.astype(MXU_DTYPE), preferred_element_type=F32)


def _mm_nt(a, b):
    return lax.dot_general(a.astype(MXU_DTYPE), b.astype(MXU_DTYPE), (((1,), (1,)), ((), ())),
                           preferred_element_type=F32)


def _mm_tn(a, b):
    return lax.dot_general(a.astype(MXU_DTYPE), b.astype(MXU_DTYPE), (((0,), (0,)), ((), ())),
                           preferred_element_type=F32)


def _mean(v):
    return jnp.mean(v, axis=-1, keepdims=True)


def _colsum(v):
    return jnp.sum(v, axis=0, keepdims=True)


def _sigmoid(v):
    return jax.nn.sigmoid(v)


def _params(n_grid=1):
    return pltpu.CompilerParams(dimension_semantics=("arbitrary",) * n_grid, vmem_limit_bytes=VMEM_LIMIT)


def _const_spec(shape, index):
    return pl.BlockSpec(shape, lambda *_: index)


def _position():
    x, y, c = lax.axis_index("x"), lax.axis_index("y"), lax.axis_index("c")
    return x, y, c


def _all_gather(arrs, name):
    n = len(arrs)

    def body(*refs):
        ins, outs = refs[:n], refs[n:2 * n]
        send_sems, recv_sems, local_sems = refs[2 * n:]
        x, y, c = _position()
        me, sibling = (x, y, c), (x, y, 1 - c)
        chips = [(1 - x, y), (x, 1 - y), (1 - x, 1 - y)]

        def slot(k, px, py, pc):
            return outs[k].at[4 * px + 2 * py + pc]

        def copy(k, s, block, to, src=None):
            return pltpu.make_async_remote_copy(
                src_ref=slot(k, *block) if src is None else src, dst_ref=slot(k, *block),
                send_sem=send_sems.at[k, s], recv_sem=recv_sems.at[k, s], device_id=to, device_id_type=MESH)

        mine = [pltpu.make_async_copy(ins[k], slot(k, *me), local_sems.at[k]) for k in range(n)]
        for cp in mine:
            cp.start()
        first = []
        for k in range(n):
            first.append(copy(k, 0, me, sibling, src=ins[k]))
            first += [copy(k, 1 + j, me, (*chip, c), src=ins[k]) for j, chip in enumerate(chips)]
        for cp in first:
            cp.start()
        passed = []
        for j, chip in enumerate(chips):
            for k in range(n):
                copy(k, 1 + j, (*chip, c), me).wait_recv()
                fwd = copy(k, 4 + j, (*chip, c), sibling)
                fwd.start()
                passed.append(fwd)
        for k in range(n):
            copy(k, 0, sibling, me).wait_recv()
            for j, chip in enumerate(chips):
                copy(k, 4 + j, (*chip, 1 - c), me).wait_recv()
        for cp in first + passed:
            cp.wait_send()
        for cp in mine:
            cp.wait()

    any_spec = pl.BlockSpec(memory_space=pl.ANY)
    return pl.pallas_call(
        body, name=name,
        out_shape=[jax.ShapeDtypeStruct((N_DEV,) + a.shape, a.dtype) for a in arrs],
        in_specs=[any_spec] * n, out_specs=[any_spec] * n,
        scratch_shapes=[pltpu.SemaphoreType.DMA((n, 7)), pltpu.SemaphoreType.DMA((n, 7)),
                        pltpu.SemaphoreType.DMA((n,))],
    )(*arrs)


def _all_to_all(groups, name):
    n = len(groups)
    n_layers = len(groups[0])
    flat = [a for grp in groups for a in grp]

    def body(*refs):
        ins = refs[:n * n_layers]
        outs = refs[n * n_layers:n * n_layers + n]
        send_sems, recv_sems, local_sems = refs[n * n_layers + n:]
        x, y, c = _position()
        me = 4 * x + 2 * y + c
        copies = []
        for k in range(n):
            for l in range(n_layers):
                src = ins[k * n_layers + l]
                local = pltpu.make_async_copy(src.at[me], outs[k].at[me, l], local_sems.at[k, l])
                local.start()
                copies.append(local)
                for j in range(1, N_DEV):
                    px = (1 - x) if (j & 4) else x
                    py = (1 - y) if (j & 2) else y
                    pc = (1 - c) if (j & 1) else c
                    peer = 4 * px + 2 * py + pc
                    cp = pltpu.make_async_remote_copy(
                        src_ref=src.at[peer], dst_ref=outs[k].at[me, l],
                        send_sem=send_sems.at[k, l, j - 1], recv_sem=recv_sems.at[k, l, j - 1],
                        device_id=(px, py, pc), device_id_type=MESH)
                    cp.start()
                    copies.append(pltpu.make_async_remote_copy(
                        src_ref=src.at[peer], dst_ref=outs[k].at[peer, l],
                        send_sem=send_sems.at[k, l, j - 1], recv_sem=recv_sems.at[k, l, j - 1],
                        device_id=(px, py, pc), device_id_type=MESH))
        for cp in copies:
            cp.wait()

    any_spec = pl.BlockSpec(memory_space=pl.ANY)
    return pl.pallas_call(
        body, name=name,
        out_shape=[jax.ShapeDtypeStruct((N_DEV, n_layers) + grp[0].shape[1:], grp[0].dtype) for grp in groups],
        in_specs=[any_spec] * len(flat), out_specs=[any_spec] * n,
        scratch_shapes=[pltpu.SemaphoreType.DMA((n, n_layers, 7)), pltpu.SemaphoreType.DMA((n, n_layers, 7)),
                        pltpu.SemaphoreType.DMA((n, n_layers))],
    )(*flat)


def _adamw(w, g, m, v):
    m = ADAM_B1 * m + (1.0 - ADAM_B1) * g
    v = ADAM_B2 * v + (1.0 - ADAM_B2) * jnp.square(g)
    m_hat = m / (1.0 - ADAM_B1 ** ADAM_STEP)
    v_hat = v / (1.0 - ADAM_B2 ** ADAM_STEP)
    delta = -ADAM_LR * (m_hat / (jnp.sqrt(v_hat) + ADAM_EPS) + ADAM_WD * w)
    return delta, m, v


def _ada_fwd(c_all, ada_w, ada_b_cols):
    n_layers, d, cols = ada_w.shape
    b = c_all.shape[0]

    def body(c_ref, w_ref, b_ref, o_ref):
        cv = c_ref[...]
        act = cv * _sigmoid(cv)
        o_ref[...] = jnp.dot(act, w_ref[...], preferred_element_type=F32,
                             precision=lax.Precision.HIGHEST) + b_ref[...]

    return pl.pallas_call(
        body, name="ada_fwd", grid=(n_layers,),
        out_shape=jax.ShapeDtypeStruct((n_layers, b, cols), F32),
        in_specs=[pl.BlockSpec((b, d), lambda l: (0, 0)),
                  pl.BlockSpec((None, d, cols), lambda l: (l, 0, 0)),
                  pl.BlockSpec((None, 1, cols), lambda l: (l, 0, 0))],
        out_specs=pl.BlockSpec((None, b, cols), lambda l: (l, 0, 0)),
        compiler_params=_params(1),
    )(c_all, ada_w, ada_b_cols.reshape(n_layers, 1, cols))


def _ada_bwd(c_all_t, dmod_cols, w, m, v):
    n_layers, d, cols = w.shape
    b = c_all_t.shape[1]
    td = 256 if d % 256 == 0 else d

    def body(c_ref, dm_ref, w_ref, m_ref, v_ref, g_ref, dl_ref, nm_ref, nv_ref):
        cv = c_ref[...]
        act = cv * _sigmoid(cv)
        g = jnp.dot(act, dm_ref[...], preferred_element_type=F32, precision=lax.Precision.HIGHEST)
        delta, nm, nv = _adamw(w_ref[...], g, m_ref[...], v_ref[...])
        g_ref[...] = g
        dl_ref[...] = delta
        nm_ref[...] = nm
        nv_ref[...] = nv

    blk = pl.BlockSpec((None, td, cols), lambda l, i: (l, i, 0))
    shp = jax.ShapeDtypeStruct(w.shape, F32)
    return pl.pallas_call(
        body, name="ada_bwd", grid=(n_layers, d // td),
        out_shape=[shp] * 4,
        in_specs=[pl.BlockSpec((td, b), lambda l, i: (i, 0)),
                  pl.BlockSpec((None, b, cols), lambda l, i: (l, 0, 0)), blk, blk, blk],
        out_specs=[blk] * 4,
        compiler_params=_params(2),
    )(c_all_t, dmod_cols, w, m, v)


def _row_tile(rows, cols, budget=128 * 1024):
    best = None
    for t in range(8, rows + 1, 8):
        if rows % t == 0 and t * cols <= budget:
            best = t
    return best if best is not None else rows


def _adam_reduce(name, parts, w, m, v):
    p, rows, cols = parts.shape
    tr = _row_tile(rows, cols, budget=(256 * 1024) // max(1, p // 4))

    def body(p_ref, w_ref, m_ref, v_ref, g_ref, dl_ref, nm_ref, nv_ref):
        g = p_ref[0]
        for k in range(1, p):
            g = g + p_ref[k]
        delta, nm, nv = _adamw(w_ref[...], g, m_ref[...], v_ref[...])
        g_ref[...] = g
        dl_ref[...] = delta
        nm_ref[...] = nm
        nv_ref[...] = nv

    blk = pl.BlockSpec((tr, cols), lambda i: (i, 0))
    shp = jax.ShapeDtypeStruct((rows, cols), F32)
    return pl.pallas_call(
        body, name=name, grid=(rows // tr,),
        out_shape=[shp] * 4,
        in_specs=[pl.BlockSpec((p, tr, cols), lambda i: (0, i, 0)), blk, blk, blk],
        out_specs=[blk] * 4,
        compiler_params=_params(1),
    )(parts, w, m, v)


def _sum_parts(parts):
    p = parts.shape[0]

    def body(p_ref, o_ref):
        acc = p_ref[0]
        for k in range(1, p):
            acc = acc + p_ref[k]
        o_ref[...] = acc

    return pl.pallas_call(body, name="loss_sum", out_shape=jax.ShapeDtypeStruct(parts.shape[1:], F32))(parts)


def _loss_grad(y, target, tm):
    t, d = y.shape

    def body(y_ref, t_ref, dy_ref, loss_ref):
        @pl.when(pl.program_id(0) == 0)
        def _():
            loss_ref[...] = jnp.zeros_like(loss_ref)

        diff = y_ref[...] - t_ref[...]
        dy_ref[...] = diff / d
        part = 0.5 * jnp.sum(_mean(diff * diff), axis=0, keepdims=True)
        loss_ref[...] += jnp.broadcast_to(part, loss_ref.shape)

    blk = pl.BlockSpec((tm, d), lambda i: (i, 0))
    return pl.pallas_call(
        body, name="loss_grad", grid=(t // tm,),
        out_shape=[jax.ShapeDtypeStruct((t, d), F32), jax.ShapeDtypeStruct((8, LANES), F32)],
        in_specs=[blk, blk], out_specs=[blk, pl.BlockSpec((8, LANES), lambda i: (0, 0))],
        compiler_params=_params(1),
    )(y, target)


def _matmul_tn(name, a, b, tk):
    ga, t, m = a.shape
    gb, _, n = b.shape
    g = max(ga, gb)

    def body(a_ref, b_ref, o_ref):
        @pl.when(pl.program_id(1) == 0)
        def _():
            o_ref[...] = jnp.zeros_like(o_ref)

        o_ref[...] += _mm_tn(a_ref[...], b_ref[...])

    return pl.pallas_call(
        body, name=name, grid=(g, t // tk),
        out_shape=jax.ShapeDtypeStruct((g, m, n), F32),
        in_specs=[pl.BlockSpec((None, tk, m), (lambda gi, k: (gi, k, 0)) if ga > 1 else (lambda gi, k: (0, k, 0))),
                  pl.BlockSpec((None, tk, n), (lambda gi, k: (gi, k, 0)) if gb > 1 else (lambda gi, k: (0, k, 0)))],
        out_specs=pl.BlockSpec((None, m, n), lambda gi, k: (gi, 0, 0)),
        compiler_params=_params(2),
    )(a, b)


def _conv_taps(src_ref, col0, ncols, tm, tap_rows, weight_of, init_of, store):
    rc = min(ROW_CHUNK, tm)
    for cb in range(ncols // LANES):
        cs = slice(cb * LANES, (cb + 1) * LANES)
        ss = slice(col0 + cb * LANES, col0 + (cb + 1) * LANES)
        for r0 in range(0, tm, rc):
            acc = init_of(cs, rc)
            for k, row in enumerate(tap_rows):
                acc = acc + weight_of(k, cs) * src_ref[r0 + row:r0 + row + rc, ss]
            store(r0, rc, cs, acc)


def _mixer_fwd(l, x, mod, gains, w_in, conv_w, vec, pool_w, w_out, *, seq, tm, act_dtype):
    t, d = x.shape
    d_in = w_in.shape[-1]
    dc = conv_w.shape[-1]
    d_mix = w_out.shape[-2]
    n_taps = 31
    nt, tps = t // tm, seq // tm
    h = CONV_HALO

    def body(x_ref, mod_ref, g_ref, win_ref, cw_ref, v_ref, pw_ref, wout_ref,
             x1_ref, h1_ref, u_ref, a1_ref, ap_ref, o_ref, ext_ref):
        i = pl.program_id(0)
        first = (i % tps) == 0
        xv = x_ref[...]
        r = lax.rsqrt(_mean(xv * xv) + EPS)
        hv = (xv * r) * g_ref[0:1, :] * (1.0 + mod_ref[1:2, :]) + mod_ref[0:1, :]
        hb = hv.astype(act_dtype)
        h1_ref[...] = hb
        u = _mm(hb, win_ref[...])
        u_ref[...] = u
        a0 = u[:, :dc] * _sigmoid(u[:, dc:2 * dc])

        @pl.when(first)
        def _():
            ext_ref[0:h, :] = jnp.zeros((h, ext_ref.shape[1]), F32)

        @pl.when(jnp.logical_not(first))
        def _():
            ext_ref[0:h, :] = ext_ref[tm:tm + h, :]

        ext_ref[h:h + tm, 0:dc] = a0
        ext_ref[h:h + tm, dc:] = u[:, 2 * dc:]

        def store(r0, rc, cs, acc):
            a1_ref[r0:r0 + rc, cs] = acc

        _conv_taps(ext_ref, 0, dc, tm, [h - (n_taps - 1) + k for k in range(n_taps)],
                   lambda k, cs: cw_ref[k:k + 1, cs],
                   lambda cs, rc: jnp.broadcast_to(v_ref[0:1, cs], (rc, LANES)), store)
        a1 = a1_ref[...]
        mu = _mean(a1)
        xc = a1 - mu
        rstd = lax.rsqrt(_mean(xc * xc) + EPS)
        a2 = (xc * rstd) * v_ref[1:2, :] + v_ref[2:3, :]
        ap_ref[:, 0:dc] = (a2 * _sigmoid(a2)).astype(act_dtype)

        pos = (i % tps) * tm + lax.broadcasted_iota(jnp.int32, (tm, 1), 0)
        for g, w in enumerate(POOL_WINDOWS):
            cs = slice(dc + g * POOL_GROUP, dc + (g + 1) * POOL_GROUP)
            s = ext_ref[h:h + tm, cs]
            for j in range(1, w):
                s = s + ext_ref[h - j:h - j + tm, cs]
            cnt = jnp.minimum(pos + 1, w).astype(F32)
            dv = s / cnt - ext_ref[h:h + tm, cs]
            q = _mm(dv, pw_ref[g])
            ap_ref[:, cs] = (q * v_ref[3:4, g * POOL_GROUP:(g + 1) * POOL_GROUP]).astype(act_dtype)

        o = _mm(ap_ref[...], wout_ref[...])
        o_ref[...] = o
        ro = lax.rsqrt(_mean(o * o) + EPS)
        x1_ref[...] = xv + (1.0 + mod_ref[2:3, :]) * ((o * ro) * g_ref[1:2, :])

    row = lambda width: pl.BlockSpec((tm, width), lambda i: (i, 0))
    return pl.pallas_call(
        body, name=f"mixer_fwd_{l}", grid=(nt,),
        out_shape=[jax.ShapeDtypeStruct((t, d), F32), jax.ShapeDtypeStruct((t, d), act_dtype),
                   jax.ShapeDtypeStruct((t, d_in), F32), jax.ShapeDtypeStruct((t, dc), F32),
                   jax.ShapeDtypeStruct((t, d_mix), act_dtype), jax.ShapeDtypeStruct((t, d), F32)],
        in_specs=[row(d),
                  pl.BlockSpec((None, None, 8, d), lambda i: (l, i // tps, 0, 0)),
                  _const_spec((None, 8, d), (l, 0, 0)),
                  _const_spec((None, d, d_in), (l, 0, 0)),
                  _const_spec((None, 32, dc), (l, 0, 0)),
                  _const_spec((None, 8, dc), (l, 0, 0)),
                  _const_spec((None, len(POOL_WINDOWS), POOL_GROUP, POOL_GROUP), (l, 0, 0, 0)),
                  _const_spec((None, d_mix, d), (l, 0, 0))],
        out_specs=[row(d), row(d), row(d_in), row(dc), row(d_mix), row(d)],
        scratch_shapes=[pltpu.VMEM((tm + h, d_mix), F32)],
        compiler_params=_params(1),
    )(x, mod, gains, w_in, conv_w, vec, pool_w, w_out)


def _mixer_bwd(l, dx1, x, mod, gains, u, a1, o, w_in, conv_w, vec, pool_w, w_out, *, seq, tm, act_dtype):
    t, d = x.shape
    d_in = w_in.shape[-1]
    dc = conv_w.shape[-1]
    d_mix = w_out.shape[-2]
    n_taps = 31
    nt, tps = t // tm, seq // tm
    h = CONV_HALO
    n_groups = len(POOL_WINDOWS)

    def body(dx1_ref, x_ref, mod_ref, g_ref, u_ref, uh_ref, a1_ref, o_ref, win_ref, cw_ref, v_ref, pw_ref, wout_ref,
             dx_ref, du_ref, do_ref, dmod_ref, gd_ref, gv_ref, dcw_ref, dpw_ref, ext_ref, fext_ref):
        i = pl.program_id(0)
        j = nt - 1 - i
        first_in_seq = (j % tps) == 0
        last_in_seq = (j % tps) == tps - 1

        @pl.when(i == 0)
        def _():
            gd_ref[...] = jnp.zeros_like(gd_ref)
            gv_ref[...] = jnp.zeros_like(gv_ref)
            dcw_ref[...] = jnp.zeros_like(dcw_ref)
            dpw_ref[...] = jnp.zeros_like(dpw_ref)

        @pl.when(last_in_seq)
        def _():
            dmod_ref[...] = jnp.zeros_like(dmod_ref)
            fext_ref[tm:tm + h, :] = jnp.zeros((h, fext_ref.shape[1]), F32)

        @pl.when(jnp.logical_not(last_in_seq))
        def _():
            fext_ref[tm:tm + h, :] = fext_ref[0:h, :]

        xv = x_ref[...]
        dx1v = dx1_ref[...]
        pre_g, post_g = g_ref[0:1, :], g_ref[1:2, :]

        ov = o_ref[...]
        ro = lax.rsqrt(_mean(ov * ov) + EPS)
        yo = ov * ro
        dmod_ref[2:3, :] += _colsum(dx1v * (yo * post_g))
        dn = dx1v * (1.0 + mod_ref[2:3, :])
        gd_ref[1:2, :] += _colsum(dn * yo)
        dyo = dn * post_g
        do = ro * (dyo - yo * _mean(dyo * yo))
        dob = do.astype(act_dtype)
        do_ref[...] = dob
        dap = _mm_nt(dob, wout_ref[...])

        uv = u_ref[...]
        val, gate = uv[:, :dc], uv[:, dc:2 * dc]
        sg = _sigmoid(gate)
        ext_ref[h:h + tm, 0:dc] = val * sg
        ext_ref[h:h + tm, dc:] = uv[:, 2 * dc:]
        keep = jnp.where(first_in_seq, 0.0, 1.0).astype(F32)
        uh = uh_ref[...]
        ext_ref[0:h, 0:dc] = uh[:, :dc] * _sigmoid(uh[:, dc:2 * dc]) * keep
        ext_ref[0:h, dc:] = uh[:, 2 * dc:] * keep

        a1v = a1_ref[...]
        mu = _mean(a1v)
        xc = a1v - mu
        rstd = lax.rsqrt(_mean(xc * xc) + EPS)
        xh = xc * rstd
        ln_g = v_ref[1:2, :]
        a2 = xh * ln_g + v_ref[2:3, :]
        s2 = _sigmoid(a2)
        da2 = dap[:, :dc] * (s2 * (1.0 + a2 * (1.0 - s2)))
        gv_ref[1:2, :] += _colsum(da2 * xh)
        gv_ref[2:3, :] += _colsum(da2)
        dxh = da2 * ln_g
        da1 = rstd * (dxh - _mean(dxh) - xh * _mean(dxh * xh))
        gv_ref[0:1, :] += _colsum(da1)
        fext_ref[0:tm, 0:dc] = da1

        for k in range(n_taps):
            row = h - (n_taps - 1) + k
            dcw_ref[k:k + 1, :] += _colsum(fext_ref[0:tm, 0:dc] * ext_ref[row:row + tm, 0:dc])

        def store(r0, rc, cs, acc):
            sgc = _sigmoid(u_ref[r0:r0 + rc, dc + cs.start:dc + cs.stop])
            vc = u_ref[r0:r0 + rc, cs]
            du_ref[r0:r0 + rc, cs] = (acc * sgc).astype(act_dtype)
            du_ref[r0:r0 + rc, dc + cs.start:dc + cs.stop] = (acc * vc * sgc * (1.0 - sgc)).astype(act_dtype)

        _conv_taps(fext_ref, 0, dc, tm, [(n_taps - 1) - k for k in range(n_taps)],
                   lambda k, cs: cw_ref[k:k + 1, cs],
                   lambda cs, rc: jnp.zeros((rc, LANES), F32), store)

        pos = (j % tps) * tm + lax.broadcasted_iota(jnp.int32, (tm, 1), 0)
        for g, w in enumerate(POOL_WINDOWS):
            cs = slice(dc + g * POOL_GROUP, dc + (g + 1) * POOL_GROUP)
            gs = slice(g * POOL_GROUP, (g + 1) * POOL_GROUP)
            s = ext_ref[h:h + tm, cs]
            for jj in range(1, w):
                s = s + ext_ref[h - jj:h - jj + tm, cs]
            cnt = jnp.minimum(pos + 1, w).astype(F32)
            dv = (s / cnt - ext_ref[h:h + tm, cs]).astype(MXU_DTYPE)
            q = _mm(dv, pw_ref[g])
            dp = dap[:, cs]
            gv_ref[3:4, gs] += _colsum(dp * q)
            dq = (dp * v_ref[3:4, gs]).astype(MXU_DTYPE)
            dpw_ref[g] += _mm_tn(dv, dq)
            dd = _mm_nt(dq, pw_ref[g])
            fext_ref[0:tm, cs] = dd / cnt
            dhp = fext_ref[0:tm, cs]
            for jj in range(1, w):
                dhp = dhp + fext_ref[jj:jj + tm, cs]
            du_ref[:, dc + cs.start:dc + cs.stop] = (dhp - dd).astype(act_dtype)

        dh = _mm_nt(du_ref[...], win_ref[...])

        r = lax.rsqrt(_mean(xv * xv) + EPS)
        xn = xv * r
        dmod_ref[0:1, :] += _colsum(dh)
        dmod_ref[1:2, :] += _colsum(dh * (xn * pre_g))
        dy = dh * (1.0 + mod_ref[1:2, :])
        gd_ref[0:1, :] += _colsum(dy * xn)
        dxn = dy * pre_g
        dx_ref[...] = dx1v + r * (dxn - xn * _mean(dxn * xn))

    rev = lambda width: pl.BlockSpec((tm, width), lambda i: (nt - 1 - i, 0))
    blocks_per_tile = tm // h
    return pl.pallas_call(
        body, name=f"mixer_bwd_{l}", grid=(nt,),
        out_shape=[jax.ShapeDtypeStruct((t, d), F32), jax.ShapeDtypeStruct((t, d_in), act_dtype),
                   jax.ShapeDtypeStruct((t, d), act_dtype),
                   jax.ShapeDtypeStruct((t // seq, 8, d), F32), jax.ShapeDtypeStruct((8, d), F32),
                   jax.ShapeDtypeStruct((8, dc), F32), jax.ShapeDtypeStruct((32, dc), F32),
                   jax.ShapeDtypeStruct((n_groups, POOL_GROUP, POOL_GROUP), F32)],
        in_specs=[rev(d), rev(d),
                  pl.BlockSpec((None, None, 8, d), lambda i: (l, (nt - 1 - i) // tps, 0, 0)),
                  _const_spec((None, 8, d), (l, 0, 0)),
                  rev(d_in),
                  pl.BlockSpec((h, d_in), lambda i: (jnp.maximum((nt - 1 - i) * blocks_per_tile - 1, 0), 0)),
                  rev(dc), rev(d),
                  _const_spec((None, d, d_in), (l, 0, 0)),
                  _const_spec((None, 32, dc), (l, 0, 0)),
                  _const_spec((None, 8, dc), (l, 0, 0)),
                  _const_spec((None, n_groups, POOL_GROUP, POOL_GROUP), (l, 0, 0, 0)),
                  _const_spec((None, d_mix, d), (l, 0, 0))],
        out_specs=[rev(d), rev(d_in), rev(d),
                   pl.BlockSpec((None, 8, d), lambda i: ((nt - 1 - i) // tps, 0, 0)),
                   _const_spec((8, d), (0, 0)), _const_spec((8, dc), (0, 0)), _const_spec((32, dc), (0, 0)),
                   _const_spec((n_groups, POOL_GROUP, POOL_GROUP), (0, 0, 0))],
        scratch_shapes=[pltpu.VMEM((tm + h, d_mix), F32), pltpu.VMEM((tm + h, d_mix), F32)],
        compiler_params=_params(1),
    )(dx1, x, mod, gains, u, u, a1, o, w_in, conv_w, vec, pool_w, w_out)


def _ffn_fwd(l, x, mod, gains, up, fcw, down, *, seq, tm, act_dtype):
    t, d = x.shape
    n_chunks, _, fc = up.shape[1:]
    half = n_chunks // 2
    nt, tps = t // tm, seq // tm
    h = FFN_HALO

    def body(x_ref, mod_ref, g_ref, up_ref, fcw_ref, down_ref,
             x2_ref, h2_ref, u2_ref, hid_ref, o2_ref, ext_ref):
        i = pl.program_id(0)
        first = (i % tps) == 0
        xv = x_ref[...]
        r = lax.rsqrt(_mean(xv * xv) + EPS)
        hv = (xv * r) * g_ref[0:1, :] * (1.0 + mod_ref[4:5, :]) + mod_ref[3:4, :]
        hb = hv.astype(act_dtype)
        h2_ref[...] = hb

        @pl.when(first)
        def _():
            for n in range(n_chunks):
                ext_ref[n, 0:h, :] = jnp.zeros((h, fc), F32)

        @pl.when(jnp.logical_not(first))
        def _():
            for n in range(n_chunks):
                ext_ref[n, 0:h, :] = ext_ref[n, tm:tm + h, :]

        for n in range(n_chunks):
            un = _mm(hb, up_ref[n])
            ext_ref[n, h:h + tm, :] = un
            u2_ref[n] = un.astype(act_dtype)

        def conv(n):
            return (fcw_ref[n, 3:4, :] + fcw_ref[n, 0:1, :] * ext_ref[n, h - 2:h - 2 + tm, :]
                    + fcw_ref[n, 1:2, :] * ext_ref[n, h - 1:h - 1 + tm, :]
                    + fcw_ref[n, 2:3, :] * ext_ref[n, h:h + tm, :])

        o2 = jnp.zeros((tm, d), F32)
        for n in range(half):
            gt = conv(n + half)
            hid = ((gt * _sigmoid(gt)) * conv(n)).astype(act_dtype)
            hid_ref[n] = hid
            o2 = o2 + _mm(hid, down_ref[n])
        o2_ref[...] = o2
        ro = lax.rsqrt(_mean(o2 * o2) + EPS)
        x2_ref[...] = xv + (1.0 + mod_ref[5:6, :]) * ((o2 * ro) * g_ref[1:2, :])

    row = lambda width: pl.BlockSpec((tm, width), lambda i: (i, 0))
    return pl.pallas_call(
        body, name=f"ffn_fwd_{l}", grid=(nt,),
        out_shape=[jax.ShapeDtypeStruct((t, d), F32), jax.ShapeDtypeStruct((t, d), act_dtype),
                   jax.ShapeDtypeStruct((n_chunks, t, fc), act_dtype),
                   jax.ShapeDtypeStruct((half, t, fc), act_dtype), jax.ShapeDtypeStruct((t, d), F32)],
        in_specs=[row(d),
                  pl.BlockSpec((None, None, 8, d), lambda i: (l, i // tps, 0, 0)),
                  _const_spec((None, 8, d), (l, 0, 0)),
                  _const_spec((None, n_chunks, d, fc), (l, 0, 0, 0)),
                  _const_spec((None, n_chunks, 8, fc), (l, 0, 0, 0)),
                  _const_spec((None, half, fc, d), (l, 0, 0, 0))],
        out_specs=[row(d), row(d),
                   pl.BlockSpec((n_chunks, tm, fc), lambda i: (0, i, 0)),
                   pl.BlockSpec((half, tm, fc), lambda i: (0, i, 0)), row(d)],
        scratch_shapes=[pltpu.VMEM((n_chunks, tm + h, fc), F32)],
        compiler_params=_params(1),
    )(x, mod, gains, up, fcw, down)


def _ffn_bwd(l, dx2, x, mod, gains, u2, o2, up, fcw, down, *, seq, tm, act_dtype):
    t, d = x.shape
    n_chunks, _, fc = up.shape[1:]
    half = n_chunks // 2
    nt, tps = t // tm, seq // tm
    h = FFN_HALO
    hi = FFN_HALO_IN

    def body(dx2_ref, x_ref, mod_ref, g_ref, u2_ref, u2h_ref, o2_ref, up_ref, fcw_ref, down_ref,
             dx1_ref, du2_ref, do2_ref, dmod_ref, gd_ref, dfcw_ref, ext_ref, fext_ref):
        i = pl.program_id(0)
        j = nt - 1 - i
        first_in_seq = (j % tps) == 0
        last_in_seq = (j % tps) == tps - 1

        @pl.when(i == 0)
        def _():
            gd_ref[...] = jnp.zeros_like(gd_ref)
            dfcw_ref[...] = jnp.zeros_like(dfcw_ref)

        @pl.when(last_in_seq)
        def _():
            dmod_ref[...] = jnp.zeros_like(dmod_ref)
            for n in range(n_chunks):
                fext_ref[n, tm:tm + h, :] = jnp.zeros((h, fc), F32)

        @pl.when(jnp.logical_not(last_in_seq))
        def _():
            for n in range(n_chunks):
                fext_ref[n, tm:tm + h, :] = fext_ref[n, 0:h, :]

        xv = x_ref[...]
        dx2v = dx2_ref[...]
        pre_g, post_g = g_ref[0:1, :], g_ref[1:2, :]

        ov = o2_ref[...]
        ro = lax.rsqrt(_mean(ov * ov) + EPS)
        yo = ov * ro
        dmod_ref[2:3, :] += _colsum(dx2v * (yo * post_g))
        dn = dx2v * (1.0 + mod_ref[5:6, :])
        gd_ref[1:2, :] += _colsum(dn * yo)
        dyo = dn * post_g
        do = ro * (dyo - yo * _mean(dyo * yo))
        dob = do.astype(act_dtype)
        do2_ref[...] = dob

        keep = jnp.where(first_in_seq, 0.0, 1.0).astype(F32)
        for n in range(n_chunks):
            ext_ref[n, hi:hi + tm, :] = u2_ref[n].astype(F32)
            ext_ref[n, 0:hi, :] = u2h_ref[n].astype(F32) * keep

        def conv(n):
            return (fcw_ref[n, 3:4, :] + fcw_ref[n, 0:1, :] * ext_ref[n, hi - 2:hi - 2 + tm, :]
                    + fcw_ref[n, 1:2, :] * ext_ref[n, hi - 1:hi - 1 + tm, :]
                    + fcw_ref[n, 2:3, :] * ext_ref[n, hi:hi + tm, :])

        for n in range(half):
            v = conv(n)
            gt = conv(n + half)
            sg = _sigmoid(gt)
            dhid = _mm_nt(dob, down_ref[n])
            fext_ref[n, 0:tm, :] = dhid * (gt * sg)
            fext_ref[n + half, 0:tm, :] = dhid * v * (sg * (1.0 + gt * (1.0 - sg)))

        dh = jnp.zeros((tm, d), F32)
        for n in range(n_chunks):
            du3 = fext_ref[n, 0:tm, :]
            dfcw_ref[n, 3:4, :] += _colsum(du3)
            for k in range(3):
                dfcw_ref[n, k:k + 1, :] += _colsum(du3 * ext_ref[n, hi - 2 + k:hi - 2 + k + tm, :])
            du2 = (fcw_ref[n, 0:1, :] * fext_ref[n, 2:2 + tm, :] + fcw_ref[n, 1:2, :] * fext_ref[n, 1:1 + tm, :]
                   + fcw_ref[n, 2:3, :] * du3).astype(act_dtype)
            du2_ref[n] = du2
            dh = dh + _mm_nt(du2, up_ref[n])

        r = lax.rsqrt(_mean(xv * xv) + EPS)
        xn = xv * r
        dmod_ref[0:1, :] += _colsum(dh)
        dmod_ref[1:2, :] += _colsum(dh * (xn * pre_g))
        dy = dh * (1.0 + mod_ref[4:5, :])
        gd_ref[0:1, :] += _colsum(dy * xn)
        dxn = dy * pre_g
        dx1_ref[...] = dx2v + r * (dxn - xn * _mean(dxn * xn))

    rev = lambda width: pl.BlockSpec((tm, width), lambda i: (nt - 1 - i, 0))
    blocks_per_tile = tm // hi
    return pl.pallas_call(
        body, name=f"ffn_bwd_{l}", grid=(nt,),
        out_shape=[jax.ShapeDtypeStruct((t, d), F32), jax.ShapeDtypeStruct((n_chunks, t, fc), act_dtype),
                   jax.ShapeDtypeStruct((t, d), act_dtype),
                   jax.ShapeDtypeStruct((t // seq, 8, d), F32), jax.ShapeDtypeStruct((8, d), F32),
                   jax.ShapeDtypeStruct((n_chunks, 8, fc), F32)],
        in_specs=[rev(d), rev(d),
                  pl.BlockSpec((None, None, 8, d), lambda i: (l, (nt - 1 - i) // tps, 0, 0)),
                  _const_spec((None, 8, d), (l, 0, 0)),
                  pl.BlockSpec((n_chunks, tm, fc), lambda i: (0, nt - 1 - i, 0)),
                  pl.BlockSpec((n_chunks, hi, fc),
                               lambda i: (0, jnp.maximum((nt - 1 - i) * blocks_per_tile - 1, 0), 0)),
                  rev(d),
                  _const_spec((None, n_chunks, d, fc), (l, 0, 0, 0)),
                  _const_spec((None, n_chunks, 8, fc), (l, 0, 0, 0)),
                  _const_spec((None, half, fc, d), (l, 0, 0, 0))],
        out_specs=[rev(d), pl.BlockSpec((n_chunks, tm, fc), lambda i: (0, nt - 1 - i, 0)), rev(d),
                   pl.BlockSpec((None, 8, d), lambda i: ((nt - 1 - i) // tps, 0, 0)),
                   _const_spec((8, d), (0, 0)), _const_spec((n_chunks, 8, fc), (0, 0, 0))],
        scratch_shapes=[pltpu.VMEM((n_chunks, tm + hi, fc), F32), pltpu.VMEM((n_chunks, tm + h, fc), F32)],
        compiler_params=_params(1),
    )(dx2, x, mod, gains, u2, u2, o2, up, fcw, down)


def _pad_rows(a, rows):
    pad = [(0, 0)] * a.ndim
    pad[-2] = (0, rows - a.shape[-2])
    return jnp.pad(a, pad)


def kernel(x, c, ada_w, ada_b, pre_mix_g, post_mix_g, w_in, conv_w, conv_b, conv_ln_g, conv_ln_b, pool_w, pool_scale, w_out, pre_ffn_g, post_ffn_g, ffn_up, ffn_conv_w, ffn_conv_b, ffn_down, loss_target, m_ada_w, m_ada_b, m_pre_mix_g, m_post_mix_g, m_w_in, m_conv_w, m_conv_b, m_conv_ln_g, m_conv_ln_b, m_pool_w, m_pool_scale, m_w_out, m_pre_ffn_g, m_post_ffn_g, m_ffn_up, m_ffn_conv_w, m_ffn_conv_b, m_ffn_down, v_ada_w, v_ada_b, v_pre_mix_g, v_post_mix_g, v_w_in, v_conv_w, v_conv_b, v_conv_ln_g, v_conv_ln_b, v_pool_w, v_pool_scale, v_w_out, v_pre_ffn_g, v_post_ffn_g, v_ffn_up, v_ffn_conv_w, v_ffn_conv_b, v_ffn_down):
    bl, seq, d = x.shape
    n_layers = ada_w.shape[0]
    t = bl * seq
    dc = conv_b.shape[1]
    d_in = w_in.shape[2] * N_DEV
    d_mix = w_out.shape[1] * N_DEV
    n_taps = conv_w.shape[1]
    fc = ffn_up.shape[2]
    n_chunks, half = N_DEV, N_DEV // 2
    ada_cols = ada_w.shape[2]
    n_mod = ada_cols * N_DEV // d
    assert pool_scale.shape[1] == dc and n_taps == 31 and n_mod == 6 and ffn_conv_w.shape[1] == 3
    assert pool_w.shape[1:] == (len(POOL_WINDOWS), POOL_GROUP, POOL_GROUP)
    tm = 256 if seq % 256 == 0 else 64
    tk = 512 if t % 512 == 0 else tm
    act = MXU_DTYPE

    ax = lax.axis_index
    me = 4 * ax("x") + 2 * ax("y") + ax("c")

    (c_all,) = _all_gather([c], "gather_c")
    c_all = c_all.reshape(N_DEV * bl, d)
    ada_b_cols = lax.dynamic_slice_in_dim(ada_b, me * ada_cols, ada_cols, axis=1)
    mod_cols = _ada_fwd(c_all, ada_w, ada_b_cols)

    g_mod, g_w_in, g_w_out, g_up, g_down, g_conv_w, g_fcw = _all_gather(
        [mod_cols, w_in.astype(act), w_out.astype(act), ffn_up.astype(act), ffn_down.astype(act),
         conv_w, ffn_conv_w], "gather_w")
    mod_all = g_mod.transpose(1, 2, 0, 3).reshape(n_layers, N_DEV * bl, n_mod, d)
    mod = _pad_rows(lax.dynamic_slice_in_dim(mod_all, me * bl, bl, axis=1), 8)
    w_in_f = g_w_in.transpose(1, 2, 0, 3).reshape(n_layers, d, d_in)
    w_out_f = g_w_out.transpose(1, 0, 2, 3).reshape(n_layers, d_mix, d)
    up_f = g_up.transpose(1, 0, 2, 3)
    down_f = g_down.transpose(1, 0, 2, 3).reshape(n_layers, half, fc, d)
    conv_w_f = _pad_rows(g_conv_w.transpose(1, 2, 0, 3).reshape(n_layers, n_taps, dc), 32)
    fcw_f = _pad_rows(jnp.concatenate(
        [g_fcw.transpose(1, 0, 2, 3), ffn_conv_b.reshape(n_layers, n_chunks, 1, fc)], axis=2), 8)
    gains_mix = _pad_rows(jnp.stack([pre_mix_g, post_mix_g], axis=1), 8)
    gains_ffn = _pad_rows(jnp.stack([pre_ffn_g, post_ffn_g], axis=1), 8)
    vec = _pad_rows(jnp.stack([conv_b, conv_ln_g, conv_ln_b, pool_scale], axis=1), 8)
    pool_w_b = pool_w.astype(act)

    kw = dict(seq=seq, tm=tm, act_dtype=act)
    xs = x.reshape(t, d)
    saved = []
    for l in range(n_layers):
        x1, h1, u, a1, ap, o = _mixer_fwd(l, xs, mod, gains_mix, w_in_f, conv_w_f, vec, pool_w_b, w_out_f, **kw)
        x2, h2, u2, hid, o2 = _ffn_fwd(l, x1, mod, gains_ffn, up_f, fcw_f, down_f, **kw)
        saved.append((xs, h1, u, a1, ap, o, x1, h2, u2, hid, o2))
        xs = x2

    dx, loss_part = _loss_grad(xs, loss_target.reshape(t, d), tm)

    p_w_in, p_w_out, p_up, p_down, p_conv_w, p_fcw = [], [], [], [], [], []
    dmods, smalls = [], []
    for l in reversed(range(n_layers)):
        xin, h1, u, a1, ap, o, x1, h2, u2, hid, o2 = saved[l]
        dx1, du2, do2, dmod_b, gd_b, dfcw = _ffn_bwd(l, dx, x1, mod, gains_ffn, u2, o2, up_f, fcw_f, down_f, **kw)
        p_up.append(_matmul_tn(f"dw_up_{l}", h2[None], du2, tk))
        p_down.append(_matmul_tn(f"dw_down_{l}", hid, do2[None], tk)
                      .reshape(N_DEV, fc // 2, d))
        dx, du, do, dmod_a, gd_a, gv, dcw, dpw = _mixer_bwd(
            l, dx1, xin, mod, gains_mix, u, a1, o, w_in_f, conv_w_f, vec, pool_w_b, w_out_f, **kw)
        dw_in = _matmul_tn(f"dw_in_{l}", h1[None], du[None], tk)[0]
        p_w_in.append(dw_in.reshape(d, N_DEV, d_in // N_DEV).transpose(1, 0, 2))
        p_w_out.append(_matmul_tn(f"dw_out_{l}", ap[None], do[None], tk)[0].reshape(N_DEV, d_mix // N_DEV, d))
        p_conv_w.append(dcw[:n_taps].reshape(n_taps, N_DEV, dc // N_DEV).transpose(1, 0, 2))
        p_fcw.append(dfcw[:, :3, :])
        dmods.append(jnp.concatenate([dmod_a[:, 0:3], dmod_b[:, 0:3]], axis=1).reshape(bl, n_mod * d))
        smalls.append(jnp.concatenate(
            [gd_a[0], gd_a[1], gv[0], gv[1], gv[2], gv[3], gd_b[0], gd_b[1], dfcw[:, 3, :].reshape(-1),
             dpw.reshape(-1)]))
    for lst in (p_w_in, p_w_out, p_up, p_down, p_conv_w, p_fcw, dmods, smalls):
        lst.reverse()

    dmod_loc = jnp.stack(dmods)
    small_loc = jnp.stack(smalls)
    n_small = small_loc.shape[1]
    g_dmod, g_small, g_loss = _all_gather([dmod_loc, small_loc.reshape(-1, LANES), loss_part], "gather_g")
    r_w_in, r_w_out, r_up, r_down, r_conv_w, r_fcw = _all_to_all(
        [p_w_in, p_w_out, p_up, p_down, p_conv_w, p_fcw], "exchange_dw")

    loss = _sum_parts(g_loss)[0, 0]

    def flat2(a):
        return a.reshape(-1, a.shape[-1])

    def update(name, parts, w, m, v):
        outs = _adam_reduce(name, parts.reshape(parts.shape[0], -1, w.shape[-1]), flat2(w), flat2(m), flat2(v))
        return [o_.reshape(w.shape) for o_ in outs]

    res = {}
    res["w_in"] = update("adam_w_in", r_w_in, w_in, m_w_in, v_w_in)
    res["w_out"] = update("adam_w_out", r_w_out, w_out, m_w_out, v_w_out)
    res["ffn_up"] = update("adam_ffn_up", r_up, ffn_up, m_ffn_up, v_ffn_up)
    res["ffn_down"] = update("adam_ffn_down", r_down, ffn_down, m_ffn_down, v_ffn_down)
    res["conv_w"] = update("adam_conv_w", r_conv_w, conv_w, m_conv_w, v_conv_w)
    res["ffn_conv_w"] = update("adam_ffn_conv_w", r_fcw, ffn_conv_w, m_ffn_conv_w, v_ffn_conv_w)

    dmod_all = g_dmod.transpose(1, 0, 2, 3).reshape(n_layers, N_DEV * bl, n_mod * d)
    dmod_cols = lax.dynamic_slice_in_dim(dmod_all, me * ada_cols, ada_cols, axis=2)
    res["ada_w"] = list(_ada_bwd(c_all.T, dmod_cols, ada_w, m_ada_w, v_ada_w))
    res["ada_b"] = update("adam_ada_b", dmod_all.transpose(1, 0, 2), ada_b, m_ada_b, v_ada_b)

    small_names = ["pre_mix_g", "post_mix_g", "conv_b", "conv_ln_g", "conv_ln_b", "pool_scale", "pre_ffn_g",
                   "post_ffn_g", "ffn_conv_b", "pool_w"]
    small_w = [pre_mix_g, post_mix_g, conv_b, conv_ln_g, conv_ln_b, pool_scale, pre_ffn_g, post_ffn_g,
               ffn_conv_b, pool_w]
    small_m = [m_pre_mix_g, m_post_mix_g, m_conv_b, m_conv_ln_g, m_conv_ln_b, m_pool_scale, m_pre_ffn_g,
               m_post_ffn_g, m_ffn_conv_b, m_pool_w]
    small_v = [v_pre_mix_g, v_post_mix_g, v_conv_b, v_conv_ln_g, v_conv_ln_b, v_pool_scale, v_pre_ffn_g,
               v_post_ffn_g, v_ffn_conv_b, v_pool_w]

    def pack(arrs):
        return jnp.concatenate([a.reshape(n_layers, -1) for a in arrs], axis=1).reshape(-1, LANES)

    outs = _adam_reduce("adam_small", g_small, pack(small_w), pack(small_m), pack(small_v))
    outs = [o_.reshape(n_layers, n_small) for o_ in outs]
    off = 0
    for name, w in zip(small_names, small_w):
        size = w[0].size
        res[name] = [o_[:, off:off + size].reshape(w.shape) for o_ in outs]
        off += size

    order = ["ada_w", "ada_b", "pre_mix_g", "post_mix_g", "w_in", "conv_w", "conv_b", "conv_ln_g", "conv_ln_b",
             "pool_w", "pool_scale", "w_out", "pre_ffn_g", "post_ffn_g", "ffn_up", "ffn_conv_w", "ffn_conv_b",
             "ffn_down"]
    return (loss, dx.reshape(bl, seq, d), *[res[n][0] for n in order], *[res[n][1] for n in order],
            *[res[n][2] for n in order], *[res[n][3] for n in order])
```

```python
import functools

import jax
import jax.numpy as jnp
from jax import lax
from jax.experimental import pallas as pl
from jax.experimental.pallas import tpu as pltpu

N_DEV = 8
EPS = 1e-6
POOL_WINDOWS = (2, 4, 8, 16)
POOL_GROUP = 128
CONV_HALO = 32
FFN_HALO = 8
FFN_HALO_IN = 16
LANES = 128
ROW_CHUNK = 64
MXU_DTYPE = jnp.bfloat16
VMEM_LIMIT = 60 * 1024 * 1024

ADAM_LR = 0.001
ADAM_B1 = 0.9
ADAM_B2 = 0.999
ADAM_EPS = 1e-08
ADAM_WD = 0.01
ADAM_STEP = 10

MESH = pl.DeviceIdType.MESH
F32 = jnp.float32


def _mm(a, b):
    return jnp.dot(a.astype(MXU_DTYPE), b.astype(MXU_DTYPE), preferred_element_type=F32)


def _mm_nt(a, b):
    return lax.dot_general(a.astype(MXU_DTYPE), b.astype(MXU_DTYPE), (((1,), (1,)), ((), ())),
                           preferred_element_type=F32)


def _mm_tn(a, b):
    return lax.dot_general(a.astype(MXU_DTYPE), b.astype(MXU_DTYPE), (((0,), (0,)), ((), ())),
                           preferred_element_type=F32)


def _mean(v):
    return jnp.mean(v, axis=-1, keepdims=True)


def _colsum(v):
    return jnp.sum(v, axis=0, keepdims=True)


def _sigmoid(v):
    return jax.nn.sigmoid(v)


def _params(n_grid=1):
    return pltpu.CompilerParams(dimension_semantics=("arbitrary",) * n_grid, vmem_limit_bytes=VMEM_LIMIT)


def _const_spec(shape, index):
    return pl.BlockSpec(shape, lambda *_: index)


def _position():
    x, y, c = lax.axis_index("x"), lax.axis_index("y"), lax.axis_index("c")
    return x, y, c


def _all_gather(arrs, name):
    n = len(arrs)

    def body(*refs):
        ins, outs = refs[:n], refs[n:2 * n]
        send_sems, recv_sems, local_sems = refs[2 * n:]
        x, y, c = _position()
        me, sibling = (x, y, c), (x, y, 1 - c)
        chips = [(1 - x, y), (x, 1 - y), (1 - x, 1 - y)]

        def slot(k, px, py, pc):
            return outs[k].at[4 * px + 2 * py + pc]

        def copy(k, s, block, to, src=None):
            return pltpu.make_async_remote_copy(
                src_ref=slot(k, *block) if src is None else src, dst_ref=slot(k, *block),
                send_sem=send_sems.at[k, s], recv_sem=recv_sems.at[k, s], device_id=to, device_id_type=MESH)

        mine = [pltpu.make_async_copy(ins[k], slot(k, *me), local_sems.at[k]) for k in range(n)]
        for cp in mine:
            cp.start()
        first = []
        for k in range(n):
            first.append(copy(k, 0, me, sibling, src=ins[k]))
            first += [copy(k, 1 + j, me, (*chip, c), src=ins[k]) for j, chip in enumerate(chips)]
        for cp in first:
            cp.start()
        passed = []
        for j, chip in enumerate(chips):
            for k in range(n):
                copy(k, 1 + j, (*chip, c), me).wait_recv()
                fwd = copy(k, 4 + j, (*chip, c), sibling)
                fwd.start()
                passed.append(fwd)
        for k in range(n):
            copy(k, 0, sibling, me).wait_recv()
            for j, chip in enumerate(chips):
                copy(k, 4 + j, (*chip, 1 - c), me).wait_recv()
        for cp in first + passed:
            cp.wait_send()
        for cp in mine:
            cp.wait()

    any_spec = pl.BlockSpec(memory_space=pl.ANY)
    return pl.pallas_call(
        body, name=name,
        out_shape=[jax.ShapeDtypeStruct((N_DEV,) + a.shape, a.dtype) for a in arrs],
        in_specs=[any_spec] * n, out_specs=[any_spec] * n,
        scratch_shapes=[pltpu.SemaphoreType.DMA((n, 7)), pltpu.SemaphoreType.DMA((n, 7)),
                        pltpu.SemaphoreType.DMA((n,))],
    )(*arrs)


def _exchange(payload, recvs, layer, send_sems, recv_sems, local_sems, start):
    x, y, c = _position()
    me = 4 * x + 2 * y + c
    for k, (src, recv) in enumerate(zip(payload, recvs)):
        local = pltpu.make_async_copy(src.at[me], recv.at[me, layer], local_sems.at[k])
        if start:
            local.start()
        else:
            local.wait()
        for j in range(1, N_DEV):
            px = (1 - x) if (j & 4) else x
            py = (1 - y) if (j & 2) else y
            pc = (1 - c) if (j & 1) else c
            peer = 4 * px + 2 * py + pc
            landing = recv.at[me, layer] if start else recv.at[peer, layer]
            cp = pltpu.make_async_remote_copy(
                src_ref=src.at[peer], dst_ref=landing, send_sem=send_sems.at[k, j - 1],
                recv_sem=recv_sems.at[k, j - 1], device_id=(px, py, pc), device_id_type=MESH)
            if start:
                cp.start()
            else:
                cp.wait()


def _exchange_scratch(n):
    return [pltpu.SemaphoreType.DMA((n, N_DEV - 1)), pltpu.SemaphoreType.DMA((n, N_DEV - 1)),
            pltpu.SemaphoreType.DMA((n,))]


def _all_to_all(payload, recvs, layer, name):
    n = len(payload)

    def body(*refs):
        pay, outs = refs[:n], refs[2 * n:3 * n]
        sems = refs[3 * n:]
        _exchange(pay, outs, layer, *sems, start=True)
        _exchange(pay, outs, layer, *sems, start=False)

    any_spec = pl.BlockSpec(memory_space=pl.ANY)
    return pl.pallas_call(
        body, name=name,
        out_shape=[jax.ShapeDtypeStruct(r.shape, r.dtype) for r in recvs],
        in_specs=[any_spec] * (2 * n), out_specs=[any_spec] * n,
        input_output_aliases={n + k: k for k in range(n)},
        scratch_shapes=_exchange_scratch(n),
    )(*payload, *recvs)


def _adamw(w, g, m, v):
    m = ADAM_B1 * m + (1.0 - ADAM_B1) * g
    v = ADAM_B2 * v + (1.0 - ADAM_B2) * jnp.square(g)
    m_hat = m / (1.0 - ADAM_B1 ** ADAM_STEP)
    v_hat = v / (1.0 - ADAM_B2 ** ADAM_STEP)
    delta = -ADAM_LR * (m_hat / (jnp.sqrt(v_hat) + ADAM_EPS) + ADAM_WD * w)
    return delta, m, v


def _ada_fwd(c_all, ada_w, ada_b_cols):
    n_layers, d, cols = ada_w.shape
    b = c_all.shape[0]

    def body(c_ref, w_ref, b_ref, o_ref):
        cv = c_ref[...]
        act = cv * _sigmoid(cv)
        o_ref[...] = jnp.dot(act, w_ref[...], preferred_element_type=F32,
                             precision=lax.Precision.HIGHEST) + b_ref[...]

    return pl.pallas_call(
        body, name="ada_fwd", grid=(n_layers,),
        out_shape=jax.ShapeDtypeStruct((n_layers, b, cols), F32),
        in_specs=[pl.BlockSpec((b, d), lambda l: (0, 0)),
                  pl.BlockSpec((None, d, cols), lambda l: (l, 0, 0)),
                  pl.BlockSpec((None, 1, cols), lambda l: (l, 0, 0))],
        out_specs=pl.BlockSpec((None, b, cols), lambda l: (l, 0, 0)),
        compiler_params=_params(1),
    )(c_all, ada_w, ada_b_cols.reshape(n_layers, 1, cols))


def _ada_bwd(c_all_t, dmod_cols, w, m, v):
    n_layers, d, cols = w.shape
    b = c_all_t.shape[1]
    td = 256 if d % 256 == 0 else d

    def body(c_ref, dm_ref, w_ref, m_ref, v_ref, g_ref, dl_ref, nm_ref, nv_ref):
        cv = c_ref[...]
        act = cv * _sigmoid(cv)
        g = jnp.dot(act, dm_ref[...], preferred_element_type=F32, precision=lax.Precision.HIGHEST)
        delta, nm, nv = _adamw(w_ref[...], g, m_ref[...], v_ref[...])
        g_ref[...] = g
        dl_ref[...] = delta
        nm_ref[...] = nm
        nv_ref[...] = nv

    blk = pl.BlockSpec((None, td, cols), lambda l, i: (l, i, 0))
    shp = jax.ShapeDtypeStruct(w.shape, F32)
    return pl.pallas_call(
        body, name="ada_bwd", grid=(n_layers, d // td),
        out_shape=[shp] * 4,
        in_specs=[pl.BlockSpec((td, b), lambda l, i: (i, 0)),
                  pl.BlockSpec((None, b, cols), lambda l, i: (l, 0, 0)), blk, blk, blk],
        out_specs=[blk] * 4,
        compiler_params=_params(2),
    )(c_all_t, dmod_cols, w, m, v)


def _row_tile(rows, cols, budget=128 * 1024, step=8):
    best = None
    for t in range(step, rows + 1, step):
        if rows % t == 0 and t * cols <= budget:
            best = t
    return best if best is not None else rows


def _adam_reduce(name, parts, w, m, v):
    p, rows, cols = parts.shape
    tr = _row_tile(rows, cols, budget=(256 * 1024) // max(1, p // 4), step=8 if parts.dtype == F32 else 16)

    def body(p_ref, w_ref, m_ref, v_ref, g_ref, dl_ref, nm_ref, nv_ref):
        g = p_ref[0].astype(F32)
        for k in range(1, p):
            g = g + p_ref[k].astype(F32)
        delta, nm, nv = _adamw(w_ref[...], g, m_ref[...], v_ref[...])
        g_ref[...] = g
        dl_ref[...] = delta
        nm_ref[...] = nm
        nv_ref[...] = nv

    blk = pl.BlockSpec((tr, cols), lambda i: (i, 0))
    shp = jax.ShapeDtypeStruct((rows, cols), F32)
    return pl.pallas_call(
        body, name=name, grid=(rows // tr,),
        out_shape=[shp] * 4,
        in_specs=[pl.BlockSpec((p, tr, cols), lambda i: (0, i, 0)), blk, blk, blk],
        out_specs=[blk] * 4,
        compiler_params=_params(1),
    )(parts, w, m, v)


def _sum_parts(parts):
    p = parts.shape[0]

    def body(p_ref, o_ref):
        acc = p_ref[0]
        for k in range(1, p):
            acc = acc + p_ref[k]
        o_ref[...] = acc

    return pl.pallas_call(body, name="loss_sum", out_shape=jax.ShapeDtypeStruct(parts.shape[1:], F32))(parts)


def _loss_grad(y, target, tm):
    t, d = y.shape

    def body(y_ref, t_ref, dy_ref, loss_ref):
        @pl.when(pl.program_id(0) == 0)
        def _():
            loss_ref[...] = jnp.zeros_like(loss_ref)

        diff = y_ref[...] - t_ref[...]
        dy_ref[...] = diff / d
        part = 0.5 * jnp.sum(_mean(diff * diff), axis=0, keepdims=True)
        loss_ref[...] += jnp.broadcast_to(part, loss_ref.shape)

    blk = pl.BlockSpec((tm, d), lambda i: (i, 0))
    return pl.pallas_call(
        body, name="loss_grad", grid=(t // tm,),
        out_shape=[jax.ShapeDtypeStruct((t, d), F32), jax.ShapeDtypeStruct((8, LANES), F32)],
        in_specs=[blk, blk], out_specs=[blk, pl.BlockSpec((8, LANES), lambda i: (0, 0))],
        compiler_params=_params(1),
    )(y, target)


def _matmul_tn(name, a, b, tk, out_dtype):
    ga, t, m = a.shape
    gb, _, n = b.shape
    g = max(ga, gb)
    n_k = t // tk

    def body(a_ref, b_ref, o_ref, acc_ref):
        k = pl.program_id(1)

        @pl.when(k == 0)
        def _():
            acc_ref[...] = jnp.zeros_like(acc_ref)

        acc_ref[...] += _mm_tn(a_ref[...], b_ref[...])

        @pl.when(k == n_k - 1)
        def _():
            o_ref[...] = acc_ref[...].astype(out_dtype)

    return pl.pallas_call(
        body, name=name, grid=(g, n_k),
        out_shape=jax.ShapeDtypeStruct((g, m, n), out_dtype),
        in_specs=[pl.BlockSpec((None, tk, m), (lambda gi, k: (gi, k, 0)) if ga > 1 else (lambda gi, k: (0, k, 0))),
                  pl.BlockSpec((None, tk, n), (lambda gi, k: (gi, k, 0)) if gb > 1 else (lambda gi, k: (0, k, 0)))],
        out_specs=pl.BlockSpec((None, m, n), lambda gi, k: (gi, 0, 0)),
        scratch_shapes=[pltpu.VMEM((m, n), F32)],
        compiler_params=_params(2),
    )(a, b)


def _conv_taps(src_ref, col0, ncols, tm, tap_rows, weight_of, init_of, store):
    rc = min(ROW_CHUNK, tm)
    for cb in range(ncols // LANES):
        cs = slice(cb * LANES, (cb + 1) * LANES)
        ss = slice(col0 + cb * LANES, col0 + (cb + 1) * LANES)
        for r0 in range(0, tm, rc):
            acc = init_of(cs, rc)
            for k, row in enumerate(tap_rows):
                acc = acc + weight_of(k, cs) * src_ref[r0 + row:r0 + row + rc, ss]
            store(r0, rc, cs, acc)


def _mixer_fwd(l, x, mod, gains, w_in, conv_w, vec, pool_w, w_out, *, seq, tm, act_dtype):
    t, d = x.shape
    d_in = w_in.shape[-1]
    dc = conv_w.shape[-1]
    d_mix = w_out.shape[-2]
    n_taps = 31
    nt, tps = t // tm, seq // tm
    h = CONV_HALO

    def body(x_ref, mod_ref, g_ref, win_ref, cw_ref, v_ref, pw_ref, wout_ref,
             x1_ref, h1_ref, u_ref, a1_ref, ap_ref, o_ref, ext_ref):
        i = pl.program_id(0)
        first = (i % tps) == 0
        xv = x_ref[...]
        r = lax.rsqrt(_mean(xv * xv) + EPS)
        hv = (xv * r) * g_ref[0:1, :] * (1.0 + mod_ref[1:2, :]) + mod_ref[0:1, :]
        hb = hv.astype(act_dtype)
        h1_ref[...] = hb
        u = _mm(hb, win_ref[...])
        u_ref[...] = u
        a0 = u[:, :dc] * _sigmoid(u[:, dc:2 * dc])

        @pl.when(first)
        def _():
            ext_ref[0:h, :] = jnp.zeros((h, ext_ref.shape[1]), F32)

        @pl.when(jnp.logical_not(first))
        def _():
            ext_ref[0:h, :] = ext_ref[tm:tm + h, :]

        ext_ref[h:h + tm, 0:dc] = a0
        ext_ref[h:h + tm, dc:] = u[:, 2 * dc:]

        def store(r0, rc, cs, acc):
            a1_ref[r0:r0 + rc, cs] = acc

        _conv_taps(ext_ref, 0, dc, tm, [h - (n_taps - 1) + k for k in range(n_taps)],
                   lambda k, cs: cw_ref[k:k + 1, cs],
                   lambda cs, rc: jnp.broadcast_to(v_ref[0:1, cs], (rc, LANES)), store)
        a1 = a1_ref[...]
        mu = _mean(a1)
        xc = a1 - mu
        rstd = lax.rsqrt(_mean(xc * xc) + EPS)
        a2 = (xc * rstd) * v_ref[1:2, :] + v_ref[2:3, :]
        ap_ref[:, 0:dc] = (a2 * _sigmoid(a2)).astype(act_dtype)

        pos = (i % tps) * tm + lax.broadcasted_iota(jnp.int32, (tm, 1), 0)
        for g, w in enumerate(POOL_WINDOWS):
            cs = slice(dc + g * POOL_GROUP, dc + (g + 1) * POOL_GROUP)
            s = ext_ref[h:h + tm, cs]
            for j in range(1, w):
                s = s + ext_ref[h - j:h - j + tm, cs]
            cnt = jnp.minimum(pos + 1, w).astype(F32)
            dv = s / cnt - ext_ref[h:h + tm, cs]
            q = _mm(dv, pw_ref[g])
            ap_ref[:, cs] = (q * v_ref[3:4, g * POOL_GROUP:(g + 1) * POOL_GROUP]).astype(act_dtype)

        o = _mm(ap_ref[...], wout_ref[...])
        o_ref[...] = o
        ro = lax.rsqrt(_mean(o * o) + EPS)
        x1_ref[...] = xv + (1.0 + mod_ref[2:3, :]) * ((o * ro) * g_ref[1:2, :])

    row = lambda width: pl.BlockSpec((tm, width), lambda i: (i, 0))
    return pl.pallas_call(
        body, name=f"mixer_fwd_{l}", grid=(nt,),
        out_shape=[jax.ShapeDtypeStruct((t, d), F32), jax.ShapeDtypeStruct((t, d), act_dtype),
                   jax.ShapeDtypeStruct((t, d_in), F32), jax.ShapeDtypeStruct((t, dc), F32),
                   jax.ShapeDtypeStruct((t, d_mix), act_dtype), jax.ShapeDtypeStruct((t, d), F32)],
        in_specs=[row(d),
                  pl.BlockSpec((None, None, 8, d), lambda i: (l, i // tps, 0, 0)),
                  _const_spec((None, 8, d), (l, 0, 0)),
                  _const_spec((None, d, d_in), (l, 0, 0)),
                  _const_spec((None, 32, dc), (l, 0, 0)),
                  _const_spec((None, 8, dc), (l, 0, 0)),
                  _const_spec((None, len(POOL_WINDOWS), POOL_GROUP, POOL_GROUP), (l, 0, 0, 0)),
                  _const_spec((None, d_mix, d), (l, 0, 0))],
        out_specs=[row(d), row(d), row(d_in), row(dc), row(d_mix), row(d)],
        scratch_shapes=[pltpu.VMEM((tm + h, d_mix), F32)],
        compiler_params=_params(1),
    )(x, mod, gains, w_in, conv_w, vec, pool_w, w_out)


def _carried_refs(refs, n_in, n_out, n_scratch, n_pay):
    ins = refs[:n_in]
    pay = refs[n_in:n_in + n_pay]
    o0 = n_in + 2 * n_pay
    outs = refs[o0:o0 + n_out]
    recvs = refs[o0 + n_out:o0 + n_out + n_pay]
    s0 = o0 + n_out + n_pay
    return ins, outs, refs[s0:s0 + n_scratch], pay, recvs, refs[s0 + n_scratch:]


def _carrier_call(body, name, grid, in_specs, out_specs, out_shape, scratch, operands, payload, recvs):
    n_in, n_out, n_pay = len(in_specs), len(out_specs), len(payload)
    any_spec = pl.BlockSpec(memory_space=pl.ANY)
    res = pl.pallas_call(
        body, name=name, grid=grid,
        out_shape=list(out_shape) + [jax.ShapeDtypeStruct(r.shape, r.dtype) for r in recvs],
        in_specs=list(in_specs) + [any_spec] * (2 * n_pay),
        out_specs=list(out_specs) + [any_spec] * n_pay,
        input_output_aliases={n_in + n_pay + k: n_out + k for k in range(n_pay)},
        scratch_shapes=list(scratch) + (_exchange_scratch(n_pay) if n_pay else []),
        compiler_params=_params(len(grid)),
    )(*operands, *payload, *recvs)
    return res[:n_out], res[n_out:]


def _mixer_bwd(l, dx1, x, mod, gains, u, a1, o, w_in, conv_w, vec, pool_w, w_out, *, seq, tm, act_dtype,
               payload=(), recvs=(), pay_layer=0):
    t, d = x.shape
    d_in = w_in.shape[-1]
    dc = conv_w.shape[-1]
    d_mix = w_out.shape[-2]
    n_taps = 31
    nt, tps = t // tm, seq // tm
    h = CONV_HALO
    n_groups = len(POOL_WINDOWS)
    n_pay = len(payload)

    def body(*refs):
        ins, outs, scr, pay_refs, recv_refs, sems = _carried_refs(refs, 13, 8, 2, n_pay)
        dx1_ref, x_ref, mod_ref, g_ref, u_ref, uh_ref, a1_ref, o_ref, win_ref, cw_ref, v_ref, pw_ref, wout_ref = ins
        dx_ref, du_ref, do_ref, dmod_ref, gd_ref, gv_ref, dcw_ref, dpw_ref = outs
        ext_ref, fext_ref = scr
        i = pl.program_id(0)
        j = nt - 1 - i
        first_in_seq = (j % tps) == 0
        last_in_seq = (j % tps) == tps - 1

        if n_pay:
            @pl.when(i == 0)
            def _():
                _exchange(pay_refs, recv_refs, pay_layer, *sems, start=True)

        @pl.when(i == 0)
        def _():
            gd_ref[...] = jnp.zeros_like(gd_ref)
            gv_ref[...] = jnp.zeros_like(gv_ref)
            dcw_ref[...] = jnp.zeros_like(dcw_ref)
            dpw_ref[...] = jnp.zeros_like(dpw_ref)

        @pl.when(last_in_seq)
        def _():
            dmod_ref[...] = jnp.zeros_like(dmod_ref)
            fext_ref[tm:tm + h, :] = jnp.zeros((h, fext_ref.shape[1]), F32)

        @pl.when(jnp.logical_not(last_in_seq))
        def _():
            fext_ref[tm:tm + h, :] = fext_ref[0:h, :]

        xv = x_ref[...]
        dx1v = dx1_ref[...]
        pre_g, post_g = g_ref[0:1, :], g_ref[1:2, :]

        ov = o_ref[...]
        ro = lax.rsqrt(_mean(ov * ov) + EPS)
        yo = ov * ro
        dmod_ref[2:3, :] += _colsum(dx1v * (yo * post_g))
        dn = dx1v * (1.0 + mod_ref[2:3, :])
        gd_ref[1:2, :] += _colsum(dn * yo)
        dyo = dn * post_g
        do = ro * (dyo - yo * _mean(dyo * yo))
        dob = do.astype(act_dtype)
        do_ref[...] = dob
        dap = _mm_nt(dob, wout_ref[...])

        uv = u_ref[...]
        val, gate = uv[:, :dc], uv[:, dc:2 * dc]
        sg = _sigmoid(gate)
        ext_ref[h:h + tm, 0:dc] = val * sg
        ext_ref[h:h + tm, dc:] = uv[:, 2 * dc:]
        keep = jnp.where(first_in_seq, 0.0, 1.0).astype(F32)
        uh = uh_ref[...]
        ext_ref[0:h, 0:dc] = uh[:, :dc] * _sigmoid(uh[:, dc:2 * dc]) * keep
        ext_ref[0:h, dc:] = uh[:, 2 * dc:] * keep

        a1v = a1_ref[...]
        mu = _mean(a1v)
        xc = a1v - mu
        rstd = lax.rsqrt(_mean(xc * xc) + EPS)
        xh = xc * rstd
        ln_g = v_ref[1:2, :]
        a2 = xh * ln_g + v_ref[2:3, :]
        s2 = _sigmoid(a2)
        da2 = dap[:, :dc] * (s2 * (1.0 + a2 * (1.0 - s2)))
        gv_ref[1:2, :] += _colsum(da2 * xh)
        gv_ref[2:3, :] += _colsum(da2)
        dxh = da2 * ln_g
        da1 = rstd * (dxh - _mean(dxh) - xh * _mean(dxh * xh))
        gv_ref[0:1, :] += _colsum(da1)
        fext_ref[0:tm, 0:dc] = da1

        for k in range(n_taps):
            row = h - (n_taps - 1) + k
            dcw_ref[k:k + 1, :] += _colsum(fext_ref[0:tm, 0:dc] * ext_ref[row:row + tm, 0:dc])

        def store(r0, rc, cs, acc):
            sgc = _sigmoid(u_ref[r0:r0 + rc, dc + cs.start:dc + cs.stop])
            vc = u_ref[r0:r0 + rc, cs]
            du_ref[r0:r0 + rc, cs] = (acc * sgc).astype(act_dtype)
            du_ref[r0:r0 + rc, dc + cs.start:dc + cs.stop] = (acc * vc * sgc * (1.0 - sgc)).astype(act_dtype)

        _conv_taps(fext_ref, 0, dc, tm, [(n_taps - 1) - k for k in range(n_taps)],
                   lambda k, cs: cw_ref[k:k + 1, cs],
                   lambda cs, rc: jnp.zeros((rc, LANES), F32), store)

        pos = (j % tps) * tm + lax.broadcasted_iota(jnp.int32, (tm, 1), 0)
        for g, w in enumerate(POOL_WINDOWS):
            cs = slice(dc + g * POOL_GROUP, dc + (g + 1) * POOL_GROUP)
            gs = slice(g * POOL_GROUP, (g + 1) * POOL_GROUP)
            s = ext_ref[h:h + tm, cs]
            for jj in range(1, w):
                s = s + ext_ref[h - jj:h - jj + tm, cs]
            cnt = jnp.minimum(pos + 1, w).astype(F32)
            dv = (s / cnt - ext_ref[h:h + tm, cs]).astype(MXU_DTYPE)
            q = _mm(dv, pw_ref[g])
            dp = dap[:, cs]
            gv_ref[3:4, gs] += _colsum(dp * q)
            dq = (dp * v_ref[3:4, gs]).astype(MXU_DTYPE)
            dpw_ref[g] += _mm_tn(dv, dq)
            dd = _mm_nt(dq, pw_ref[g])
            fext_ref[0:tm, cs] = dd / cnt
            dhp = fext_ref[0:tm, cs]
            for jj in range(1, w):
                dhp = dhp + fext_ref[jj:jj + tm, cs]
            du_ref[:, dc + cs.start:dc + cs.stop] = (dhp - dd).astype(act_dtype)

        dh = _mm_nt(du_ref[...], win_ref[...])

        r = lax.rsqrt(_mean(xv * xv) + EPS)
        xn = xv * r
        dmod_ref[0:1, :] += _colsum(dh)
        dmod_ref[1:2, :] += _colsum(dh * (xn * pre_g))
        dy = dh * (1.0 + mod_ref[1:2, :])
        gd_ref[0:1, :] += _colsum(dy * xn)
        dxn = dy * pre_g
        dx_ref[...] = dx1v + r * (dxn - xn * _mean(dxn * xn))

        if n_pay:
            @pl.when(i == nt - 1)
            def _():
                _exchange(pay_refs, recv_refs, pay_layer, *sems, start=False)

    rev = lambda width: pl.BlockSpec((tm, width), lambda i: (nt - 1 - i, 0))
    blocks_per_tile = tm // h
    return _carrier_call(
        body, f"mixer_bwd_{l}", (nt,),
        out_shape=[jax.ShapeDtypeStruct((t, d), F32), jax.ShapeDtypeStruct((t, d_in), act_dtype),
                   jax.ShapeDtypeStruct((t, d), act_dtype),
                   jax.ShapeDtypeStruct((t // seq, 8, d), F32), jax.ShapeDtypeStruct((8, d), F32),
                   jax.ShapeDtypeStruct((8, dc), F32), jax.ShapeDtypeStruct((32, dc), F32),
                   jax.ShapeDtypeStruct((n_groups, POOL_GROUP, POOL_GROUP), F32)],
        in_specs=[rev(d), rev(d),
                  pl.BlockSpec((None, None, 8, d), lambda i: (l, (nt - 1 - i) // tps, 0, 0)),
                  _const_spec((None, 8, d), (l, 0, 0)),
                  rev(d_in),
                  pl.BlockSpec((h, d_in), lambda i: (jnp.maximum((nt - 1 - i) * blocks_per_tile - 1, 0), 0)),
                  rev(dc), rev(d),
                  _const_spec((None, d, d_in), (l, 0, 0)),
                  _const_spec((None, 32, dc), (l, 0, 0)),
                  _const_spec((None, 8, dc), (l, 0, 0)),
                  _const_spec((None, n_groups, POOL_GROUP, POOL_GROUP), (l, 0, 0, 0)),
                  _const_spec((None, d_mix, d), (l, 0, 0))],
        out_specs=[rev(d), rev(d_in), rev(d),
                   pl.BlockSpec((None, 8, d), lambda i: ((nt - 1 - i) // tps, 0, 0)),
                   _const_spec((8, d), (0, 0)), _const_spec((8, dc), (0, 0)), _const_spec((32, dc), (0, 0)),
                   _const_spec((n_groups, POOL_GROUP, POOL_GROUP), (0, 0, 0))],
        scratch=[pltpu.VMEM((tm + h, d_mix), F32), pltpu.VMEM((tm + h, d_mix), F32)],
        operands=(dx1, x, mod, gains, u, u, a1, o, w_in, conv_w, vec, pool_w, w_out),
        payload=payload, recvs=recvs)


def _ffn_fwd(l, x, mod, gains, up, fcw, down, *, seq, tm, act_dtype):
    t, d = x.shape
    n_chunks, _, fc = up.shape[1:]
    half = n_chunks // 2
    nt, tps = t // tm, seq // tm
    h = FFN_HALO

    def body(x_ref, mod_ref, g_ref, up_ref, fcw_ref, down_ref,
             x2_ref, h2_ref, u2_ref, hid_ref, o2_ref, ext_ref):
        i = pl.program_id(0)
        first = (i % tps) == 0
        xv = x_ref[...]
        r = lax.rsqrt(_mean(xv * xv) + EPS)
        hv = (xv * r) * g_ref[0:1, :] * (1.0 + mod_ref[4:5, :]) + mod_ref[3:4, :]
        hb = hv.astype(act_dtype)
        h2_ref[...] = hb

        @pl.when(first)
        def _():
            for n in range(n_chunks):
                ext_ref[n, 0:h, :] = jnp.zeros((h, fc), F32)

        @pl.when(jnp.logical_not(first))
        def _():
            for n in range(n_chunks):
                ext_ref[n, 0:h, :] = ext_ref[n, tm:tm + h, :]

        for n in range(n_chunks):
            un = _mm(hb, up_ref[n])
            ext_ref[n, h:h + tm, :] = un
            u2_ref[n] = un.astype(act_dtype)

        def conv(n):
            return (fcw_ref[n, 3:4, :] + fcw_ref[n, 0:1, :] * ext_ref[n, h - 2:h - 2 + tm, :]
                    + fcw_ref[n, 1:2, :] * ext_ref[n, h - 1:h - 1 + tm, :]
                    + fcw_ref[n, 2:3, :] * ext_ref[n, h:h + tm, :])

        o2 = jnp.zeros((tm, d), F32)
        for n in range(half):
            gt = conv(n + half)
            hid = ((gt * _sigmoid(gt)) * conv(n)).astype(act_dtype)
            hid_ref[n] = hid
            o2 = o2 + _mm(hid, down_ref[n])
        o2_ref[...] = o2
        ro = lax.rsqrt(_mean(o2 * o2) + EPS)
        x2_ref[...] = xv + (1.0 + mod_ref[5:6, :]) * ((o2 * ro) * g_ref[1:2, :])

    row = lambda width: pl.BlockSpec((tm, width), lambda i: (i, 0))
    return pl.pallas_call(
        body, name=f"ffn_fwd_{l}", grid=(nt,),
        out_shape=[jax.ShapeDtypeStruct((t, d), F32), jax.ShapeDtypeStruct((t, d), act_dtype),
                   jax.ShapeDtypeStruct((n_chunks, t, fc), act_dtype),
                   jax.ShapeDtypeStruct((half, t, fc), act_dtype), jax.ShapeDtypeStruct((t, d), F32)],
        in_specs=[row(d),
                  pl.BlockSpec((None, None, 8, d), lambda i: (l, i // tps, 0, 0)),
                  _const_spec((None, 8, d), (l, 0, 0)),
                  _const_spec((None, n_chunks, d, fc), (l, 0, 0, 0)),
                  _const_spec((None, n_chunks, 8, fc), (l, 0, 0, 0)),
                  _const_spec((None, half, fc, d), (l, 0, 0, 0))],
        out_specs=[row(d), row(d),
                   pl.BlockSpec((n_chunks, tm, fc), lambda i: (0, i, 0)),
                   pl.BlockSpec((half, tm, fc), lambda i: (0, i, 0)), row(d)],
        scratch_shapes=[pltpu.VMEM((n_chunks, tm + h, fc), F32)],
        compiler_params=_params(1),
    )(x, mod, gains, up, fcw, down)


def _ffn_bwd(l, dx2, x, mod, gains, u2, o2, up, fcw, down, *, seq, tm, act_dtype,
             payload=(), recvs=(), pay_layer=0):
    t, d = x.shape
    n_chunks, _, fc = up.shape[1:]
    half = n_chunks // 2
    nt, tps = t // tm, seq // tm
    h = FFN_HALO
    hi = FFN_HALO_IN
    n_pay = len(payload)

    def body(*refs):
        ins, outs, scr, pay_refs, recv_refs, sems = _carried_refs(refs, 10, 6, 2, n_pay)
        dx2_ref, x_ref, mod_ref, g_ref, u2_ref, u2h_ref, o2_ref, up_ref, fcw_ref, down_ref = ins
        dx1_ref, du2_ref, do2_ref, dmod_ref, gd_ref, dfcw_ref = outs
        ext_ref, fext_ref = scr
        i = pl.program_id(0)
        j = nt - 1 - i
        first_in_seq = (j % tps) == 0
        last_in_seq = (j % tps) == tps - 1

        if n_pay:
            @pl.when(i == 0)
            def _():
                _exchange(pay_refs, recv_refs, pay_layer, *sems, start=True)

        @pl.when(i == 0)
        def _():
            gd_ref[...] = jnp.zeros_like(gd_ref)
            dfcw_ref[...] = jnp.zeros_like(dfcw_ref)

        @pl.when(last_in_seq)
        def _():
            dmod_ref[...] = jnp.zeros_like(dmod_ref)
            for n in range(n_chunks):
                fext_ref[n, tm:tm + h, :] = jnp.zeros((h, fc), F32)

        @pl.when(jnp.logical_not(last_in_seq))
        def _():
            for n in range(n_chunks):
                fext_ref[n, tm:tm + h, :] = fext_ref[n, 0:h, :]

        xv = x_ref[...]
        dx2v = dx2_ref[...]
        pre_g, post_g = g_ref[0:1, :], g_ref[1:2, :]

        ov = o2_ref[...]
        ro = lax.rsqrt(_mean(ov * ov) + EPS)
        yo = ov * ro
        dmod_ref[2:3, :] += _colsum(dx2v * (yo * post_g))
        dn = dx2v * (1.0 + mod_ref[5:6, :])
        gd_ref[1:2, :] += _colsum(dn * yo)
        dyo = dn * post_g
        do = ro * (dyo - yo * _mean(dyo * yo))
        dob = do.astype(act_dtype)
        do2_ref[...] = dob

        keep = jnp.where(first_in_seq, 0.0, 1.0).astype(F32)
        for n in range(n_chunks):
            ext_ref[n, hi:hi + tm, :] = u2_ref[n].astype(F32)
            ext_ref[n, 0:hi, :] = u2h_ref[n].astype(F32) * keep

        def conv(n):
            return (fcw_ref[n, 3:4, :] + fcw_ref[n, 0:1, :] * ext_ref[n, hi - 2:hi - 2 + tm, :]
                    + fcw_ref[n, 1:2, :] * ext_ref[n, hi - 1:hi - 1 + tm, :]
                    + fcw_ref[n, 2:3, :] * ext_ref[n, hi:hi + tm, :])

        for n in range(half):
            v = conv(n)
            gt = conv(n + half)
            sg = _sigmoid(gt)
            dhid = _mm_nt(dob, down_ref[n])
            fext_ref[n, 0:tm, :] = dhid * (gt * sg)
            fext_ref[n + half, 0:tm, :] = dhid * v * (sg * (1.0 + gt * (1.0 - sg)))

        dh = jnp.zeros((tm, d), F32)
        for n in range(n_chunks):
            du3 = fext_ref[n, 0:tm, :]
            dfcw_ref[n, 3:4, :] += _colsum(du3)
            for k in range(3):
                dfcw_ref[n, k:k + 1, :] += _colsum(du3 * ext_ref[n, hi - 2 + k:hi - 2 + k + tm, :])
            du2 = (fcw_ref[n, 0:1, :] * fext_ref[n, 2:2 + tm, :] + fcw_ref[n, 1:2, :] * fext_ref[n, 1:1 + tm, :]
                   + fcw_ref[n, 2:3, :] * du3).astype(act_dtype)
            du2_ref[n] = du2
            dh = dh + _mm_nt(du2, up_ref[n])

        r = lax.rsqrt(_mean(xv * xv) + EPS)
        xn = xv * r
        dmod_ref[0:1, :] += _colsum(dh)
        dmod_ref[1:2, :] += _colsum(dh * (xn * pre_g))
        dy = dh * (1.0 + mod_ref[4:5, :])
        gd_ref[0:1, :] += _colsum(dy * xn)
        dxn = dy * pre_g
        dx1_ref[...] = dx2v + r * (dxn - xn * _mean(dxn * xn))

        if n_pay:
            @pl.when(i == nt - 1)
            def _():
                _exchange(pay_refs, recv_refs, pay_layer, *sems, start=False)

    rev = lambda width: pl.BlockSpec((tm, width), lambda i: (nt - 1 - i, 0))
    blocks_per_tile = tm // hi
    return _carrier_call(
        body, f"ffn_bwd_{l}", (nt,),
        out_shape=[jax.ShapeDtypeStruct((t, d), F32), jax.ShapeDtypeStruct((n_chunks, t, fc), act_dtype),
                   jax.ShapeDtypeStruct((t, d), act_dtype),
                   jax.ShapeDtypeStruct((t // seq, 8, d), F32), jax.ShapeDtypeStruct((8, d), F32),
                   jax.ShapeDtypeStruct((n_chunks, 8, fc), F32)],
        in_specs=[rev(d), rev(d),
                  pl.BlockSpec((None, None, 8, d), lambda i: (l, (nt - 1 - i) // tps, 0, 0)),
                  _const_spec((None, 8, d), (l, 0, 0)),
                  pl.BlockSpec((n_chunks, tm, fc), lambda i: (0, nt - 1 - i, 0)),
                  pl.BlockSpec((n_chunks, hi, fc),
                               lambda i: (0, jnp.maximum((nt - 1 - i) * blocks_per_tile - 1, 0), 0)),
                  rev(d),
                  _const_spec((None, n_chunks, d, fc), (l, 0, 0, 0)),
                  _const_spec((None, n_chunks, 8, fc), (l, 0, 0, 0)),
                  _const_spec((None, half, fc, d), (l, 0, 0, 0))],
        out_specs=[rev(d), pl.BlockSpec((n_chunks, tm, fc), lambda i: (0, nt - 1 - i, 0)), rev(d),
                   pl.BlockSpec((None, 8, d), lambda i: ((nt - 1 - i) // tps, 0, 0)),
                   _const_spec((8, d), (0, 0)), _const_spec((n_chunks, 8, fc), (0, 0, 0))],
        scratch=[pltpu.VMEM((n_chunks, tm + hi, fc), F32), pltpu.VMEM((n_chunks, tm + h, fc), F32)],
        operands=(dx2, x, mod, gains, u2, u2, o2, up, fcw, down),
        payload=payload, recvs=recvs)


def _pad_rows(a, rows):
    pad = [(0, 0)] * a.ndim
    pad[-2] = (0, rows - a.shape[-2])
    return jnp.pad(a, pad)


def kernel(x, c, ada_w, ada_b, pre_mix_g, post_mix_g, w_in, conv_w, conv_b, conv_ln_g, conv_ln_b, pool_w, pool_scale, w_out, pre_ffn_g, post_ffn_g, ffn_up, ffn_conv_w, ffn_conv_b, ffn_down, loss_target, m_ada_w, m_ada_b, m_pre_mix_g, m_post_mix_g, m_w_in, m_conv_w, m_conv_b, m_conv_ln_g, m_conv_ln_b, m_pool_w, m_pool_scale, m_w_out, m_pre_ffn_g, m_post_ffn_g, m_ffn_up, m_ffn_conv_w, m_ffn_conv_b, m_ffn_down, v_ada_w, v_ada_b, v_pre_mix_g, v_post_mix_g, v_w_in, v_conv_w, v_conv_b, v_conv_ln_g, v_conv_ln_b, v_pool_w, v_pool_scale, v_w_out, v_pre_ffn_g, v_post_ffn_g, v_ffn_up, v_ffn_conv_w, v_ffn_conv_b, v_ffn_down):
    bl, seq, d = x.shape
    n_layers = ada_w.shape[0]
    t = bl * seq
    dc = conv_b.shape[1]
    d_in = w_in.shape[2] * N_DEV
    d_mix = w_out.shape[1] * N_DEV
    n_taps = conv_w.shape[1]
    fc = ffn_up.shape[2]
    n_chunks, half = N_DEV, N_DEV // 2
    ada_cols = ada_w.shape[2]
    n_mod = ada_cols * N_DEV // d
    assert pool_scale.shape[1] == dc and n_taps == 31 and n_mod == 6 and ffn_conv_w.shape[1] == 3
    assert pool_w.shape[1:] == (len(POOL_WINDOWS), POOL_GROUP, POOL_GROUP)
    tm = 256 if seq % 256 == 0 else 64
    tk = 512 if t % 512 == 0 else tm
    act = MXU_DTYPE

    ax = lax.axis_index
    me = 4 * ax("x") + 2 * ax("y") + ax("c")

    (c_all,) = _all_gather([c], "gather_c")
    c_all = c_all.reshape(N_DEV * bl, d)
    ada_b_cols = lax.dynamic_slice_in_dim(ada_b, me * ada_cols, ada_cols, axis=1)
    mod_cols = _ada_fwd(c_all, ada_w, ada_b_cols)

    g_mod, g_w_in, g_w_out, g_up, g_down, g_conv_w, g_fcw = _all_gather(
        [mod_cols, w_in.astype(act), w_out.astype(act), ffn_up.astype(act), ffn_down.astype(act),
         conv_w, ffn_conv_w], "gather_w")
    mod_all = g_mod.transpose(1, 2, 0, 3).reshape(n_layers, N_DEV * bl, n_mod, d)
    mod = _pad_rows(lax.dynamic_slice_in_dim(mod_all, me * bl, bl, axis=1), 8)
    w_in_f = g_w_in.transpose(1, 2, 0, 3).reshape(n_layers, d, d_in)
    w_out_f = g_w_out.transpose(1, 0, 2, 3).reshape(n_layers, d_mix, d)
    up_f = g_up.transpose(1, 0, 2, 3)
    down_f = g_down.transpose(1, 0, 2, 3).reshape(n_layers, half, fc, d)
    conv_w_f = _pad_rows(g_conv_w.transpose(1, 2, 0, 3).reshape(n_layers, n_taps, dc), 32)
    fcw_f = _pad_rows(jnp.concatenate(
        [g_fcw.transpose(1, 0, 2, 3), ffn_conv_b.reshape(n_layers, n_chunks, 1, fc)], axis=2), 8)
    gains_mix = _pad_rows(jnp.stack([pre_mix_g, post_mix_g], axis=1), 8)
    gains_ffn = _pad_rows(jnp.stack([pre_ffn_g, post_ffn_g], axis=1), 8)
    vec = _pad_rows(jnp.stack([conv_b, conv_ln_g, conv_ln_b, pool_scale], axis=1), 8)
    pool_w_b = pool_w.astype(act)

    kw = dict(seq=seq, tm=tm, act_dtype=act)
    xs = x.reshape(t, d)
    saved = []
    for l in range(n_layers):
        x1, h1, u, a1, ap, o = _mixer_fwd(l, xs, mod, gains_mix, w_in_f, conv_w_f, vec, pool_w_b, w_out_f, **kw)
        x2, h2, u2, hid, o2 = _ffn_fwd(l, x1, mod, gains_ffn, up_f, fcw_f, down_f, **kw)
        saved.append((xs, h1, u, a1, ap, o, x1, h2, u2, hid, o2))
        xs = x2

    dx, loss_part = _loss_grad(xs, loss_target.reshape(t, d), tm)

    def landing(shard, dtype):
        return lax.empty((N_DEV, n_layers) + shard.shape[1:], dtype)

    r_up, r_down = landing(ffn_up, act), landing(ffn_down, act)
    r_w_in, r_w_out = landing(w_in, act), landing(w_out, act)
    r_conv_w, r_fcw = landing(conv_w, F32), landing(ffn_conv_w, F32)
    dmods, smalls = [], []
    pending = ()
    for l in reversed(range(n_layers)):
        xin, h1, u, a1, ap, o, x1, h2, u2, hid, o2 = saved[l]
        (dx1, du2, do2, dmod_b, gd_b, dfcw), got = _ffn_bwd(
            l, dx, x1, mod, gains_ffn, u2, o2, up_f, fcw_f, down_f, **kw,
            payload=pending, recvs=(r_up, r_w_in, r_w_out, r_conv_w, r_fcw) if pending else (), pay_layer=l + 1)
        if pending:
            r_up, r_w_in, r_w_out, r_conv_w, r_fcw = got
        p_up = _matmul_tn(f"dw_up_{l}", h2[None], du2, tk, act)
        p_down = _matmul_tn(f"dw_down_{l}", hid, do2[None], tk, act).reshape(N_DEV, fc // 2, d)
        (dx, du, do, dmod_a, gd_a, gv, dcw, dpw), (r_down,) = _mixer_bwd(
            l, dx1, xin, mod, gains_mix, u, a1, o, w_in_f, conv_w_f, vec, pool_w_b, w_out_f, **kw,
            payload=(p_down,), recvs=(r_down,), pay_layer=l)
        dw_in = _matmul_tn(f"dw_in_{l}", h1[None], du[None], tk, act)[0]
        p_w_in = dw_in.reshape(d, N_DEV, d_in // N_DEV).transpose(1, 0, 2)
        p_w_out = _matmul_tn(f"dw_out_{l}", ap[None], do[None], tk, act)[0].reshape(N_DEV, d_mix // N_DEV, d)
        p_conv_w = dcw[:n_taps].reshape(n_taps, N_DEV, dc // N_DEV).transpose(1, 0, 2)
        pending = (p_up, p_w_in, p_w_out, p_conv_w, dfcw[:, :3, :])
        dmods.append(jnp.concatenate([dmod_a[:, 0:3], dmod_b[:, 0:3]], axis=1).reshape(bl, n_mod * d))
        smalls.append(jnp.concatenate(
            [gd_a[0], gd_a[1], gv[0], gv[1], gv[2], gv[3], gd_b[0], gd_b[1], dfcw[:, 3, :].reshape(-1),
             dpw.reshape(-1)]))
    dmods.reverse()
    smalls.reverse()
    r_up, r_w_in, r_w_out, r_conv_w, r_fcw = _all_to_all(
        pending, (r_up, r_w_in, r_w_out, r_conv_w, r_fcw), 0, "exchange_tail")

    dmod_loc = jnp.stack(dmods)
    small_loc = jnp.stack(smalls)
    n_small = small_loc.shape[1]
    g_dmod, g_small, g_loss = _all_gather([dmod_loc, small_loc.reshape(-1, LANES), loss_part], "gather_g")

    loss = _sum_parts(g_loss)[0, 0]

    def flat2(a):
        return a.reshape(-1, a.shape[-1])

    def update(name, parts, w, m, v):
        outs = _adam_reduce(name, parts.reshape(parts.shape[0], -1, w.shape[-1]), flat2(w), flat2(m), flat2(v))
        return [o_.reshape(w.shape) for o_ in outs]

    res = {}
    res["w_in"] = update("adam_w_in", r_w_in, w_in, m_w_in, v_w_in)
    res["w_out"] = update("adam_w_out", r_w_out, w_out, m_w_out, v_w_out)
    res["ffn_up"] = update("adam_ffn_up", r_up, ffn_up, m_ffn_up, v_ffn_up)
    res["ffn_down"] = update("adam_ffn_down", r_down, ffn_down, m_ffn_down, v_ffn_down)
    res["conv_w"] = update("adam_conv_w", r_conv_w, conv_w, m_conv_w, v_conv_w)
    res["ffn_conv_w"] = update("adam_ffn_conv_w", r_fcw, ffn_conv_w, m_ffn_conv_w, v_ffn_conv_w)

    dmod_all = g_dmod.transpose(1, 0, 2, 3).reshape(n_layers, N_DEV * bl, n_mod * d)
    dmod_cols = lax.dynamic_slice_in_dim(dmod_all, me * ada_cols, ada_cols, axis=2)
    res["ada_w"] = list(_ada_bwd(c_all.T, dmod_cols, ada_w, m_ada_w, v_ada_w))
    res["ada_b"] = update("adam_ada_b", dmod_all.transpose(1, 0, 2), ada_b, m_ada_b, v_ada_b)

    small_names = ["pre_mix_g", "post_mix_g", "conv_b", "conv_ln_g", "conv_ln_b", "pool_scale", "pre_ffn_g",
                   "post_ffn_g", "ffn_conv_b", "pool_w"]
    small_w = [pre_mix_g, post_mix_g, conv_b, conv_ln_g, conv_ln_b, pool_scale, pre_ffn_g, post_ffn_g,
               ffn_conv_b, pool_w]
    small_m = [m_pre_mix_g, m_post_mix_g, m_conv_b, m_conv_ln_g, m_conv_ln_b, m_pool_scale, m_pre_ffn_g,
               m_post_ffn_g, m_ffn_conv_b, m_pool_w]
    small_v = [v_pre_mix_g, v_post_mix_g, v_conv_b, v_conv_ln_g, v_conv_ln_b, v_pool_scale, v_pre_ffn_g,
               v_post_ffn_g, v_ffn_conv_b, v_pool_w]

    def pack(arrs):
        return jnp.concatenate([a.reshape(n_layers, -1) for a in arrs], axis=1).reshape(-1, LANES)

    outs = _adam_reduce("adam_small", g_small, pack(small_w), pack(small_m), pack(small_v))
    outs = [o_.reshape(n_layers, n_small) for o_ in outs]
    off = 0
    for name, w in zip(small_names, small_w):
        size = w[0].size
        res[name] = [o_[:, off:off + size].reshape(w.shape) for o_ in outs]
        off += size

    order = ["ada_w", "ada_b", "pre_mix_g", "post_mix_g", "w_in", "conv_w", "conv_b", "conv_ln_g", "conv_ln_b",
             "pool_w", "pool_scale", "w_out", "pre_ffn_g", "post_ffn_g", "ffn_up", "ffn_conv_w", "ffn_conv_b",
             "ffn_down"]
    return (loss, dx.reshape(bl, seq, d), *[res[n][0] for n in order], *[res[n][1] for n in order],
            *[res[n][2] for n in order], *[res[n][3] for n in order])
```

```python
import jax
import jax.numpy as jnp
from jax import lax
from jax.experimental import pallas as pl
from jax.experimental.pallas import tpu as pltpu

N_DEV = 8
EPS = 1e-6
POOL_WINDOWS = (2, 4, 8, 16)
POOL_GROUP = 128
CONV_HALO = 32
FFN_HALO = 8
LANES = 128
ROW_CHUNK = 64
MXU_DTYPE = jnp.bfloat16
VMEM_LIMIT = 60 * 1024 * 1024

ADAM_LR = 0.001
ADAM_B1 = 0.9
ADAM_B2 = 0.999
ADAM_EPS = 1e-08
ADAM_WD = 0.01
ADAM_STEP = 10

MESH = pl.DeviceIdType.MESH
F32 = jnp.float32


def _mm(a, b):
    return jnp.dot(a.astype(MXU_DTYPE), b.astype(MXU_DTYPE), preferred_element_type=F32)


def _mm_nt(a, b):
    return lax.dot_general(a.astype(MXU_DTYPE), b.astype(MXU_DTYPE), (((1,), (1,)), ((), ())),
                           preferred_element_type=F32)


def _mm_tn(a, b):
    return lax.dot_general(a.astype(MXU_DTYPE), b.astype(MXU_DTYPE), (((0,), (0,)), ((), ())),
                           preferred_element_type=F32)


def _mean(v):
    return jnp.mean(v, axis=-1, keepdims=True)


def _colsum(v):
    return jnp.sum(v, axis=0, keepdims=True)


def _sigmoid(v):
    return jax.nn.sigmoid(v)


def _params(n_grid=1):
    return pltpu.CompilerParams(dimension_semantics=("arbitrary",) * n_grid, vmem_limit_bytes=VMEM_LIMIT)


def _const_spec(shape, index):
    return pl.BlockSpec(shape, lambda *_: index, pipeline_mode=pl.Buffered(1))


def _acc_spec(shape, index):
    return pl.BlockSpec(shape, lambda *_: index)


def _position():
    x, y, c = lax.axis_index("x"), lax.axis_index("y"), lax.axis_index("c")
    return x, y, c


def _gather_phase(ins, outs, send_sems, recv_sems, local_sems, phase):
    n = len(ins)
    x, y, c = _position()
    me, sibling = (x, y, c), (x, y, 1 - c)
    chips = [(1 - x, y), (x, 1 - y), (1 - x, 1 - y)]

    def slot(k, px, py, pc):
        return outs[k].at[4 * px + 2 * py + pc]

    def copy(k, s, block, to, src=None):
        return pltpu.make_async_remote_copy(
            src_ref=slot(k, *block) if src is None else src, dst_ref=slot(k, *block),
            send_sem=send_sems.at[k, s], recv_sem=recv_sems.at[k, s], device_id=to, device_id_type=MESH)

    mine = [pltpu.make_async_copy(ins[k], slot(k, *me), local_sems.at[k]) for k in range(n)]
    first = []
    for k in range(n):
        first.append(copy(k, 0, me, sibling, src=ins[k]))
        first += [copy(k, 1 + j, me, (*chip, c), src=ins[k]) for j, chip in enumerate(chips)]
    passed = [copy(k, 4 + j, (*chip, c), sibling) for j, chip in enumerate(chips) for k in range(n)]
    if phase == 0:
        for cp in mine + first:
            cp.start()
    elif phase == 1:
        for j, chip in enumerate(chips):
            for k in range(n):
                copy(k, 1 + j, (*chip, c), me).wait_recv()
                copy(k, 4 + j, (*chip, c), sibling).start()
    else:
        for k in range(n):
            copy(k, 0, sibling, me).wait_recv()
            for j, chip in enumerate(chips):
                copy(k, 4 + j, (*chip, 1 - c), me).wait_recv()
        for cp in first + passed:
            cp.wait_send()
        for cp in mine:
            cp.wait()


def _gather_scratch(n):
    return [pltpu.SemaphoreType.DMA((n, 7)), pltpu.SemaphoreType.DMA((n, 7)), pltpu.SemaphoreType.DMA((n,))]


def _all_gather(arrs, name):
    n = len(arrs)

    def body(*refs):
        for phase in range(3):
            _gather_phase(refs[:n], refs[n:2 * n], *refs[2 * n:], phase)

    any_spec = pl.BlockSpec(memory_space=pl.ANY)
    return pl.pallas_call(
        body, name=name,
        out_shape=[jax.ShapeDtypeStruct((N_DEV,) + a.shape, a.dtype) for a in arrs],
        in_specs=[any_spec] * n, out_specs=[any_spec] * n,
        scratch_shapes=_gather_scratch(n),
    )(*arrs)


def _exchange(payload, recvs, layer, send_sems, recv_sems, local_sems, start):
    x, y, c = _position()
    me = 4 * x + 2 * y + c
    for k, (src, recv) in enumerate(zip(payload, recvs)):
        local = pltpu.make_async_copy(src.at[me], recv.at[me, layer], local_sems.at[k])
        if start:
            local.start()
        else:
            local.wait()
        for j in range(1, N_DEV):
            px = (1 - x) if (j & 4) else x
            py = (1 - y) if (j & 2) else y
            pc = (1 - c) if (j & 1) else c
            peer = 4 * px + 2 * py + pc
            landing = recv.at[me, layer] if start else recv.at[peer, layer]
            cp = pltpu.make_async_remote_copy(
                src_ref=src.at[peer], dst_ref=landing, send_sem=send_sems.at[k, j - 1],
                recv_sem=recv_sems.at[k, j - 1], device_id=(px, py, pc), device_id_type=MESH)
            if start:
                cp.start()
            else:
                cp.wait()


def _exchange_scratch(n):
    return [pltpu.SemaphoreType.DMA((n, N_DEV - 1)), pltpu.SemaphoreType.DMA((n, N_DEV - 1)),
            pltpu.SemaphoreType.DMA((n,))]


def _all_to_all(payload, recvs, layer, name):
    n = len(payload)

    def body(*refs):
        pay, outs = refs[:n], refs[2 * n:3 * n]
        sems = refs[3 * n:]
        _exchange(pay, outs, layer, *sems, start=True)
        _exchange(pay, outs, layer, *sems, start=False)

    any_spec = pl.BlockSpec(memory_space=pl.ANY)
    return pl.pallas_call(
        body, name=name,
        out_shape=[jax.ShapeDtypeStruct(r.shape, r.dtype) for r in recvs],
        in_specs=[any_spec] * (2 * n), out_specs=[any_spec] * n,
        input_output_aliases={n + k: k for k in range(n)},
        scratch_shapes=_exchange_scratch(n),
    )(*payload, *recvs)


def _carried_refs(refs, n_in, n_out, n_scratch, n_pay, aliased):
    ins = refs[:n_in]
    pay = refs[n_in:n_in + n_pay]
    o0 = n_in + (2 * n_pay if aliased else n_pay)
    outs = refs[o0:o0 + n_out]
    recvs = refs[o0 + n_out:o0 + n_out + n_pay]
    s0 = o0 + n_out + n_pay
    return ins, outs, refs[s0:s0 + n_scratch], pay, recvs, refs[s0 + n_scratch:]


def _carrier_call(body, name, grid, in_specs, out_specs, out_shape, scratch, operands, payload, recvs=None):
    n_in, n_out, n_pay = len(in_specs), len(out_specs), len(payload)
    any_spec = pl.BlockSpec(memory_space=pl.ANY)
    if recvs is None:
        landing = [jax.ShapeDtypeStruct((N_DEV,) + a.shape, a.dtype) for a in payload]
        extra_in, aliases = list(payload), {}
        sems = _gather_scratch(n_pay) if n_pay else []
    else:
        landing = [jax.ShapeDtypeStruct(r.shape, r.dtype) for r in recvs]
        extra_in = list(payload) + list(recvs)
        aliases = {n_in + n_pay + k: n_out + k for k in range(n_pay)}
        sems = _exchange_scratch(n_pay) if n_pay else []
    res = pl.pallas_call(
        body, name=name, grid=grid,
        out_shape=list(out_shape) + landing,
        in_specs=list(in_specs) + [any_spec] * len(extra_in),
        out_specs=list(out_specs) + [any_spec] * n_pay,
        input_output_aliases=aliases,
        scratch_shapes=list(scratch) + sems,
        compiler_params=_params(len(grid)),
    )(*operands, *extra_in)
    return res[:n_out], res[n_out:]


def _adamw(w, g, m, v):
    m = ADAM_B1 * m + (1.0 - ADAM_B1) * g
    v = ADAM_B2 * v + (1.0 - ADAM_B2) * jnp.square(g)
    m_hat = m / (1.0 - ADAM_B1 ** ADAM_STEP)
    v_hat = v / (1.0 - ADAM_B2 ** ADAM_STEP)
    delta = -ADAM_LR * (m_hat / (jnp.sqrt(v_hat) + ADAM_EPS) + ADAM_WD * w)
    return delta, m, v


def _ada_fwd(c_all, ada_w, ada_b_cols):
    n_layers, d, cols = ada_w.shape
    b = c_all.shape[0]

    def body(c_ref, w_ref, b_ref, o_ref):
        cv = c_ref[...]
        act = cv * _sigmoid(cv)
        o_ref[...] = jnp.dot(act, w_ref[...], preferred_element_type=F32,
                             precision=lax.Precision.HIGHEST) + b_ref[...]

    return pl.pallas_call(
        body, name="ada_fwd", grid=(n_layers,),
        out_shape=jax.ShapeDtypeStruct((n_layers, b, cols), F32),
        in_specs=[pl.BlockSpec((b, d), lambda l: (0, 0)),
                  pl.BlockSpec((None, d, cols), lambda l: (l, 0, 0)),
                  pl.BlockSpec((None, 1, cols), lambda l: (l, 0, 0))],
        out_specs=pl.BlockSpec((None, b, cols), lambda l: (l, 0, 0)),
        compiler_params=_params(1),
    )(c_all, ada_w, ada_b_cols.reshape(n_layers, 1, cols))


def _ada_bwd(c_all_t, dmod_cols, w, m, v):
    n_layers, d, cols = w.shape
    b = c_all_t.shape[1]
    td = 256 if d % 256 == 0 else d

    def body(c_ref, dm_ref, w_ref, m_ref, v_ref, g_ref, dl_ref, nm_ref, nv_ref):
        cv = c_ref[...]
        act = cv * _sigmoid(cv)
        g = jnp.dot(act, dm_ref[...], preferred_element_type=F32, precision=lax.Precision.HIGHEST)
        delta, nm, nv = _adamw(w_ref[...], g, m_ref[...], v_ref[...])
        g_ref[...] = g
        dl_ref[...] = delta
        nm_ref[...] = nm
        nv_ref[...] = nv

    blk = pl.BlockSpec((None, td, cols), lambda l, i: (l, i, 0))
    shp = jax.ShapeDtypeStruct(w.shape, F32)
    return pl.pallas_call(
        body, name="ada_bwd", grid=(n_layers, d // td),
        out_shape=[shp] * 4,
        in_specs=[pl.BlockSpec((td, b), lambda l, i: (i, 0)),
                  pl.BlockSpec((None, b, cols), lambda l, i: (l, 0, 0)), blk, blk, blk],
        out_specs=[blk] * 4,
        compiler_params=_params(2),
    )(c_all_t, dmod_cols, w, m, v)


def _row_tile(rows, cols, budget=128 * 1024, step=8):
    best = None
    for t in range(step, rows + 1, step):
        if rows % t == 0 and t * cols <= budget:
            best = t
    return best if best is not None else rows


def _adam_reduce(name, parts, w, m, v):
    p, rows, cols = parts.shape
    tr = _row_tile(rows, cols, budget=(256 * 1024) // max(1, p // 4), step=8 if parts.dtype == F32 else 16)

    def body(p_ref, w_ref, m_ref, v_ref, g_ref, dl_ref, nm_ref, nv_ref):
        g = p_ref[0].astype(F32)
        for k in range(1, p):
            g = g + p_ref[k].astype(F32)
        delta, nm, nv = _adamw(w_ref[...], g, m_ref[...], v_ref[...])
        g_ref[...] = g
        dl_ref[...] = delta
        nm_ref[...] = nm
        nv_ref[...] = nv

    blk = pl.BlockSpec((tr, cols), lambda i: (i, 0))
    shp = jax.ShapeDtypeStruct((rows, cols), F32)
    return pl.pallas_call(
        body, name=name, grid=(rows // tr,),
        out_shape=[shp] * 4,
        in_specs=[pl.BlockSpec((p, tr, cols), lambda i: (0, i, 0)), blk, blk, blk],
        out_specs=[blk] * 4,
        compiler_params=_params(1),
    )(parts, w, m, v)


def _sum_parts(parts):
    p = parts.shape[0]

    def body(p_ref, o_ref):
        acc = p_ref[0]
        for k in range(1, p):
            acc = acc + p_ref[k]
        o_ref[...] = acc

    return pl.pallas_call(body, name="loss_sum", out_shape=jax.ShapeDtypeStruct(parts.shape[1:], F32))(parts)


def _loss_grad(y, target, tm):
    t, d = y.shape

    def body(y_ref, t_ref, dy_ref, loss_ref):
        @pl.when(pl.program_id(0) == 0)
        def _():
            loss_ref[...] = jnp.zeros_like(loss_ref)

        diff = y_ref[...] - t_ref[...]
        dy_ref[...] = diff / d
        part = 0.5 * jnp.sum(_mean(diff * diff), axis=0, keepdims=True)
        loss_ref[...] += jnp.broadcast_to(part, loss_ref.shape)

    blk = pl.BlockSpec((tm, d), lambda i: (i, 0))
    return pl.pallas_call(
        body, name="loss_grad", grid=(t // tm,),
        out_shape=[jax.ShapeDtypeStruct((t, d), F32), jax.ShapeDtypeStruct((8, LANES), F32)],
        in_specs=[blk, blk], out_specs=[blk, pl.BlockSpec((8, LANES), lambda i: (0, 0))],
        compiler_params=_params(1),
    )(y, target)


def _matmul_tn(name, a, b, tk, out_dtype):
    ga, t, m = a.shape
    gb, _, n = b.shape
    g = max(ga, gb)
    n_k = t // tk

    def body(a_ref, b_ref, o_ref, acc_ref):
        k = pl.program_id(1)

        @pl.when(k == 0)
        def _():
            acc_ref[...] = jnp.zeros_like(acc_ref)

        acc_ref[...] += _mm_tn(a_ref[...], b_ref[...])

        @pl.when(k == n_k - 1)
        def _():
            o_ref[...] = acc_ref[...].astype(out_dtype)

    return pl.pallas_call(
        body, name=name, grid=(g, n_k),
        out_shape=jax.ShapeDtypeStruct((g, m, n), out_dtype),
        in_specs=[pl.BlockSpec((None, tk, m), (lambda gi, k: (gi, k, 0)) if ga > 1 else (lambda gi, k: (0, k, 0))),
                  pl.BlockSpec((None, tk, n), (lambda gi, k: (gi, k, 0)) if gb > 1 else (lambda gi, k: (0, k, 0)))],
        out_specs=pl.BlockSpec((None, m, n), lambda gi, k: (gi, 0, 0)),
        scratch_shapes=[pltpu.VMEM((m, n), F32)],
        compiler_params=_params(2),
    )(a, b)


def _conv_taps(src_ref, col0, ncols, tm, tap_rows, weight_of, init_of, store):
    rc = min(ROW_CHUNK, tm)
    for cb in range(ncols // LANES):
        cs = slice(cb * LANES, (cb + 1) * LANES)
        ss = slice(col0 + cb * LANES, col0 + (cb + 1) * LANES)
        for r0 in range(0, tm, rc):
            acc = init_of(cs, rc)
            for k, row in enumerate(tap_rows):
                acc = acc + weight_of(k, cs) * src_ref[r0 + row:r0 + row + rc, ss]
            store(r0, rc, cs, acc)


def _mixer_fwd(l, x, mod, gains, w_in_t, conv_w, vec, pool_w, w_out, *, seq, tm, act_dtype, payload=()):
    t, d = x.shape
    d_in = w_in_t.shape[0]
    dc = conv_w.shape[-1]
    d_mix = w_out.shape[0]
    n_taps = 31
    nt, tps = t // tm, seq // tm
    h = CONV_HALO
    n_pay = len(payload)

    def body(*refs):
        ins, outs, scr, pay_refs, got_refs, sems = _carried_refs(refs, 8, 6, 1, n_pay, aliased=False)
        x_ref, mod_ref, g_ref, win_ref, cw_ref, v_ref, pw_ref, wout_ref = ins
        x1_ref, h1_ref, u_ref, a1_ref, ap_ref, o_ref = outs
        (ext_ref,) = scr
        i = pl.program_id(0)
        first = (i % tps) == 0

        if n_pay:
            @pl.when(i == 0)
            def _():
                _gather_phase(pay_refs, got_refs, *sems, 0)

        xv = x_ref[...]
        r = lax.rsqrt(_mean(xv * xv) + EPS)
        hv = (xv * r) * g_ref[0:1, :] * (1.0 + mod_ref[1:2, :]) + mod_ref[0:1, :]
        hb = hv.astype(act_dtype)
        h1_ref[...] = hb
        u = _mm_nt(hb, win_ref[...])
        u_ref[...] = u
        a0 = u[:, :dc] * _sigmoid(u[:, dc:2 * dc])

        @pl.when(first)
        def _():
            ext_ref[0:h, :] = jnp.zeros((h, ext_ref.shape[1]), F32)

        @pl.when(jnp.logical_not(first))
        def _():
            ext_ref[0:h, :] = ext_ref[tm:tm + h, :]

        ext_ref[h:h + tm, 0:dc] = a0
        ext_ref[h:h + tm, dc:] = u[:, 2 * dc:]

        def store(r0, rc, cs, acc):
            a1_ref[r0:r0 + rc, cs] = acc

        _conv_taps(ext_ref, 0, dc, tm, [h - (n_taps - 1) + k for k in range(n_taps)],
                   lambda k, cs: cw_ref[k:k + 1, cs],
                   lambda cs, rc: jnp.broadcast_to(v_ref[0:1, cs], (rc, LANES)), store)
        a1 = a1_ref[...]
        mu = _mean(a1)
        xc = a1 - mu
        rstd = lax.rsqrt(_mean(xc * xc) + EPS)
        a2 = (xc * rstd) * v_ref[1:2, :] + v_ref[2:3, :]
        ap_ref[:, 0:dc] = (a2 * _sigmoid(a2)).astype(act_dtype)

        pos = (i % tps) * tm + lax.broadcasted_iota(jnp.int32, (tm, 1), 0)
        for g, w in enumerate(POOL_WINDOWS):
            cs = slice(dc + g * POOL_GROUP, dc + (g + 1) * POOL_GROUP)
            s = ext_ref[h:h + tm, cs]
            for j in range(1, w):
                s = s + ext_ref[h - j:h - j + tm, cs]
            cnt = jnp.minimum(pos + 1, w).astype(F32)
            dv = s / cnt - ext_ref[h:h + tm, cs]
            q = _mm(dv, pw_ref[g])
            ap_ref[:, cs] = (q * v_ref[3:4, g * POOL_GROUP:(g + 1) * POOL_GROUP]).astype(act_dtype)

        o = _mm(ap_ref[...], wout_ref[...])
        o_ref[...] = o
        ro = lax.rsqrt(_mean(o * o) + EPS)
        x1_ref[...] = xv + (1.0 + mod_ref[2:3, :]) * ((o * ro) * g_ref[1:2, :])

        if n_pay:
            @pl.when(i == max(nt - 2, 0))
            def _():
                _gather_phase(pay_refs, got_refs, *sems, 1)

            @pl.when(i == nt - 1)
            def _():
                _gather_phase(pay_refs, got_refs, *sems, 2)

    row = lambda width: pl.BlockSpec((tm, width), lambda i: (i, 0))
    return _carrier_call(
        body, f"mixer_fwd_{l}", (nt,),
        out_shape=[jax.ShapeDtypeStruct((t, d), F32), jax.ShapeDtypeStruct((t, d), act_dtype),
                   jax.ShapeDtypeStruct((t, d_in), F32), jax.ShapeDtypeStruct((t, dc), F32),
                   jax.ShapeDtypeStruct((t, d_mix), act_dtype), jax.ShapeDtypeStruct((t, d), F32)],
        in_specs=[row(d),
                  pl.BlockSpec((None, None, 8, d), lambda i: (l, i // tps, 0, 0)),
                  _const_spec((None, 8, d), (l, 0, 0)),
                  _const_spec((d_in, d), (0, 0)),
                  _const_spec((32, dc), (0, 0)),
                  _const_spec((None, 8, dc), (l, 0, 0)),
                  _const_spec((None, len(POOL_WINDOWS), POOL_GROUP, POOL_GROUP), (l, 0, 0, 0)),
                  _const_spec((d_mix, d), (0, 0))],
        out_specs=[row(d), row(d), row(d_in), row(dc), row(d_mix), row(d)],
        scratch=[pltpu.VMEM((tm + h, d_mix), F32)],
        operands=(x, mod, gains, w_in_t, conv_w, vec, pool_w, w_out),
        payload=payload)


def _mixer_bwd(l, dx1, x, mod, gains, u, a1, o, w_in_t, conv_w, vec, pool_w, w_out, *, seq, tm, act_dtype,
               payload=(), recvs=(), pay_layer=0):
    t, d = x.shape
    d_in = w_in_t.shape[0]
    dc = conv_w.shape[-1]
    d_mix = w_out.shape[0]
    n_taps = 31
    nt, tps = t // tm, seq // tm
    h = CONV_HALO
    n_groups = len(POOL_WINDOWS)
    n_pay = len(payload)

    def body(*refs):
        ins, outs, scr, pay_refs, recv_refs, sems = _carried_refs(refs, 13, 8, 2, n_pay, aliased=True)
        dx1_ref, x_ref, mod_ref, g_ref, u_ref, uh_ref, a1_ref, o_ref, win_ref, cw_ref, v_ref, pw_ref, wout_ref = ins
        dx_ref, du_ref, do_ref, dmod_ref, gd_ref, gv_ref, dcw_ref, dpw_ref = outs
        ext_ref, fext_ref = scr
        i = pl.program_id(0)
        j = nt - 1 - i
        first_in_seq = (j % tps) == 0
        last_in_seq = (j % tps) == tps - 1

        if n_pay:
            @pl.when(i == 0)
            def _():
                _exchange(pay_refs, recv_refs, pay_layer, *sems, start=True)

        @pl.when(i == 0)
        def _():
            gd_ref[...] = jnp.zeros_like(gd_ref)
            gv_ref[...] = jnp.zeros_like(gv_ref)
            dcw_ref[...] = jnp.zeros_like(dcw_ref)
            dpw_ref[...] = jnp.zeros_like(dpw_ref)

        @pl.when(last_in_seq)
        def _():
            dmod_ref[...] = jnp.zeros_like(dmod_ref)
            fext_ref[tm:tm + h, :] = jnp.zeros((h, fext_ref.shape[1]), F32)

        @pl.when(jnp.logical_not(last_in_seq))
        def _():
            fext_ref[tm:tm + h, :] = fext_ref[0:h, :]

        xv = x_ref[...]
        dx1v = dx1_ref[...]
        pre_g, post_g = g_ref[0:1, :], g_ref[1:2, :]

        ov = o_ref[...]
        ro = lax.rsqrt(_mean(ov * ov) + EPS)
        yo = ov * ro
        dmod_ref[2:3, :] += _colsum(dx1v * (yo * post_g))
        dn = dx1v * (1.0 + mod_ref[2:3, :])
        gd_ref[1:2, :] += _colsum(dn * yo)
        dyo = dn * post_g
        do = ro * (dyo - yo * _mean(dyo * yo))
        dob = do.astype(act_dtype)
        do_ref[...] = dob
        dap = _mm_nt(dob, wout_ref[...])

        uv = u_ref[...]
        val, gate = uv[:, :dc], uv[:, dc:2 * dc]
        sg = _sigmoid(gate)
        ext_ref[h:h + tm, 0:dc] = val * sg
        ext_ref[h:h + tm, dc:] = uv[:, 2 * dc:]
        keep = jnp.where(first_in_seq, 0.0, 1.0).astype(F32)
        uh = uh_ref[...]
        ext_ref[0:h, 0:dc] = uh[:, :dc] * _sigmoid(uh[:, dc:2 * dc]) * keep
        ext_ref[0:h, dc:] = uh[:, 2 * dc:] * keep

        a1v = a1_ref[...]
        mu = _mean(a1v)
        xc = a1v - mu
        rstd = lax.rsqrt(_mean(xc * xc) + EPS)
        xh = xc * rstd
        ln_g = v_ref[1:2, :]
        a2 = xh * ln_g + v_ref[2:3, :]
        s2 = _sigmoid(a2)
        da2 = dap[:, :dc] * (s2 * (1.0 + a2 * (1.0 - s2)))
        gv_ref[1:2, :] += _colsum(da2 * xh)
        gv_ref[2:3, :] += _colsum(da2)
        dxh = da2 * ln_g
        da1 = rstd * (dxh - _mean(dxh) - xh * _mean(dxh * xh))
        gv_ref[0:1, :] += _colsum(da1)
        fext_ref[0:tm, 0:dc] = da1

        for k in range(n_taps):
            row = h - (n_taps - 1) + k
            dcw_ref[k:k + 1, :] += _colsum(fext_ref[0:tm, 0:dc] * ext_ref[row:row + tm, 0:dc])

        def store(r0, rc, cs, acc):
            sgc = _sigmoid(u_ref[r0:r0 + rc, dc + cs.start:dc + cs.stop])
            vc = u_ref[r0:r0 + rc, cs]
            du_ref[r0:r0 + rc, cs] = (acc * sgc).astype(act_dtype)
            du_ref[r0:r0 + rc, dc + cs.start:dc + cs.stop] = (acc * vc * sgc * (1.0 - sgc)).astype(act_dtype)

        _conv_taps(fext_ref, 0, dc, tm, [(n_taps - 1) - k for k in range(n_taps)],
                   lambda k, cs: cw_ref[k:k + 1, cs],
                   lambda cs, rc: jnp.zeros((rc, LANES), F32), store)

        pos = (j % tps) * tm + lax.broadcasted_iota(jnp.int32, (tm, 1), 0)
        for g, w in enumerate(POOL_WINDOWS):
            cs = slice(dc + g * POOL_GROUP, dc + (g + 1) * POOL_GROUP)
            gs = slice(g * POOL_GROUP, (g + 1) * POOL_GROUP)
            s = ext_ref[h:h + tm, cs]
            for jj in range(1, w):
                s = s + ext_ref[h - jj:h - jj + tm, cs]
            cnt = jnp.minimum(pos + 1, w).astype(F32)
            dv = (s / cnt - ext_ref[h:h + tm, cs]).astype(MXU_DTYPE)
            q = _mm(dv, pw_ref[g])
            dp = dap[:, cs]
            gv_ref[3:4, gs] += _colsum(dp * q)
            dq = (dp * v_ref[3:4, gs]).astype(MXU_DTYPE)
            dpw_ref[g] += _mm_tn(dv, dq)
            dd = _mm_nt(dq, pw_ref[g])
            fext_ref[0:tm, cs] = dd / cnt
            dhp = fext_ref[0:tm, cs]
            for jj in range(1, w):
                dhp = dhp + fext_ref[jj:jj + tm, cs]
            du_ref[:, dc + cs.start:dc + cs.stop] = (dhp - dd).astype(act_dtype)

        dh = _mm(du_ref[...], win_ref[...])

        r = lax.rsqrt(_mean(xv * xv) + EPS)
        xn = xv * r
        dmod_ref[0:1, :] += _colsum(dh)
        dmod_ref[1:2, :] += _colsum(dh * (xn * pre_g))
        dy = dh * (1.0 + mod_ref[1:2, :])
        gd_ref[0:1, :] += _colsum(dy * xn)
        dxn = dy * pre_g
        dx_ref[...] = dx1v + r * (dxn - xn * _mean(dxn * xn))

        if n_pay:
            @pl.when(i == nt - 1)
            def _():
                _exchange(pay_refs, recv_refs, pay_layer, *sems, start=False)

    rev = lambda width: pl.BlockSpec((tm, width), lambda i: (nt - 1 - i, 0))
    blocks_per_tile = tm // h
    return _carrier_call(
        body, f"mixer_bwd_{l}", (nt,),
        out_shape=[jax.ShapeDtypeStruct((t, d), F32), jax.ShapeDtypeStruct((t, d_in), act_dtype),
                   jax.ShapeDtypeStruct((t, d), act_dtype),
                   jax.ShapeDtypeStruct((t // seq, 8, d), F32), jax.ShapeDtypeStruct((8, d), F32),
                   jax.ShapeDtypeStruct((8, dc), F32), jax.ShapeDtypeStruct((32, dc), F32),
                   jax.ShapeDtypeStruct((n_groups, POOL_GROUP, POOL_GROUP), F32)],
        in_specs=[rev(d), rev(d),
                  pl.BlockSpec((None, None, 8, d), lambda i: (l, (nt - 1 - i) // tps, 0, 0)),
                  _const_spec((None, 8, d), (l, 0, 0)),
                  rev(d_in),
                  pl.BlockSpec((h, d_in), lambda i: (jnp.maximum((nt - 1 - i) * blocks_per_tile - 1, 0), 0)),
                  rev(dc), rev(d),
                  _const_spec((d_in, d), (0, 0)),
                  _const_spec((32, dc), (0, 0)),
                  _const_spec((None, 8, dc), (l, 0, 0)),
                  _const_spec((None, n_groups, POOL_GROUP, POOL_GROUP), (l, 0, 0, 0)),
                  _const_spec((d_mix, d), (0, 0))],
        out_specs=[rev(d), rev(d_in), rev(d),
                   pl.BlockSpec((None, 8, d), lambda i: ((nt - 1 - i) // tps, 0, 0)),
                   _acc_spec((8, d), (0, 0)), _acc_spec((8, dc), (0, 0)), _acc_spec((32, dc), (0, 0)),
                   _acc_spec((n_groups, POOL_GROUP, POOL_GROUP), (0, 0, 0))],
        scratch=[pltpu.VMEM((tm + h, d_mix), F32), pltpu.VMEM((tm + h, d_mix), F32)],
        operands=(dx1, x, mod, gains, u, u, a1, o, w_in_t, conv_w, vec, pool_w, w_out),
        payload=payload, recvs=recvs)


def _ffn_fwd(l, x, mod, gains, up, fcw, down, *, seq, tm, act_dtype, payload=()):
    t, d = x.shape
    n_chunks, _, fc = up.shape
    half = n_chunks // 2
    nt, tps = t // tm, seq // tm
    h = FFN_HALO
    n_pay = len(payload)

    def body(*refs):
        ins, outs, scr, pay_refs, got_refs, sems = _carried_refs(refs, 6, 6, 1, n_pay, aliased=False)
        x_ref, mod_ref, g_ref, up_ref, fcw_ref, down_ref = ins
        x2_ref, h2_ref, u2_ref, u3_ref, hid_ref, o2_ref = outs
        (ext_ref,) = scr
        i = pl.program_id(0)
        first = (i % tps) == 0

        if n_pay:
            @pl.when(i == 0)
            def _():
                _gather_phase(pay_refs, got_refs, *sems, 0)

        xv = x_ref[...]
        r = lax.rsqrt(_mean(xv * xv) + EPS)
        hv = (xv * r) * g_ref[0:1, :] * (1.0 + mod_ref[4:5, :]) + mod_ref[3:4, :]
        hb = hv.astype(act_dtype)
        h2_ref[...] = hb

        @pl.when(first)
        def _():
            for n in range(n_chunks):
                ext_ref[n, 0:h, :] = jnp.zeros((h, fc), F32)

        @pl.when(jnp.logical_not(first))
        def _():
            for n in range(n_chunks):
                ext_ref[n, 0:h, :] = ext_ref[n, tm:tm + h, :]

        for n in range(n_chunks):
            un = _mm(hb, up_ref[n])
            ext_ref[n, h:h + tm, :] = un
            u2_ref[n] = un.astype(act_dtype)

        def conv(n):
            return (fcw_ref[n, 3:4, :] + fcw_ref[n, 0:1, :] * ext_ref[n, h - 2:h - 2 + tm, :]
                    + fcw_ref[n, 1:2, :] * ext_ref[n, h - 1:h - 1 + tm, :]
                    + fcw_ref[n, 2:3, :] * ext_ref[n, h:h + tm, :])

        o2 = jnp.zeros((tm, d), F32)
        for n in range(half):
            gt = conv(n + half)
            v = conv(n)
            u3_ref[n] = v.astype(act_dtype)
            u3_ref[n + half] = gt.astype(act_dtype)
            hid = ((gt * _sigmoid(gt)) * v).astype(act_dtype)
            hid_ref[n] = hid
            o2 = o2 + _mm(hid, down_ref[n])
        o2_ref[...] = o2
        ro = lax.rsqrt(_mean(o2 * o2) + EPS)
        x2_ref[...] = xv + (1.0 + mod_ref[5:6, :]) * ((o2 * ro) * g_ref[1:2, :])

        if n_pay:
            @pl.when(i == max(nt - 2, 0))
            def _():
                _gather_phase(pay_refs, got_refs, *sems, 1)

            @pl.when(i == nt - 1)
            def _():
                _gather_phase(pay_refs, got_refs, *sems, 2)

    row = lambda width: pl.BlockSpec((tm, width), lambda i: (i, 0))
    chunked = lambda n: pl.BlockSpec((n, tm, fc), lambda i: (0, i, 0))
    return _carrier_call(
        body, f"ffn_fwd_{l}", (nt,),
        out_shape=[jax.ShapeDtypeStruct((t, d), F32), jax.ShapeDtypeStruct((t, d), act_dtype),
                   jax.ShapeDtypeStruct((n_chunks, t, fc), act_dtype),
                   jax.ShapeDtypeStruct((n_chunks, t, fc), act_dtype),
                   jax.ShapeDtypeStruct((half, t, fc), act_dtype), jax.ShapeDtypeStruct((t, d), F32)],
        in_specs=[row(d),
                  pl.BlockSpec((None, None, 8, d), lambda i: (l, i // tps, 0, 0)),
                  _const_spec((None, 8, d), (l, 0, 0)),
                  _const_spec((n_chunks, d, fc), (0, 0, 0)),
                  _const_spec((n_chunks, 8, fc), (0, 0, 0)),
                  _const_spec((half, fc, d), (0, 0, 0))],
        out_specs=[row(d), row(d), chunked(n_chunks), chunked(n_chunks), chunked(half), row(d)],
        scratch=[pltpu.VMEM((n_chunks, tm + h, fc), F32)],
        operands=(x, mod, gains, up, fcw, down),
        payload=payload)


def _ffn_bwd(l, dx2, x, mod, gains, u2, u3, o2, up, fcw, down, *, seq, tm, act_dtype,
             payload=(), recvs=(), pay_layer=0):
    t, d = x.shape
    n_chunks, _, fc = up.shape
    half = n_chunks // 2
    nt, tps = t // tm, seq // tm
    h = FFN_HALO
    n_pay = len(payload)

    def body(*refs):
        ins, outs, scr, pay_refs, recv_refs, sems = _carried_refs(refs, 10, 6, 1, n_pay, aliased=True)
        dx2_ref, x_ref, mod_ref, g_ref, u2_ref, u3_ref, o2_ref, up_ref, fcw_ref, down_ref = ins
        dx1_ref, du2_ref, do2_ref, dmod_ref, gd_ref, dfcw_ref = outs
        (fext_ref,) = scr
        i = pl.program_id(0)
        j = nt - 1 - i
        last_in_seq = (j % tps) == tps - 1

        if n_pay:
            @pl.when(i == 0)
            def _():
                _exchange(pay_refs, recv_refs, pay_layer, *sems, start=True)

        @pl.when(i == 0)
        def _():
            gd_ref[...] = jnp.zeros_like(gd_ref)
            dfcw_ref[...] = jnp.zeros_like(dfcw_ref)

        @pl.when(last_in_seq)
        def _():
            dmod_ref[...] = jnp.zeros_like(dmod_ref)
            for n in range(n_chunks):
                fext_ref[n, tm:tm + h, :] = jnp.zeros((h, fc), F32)

        @pl.when(jnp.logical_not(last_in_seq))
        def _():
            for n in range(n_chunks):
                fext_ref[n, tm:tm + h, :] = fext_ref[n, 0:h, :]

        xv = x_ref[...]
        dx2v = dx2_ref[...]
        pre_g, post_g = g_ref[0:1, :], g_ref[1:2, :]

        ov = o2_ref[...]
        ro = lax.rsqrt(_mean(ov * ov) + EPS)
        yo = ov * ro
        dmod_ref[2:3, :] += _colsum(dx2v * (yo * post_g))
        dn = dx2v * (1.0 + mod_ref[5:6, :])
        gd_ref[1:2, :] += _colsum(dn * yo)
        dyo = dn * post_g
        do = ro * (dyo - yo * _mean(dyo * yo))
        dob = do.astype(act_dtype)
        do2_ref[...] = dob

        for n in range(half):
            v = u3_ref[n].astype(F32)
            gt = u3_ref[n + half].astype(F32)
            sg = _sigmoid(gt)
            dhid = _mm_nt(dob, down_ref[n])
            fext_ref[n, 0:tm, :] = dhid * (gt * sg)
            fext_ref[n + half, 0:tm, :] = dhid * v * (sg * (1.0 + gt * (1.0 - sg)))

        dh = jnp.zeros((tm, d), F32)
        for n in range(n_chunks):
            d2 = fext_ref[n, 0:tm, :]
            d1 = fext_ref[n, 1:1 + tm, :]
            d0 = fext_ref[n, 2:2 + tm, :]
            u2v = u2_ref[n].astype(F32)
            dfcw_ref[n, 3:4, :] += _colsum(d2)
            dfcw_ref[n, 0:1, :] += _colsum(d0 * u2v)
            dfcw_ref[n, 1:2, :] += _colsum(d1 * u2v)
            dfcw_ref[n, 2:3, :] += _colsum(d2 * u2v)
            du2 = (fcw_ref[n, 0:1, :] * d0 + fcw_ref[n, 1:2, :] * d1 + fcw_ref[n, 2:3, :] * d2).astype(act_dtype)
            du2_ref[n] = du2
            dh = dh + _mm_nt(du2, up_ref[n])

        r = lax.rsqrt(_mean(xv * xv) + EPS)
        xn = xv * r
        dmod_ref[0:1, :] += _colsum(dh)
        dmod_ref[1:2, :] += _colsum(dh * (xn * pre_g))
        dy = dh * (1.0 + mod_ref[4:5, :])
        gd_ref[0:1, :] += _colsum(dy * xn)
        dxn = dy * pre_g
        dx1_ref[...] = dx2v + r * (dxn - xn * _mean(dxn * xn))

        if n_pay:
            @pl.when(i == nt - 1)
            def _():
                _exchange(pay_refs, recv_refs, pay_layer, *sems, start=False)

    rev = lambda width: pl.BlockSpec((tm, width), lambda i: (nt - 1 - i, 0))
    chunked = pl.BlockSpec((n_chunks, tm, fc), lambda i: (0, nt - 1 - i, 0))
    return _carrier_call(
        body, f"ffn_bwd_{l}", (nt,),
        out_shape=[jax.ShapeDtypeStruct((t, d), F32), jax.ShapeDtypeStruct((n_chunks, t, fc), act_dtype),
                   jax.ShapeDtypeStruct((t, d), act_dtype),
                   jax.ShapeDtypeStruct((t // seq, 8, d), F32), jax.ShapeDtypeStruct((8, d), F32),
                   jax.ShapeDtypeStruct((n_chunks, 8, fc), F32)],
        in_specs=[rev(d), rev(d),
                  pl.BlockSpec((None, None, 8, d), lambda i: (l, (nt - 1 - i) // tps, 0, 0)),
                  _const_spec((None, 8, d), (l, 0, 0)),
                  chunked, chunked, rev(d),
                  _const_spec((n_chunks, d, fc), (0, 0, 0)),
                  _const_spec((n_chunks, 8, fc), (0, 0, 0)),
                  _const_spec((half, fc, d), (0, 0, 0))],
        out_specs=[rev(d), chunked, rev(d),
                   pl.BlockSpec((None, 8, d), lambda i: ((nt - 1 - i) // tps, 0, 0)),
                   _acc_spec((8, d), (0, 0)), _acc_spec((n_chunks, 8, fc), (0, 0, 0))],
        scratch=[pltpu.VMEM((n_chunks, tm + h, fc), F32)],
        operands=(dx2, x, mod, gains, u2, u3, o2, up, fcw, down),
        payload=payload, recvs=recvs)


def _pad_rows(a, rows):
    pad = [(0, 0)] * a.ndim
    pad[-2] = (0, rows - a.shape[-2])
    return jnp.pad(a, pad)


def kernel(x, c, ada_w, ada_b, pre_mix_g, post_mix_g, w_in, conv_w, conv_b, conv_ln_g, conv_ln_b, pool_w, pool_scale, w_out, pre_ffn_g, post_ffn_g, ffn_up, ffn_conv_w, ffn_conv_b, ffn_down, loss_target, m_ada_w, m_ada_b, m_pre_mix_g, m_post_mix_g, m_w_in, m_conv_w, m_conv_b, m_conv_ln_g, m_conv_ln_b, m_pool_w, m_pool_scale, m_w_out, m_pre_ffn_g, m_post_ffn_g, m_ffn_up, m_ffn_conv_w, m_ffn_conv_b, m_ffn_down, v_ada_w, v_ada_b, v_pre_mix_g, v_post_mix_g, v_w_in, v_conv_w, v_conv_b, v_conv_ln_g, v_conv_ln_b, v_pool_w, v_pool_scale, v_w_out, v_pre_ffn_g, v_post_ffn_g, v_ffn_up, v_ffn_conv_w, v_ffn_conv_b, v_ffn_down):
    bl, seq, d = x.shape
    n_layers = ada_w.shape[0]
    t = bl * seq
    dc = conv_b.shape[1]
    d_in = w_in.shape[2] * N_DEV
    d_mix = w_out.shape[1] * N_DEV
    n_taps = conv_w.shape[1]
    fc = ffn_up.shape[2]
    half = N_DEV // 2
    ada_cols = ada_w.shape[2]
    n_mod = ada_cols * N_DEV // d
    assert pool_scale.shape[1] == dc and n_taps == 31 and n_mod == 6 and ffn_conv_w.shape[1] == 3
    assert pool_w.shape[1:] == (len(POOL_WINDOWS), POOL_GROUP, POOL_GROUP)
    tm = 256 if seq % 256 == 0 else 64
    tk = 2048 if t % 2048 == 0 else tm
    act = MXU_DTYPE

    ax = lax.axis_index
    me = 4 * ax("x") + 2 * ax("y") + ax("c")

    (c_all,) = _all_gather([c], "gather_c")
    c_all = c_all.reshape(N_DEV * bl, d)
    ada_b_cols = lax.dynamic_slice_in_dim(ada_b, me * ada_cols, ada_cols, axis=1)
    mod_cols = _ada_fwd(c_all, ada_w, ada_b_cols)

    w_in_s = w_in.astype(act).transpose(0, 2, 1)
    w_out_s, up_s, down_s = w_out.astype(act), ffn_up.astype(act), ffn_down.astype(act)
    conv_w_s = _pad_rows(conv_w, 32)
    fcw_s = _pad_rows(jnp.concatenate(
        [ffn_conv_w, lax.dynamic_slice_in_dim(ffn_conv_b, me * fc, fc, axis=1)[:, None, :]], axis=1), 8)

    def mixer_shards(l):
        return (w_in_s[l], w_out_s[l], conv_w_s[l])

    def ffn_shards(l):
        return (up_s[l], down_s[l], fcw_s[l])

    def mixer_weights(g_w_in, g_w_out, g_conv_w):
        return (g_w_in.reshape(d_in, d), g_conv_w.transpose(1, 0, 2).reshape(32, dc), g_w_out.reshape(d_mix, d))

    def ffn_weights(g_up, g_down, g_fcw):
        return (g_up, g_fcw, g_down.reshape(half, fc, d))

    g0 = _all_gather([mod_cols, *mixer_shards(0), *ffn_shards(0)], "gather_w0")
    mod_all = g0[0].transpose(1, 2, 0, 3).reshape(n_layers, N_DEV * bl, n_mod, d)
    mod = _pad_rows(lax.dynamic_slice_in_dim(mod_all, me * bl, bl, axis=1), 8)
    wm, wf = [mixer_weights(*g0[1:4])], [ffn_weights(*g0[4:7])]
    gains_mix = _pad_rows(jnp.stack([pre_mix_g, post_mix_g], axis=1), 8)
    gains_ffn = _pad_rows(jnp.stack([pre_ffn_g, post_ffn_g], axis=1), 8)
    vec = _pad_rows(jnp.stack([conv_b, conv_ln_g, conv_ln_b, pool_scale], axis=1), 8)
    pool_w_b = pool_w.astype(act)

    kw = dict(seq=seq, tm=tm, act_dtype=act)
    xs = x.reshape(t, d)
    saved = []
    for l in range(n_layers):
        nxt = l + 1 < n_layers
        w_in_t, cw, w_o = wm[l]
        (x1, h1, u, a1, ap, o), got = _mixer_fwd(l, xs, mod, gains_mix, w_in_t, cw, vec, pool_w_b, w_o, **kw,
                                                 payload=mixer_shards(l + 1) if nxt else ())
        if nxt:
            wm.append(mixer_weights(*got))
        up_l, fcw_l, down_l = wf[l]
        (x2, h2, u2, u3, hid, o2), got = _ffn_fwd(l, x1, mod, gains_ffn, up_l, fcw_l, down_l, **kw,
                                                  payload=ffn_shards(l + 1) if nxt else ())
        if nxt:
            wf.append(ffn_weights(*got))
        saved.append((xs, h1, u, a1, ap, o, x1, h2, u2, u3, hid, o2))
        xs = x2

    dx, loss_part = _loss_grad(xs, loss_target.reshape(t, d), tm)

    def landing(shard, dtype):
        return lax.empty((N_DEV, n_layers) + shard.shape[1:], dtype)

    r_up, r_down = landing(ffn_up, act), landing(ffn_down, act)
    r_w_in, r_w_out = landing(w_in, act), landing(w_out, act)
    r_conv_w, r_fcw = landing(conv_w, F32), landing(ffn_conv_w, F32)
    dmods, smalls = [], []
    pending = ()
    for l in reversed(range(n_layers)):
        xin, h1, u, a1, ap, o, x1, h2, u2, u3, hid, o2 = saved[l]
        w_in_t, cw, w_o = wm[l]
        up_l, fcw_l, down_l = wf[l]
        (dx1, du2, do2, dmod_b, gd_b, dfcw), got = _ffn_bwd(
            l, dx, x1, mod, gains_ffn, u2, u3, o2, up_l, fcw_l, down_l, **kw,
            payload=pending, recvs=(r_up, r_w_in, r_w_out, r_conv_w, r_fcw) if pending else (), pay_layer=l + 1)
        if pending:
            r_up, r_w_in, r_w_out, r_conv_w, r_fcw = got
        p_up = _matmul_tn(f"dw_up_{l}", h2[None], du2, tk, act)
        p_down = _matmul_tn(f"dw_down_{l}", hid, do2[None], tk, act).reshape(N_DEV, fc // 2, d)
        (dx, du, do, dmod_a, gd_a, gv, dcw, dpw), (r_down,) = _mixer_bwd(
            l, dx1, xin, mod, gains_mix, u, a1, o, w_in_t, cw, vec, pool_w_b, w_o, **kw,
            payload=(p_down,), recvs=(r_down,), pay_layer=l)
        dw_in = _matmul_tn(f"dw_in_{l}", h1[None], du[None], tk, act)[0]
        p_w_in = dw_in.reshape(d, N_DEV, d_in // N_DEV).transpose(1, 0, 2)
        p_w_out = _matmul_tn(f"dw_out_{l}", ap[None], do[None], tk, act)[0].reshape(N_DEV, d_mix // N_DEV, d)
        p_conv_w = dcw[:n_taps].reshape(n_taps, N_DEV, dc // N_DEV).transpose(1, 0, 2)
        pending = (p_up, p_w_in, p_w_out, p_conv_w, dfcw[:, :3, :])
        dmods.append(jnp.concatenate([dmod_a[:, 0:3], dmod_b[:, 0:3]], axis=1).reshape(bl, n_mod * d))
        smalls.append(jnp.concatenate(
            [gd_a[0], gd_a[1], gv[0], gv[1], gv[2], gv[3], gd_b[0], gd_b[1], dfcw[:, 3, :].reshape(-1),
             dpw.reshape(-1)]))
    dmods.reverse()
    smalls.reverse()
    r_up, r_w_in, r_w_out, r_conv_w, r_fcw = _all_to_all(
        pending, (r_up, r_w_in, r_w_out, r_conv_w, r_fcw), 0, "exchange_tail")

    dmod_loc = jnp.stack(dmods)
    small_loc = jnp.stack(smalls)
    n_small = small_loc.shape[1]
    g_dmod, g_small, g_loss = _all_gather([dmod_loc, small_loc.reshape(-1, LANES), loss_part], "gather_g")

    loss = _sum_parts(g_loss)[0, 0]

    def flat2(a):
        return a.reshape(-1, a.shape[-1])

    def update(name, parts, w, m, v):
        outs = _adam_reduce(name, parts.reshape(parts.shape[0], -1, w.shape[-1]), flat2(w), flat2(m), flat2(v))
        return [o_.reshape(w.shape) for o_ in outs]

    res = {}
    res["w_in"] = update("adam_w_in", r_w_in, w_in, m_w_in, v_w_in)
    res["w_out"] = update("adam_w_out", r_w_out, w_out, m_w_out, v_w_out)
    res["ffn_up"] = update("adam_ffn_up", r_up, ffn_up, m_ffn_up, v_ffn_up)
    res["ffn_down"] = update("adam_ffn_down", r_down, ffn_down, m_ffn_down, v_ffn_down)
    res["conv_w"] = update("adam_conv_w", r_conv_w, conv_w, m_conv_w, v_conv_w)
    res["ffn_conv_w"] = update("adam_ffn_conv_w", r_fcw, ffn_conv_w, m_ffn_conv_w, v_ffn_conv_w)

    dmod_all = g_dmod.transpose(1, 0, 2, 3).reshape(n_layers, N_DEV * bl, n_mod * d)
    dmod_cols = lax.dynamic_slice_in_dim(dmod_all, me * ada_cols, ada_cols, axis=2)
    res["ada_w"] = list(_ada_bwd(c_all.T, dmod_cols, ada_w, m_ada_w, v_ada_w))
    res["ada_b"] = update("adam_ada_b", dmod_all.transpose(1, 0, 2), ada_b, m_ada_b, v_ada_b)

    small_names = ["pre_mix_g", "post_mix_g", "conv_b", "conv_ln_g", "conv_ln_b", "pool_scale", "pre_ffn_g",
                   "post_ffn_g", "ffn_conv_b", "pool_w"]
    small_w = [pre_mix_g, post_mix_g, conv_b, conv_ln_g, conv_ln_b, pool_scale, pre_ffn_g, post_ffn_g,
               ffn_conv_b, pool_w]
    small_m = [m_pre_mix_g, m_post_mix_g, m_conv_b, m_conv_ln_g, m_conv_ln_b, m_pool_scale, m_pre_ffn_g,
               m_post_ffn_g, m_ffn_conv_b, m_pool_w]
    small_v = [v_pre_mix_g, v_post_mix_g, v_conv_b, v_conv_ln_g, v_conv_ln_b, v_pool_scale, v_pre_ffn_g,
               v_post_ffn_g, v_ffn_conv_b, v_pool_w]

    def pack(arrs):
        return jnp.concatenate([a.reshape(n_layers, -1) for a in arrs], axis=1).reshape(-1, LANES)

    outs = _adam_reduce("adam_small", g_small, pack(small_w), pack(small_m), pack(small_v))
    outs = [o_.reshape(n_layers, n_small) for o_ in outs]
    off = 0
    for name, w in zip(small_names, small_w):
        size = w[0].size
        res[name] = [o_[:, off:off + size].reshape(w.shape) for o_ in outs]
        off += size

    order = ["ada_w", "ada_b", "pre_mix_g", "post_mix_g", "w_in", "conv_w", "conv_b", "conv_ln_g", "conv_ln_b",
             "pool_w", "pool_scale", "w_out", "pre_ffn_g", "post_ffn_g", "ffn_up", "ffn_conv_w", "ffn_conv_b",
             "ffn_down"]
    return (loss, dx.reshape(bl, seq, d), *[res[n][0] for n in order], *[res[n][1] for n in order],
            *[res[n][2] for n in order], *[res[n][3] for n in order])
```

```python
import jax
import jax.numpy as jnp
from jax import lax
from jax.experimental import pallas as pl
from jax.experimental.pallas import tpu as pltpu

N_DEV = 8
EPS = 1e-6
POOL_WINDOWS = (2, 4, 8, 16)
POOL_GROUP = 128
TILE_TOKENS = 256
CONV_PREFIX = 8 * 30
FFN_PREFIX = 8 * 2
LANES = 128
ROW_CHUNK = 64
MXU_DTYPE = jnp.bfloat16
VMEM_LIMIT = 60 * 1024 * 1024

ADAM_LR = 0.001
ADAM_B1 = 0.9
ADAM_B2 = 0.999
ADAM_EPS = 1e-08
ADAM_WD = 0.01
ADAM_STEP = 10

MESH = pl.DeviceIdType.MESH
F32 = jnp.float32


def _mm(a, b):
    return jnp.dot(a.astype(MXU_DTYPE), b.astype(MXU_DTYPE), preferred_element_type=F32)


def _mm_nt(a, b):
    return lax.dot_general(a.astype(MXU_DTYPE), b.astype(MXU_DTYPE), (((1,), (1,)), ((), ())),
                           preferred_element_type=F32)


def _mm_tn(a, b):
    return lax.dot_general(a.astype(MXU_DTYPE), b.astype(MXU_DTYPE), (((0,), (0,)), ((), ())),
                           preferred_element_type=F32)


def _mean(v):
    return jnp.mean(v, axis=-1, keepdims=True)


def _colsum(v):
    return jnp.sum(v, axis=0, keepdims=True)


def _sigmoid(v):
    return jax.nn.sigmoid(v)


def _params(n_grid=1):
    return pltpu.CompilerParams(dimension_semantics=("arbitrary",) * n_grid, vmem_limit_bytes=VMEM_LIMIT)


def _const_spec(shape, index):
    return pl.BlockSpec(shape, lambda *_: index, pipeline_mode=pl.Buffered(1))


def _acc_spec(shape, index):
    return pl.BlockSpec(shape, lambda *_: index)


def _position():
    x, y, c = lax.axis_index("x"), lax.axis_index("y"), lax.axis_index("c")
    return x, y, c


def _gather_phase(ins, outs, send_sems, recv_sems, local_sems, phase):
    n = len(ins)
    x, y, c = _position()
    me, sibling = (x, y, c), (x, y, 1 - c)
    chips = [(1 - x, y), (x, 1 - y), (1 - x, 1 - y)]

    def slot(k, px, py, pc):
        return outs[k].at[4 * px + 2 * py + pc]

    def copy(k, s, block, to, src=None):
        return pltpu.make_async_remote_copy(
            src_ref=slot(k, *block) if src is None else src, dst_ref=slot(k, *block),
            send_sem=send_sems.at[k, s], recv_sem=recv_sems.at[k, s], device_id=to, device_id_type=MESH)

    mine = [pltpu.make_async_copy(ins[k], slot(k, *me), local_sems.at[k]) for k in range(n)]
    first = []
    for k in range(n):
        first.append(copy(k, 0, me, sibling, src=ins[k]))
        first += [copy(k, 1 + j, me, (*chip, c), src=ins[k]) for j, chip in enumerate(chips)]
    passed = [copy(k, 4 + j, (*chip, c), sibling) for j, chip in enumerate(chips) for k in range(n)]
    if phase == 0:
        for cp in mine + first:
            cp.start()
    elif phase == 1:
        for j, chip in enumerate(chips):
            for k in range(n):
                copy(k, 1 + j, (*chip, c), me).wait_recv()
                copy(k, 4 + j, (*chip, c), sibling).start()
    else:
        for k in range(n):
            copy(k, 0, sibling, me).wait_recv()
            for j, chip in enumerate(chips):
                copy(k, 4 + j, (*chip, 1 - c), me).wait_recv()
        for cp in first + passed:
            cp.wait_send()
        for cp in mine:
            cp.wait()


def _gather_scratch(n):
    return [pltpu.SemaphoreType.DMA((n, 7)), pltpu.SemaphoreType.DMA((n, 7)), pltpu.SemaphoreType.DMA((n,))]


def _all_gather(arrs, name):
    n = len(arrs)

    def body(*refs):
        for phase in range(3):
            _gather_phase(refs[:n], refs[n:2 * n], *refs[2 * n:], phase)

    any_spec = pl.BlockSpec(memory_space=pl.ANY)
    return pl.pallas_call(
        body, name=name,
        out_shape=[jax.ShapeDtypeStruct((N_DEV,) + a.shape, a.dtype) for a in arrs],
        in_specs=[any_spec] * n, out_specs=[any_spec] * n,
        scratch_shapes=_gather_scratch(n),
    )(*arrs)


def _exchange(payload, recvs, layer, send_sems, recv_sems, local_sems, start):
    x, y, c = _position()
    me = 4 * x + 2 * y + c
    for k, (src, recv) in enumerate(zip(payload, recvs)):
        local = pltpu.make_async_copy(src.at[me], recv.at[me, layer], local_sems.at[k])
        if start:
            local.start()
        else:
            local.wait()
        for j in range(1, N_DEV):
            px = (1 - x) if (j & 4) else x
            py = (1 - y) if (j & 2) else y
            pc = (1 - c) if (j & 1) else c
            peer = 4 * px + 2 * py + pc
            landing = recv.at[me, layer] if start else recv.at[peer, layer]
            cp = pltpu.make_async_remote_copy(
                src_ref=src.at[peer], dst_ref=landing, send_sem=send_sems.at[k, j - 1],
                recv_sem=recv_sems.at[k, j - 1], device_id=(px, py, pc), device_id_type=MESH)
            if start:
                cp.start()
            else:
                cp.wait()


def _exchange_scratch(n):
    return [pltpu.SemaphoreType.DMA((n, N_DEV - 1)), pltpu.SemaphoreType.DMA((n, N_DEV - 1)),
            pltpu.SemaphoreType.DMA((n,))]


def _all_to_all(payload, recvs, layer, name):
    n = len(payload)

    def body(*refs):
        pay, outs = refs[:n], refs[2 * n:3 * n]
        sems = refs[3 * n:]
        _exchange(pay, outs, layer, *sems, start=True)
        _exchange(pay, outs, layer, *sems, start=False)

    any_spec = pl.BlockSpec(memory_space=pl.ANY)
    return pl.pallas_call(
        body, name=name,
        out_shape=[jax.ShapeDtypeStruct(r.shape, r.dtype) for r in recvs],
        in_specs=[any_spec] * (2 * n), out_specs=[any_spec] * n,
        input_output_aliases={n + k: k for k in range(n)},
        scratch_shapes=_exchange_scratch(n),
    )(*payload, *recvs)


def _carried_refs(refs, n_in, n_out, n_scratch, n_pay, aliased):
    ins = refs[:n_in]
    pay = refs[n_in:n_in + n_pay]
    o0 = n_in + (2 * n_pay if aliased else n_pay)
    outs = refs[o0:o0 + n_out]
    recvs = refs[o0 + n_out:o0 + n_out + n_pay]
    s0 = o0 + n_out + n_pay
    return ins, outs, refs[s0:s0 + n_scratch], pay, recvs, refs[s0 + n_scratch:]


def _carrier_call(body, name, grid, in_specs, out_specs, out_shape, scratch, operands, payload, recvs=None):
    n_in, n_out, n_pay = len(in_specs), len(out_specs), len(payload)
    any_spec = pl.BlockSpec(memory_space=pl.ANY)
    if recvs is None:
        landing = [jax.ShapeDtypeStruct((N_DEV,) + a.shape, a.dtype) for a in payload]
        extra_in, aliases = list(payload), {}
        sems = _gather_scratch(n_pay) if n_pay else []
    else:
        landing = [jax.ShapeDtypeStruct(r.shape, r.dtype) for r in recvs]
        extra_in = list(payload) + list(recvs)
        aliases = {n_in + n_pay + k: n_out + k for k in range(n_pay)}
        sems = _exchange_scratch(n_pay) if n_pay else []
    res = pl.pallas_call(
        body, name=name, grid=grid,
        out_shape=list(out_shape) + landing,
        in_specs=list(in_specs) + [any_spec] * len(extra_in),
        out_specs=list(out_specs) + [any_spec] * n_pay,
        input_output_aliases=aliases,
        scratch_shapes=list(scratch) + sems,
        compiler_params=_params(len(grid)),
    )(*operands, *extra_in)
    return res[:n_out], res[n_out:]


def _adamw(w, g, m, v):
    m = ADAM_B1 * m + (1.0 - ADAM_B1) * g
    v = ADAM_B2 * v + (1.0 - ADAM_B2) * jnp.square(g)
    m_hat = m / (1.0 - ADAM_B1 ** ADAM_STEP)
    v_hat = v / (1.0 - ADAM_B2 ** ADAM_STEP)
    delta = -ADAM_LR * (m_hat / (jnp.sqrt(v_hat) + ADAM_EPS) + ADAM_WD * w)
    return delta, m, v


def _ada_fwd(c_all, ada_w, ada_b_cols):
    n_layers, d, cols = ada_w.shape
    b = c_all.shape[0]

    def body(c_ref, w_ref, b_ref, o_ref):
        cv = c_ref[...]
        act = cv * _sigmoid(cv)
        o_ref[...] = jnp.dot(act, w_ref[...], preferred_element_type=F32,
                             precision=lax.Precision.HIGHEST) + b_ref[...]

    return pl.pallas_call(
        body, name="ada_fwd", grid=(n_layers,),
        out_shape=jax.ShapeDtypeStruct((n_layers, b, cols), F32),
        in_specs=[pl.BlockSpec((b, d), lambda l: (0, 0)),
                  pl.BlockSpec((None, d, cols), lambda l: (l, 0, 0)),
                  pl.BlockSpec((None, 1, cols), lambda l: (l, 0, 0))],
        out_specs=pl.BlockSpec((None, b, cols), lambda l: (l, 0, 0)),
        compiler_params=_params(1),
    )(c_all, ada_w, ada_b_cols.reshape(n_layers, 1, cols))


def _ada_bwd(c_all_t, dmod_cols, w, m, v):
    n_layers, d, cols = w.shape
    b = c_all_t.shape[1]
    td = 256 if d % 256 == 0 else d

    def body(c_ref, dm_ref, w_ref, m_ref, v_ref, g_ref, dl_ref, nm_ref, nv_ref):
        cv = c_ref[...]
        act = cv * _sigmoid(cv)
        g = jnp.dot(act, dm_ref[...], preferred_element_type=F32, precision=lax.Precision.HIGHEST)
        delta, nm, nv = _adamw(w_ref[...], g, m_ref[...], v_ref[...])
        g_ref[...] = g
        dl_ref[...] = delta
        nm_ref[...] = nm
        nv_ref[...] = nv

    blk = pl.BlockSpec((None, td, cols), lambda l, i: (l, i, 0))
    shp = jax.ShapeDtypeStruct(w.shape, F32)
    return pl.pallas_call(
        body, name="ada_bwd", grid=(n_layers, d // td),
        out_shape=[shp] * 4,
        in_specs=[pl.BlockSpec((td, b), lambda l, i: (i, 0)),
                  pl.BlockSpec((None, b, cols), lambda l, i: (l, 0, 0)), blk, blk, blk],
        out_specs=[blk] * 4,
        compiler_params=_params(2),
    )(c_all_t, dmod_cols, w, m, v)


def _row_tile(rows, cols, budget=128 * 1024, step=8):
    best = None
    for t in range(step, rows + 1, step):
        if rows % t == 0 and t * cols <= budget:
            best = t
    return best if best is not None else rows


def _adam_reduce(name, parts, w, m, v):
    p, rows, cols = parts.shape
    tr = _row_tile(rows, cols, budget=(256 * 1024) // max(1, p // 4), step=8 if parts.dtype == F32 else 16)

    def body(p_ref, w_ref, m_ref, v_ref, g_ref, dl_ref, nm_ref, nv_ref):
        g = p_ref[0].astype(F32)
        for k in range(1, p):
            g = g + p_ref[k].astype(F32)
        delta, nm, nv = _adamw(w_ref[...], g, m_ref[...], v_ref[...])
        g_ref[...] = g
        dl_ref[...] = delta
        nm_ref[...] = nm
        nv_ref[...] = nv

    blk = pl.BlockSpec((tr, cols), lambda i: (i, 0))
    shp = jax.ShapeDtypeStruct((rows, cols), F32)
    return pl.pallas_call(
        body, name=name, grid=(rows // tr,),
        out_shape=[shp] * 4,
        in_specs=[pl.BlockSpec((p, tr, cols), lambda i: (0, i, 0)), blk, blk, blk],
        out_specs=[blk] * 4,
        compiler_params=_params(1),
    )(parts, w, m, v)


def _sum_parts(parts):
    p = parts.shape[0]

    def body(p_ref, o_ref):
        acc = p_ref[0]
        for k in range(1, p):
            acc = acc + p_ref[k]
        o_ref[...] = acc

    return pl.pallas_call(body, name="loss_sum", out_shape=jax.ShapeDtypeStruct(parts.shape[1:], F32))(parts)


def _loss_grad(y, target, tm):
    t, d = y.shape

    def body(y_ref, t_ref, dy_ref, loss_ref):
        @pl.when(pl.program_id(0) == 0)
        def _():
            loss_ref[...] = jnp.zeros_like(loss_ref)

        diff = y_ref[...] - t_ref[...]
        dy_ref[...] = diff / d
        part = 0.5 * jnp.sum(_mean(diff * diff), axis=0, keepdims=True)
        loss_ref[...] += jnp.broadcast_to(part, loss_ref.shape)

    blk = pl.BlockSpec((tm, d), lambda i: (i, 0))
    return pl.pallas_call(
        body, name="loss_grad", grid=(t // tm,),
        out_shape=[jax.ShapeDtypeStruct((t, d), F32), jax.ShapeDtypeStruct((8, LANES), F32)],
        in_specs=[blk, blk], out_specs=[blk, pl.BlockSpec((8, LANES), lambda i: (0, 0))],
        compiler_params=_params(1),
    )(y, target)


def _matmul_tn(name, a, b, tk, out_dtype):
    ga, t, m = a.shape
    gb, _, n = b.shape
    g = max(ga, gb)
    n_k = t // tk

    def body(a_ref, b_ref, o_ref, acc_ref):
        k = pl.program_id(1)

        @pl.when(k == 0)
        def _():
            acc_ref[...] = jnp.zeros_like(acc_ref)

        acc_ref[...] += _mm_tn(a_ref[...], b_ref[...])

        @pl.when(k == n_k - 1)
        def _():
            o_ref[...] = acc_ref[...].astype(out_dtype)

    return pl.pallas_call(
        body, name=name, grid=(g, n_k),
        out_shape=jax.ShapeDtypeStruct((g, m, n), out_dtype),
        in_specs=[pl.BlockSpec((None, tk, m), (lambda gi, k: (gi, k, 0)) if ga > 1 else (lambda gi, k: (0, k, 0))),
                  pl.BlockSpec((None, tk, n), (lambda gi, k: (gi, k, 0)) if gb > 1 else (lambda gi, k: (0, k, 0)))],
        out_specs=pl.BlockSpec((None, m, n), lambda gi, k: (gi, 0, 0)),
        scratch_shapes=[pltpu.VMEM((m, n), F32)],
        compiler_params=_params(2),
    )(a, b)


def _time_of_row(tm):
    i = lax.broadcasted_iota(jnp.int32, (tm, 1), 0)
    return (i % 8) * (tm // 8) + i // 8


def _sublane_is(rows, s):
    return lax.broadcasted_iota(jnp.int32, (rows, 1), 0) % 8 == s


def _conv_taps(src_ref, col0, ncols, tm, tap_rows, weight_of, init_of, store):
    rc = min(ROW_CHUNK, tm)
    for cb in range(ncols // LANES):
        cs = slice(cb * LANES, (cb + 1) * LANES)
        ss = slice(col0 + cb * LANES, col0 + (cb + 1) * LANES)
        for r0 in range(0, tm, rc):
            acc = init_of(cs, rc)
            for k, row in enumerate(tap_rows):
                acc = acc + weight_of(k, cs) * src_ref[r0 + row:r0 + row + rc, ss]
            store(r0, rc, cs, acc)


def _mixer_fwd(l, x, mod, gains, w_in_t, conv_w, vec, pool_w, w_out, *, seq, tm, act_dtype, payload=()):
    t, d = x.shape
    d_in = w_in_t.shape[0]
    dc = conv_w.shape[-1]
    d_mix = w_out.shape[0]
    n_taps = 31
    nt, tps = t // tm, seq // tm
    p = CONV_PREFIX
    n_pay = len(payload)

    def body(*refs):
        ins, outs, scr, pay_refs, got_refs, sems = _carried_refs(refs, 8, 7, 2, n_pay, aliased=False)
        x_ref, mod_ref, g_ref, win_ref, cw_ref, v_ref, pw_ref, wout_ref = ins
        x1_ref, h1_ref, u_ref, a1_ref, ap_ref, o_ref, tail_ref = outs
        ext_ref, car_ref = scr
        i = pl.program_id(0)
        first = (i % tps) == 0

        if n_pay:
            @pl.when(i == 0)
            def _():
                _gather_phase(pay_refs, got_refs, *sems, 0)

        xv = x_ref[...]
        r = lax.rsqrt(_mean(xv * xv) + EPS)
        hv = (xv * r) * g_ref[0:1, :] * (1.0 + mod_ref[1:2, :]) + mod_ref[0:1, :]
        hb = hv.astype(act_dtype)
        h1_ref[...] = hb
        u = _mm_nt(hb, win_ref[...])
        u_ref[...] = u
        a0 = u[:, :dc] * _sigmoid(u[:, dc:2 * dc])

        @pl.when(first)
        def _():
            car_ref[...] = jnp.zeros_like(car_ref)

        @pl.when(i == 0)
        def _():
            ext_ref[p + tm:p + tm + 8, :] = jnp.zeros((8, ext_ref.shape[1]), F32)

        ext_ref[p:p + tm, 0:dc] = a0
        ext_ref[p:p + tm, dc:] = u[:, 2 * dc:]
        ext_ref[0:p, :] = jnp.where(_sublane_is(p, 0), car_ref[...], ext_ref[tm - 1:tm - 1 + p, :])

        def store(r0, rc, cs, acc):
            a1_ref[r0:r0 + rc, cs] = acc

        _conv_taps(ext_ref, 0, dc, tm, [p - 8 * (n_taps - 1 - k) for k in range(n_taps)],
                   lambda k, cs: cw_ref[k:k + 1, cs],
                   lambda cs, rc: jnp.broadcast_to(v_ref[0:1, cs], (rc, LANES)), store)
        a1 = a1_ref[...]
        mu = _mean(a1)
        xc = a1 - mu
        rstd = lax.rsqrt(_mean(xc * xc) + EPS)
        a2 = (xc * rstd) * v_ref[1:2, :] + v_ref[2:3, :]
        ap_ref[:, 0:dc] = (a2 * _sigmoid(a2)).astype(act_dtype)

        pos = (i % tps) * tm + _time_of_row(tm)
        for g, w in enumerate(POOL_WINDOWS):
            cs = slice(dc + g * POOL_GROUP, dc + (g + 1) * POOL_GROUP)
            s = ext_ref[p:p + tm, cs]
            for j in range(1, w):
                s = s + ext_ref[p - 8 * j:p - 8 * j + tm, cs]
            cnt = jnp.minimum(pos + 1, w).astype(F32)
            dv = s / cnt - ext_ref[p:p + tm, cs]
            q = _mm(dv, pw_ref[g])
            ap_ref[:, cs] = (q * v_ref[3:4, g * POOL_GROUP:(g + 1) * POOL_GROUP]).astype(act_dtype)

        o = _mm(ap_ref[...], wout_ref[...])
        o_ref[...] = o
        ro = lax.rsqrt(_mean(o * o) + EPS)
        x1_ref[...] = xv + (1.0 + mod_ref[2:3, :]) * ((o * ro) * g_ref[1:2, :])

        nxt = ext_ref[tm + 7:tm + 7 + p, :]
        car_ref[...] = nxt
        tail_ref[...] = nxt

        if n_pay:
            @pl.when(i == max(nt - 2, 0))
            def _():
                _gather_phase(pay_refs, got_refs, *sems, 1)

            @pl.when(i == nt - 1)
            def _():
                _gather_phase(pay_refs, got_refs, *sems, 2)

    row = lambda width: pl.BlockSpec((tm, width), lambda i: (i, 0))
    return _carrier_call(
        body, f"mixer_fwd_{l}", (nt,),
        out_shape=[jax.ShapeDtypeStruct((t, d), F32), jax.ShapeDtypeStruct((t, d), act_dtype),
                   jax.ShapeDtypeStruct((t, d_in), F32), jax.ShapeDtypeStruct((t, dc), F32),
                   jax.ShapeDtypeStruct((t, d_mix), act_dtype), jax.ShapeDtypeStruct((t, d), F32),
                   jax.ShapeDtypeStruct((nt, p, d_mix), F32)],
        in_specs=[row(d),
                  pl.BlockSpec((None, None, 8, d), lambda i: (l, i // tps, 0, 0)),
                  _const_spec((None, 8, d), (l, 0, 0)),
                  _const_spec((d_in, d), (0, 0)),
                  _const_spec((32, dc), (0, 0)),
                  _const_spec((None, 8, dc), (l, 0, 0)),
                  _const_spec((None, len(POOL_WINDOWS), POOL_GROUP, POOL_GROUP), (l, 0, 0, 0)),
                  _const_spec((d_mix, d), (0, 0))],
        out_specs=[row(d), row(d), row(d_in), row(dc), row(d_mix), row(d),
                   pl.BlockSpec((None, p, d_mix), lambda i: (i, 0, 0))],
        scratch=[pltpu.VMEM((p + tm + 8, d_mix), F32), pltpu.VMEM((p, d_mix), F32)],
        operands=(x, mod, gains, w_in_t, conv_w, vec, pool_w, w_out),
        payload=payload)


def _mixer_bwd(l, dx1, x, mod, gains, u, tails, a1, o, w_in_t, conv_w, vec, pool_w, w_out, *, seq, tm, act_dtype,
               payload=(), recvs=(), pay_layer=0):
    t, d = x.shape
    d_in = w_in_t.shape[0]
    dc = conv_w.shape[-1]
    d_mix = w_out.shape[0]
    n_taps = 31
    nt, tps = t // tm, seq // tm
    p = CONV_PREFIX
    n_groups = len(POOL_WINDOWS)
    n_pay = len(payload)

    def body(*refs):
        ins, outs, scr, pay_refs, recv_refs, sems = _carried_refs(refs, 13, 8, 3, n_pay, aliased=True)
        dx1_ref, x_ref, mod_ref, g_ref, u_ref, tail_ref, a1_ref, o_ref, win_ref, cw_ref, v_ref, pw_ref, wout_ref = ins
        dx_ref, du_ref, do_ref, dmod_ref, gd_ref, gv_ref, dcw_ref, dpw_ref = outs
        ext_ref, fext_ref, fcar_ref = scr
        i = pl.program_id(0)
        j = nt - 1 - i
        first_in_seq = (j % tps) == 0
        last_in_seq = (j % tps) == tps - 1

        if n_pay:
            @pl.when(i == 0)
            def _():
                _exchange(pay_refs, recv_refs, pay_layer, *sems, start=True)

        @pl.when(i == 0)
        def _():
            gd_ref[...] = jnp.zeros_like(gd_ref)
            gv_ref[...] = jnp.zeros_like(gv_ref)
            dcw_ref[...] = jnp.zeros_like(dcw_ref)
            dpw_ref[...] = jnp.zeros_like(dpw_ref)
            fext_ref[0:8, :] = jnp.zeros((8, fext_ref.shape[1]), F32)

        @pl.when(last_in_seq)
        def _():
            dmod_ref[...] = jnp.zeros_like(dmod_ref)
            fcar_ref[...] = jnp.zeros_like(fcar_ref)

        xv = x_ref[...]
        dx1v = dx1_ref[...]
        pre_g, post_g = g_ref[0:1, :], g_ref[1:2, :]

        ov = o_ref[...]
        ro = lax.rsqrt(_mean(ov * ov) + EPS)
        yo = ov * ro
        dmod_ref[2:3, :] += _colsum(dx1v * (yo * post_g))
        dn = dx1v * (1.0 + mod_ref[2:3, :])
        gd_ref[1:2, :] += _colsum(dn * yo)
        dyo = dn * post_g
        do = ro * (dyo - yo * _mean(dyo * yo))
        dob = do.astype(act_dtype)
        do_ref[...] = dob
        dap = _mm_nt(dob, wout_ref[...])

        uv = u_ref[...]
        val, gate = uv[:, :dc], uv[:, dc:2 * dc]
        sg = _sigmoid(gate)
        ext_ref[p:p + tm, 0:dc] = val * sg
        ext_ref[p:p + tm, dc:] = uv[:, 2 * dc:]
        keep = jnp.where(first_in_seq, 0.0, 1.0).astype(F32)
        ext_ref[0:p, :] = jnp.where(_sublane_is(p, 0), tail_ref[...] * keep, ext_ref[tm - 1:tm - 1 + p, :])

        a1v = a1_ref[...]
        mu = _mean(a1v)
        xc = a1v - mu
        rstd = lax.rsqrt(_mean(xc * xc) + EPS)
        xh = xc * rstd
        ln_g = v_ref[1:2, :]
        a2 = xh * ln_g + v_ref[2:3, :]
        s2 = _sigmoid(a2)
        da2 = dap[:, :dc] * (s2 * (1.0 + a2 * (1.0 - s2)))
        gv_ref[1:2, :] += _colsum(da2 * xh)
        gv_ref[2:3, :] += _colsum(da2)
        dxh = da2 * ln_g
        da1 = rstd * (dxh - _mean(dxh) - xh * _mean(dxh * xh))
        gv_ref[0:1, :] += _colsum(da1)
        last_sublane = _sublane_is(p, 7)

        def put(cs, value):
            fext_ref[8:8 + tm, cs] = value
            fext_ref[8 + tm:8 + tm + p, cs] = jnp.where(last_sublane, fcar_ref[:, cs], fext_ref[9:9 + p, cs])

        put(slice(0, dc), da1)

        for k in range(n_taps):
            row = p - 8 * (n_taps - 1 - k)
            dcw_ref[k:k + 1, :] += _colsum(fext_ref[8:8 + tm, 0:dc] * ext_ref[row:row + tm, 0:dc])

        def store(r0, rc, cs, acc):
            sgc = _sigmoid(u_ref[r0:r0 + rc, dc + cs.start:dc + cs.stop])
            vc = u_ref[r0:r0 + rc, cs]
            du_ref[r0:r0 + rc, cs] = (acc * sgc).astype(act_dtype)
            du_ref[r0:r0 + rc, dc + cs.start:dc + cs.stop] = (acc * vc * sgc * (1.0 - sgc)).astype(act_dtype)

        _conv_taps(fext_ref, 0, dc, tm, [8 + 8 * (n_taps - 1 - k) for k in range(n_taps)],
                   lambda k, cs: cw_ref[k:k + 1, cs],
                   lambda cs, rc: jnp.zeros((rc, LANES), F32), store)

        pos = (j % tps) * tm + _time_of_row(tm)
        for g, w in enumerate(POOL_WINDOWS):
            cs = slice(dc + g * POOL_GROUP, dc + (g + 1) * POOL_GROUP)
            gs = slice(g * POOL_GROUP, (g + 1) * POOL_GROUP)
            s = ext_ref[p:p + tm, cs]
            for jj in range(1, w):
                s = s + ext_ref[p - 8 * jj:p - 8 * jj + tm, cs]
            cnt = jnp.minimum(pos + 1, w).astype(F32)
            dv = (s / cnt - ext_ref[p:p + tm, cs]).astype(MXU_DTYPE)
            q = _mm(dv, pw_ref[g])
            dp = dap[:, cs]
            gv_ref[3:4, gs] += _colsum(dp * q)
            dq = (dp * v_ref[3:4, gs]).astype(MXU_DTYPE)
            dpw_ref[g] += _mm_tn(dv, dq)
            dd = _mm_nt(dq, pw_ref[g])
            put(cs, dd / cnt)
            dhp = fext_ref[8:8 + tm, cs]
            for jj in range(1, w):
                dhp = dhp + fext_ref[8 + 8 * jj:8 + 8 * jj + tm, cs]
            du_ref[:, dc + cs.start:dc + cs.stop] = (dhp - dd).astype(act_dtype)

        fcar_ref[...] = fext_ref[1:1 + p, :]

        dh = _mm(du_ref[...], win_ref[...])

        r = lax.rsqrt(_mean(xv * xv) + EPS)
        xn = xv * r
        dmod_ref[0:1, :] += _colsum(dh)
        dmod_ref[1:2, :] += _colsum(dh * (xn * pre_g))
        dy = dh * (1.0 + mod_ref[1:2, :])
        gd_ref[0:1, :] += _colsum(dy * xn)
        dxn = dy * pre_g
        dx_ref[...] = dx1v + r * (dxn - xn * _mean(dxn * xn))

        if n_pay:
            @pl.when(i == nt - 1)
            def _():
                _exchange(pay_refs, recv_refs, pay_layer, *sems, start=False)

    rev = lambda width: pl.BlockSpec((tm, width), lambda i: (nt - 1 - i, 0))
    return _carrier_call(
        body, f"mixer_bwd_{l}", (nt,),
        out_shape=[jax.ShapeDtypeStruct((t, d), F32), jax.ShapeDtypeStruct((t, d_in), act_dtype),
                   jax.ShapeDtypeStruct((t, d), act_dtype),
                   jax.ShapeDtypeStruct((t // seq, 8, d), F32), jax.ShapeDtypeStruct((8, d), F32),
                   jax.ShapeDtypeStruct((8, dc), F32), jax.ShapeDtypeStruct((32, dc), F32),
                   jax.ShapeDtypeStruct((n_groups, POOL_GROUP, POOL_GROUP), F32)],
        in_specs=[rev(d), rev(d),
                  pl.BlockSpec((None, None, 8, d), lambda i: (l, (nt - 1 - i) // tps, 0, 0)),
                  _const_spec((None, 8, d), (l, 0, 0)),
                  rev(d_in),
                  pl.BlockSpec((None, p, d_mix), lambda i: (jnp.maximum(nt - 2 - i, 0), 0, 0)),
                  rev(dc), rev(d),
                  _const_spec((d_in, d), (0, 0)),
                  _const_spec((32, dc), (0, 0)),
                  _const_spec((None, 8, dc), (l, 0, 0)),
                  _const_spec((None, n_groups, POOL_GROUP, POOL_GROUP), (l, 0, 0, 0)),
                  _const_spec((d_mix, d), (0, 0))],
        out_specs=[rev(d), rev(d_in), rev(d),
                   pl.BlockSpec((None, 8, d), lambda i: ((nt - 1 - i) // tps, 0, 0)),
                   _acc_spec((8, d), (0, 0)), _acc_spec((8, dc), (0, 0)), _acc_spec((32, dc), (0, 0)),
                   _acc_spec((n_groups, POOL_GROUP, POOL_GROUP), (0, 0, 0))],
        scratch=[pltpu.VMEM((p + tm, d_mix), F32), pltpu.VMEM((8 + tm + p, d_mix), F32),
                 pltpu.VMEM((p, d_mix), F32)],
        operands=(dx1, x, mod, gains, u, tails, a1, o, w_in_t, conv_w, vec, pool_w, w_out),
        payload=payload, recvs=recvs)


def _ffn_fwd(l, x, mod, gains, up, fcw, down, *, seq, tm, act_dtype, payload=()):
    t, d = x.shape
    n_chunks, _, fc = up.shape
    half = n_chunks // 2
    nt, tps = t // tm, seq // tm
    p = FFN_PREFIX
    n_pay = len(payload)

    def body(*refs):
        ins, outs, scr, pay_refs, got_refs, sems = _carried_refs(refs, 6, 6, 2, n_pay, aliased=False)
        x_ref, mod_ref, g_ref, up_ref, fcw_ref, down_ref = ins
        x2_ref, h2_ref, u2_ref, u3_ref, hid_ref, o2_ref = outs
        ext_ref, car_ref = scr
        i = pl.program_id(0)
        first = (i % tps) == 0

        if n_pay:
            @pl.when(i == 0)
            def _():
                _gather_phase(pay_refs, got_refs, *sems, 0)

        xv = x_ref[...]
        r = lax.rsqrt(_mean(xv * xv) + EPS)
        hv = (xv * r) * g_ref[0:1, :] * (1.0 + mod_ref[4:5, :]) + mod_ref[3:4, :]
        hb = hv.astype(act_dtype)
        h2_ref[...] = hb

        @pl.when(first)
        def _():
            car_ref[...] = jnp.zeros_like(car_ref)

        @pl.when(i == 0)
        def _():
            for n in range(n_chunks):
                ext_ref[n, p + tm:p + tm + 8, :] = jnp.zeros((8, fc), F32)

        first_sublane = _sublane_is(p, 0)
        for n in range(n_chunks):
            un = _mm(hb, up_ref[n])
            ext_ref[n, p:p + tm, :] = un
            u2_ref[n] = un.astype(act_dtype)
            ext_ref[n, 0:p, :] = jnp.where(first_sublane, car_ref[n], ext_ref[n, tm - 1:tm - 1 + p, :])
            car_ref[n] = ext_ref[n, tm + 7:tm + 7 + p, :]

        def conv(n):
            return (fcw_ref[n, 3:4, :] + fcw_ref[n, 0:1, :] * ext_ref[n, p - 16:p - 16 + tm, :]
                    + fcw_ref[n, 1:2, :] * ext_ref[n, p - 8:p - 8 + tm, :]
                    + fcw_ref[n, 2:3, :] * ext_ref[n, p:p + tm, :])

        o2 = jnp.zeros((tm, d), F32)
        for n in range(half):
            gt = conv(n + half)
            v = conv(n)
            u3_ref[n] = v.astype(act_dtype)
            u3_ref[n + half] = gt.astype(act_dtype)
            hid = ((gt * _sigmoid(gt)) * v).astype(act_dtype)
            hid_ref[n] = hid
            o2 = o2 + _mm(hid, down_ref[n])
        o2_ref[...] = o2
        ro = lax.rsqrt(_mean(o2 * o2) + EPS)
        x2_ref[...] = xv + (1.0 + mod_ref[5:6, :]) * ((o2 * ro) * g_ref[1:2, :])

        if n_pay:
            @pl.when(i == max(nt - 2, 0))
            def _():
                _gather_phase(pay_refs, got_refs, *sems, 1)

            @pl.when(i == nt - 1)
            def _():
                _gather_phase(pay_refs, got_refs, *sems, 2)

    row = lambda width: pl.BlockSpec((tm, width), lambda i: (i, 0))
    chunked = lambda n: pl.BlockSpec((n, tm, fc), lambda i: (0, i, 0))
    return _carrier_call(
        body, f"ffn_fwd_{l}", (nt,),
        out_shape=[jax.ShapeDtypeStruct((t, d), F32), jax.ShapeDtypeStruct((t, d), act_dtype),
                   jax.ShapeDtypeStruct((n_chunks, t, fc), act_dtype),
                   jax.ShapeDtypeStruct((n_chunks, t, fc), act_dtype),
                   jax.ShapeDtypeStruct((half, t, fc), act_dtype), jax.ShapeDtypeStruct((t, d), F32)],
        in_specs=[row(d),
                  pl.BlockSpec((None, None, 8, d), lambda i: (l, i // tps, 0, 0)),
                  _const_spec((None, 8, d), (l, 0, 0)),
                  _const_spec((n_chunks, d, fc), (0, 0, 0)),
                  _const_spec((n_chunks, 8, fc), (0, 0, 0)),
                  _const_spec((half, fc, d), (0, 0, 0))],
        out_specs=[row(d), row(d), chunked(n_chunks), chunked(n_chunks), chunked(half), row(d)],
        scratch=[pltpu.VMEM((n_chunks, p + tm + 8, fc), F32), pltpu.VMEM((n_chunks, p, fc), F32)],
        operands=(x, mod, gains, up, fcw, down),
        payload=payload)


def _ffn_bwd(l, dx2, x, mod, gains, u2, u3, o2, up, fcw, down, *, seq, tm, act_dtype,
             payload=(), recvs=(), pay_layer=0):
    t, d = x.shape
    n_chunks, _, fc = up.shape
    half = n_chunks // 2
    nt, tps = t // tm, seq // tm
    p = FFN_PREFIX
    n_pay = len(payload)

    def body(*refs):
        ins, outs, scr, pay_refs, recv_refs, sems = _carried_refs(refs, 10, 6, 2, n_pay, aliased=True)
        dx2_ref, x_ref, mod_ref, g_ref, u2_ref, u3_ref, o2_ref, up_ref, fcw_ref, down_ref = ins
        dx1_ref, du2_ref, do2_ref, dmod_ref, gd_ref, dfcw_ref = outs
        fext_ref, fcar_ref = scr
        i = pl.program_id(0)
        j = nt - 1 - i
        last_in_seq = (j % tps) == tps - 1

        if n_pay:
            @pl.when(i == 0)
            def _():
                _exchange(pay_refs, recv_refs, pay_layer, *sems, start=True)

        @pl.when(i == 0)
        def _():
            gd_ref[...] = jnp.zeros_like(gd_ref)
            dfcw_ref[...] = jnp.zeros_like(dfcw_ref)
            for n in range(n_chunks):
                fext_ref[n, 0:8, :] = jnp.zeros((8, fc), F32)

        @pl.when(last_in_seq)
        def _():
            dmod_ref[...] = jnp.zeros_like(dmod_ref)
            fcar_ref[...] = jnp.zeros_like(fcar_ref)

        xv = x_ref[...]
        dx2v = dx2_ref[...]
        pre_g, post_g = g_ref[0:1, :], g_ref[1:2, :]

        ov = o2_ref[...]
        ro = lax.rsqrt(_mean(ov * ov) + EPS)
        yo = ov * ro
        dmod_ref[2:3, :] += _colsum(dx2v * (yo * post_g))
        dn = dx2v * (1.0 + mod_ref[5:6, :])
        gd_ref[1:2, :] += _colsum(dn * yo)
        dyo = dn * post_g
        do = ro * (dyo - yo * _mean(dyo * yo))
        dob = do.astype(act_dtype)
        do2_ref[...] = dob

        for n in range(half):
            v = u3_ref[n].astype(F32)
            gt = u3_ref[n + half].astype(F32)
            sg = _sigmoid(gt)
            dhid = _mm_nt(dob, down_ref[n])
            fext_ref[n, 8:8 + tm, :] = dhid * (gt * sg)
            fext_ref[n + half, 8:8 + tm, :] = dhid * v * (sg * (1.0 + gt * (1.0 - sg)))

        last_sublane = _sublane_is(p, 7)
        dh = jnp.zeros((tm, d), F32)
        for n in range(n_chunks):
            fext_ref[n, 8 + tm:8 + tm + p, :] = jnp.where(last_sublane, fcar_ref[n], fext_ref[n, 9:9 + p, :])
            fcar_ref[n] = fext_ref[n, 1:1 + p, :]
            d2 = fext_ref[n, 8:8 + tm, :]
            d1 = fext_ref[n, 16:16 + tm, :]
            d0 = fext_ref[n, 24:24 + tm, :]
            u2v = u2_ref[n].astype(F32)
            dfcw_ref[n, 3:4, :] += _colsum(d2)
            dfcw_ref[n, 0:1, :] += _colsum(d0 * u2v)
            dfcw_ref[n, 1:2, :] += _colsum(d1 * u2v)
            dfcw_ref[n, 2:3, :] += _colsum(d2 * u2v)
            du2 = (fcw_ref[n, 0:1, :] * d0 + fcw_ref[n, 1:2, :] * d1 + fcw_ref[n, 2:3, :] * d2).astype(act_dtype)
            du2_ref[n] = du2
            dh = dh + _mm_nt(du2, up_ref[n])

        r = lax.rsqrt(_mean(xv * xv) + EPS)
        xn = xv * r
        dmod_ref[0:1, :] += _colsum(dh)
        dmod_ref[1:2, :] += _colsum(dh * (xn * pre_g))
        dy = dh * (1.0 + mod_ref[4:5, :])
        gd_ref[0:1, :] += _colsum(dy * xn)
        dxn = dy * pre_g
        dx1_ref[...] = dx2v + r * (dxn - xn * _mean(dxn * xn))

        if n_pay:
            @pl.when(i == nt - 1)
            def _():
                _exchange(pay_refs, recv_refs, pay_layer, *sems, start=False)

    rev = lambda width: pl.BlockSpec((tm, width), lambda i: (nt - 1 - i, 0))
    chunked = pl.BlockSpec((n_chunks, tm, fc), lambda i: (0, nt - 1 - i, 0))
    return _carrier_call(
        body, f"ffn_bwd_{l}", (nt,),
        out_shape=[jax.ShapeDtypeStruct((t, d), F32), jax.ShapeDtypeStruct((n_chunks, t, fc), act_dtype),
                   jax.ShapeDtypeStruct((t, d), act_dtype),
                   jax.ShapeDtypeStruct((t // seq, 8, d), F32), jax.ShapeDtypeStruct((8, d), F32),
                   jax.ShapeDtypeStruct((n_chunks, 8, fc), F32)],
        in_specs=[rev(d), rev(d),
                  pl.BlockSpec((None, None, 8, d), lambda i: (l, (nt - 1 - i) // tps, 0, 0)),
                  _const_spec((None, 8, d), (l, 0, 0)),
                  chunked, chunked, rev(d),
                  _const_spec((n_chunks, d, fc), (0, 0, 0)),
                  _const_spec((n_chunks, 8, fc), (0, 0, 0)),
                  _const_spec((half, fc, d), (0, 0, 0))],
        out_specs=[rev(d), chunked, rev(d),
                   pl.BlockSpec((None, 8, d), lambda i: ((nt - 1 - i) // tps, 0, 0)),
                   _acc_spec((8, d), (0, 0)), _acc_spec((n_chunks, 8, fc), (0, 0, 0))],
        scratch=[pltpu.VMEM((n_chunks, 8 + tm + p, fc), F32), pltpu.VMEM((n_chunks, p, fc), F32)],
        operands=(dx2, x, mod, gains, u2, u3, o2, up, fcw, down),
        payload=payload, recvs=recvs)


def _pad_rows(a, rows):
    pad = [(0, 0)] * a.ndim
    pad[-2] = (0, rows - a.shape[-2])
    return jnp.pad(a, pad)


def kernel(x, c, ada_w, ada_b, pre_mix_g, post_mix_g, w_in, conv_w, conv_b, conv_ln_g, conv_ln_b, pool_w, pool_scale, w_out, pre_ffn_g, post_ffn_g, ffn_up, ffn_conv_w, ffn_conv_b, ffn_down, loss_target, m_ada_w, m_ada_b, m_pre_mix_g, m_post_mix_g, m_w_in, m_conv_w, m_conv_b, m_conv_ln_g, m_conv_ln_b, m_pool_w, m_pool_scale, m_w_out, m_pre_ffn_g, m_post_ffn_g, m_ffn_up, m_ffn_conv_w, m_ffn_conv_b, m_ffn_down, v_ada_w, v_ada_b, v_pre_mix_g, v_post_mix_g, v_w_in, v_conv_w, v_conv_b, v_conv_ln_g, v_conv_ln_b, v_pool_w, v_pool_scale, v_w_out, v_pre_ffn_g, v_post_ffn_g, v_ffn_up, v_ffn_conv_w, v_ffn_conv_b, v_ffn_down):
    bl, seq, d = x.shape
    n_layers = ada_w.shape[0]
    t = bl * seq
    dc = conv_b.shape[1]
    d_in = w_in.shape[2] * N_DEV
    d_mix = w_out.shape[1] * N_DEV
    n_taps = conv_w.shape[1]
    fc = ffn_up.shape[2]
    half = N_DEV // 2
    ada_cols = ada_w.shape[2]
    n_mod = ada_cols * N_DEV // d
    assert pool_scale.shape[1] == dc and n_taps == 31 and n_mod == 6 and ffn_conv_w.shape[1] == 3
    assert pool_w.shape[1:] == (len(POOL_WINDOWS), POOL_GROUP, POOL_GROUP)
    tm = TILE_TOKENS
    assert seq % tm == 0 and CONV_PREFIX <= tm - 8
    tk = 2048 if t % 2048 == 0 else tm
    act = MXU_DTYPE

    def tile_order(a, inverse=False):
        shape = (t // tm, tm // 8, 8, d) if inverse else (t // tm, 8, tm // 8, d)
        return a.reshape(shape).transpose(0, 2, 1, 3).reshape(t, d)

    ax = lax.axis_index
    me = 4 * ax("x") + 2 * ax("y") + ax("c")

    (c_all,) = _all_gather([c], "gather_c")
    c_all = c_all.reshape(N_DEV * bl, d)
    ada_b_cols = lax.dynamic_slice_in_dim(ada_b, me * ada_cols, ada_cols, axis=1)
    mod_cols = _ada_fwd(c_all, ada_w, ada_b_cols)

    w_in_s = w_in.astype(act).transpose(0, 2, 1)
    w_out_s, up_s, down_s = w_out.astype(act), ffn_up.astype(act), ffn_down.astype(act)
    conv_w_s = _pad_rows(conv_w, 32)
    fcw_s = _pad_rows(jnp.concatenate(
        [ffn_conv_w, lax.dynamic_slice_in_dim(ffn_conv_b, me * fc, fc, axis=1)[:, None, :]], axis=1), 8)

    def mixer_shards(l):
        return (w_in_s[l], w_out_s[l], conv_w_s[l])

    def ffn_shards(l):
        return (up_s[l], down_s[l], fcw_s[l])

    def mixer_weights(g_w_in, g_w_out, g_conv_w):
        return (g_w_in.reshape(d_in, d), g_conv_w.transpose(1, 0, 2).reshape(32, dc), g_w_out.reshape(d_mix, d))

    def ffn_weights(g_up, g_down, g_fcw):
        return (g_up, g_fcw, g_down.reshape(half, fc, d))

    g0 = _all_gather([mod_cols, *mixer_shards(0), *ffn_shards(0)], "gather_w0")
    mod_all = g0[0].transpose(1, 2, 0, 3).reshape(n_layers, N_DEV * bl, n_mod, d)
    mod = _pad_rows(lax.dynamic_slice_in_dim(mod_all, me * bl, bl, axis=1), 8)
    wm, wf = [mixer_weights(*g0[1:4])], [ffn_weights(*g0[4:7])]
    gains_mix = _pad_rows(jnp.stack([pre_mix_g, post_mix_g], axis=1), 8)
    gains_ffn = _pad_rows(jnp.stack([pre_ffn_g, post_ffn_g], axis=1), 8)
    vec = _pad_rows(jnp.stack([conv_b, conv_ln_g, conv_ln_b, pool_scale], axis=1), 8)
    pool_w_b = pool_w.astype(act)

    kw = dict(seq=seq, tm=tm, act_dtype=act)
    xs = tile_order(x.reshape(t, d))
    saved = []
    for l in range(n_layers):
        nxt = l + 1 < n_layers
        w_in_t, cw, w_o = wm[l]
        (x1, h1, u, a1, ap, o, tails), got = _mixer_fwd(l, xs, mod, gains_mix, w_in_t, cw, vec, pool_w_b, w_o, **kw,
                                                        payload=mixer_shards(l + 1) if nxt else ())
        if nxt:
            wm.append(mixer_weights(*got))
        up_l, fcw_l, down_l = wf[l]
        (x2, h2, u2, u3, hid, o2), got = _ffn_fwd(l, x1, mod, gains_ffn, up_l, fcw_l, down_l, **kw,
                                                  payload=ffn_shards(l + 1) if nxt else ())
        if nxt:
            wf.append(ffn_weights(*got))
        saved.append((xs, h1, u, tails, a1, ap, o, x1, h2, u2, u3, hid, o2))
        xs = x2

    dx, loss_part = _loss_grad(xs, tile_order(loss_target.reshape(t, d)), tm)

    def landing(shard, dtype):
        return lax.empty((N_DEV, n_layers) + shard.shape[1:], dtype)

    r_up, r_down = landing(ffn_up, act), landing(ffn_down, act)
    r_w_in, r_w_out = landing(w_in, act), landing(w_out, act)
    r_conv_w, r_fcw = landing(conv_w, F32), landing(ffn_conv_w, F32)
    dmods, smalls = [], []
    pending = ()
    for l in reversed(range(n_layers)):
        xin, h1, u, tails, a1, ap, o, x1, h2, u2, u3, hid, o2 = saved[l]
        w_in_t, cw, w_o = wm[l]
        up_l, fcw_l, down_l = wf[l]
        (dx1, du2, do2, dmod_b, gd_b, dfcw), got = _ffn_bwd(
            l, dx, x1, mod, gains_ffn, u2, u3, o2, up_l, fcw_l, down_l, **kw,
            payload=pending, recvs=(r_up, r_w_in, r_w_out, r_conv_w, r_fcw) if pending else (), pay_layer=l + 1)
        if pending:
            r_up, r_w_in, r_w_out, r_conv_w, r_fcw = got
        p_up = _matmul_tn(f"dw_up_{l}", h2[None], du2, tk, act)
        p_down = _matmul_tn(f"dw_down_{l}", hid, do2[None], tk, act).reshape(N_DEV, fc // 2, d)
        (dx, du, do, dmod_a, gd_a, gv, dcw, dpw), (r_down,) = _mixer_bwd(
            l, dx1, xin, mod, gains_mix, u, tails, a1, o, w_in_t, cw, vec, pool_w_b, w_o, **kw,
            payload=(p_down,), recvs=(r_down,), pay_layer=l)
        dw_in = _matmul_tn(f"dw_in_{l}", h1[None], du[None], tk, act)[0]
        p_w_in = dw_in.reshape(d, N_DEV, d_in // N_DEV).transpose(1, 0, 2)
        p_w_out = _matmul_tn(f"dw_out_{l}", ap[None], do[None], tk, act)[0].reshape(N_DEV, d_mix // N_DEV, d)
        p_conv_w = dcw[:n_taps].reshape(n_taps, N_DEV, dc // N_DEV).transpose(1, 0, 2)
        pending = (p_up, p_w_in, p_w_out, p_conv_w, dfcw[:, :3, :])
        dmods.append(jnp.concatenate([dmod_a[:, 0:3], dmod_b[:, 0:3]], axis=1).reshape(bl, n_mod * d))
        smalls.append(jnp.concatenate(
            [gd_a[0], gd_a[1], gv[0], gv[1], gv[2], gv[3], gd_b[0], gd_b[1], dfcw[:, 3, :].reshape(-1),
             dpw.reshape(-1)]))
    dmods.reverse()
    smalls.reverse()
    r_up, r_w_in, r_w_out, r_conv_w, r_fcw = _all_to_all(
        pending, (r_up, r_w_in, r_w_out, r_conv_w, r_fcw), 0, "exchange_tail")

    dmod_loc = jnp.stack(dmods)
    small_loc = jnp.stack(smalls)
    n_small = small_loc.shape[1]
    g_dmod, g_small, g_loss = _all_gather([dmod_loc, small_loc.reshape(-1, LANES), loss_part], "gather_g")

    loss = _sum_parts(g_loss)[0, 0]

    def flat2(a):
        return a.reshape(-1, a.shape[-1])

    def update(name, parts, w, m, v):
        outs = _adam_reduce(name, parts.reshape(parts.shape[0], -1, w.shape[-1]), flat2(w), flat2(m), flat2(v))
        return [o_.reshape(w.shape) for o_ in outs]

    res = {}
    res["w_in"] = update("adam_w_in", r_w_in, w_in, m_w_in, v_w_in)
    res["w_out"] = update("adam_w_out", r_w_out, w_out, m_w_out, v_w_out)
    res["ffn_up"] = update("adam_ffn_up", r_up, ffn_up, m_ffn_up, v_ffn_up)
    res["ffn_down"] = update("adam_ffn_down", r_down, ffn_down, m_ffn_down, v_ffn_down)
    res["conv_w"] = update("adam_conv_w", r_conv_w, conv_w, m_conv_w, v_conv_w)
    res["ffn_conv_w"] = update("adam_ffn_conv_w", r_fcw, ffn_conv_w, m_ffn_conv_w, v_ffn_conv_w)

    dmod_all = g_dmod.transpose(1, 0, 2, 3).reshape(n_layers, N_DEV * bl, n_mod * d)
    dmod_cols = lax.dynamic_slice_in_dim(dmod_all, me * ada_cols, ada_cols, axis=2)
    res["ada_w"] = list(_ada_bwd(c_all.T, dmod_cols, ada_w, m_ada_w, v_ada_w))
    res["ada_b"] = update("adam_ada_b", dmod_all.transpose(1, 0, 2), ada_b, m_ada_b, v_ada_b)

    small_names = ["pre_mix_g", "post_mix_g", "conv_b", "conv_ln_g", "conv_ln_b", "pool_scale", "pre_ffn_g",
                   "post_ffn_g", "ffn_conv_b", "pool_w"]
    small_w = [pre_mix_g, post_mix_g, conv_b, conv_ln_g, conv_ln_b, pool_scale, pre_ffn_g, post_ffn_g,
               ffn_conv_b, pool_w]
    small_m = [m_pre_mix_g, m_post_mix_g, m_conv_b, m_conv_ln_g, m_conv_ln_b, m_pool_scale, m_pre_ffn_g,
               m_post_ffn_g, m_ffn_conv_b, m_pool_w]
    small_v = [v_pre_mix_g, v_post_mix_g, v_conv_b, v_conv_ln_g, v_conv_ln_b, v_pool_scale, v_pre_ffn_g,
               v_post_ffn_g, v_ffn_conv_b, v_pool_w]

    def pack(arrs):
        return jnp.concatenate([a.reshape(n_layers, -1) for a in arrs], axis=1).reshape(-1, LANES)

    outs = _adam_reduce("adam_small", g_small, pack(small_w), pack(small_m), pack(small_v))
    outs = [o_.reshape(n_layers, n_small) for o_ in outs]
    off = 0
    for name, w in zip(small_names, small_w):
        size = w[0].size
        res[name] = [o_[:, off:off + size].reshape(w.shape) for o_ in outs]
        off += size

    order = ["ada_w", "ada_b", "pre_mix_g", "post_mix_g", "w_in", "conv_w", "conv_b", "conv_ln_g", "conv_ln_b",
             "pool_w", "pool_scale", "w_out", "pre_ffn_g", "post_ffn_g", "ffn_up", "ffn_conv_w", "ffn_conv_b",
             "ffn_down"]
    grad_x = tile_order(dx, inverse=True).reshape(bl, seq, d)
    return (loss, grad_x, *[res[n][0] for n in order], *[res[n][1] for n in order],
            *[res[n][2] for n in order], *[res[n][3] for n in order])
```

```python
import jax
import jax.numpy as jnp
from jax import lax
from jax.experimental import pallas as pl
from jax.experimental.pallas import tpu as pltpu

N_DEV = 8
EPS = 1e-6
POOL_WINDOWS = (2, 4, 8, 16)
POOL_GROUP = 128
TILE_TOKENS = 256
CONV_PREFIX = 8 * 30
FFN_PREFIX = 8 * 2
LANES = 128
ROW_CHUNK = 64
MXU_DTYPE = jnp.bfloat16
VMEM_LIMIT = 60 * 1024 * 1024

ADAM_LR = 0.001
ADAM_B1 = 0.9
ADAM_B2 = 0.999
ADAM_EPS = 1e-08
ADAM_WD = 0.01
ADAM_STEP = 10

MESH = pl.DeviceIdType.MESH
F32 = jnp.float32


def _mm(a, b):
    return jnp.dot(a.astype(MXU_DTYPE), b.astype(MXU_DTYPE), preferred_element_type=F32)


def _mm_nt(a, b):
    return lax.dot_general(a.astype(MXU_DTYPE), b.astype(MXU_DTYPE), (((1,), (1,)), ((), ())),
                           preferred_element_type=F32)


def _mm_tn(a, b):
    return lax.dot_general(a.astype(MXU_DTYPE), b.astype(MXU_DTYPE), (((0,), (0,)), ((), ())),
                           preferred_element_type=F32)


def _mean(v):
    return jnp.mean(v, axis=-1, keepdims=True)


def _colsum(v):
    return jnp.sum(v, axis=0, keepdims=True)


def _sigmoid(v):
    return jax.nn.sigmoid(v)


def _params(n_grid=1):
    return pltpu.CompilerParams(dimension_semantics=("arbitrary",) * n_grid, vmem_limit_bytes=VMEM_LIMIT)


def _const_spec(shape, index):
    return pl.BlockSpec(shape, lambda *_: index, pipeline_mode=pl.Buffered(1))


def _acc_spec(shape, index):
    return pl.BlockSpec(shape, lambda *_: index)


def _position():
    x, y, c = lax.axis_index("x"), lax.axis_index("y"), lax.axis_index("c")
    return x, y, c


def _gather_phase(ins, outs, send_sems, recv_sems, local_sems, phase):
    n = len(ins)
    x, y, c = _position()
    me, sibling = (x, y, c), (x, y, 1 - c)
    chips = [(1 - x, y), (x, 1 - y), (1 - x, 1 - y)]

    def slot(k, px, py, pc):
        return outs[k].at[4 * px + 2 * py + pc]

    def copy(k, s, block, to, src=None):
        return pltpu.make_async_remote_copy(
            src_ref=slot(k, *block) if src is None else src, dst_ref=slot(k, *block),
            send_sem=send_sems.at[k, s], recv_sem=recv_sems.at[k, s], device_id=to, device_id_type=MESH)

    mine = [pltpu.make_async_copy(ins[k], slot(k, *me), local_sems.at[k]) for k in range(n)]
    first = []
    for k in range(n):
        first.append(copy(k, 0, me, sibling, src=ins[k]))
        first += [copy(k, 1 + j, me, (*chip, c), src=ins[k]) for j, chip in enumerate(chips)]
    passed = [copy(k, 4 + j, (*chip, c), sibling) for j, chip in enumerate(chips) for k in range(n)]
    if phase == 0:
        for cp in mine + first:
            cp.start()
    elif phase == 1:
        for j, chip in enumerate(chips):
            for k in range(n):
                copy(k, 1 + j, (*chip, c), me).wait_recv()
                copy(k, 4 + j, (*chip, c), sibling).start()
    else:
        for k in range(n):
            copy(k, 0, sibling, me).wait_recv()
            for j, chip in enumerate(chips):
                copy(k, 4 + j, (*chip, 1 - c), me).wait_recv()
        for cp in first + passed:
            cp.wait_send()
        for cp in mine:
            cp.wait()


def _gather_scratch(n):
    return [pltpu.SemaphoreType.DMA((n, 7)), pltpu.SemaphoreType.DMA((n, 7)), pltpu.SemaphoreType.DMA((n,))]


def _all_gather(arrs, name):
    n = len(arrs)

    def body(*refs):
        for phase in range(3):
            _gather_phase(refs[:n], refs[n:2 * n], *refs[2 * n:], phase)

    any_spec = pl.BlockSpec(memory_space=pl.ANY)
    return pl.pallas_call(
        body, name=name,
        out_shape=[jax.ShapeDtypeStruct((N_DEV,) + a.shape, a.dtype) for a in arrs],
        in_specs=[any_spec] * n, out_specs=[any_spec] * n,
        scratch_shapes=_gather_scratch(n),
    )(*arrs)


def _exchange(payload, recvs, rows, layer, send_sems, recv_sems, local_sems, start):
    x, y, c = _position()
    me = 4 * x + 2 * y + c
    for k, (src, recv) in enumerate(zip(payload, recvs)):
        def block(ref, *index):
            return ref.at[index] if rows[k] is None else ref.at[(*index, pl.ds(*rows[k]))]

        local = pltpu.make_async_copy(block(src, me), block(recv, me, layer), local_sems.at[k])
        if start:
            local.start()
        else:
            local.wait()
        for j in range(1, N_DEV):
            px = (1 - x) if (j & 4) else x
            py = (1 - y) if (j & 2) else y
            pc = (1 - c) if (j & 1) else c
            peer = 4 * px + 2 * py + pc
            landing = block(recv, me, layer) if start else block(recv, peer, layer)
            cp = pltpu.make_async_remote_copy(
                src_ref=block(src, peer), dst_ref=landing, send_sem=send_sems.at[k, j - 1],
                recv_sem=recv_sems.at[k, j - 1], device_id=(px, py, pc), device_id_type=MESH)
            if start:
                cp.start()
            else:
                cp.wait()


def _exchange_scratch(n):
    return [pltpu.SemaphoreType.DMA((n, N_DEV - 1)), pltpu.SemaphoreType.DMA((n, N_DEV - 1)),
            pltpu.SemaphoreType.DMA((n,))]


def _exchange_and_gather(payload, recvs, rows, layer, arrs, name):
    n, m = len(payload), len(arrs)

    def body(*refs):
        pay, srcs = refs[:n], refs[2 * n:2 * n + m]
        outs, got = refs[2 * n + m:3 * n + m], refs[3 * n + m:3 * n + 2 * m]
        xsems, gsems = refs[3 * n + 2 * m:3 * n + 2 * m + 3], refs[3 * n + 2 * m + 3:]
        _exchange(pay, outs, rows, layer, *xsems, start=True)
        for phase in range(3):
            _gather_phase(srcs, got, *gsems, phase)
        _exchange(pay, outs, rows, layer, *xsems, start=False)

    any_spec = pl.BlockSpec(memory_space=pl.ANY)
    res = pl.pallas_call(
        body, name=name,
        out_shape=[jax.ShapeDtypeStruct(r.shape, r.dtype) for r in recvs]
        + [jax.ShapeDtypeStruct((N_DEV,) + a.shape, a.dtype) for a in arrs],
        in_specs=[any_spec] * (2 * n + m), out_specs=[any_spec] * (n + m),
        input_output_aliases={n + k: k for k in range(n)},
        scratch_shapes=_exchange_scratch(n) + _gather_scratch(m),
    )(*payload, *recvs, *arrs)
    return res[:n], res[n:]


def _carried_refs(refs, n_in, n_out, n_scratch, n_pay, aliased):
    ins = refs[:n_in]
    pay = refs[n_in:n_in + n_pay]
    o0 = n_in + (2 * n_pay if aliased else n_pay)
    outs = refs[o0:o0 + n_out]
    recvs = refs[o0 + n_out:o0 + n_out + n_pay]
    s0 = o0 + n_out + n_pay
    return ins, outs, refs[s0:s0 + n_scratch], pay, recvs, refs[s0 + n_scratch:]


def _carrier_call(body, name, grid, in_specs, out_specs, out_shape, scratch, operands, payload, recvs=None):
    n_in, n_out, n_pay = len(in_specs), len(out_specs), len(payload)
    any_spec = pl.BlockSpec(memory_space=pl.ANY)
    if recvs is None:
        landing = [jax.ShapeDtypeStruct((N_DEV,) + a.shape, a.dtype) for a in payload]
        extra_in, aliases = list(payload), {}
        sems = _gather_scratch(n_pay) if n_pay else []
    else:
        landing = [jax.ShapeDtypeStruct(r.shape, r.dtype) for r in recvs]
        extra_in = list(payload) + list(recvs)
        aliases = {n_in + n_pay + k: n_out + k for k in range(n_pay)}
        sems = _exchange_scratch(n_pay) if n_pay else []
    res = pl.pallas_call(
        body, name=name, grid=grid,
        out_shape=list(out_shape) + landing,
        in_specs=list(in_specs) + [any_spec] * len(extra_in),
        out_specs=list(out_specs) + [any_spec] * n_pay,
        input_output_aliases=aliases,
        scratch_shapes=list(scratch) + sems,
        compiler_params=_params(len(grid)),
    )(*operands, *extra_in)
    return res[:n_out], res[n_out:]


def _adamw(w, g, m, v):
    m = ADAM_B1 * m + (1.0 - ADAM_B1) * g
    v = ADAM_B2 * v + (1.0 - ADAM_B2) * jnp.square(g)
    m_hat = m / (1.0 - ADAM_B1 ** ADAM_STEP)
    v_hat = v / (1.0 - ADAM_B2 ** ADAM_STEP)
    delta = -ADAM_LR * (m_hat / (jnp.sqrt(v_hat) + ADAM_EPS) + ADAM_WD * w)
    return delta, m, v


def _ada_fwd(c_all, ada_w, ada_b_cols):
    n_layers, d, cols = ada_w.shape
    b = c_all.shape[0]

    def body(c_ref, w_ref, b_ref, o_ref):
        cv = c_ref[...]
        act = cv * _sigmoid(cv)
        o_ref[...] = jnp.dot(act, w_ref[...], preferred_element_type=F32,
                             precision=lax.Precision.HIGHEST) + b_ref[...]

    return pl.pallas_call(
        body, name="ada_fwd", grid=(n_layers,),
        out_shape=jax.ShapeDtypeStruct((n_layers, b, cols), F32),
        in_specs=[pl.BlockSpec((b, d), lambda l: (0, 0)),
                  pl.BlockSpec((None, d, cols), lambda l: (l, 0, 0)),
                  pl.BlockSpec((None, 1, cols), lambda l: (l, 0, 0))],
        out_specs=pl.BlockSpec((None, b, cols), lambda l: (l, 0, 0)),
        compiler_params=_params(1),
    )(c_all, ada_w, ada_b_cols.reshape(n_layers, 1, cols))


def _ada_bwd(c_all_t, dmod_cols, w, m, v):
    n_layers, d, cols = w.shape
    b = c_all_t.shape[1]
    td = 256 if d % 256 == 0 else d

    def body(c_ref, dm_ref, w_ref, m_ref, v_ref, g_ref, dl_ref, nm_ref, nv_ref):
        cv = c_ref[...]
        act = cv * _sigmoid(cv)
        g = jnp.dot(act, dm_ref[...], preferred_element_type=F32, precision=lax.Precision.HIGHEST)
        delta, nm, nv = _adamw(w_ref[...], g, m_ref[...], v_ref[...])
        g_ref[...] = g
        dl_ref[...] = delta
        nm_ref[...] = nm
        nv_ref[...] = nv

    blk = pl.BlockSpec((None, td, cols), lambda l, i: (l, i, 0))
    shp = jax.ShapeDtypeStruct(w.shape, F32)
    return pl.pallas_call(
        body, name="ada_bwd", grid=(n_layers, d // td),
        out_shape=[shp] * 4,
        in_specs=[pl.BlockSpec((td, b), lambda l, i: (i, 0)),
                  pl.BlockSpec((None, b, cols), lambda l, i: (l, 0, 0)), blk, blk, blk],
        out_specs=[blk] * 4,
        compiler_params=_params(2),
    )(c_all_t, dmod_cols, w, m, v)


def _row_tile(rows, cols, budget=128 * 1024, step=8):
    best = None
    for t in range(step, rows + 1, step):
        if rows % t == 0 and t * cols <= budget:
            best = t
    return best if best is not None else rows


def _adam_reduce(name, parts, w, m, v):
    p, rows, cols = parts.shape
    tr = _row_tile(rows, cols, budget=(256 * 1024) // max(1, p // 4), step=8 if parts.dtype == F32 else 16)

    def body(p_ref, w_ref, m_ref, v_ref, g_ref, dl_ref, nm_ref, nv_ref):
        g = p_ref[0].astype(F32)
        for k in range(1, p):
            g = g + p_ref[k].astype(F32)
        delta, nm, nv = _adamw(w_ref[...], g, m_ref[...], v_ref[...])
        g_ref[...] = g
        dl_ref[...] = delta
        nm_ref[...] = nm
        nv_ref[...] = nv

    blk = pl.BlockSpec((tr, cols), lambda i: (i, 0))
    shp = jax.ShapeDtypeStruct((rows, cols), F32)
    return pl.pallas_call(
        body, name=name, grid=(rows // tr,),
        out_shape=[shp] * 4,
        in_specs=[pl.BlockSpec((p, tr, cols), lambda i: (0, i, 0)), blk, blk, blk],
        out_specs=[blk] * 4,
        compiler_params=_params(1),
    )(parts, w, m, v)


def _sum_parts(parts):
    p = parts.shape[0]

    def body(p_ref, o_ref):
        acc = p_ref[0]
        for k in range(1, p):
            acc = acc + p_ref[k]
        o_ref[...] = acc

    return pl.pallas_call(body, name="loss_sum", out_shape=jax.ShapeDtypeStruct(parts.shape[1:], F32))(parts)


def _loss_grad(y, target, tm):
    t, d = y.shape

    def body(y_ref, t_ref, dy_ref, loss_ref):
        @pl.when(pl.program_id(0) == 0)
        def _():
            loss_ref[...] = jnp.zeros_like(loss_ref)

        diff = y_ref[...] - t_ref[...]
        dy_ref[...] = diff / d
        part = 0.5 * jnp.sum(_mean(diff * diff), axis=0, keepdims=True)
        loss_ref[...] += jnp.broadcast_to(part, loss_ref.shape)

    blk = pl.BlockSpec((tm, d), lambda i: (i, 0))
    return pl.pallas_call(
        body, name="loss_grad", grid=(t // tm,),
        out_shape=[jax.ShapeDtypeStruct((t, d), F32), jax.ShapeDtypeStruct((8, LANES), F32)],
        in_specs=[blk, blk], out_specs=[blk, pl.BlockSpec((8, LANES), lambda i: (0, 0))],
        compiler_params=_params(1),
    )(y, target)


def _matmul_tn(name, a, b, tk, out_dtype):
    ga, t, m = a.shape
    gb, _, n = b.shape
    g = max(ga, gb)
    n_k = t // tk

    def body(a_ref, b_ref, o_ref, acc_ref):
        k = pl.program_id(1)

        @pl.when(k == 0)
        def _():
            acc_ref[...] = jnp.zeros_like(acc_ref)

        acc_ref[...] += _mm_tn(a_ref[...], b_ref[...])

        @pl.when(k == n_k - 1)
        def _():
            o_ref[...] = acc_ref[...].astype(out_dtype)

    return pl.pallas_call(
        body, name=name, grid=(g, n_k),
        out_shape=jax.ShapeDtypeStruct((g, m, n), out_dtype),
        in_specs=[pl.BlockSpec((None, tk, m), (lambda gi, k: (gi, k, 0)) if ga > 1 else (lambda gi, k: (0, k, 0))),
                  pl.BlockSpec((None, tk, n), (lambda gi, k: (gi, k, 0)) if gb > 1 else (lambda gi, k: (0, k, 0)))],
        out_specs=pl.BlockSpec((None, m, n), lambda gi, k: (gi, 0, 0)),
        scratch_shapes=[pltpu.VMEM((m, n), F32)],
        compiler_params=_params(2),
    )(a, b)


def _time_of_row(tm):
    i = lax.broadcasted_iota(jnp.int32, (tm, 1), 0)
    return (i % 8) * (tm // 8) + i // 8


def _sublane_is(rows, s):
    return lax.broadcasted_iota(jnp.int32, (rows, 1), 0) % 8 == s


def _conv_taps(src_ref, col0, ncols, tm, tap_rows, weight_of, init_of, store):
    rc = min(ROW_CHUNK, tm)
    for cb in range(ncols // LANES):
        cs = slice(cb * LANES, (cb + 1) * LANES)
        ss = slice(col0 + cb * LANES, col0 + (cb + 1) * LANES)
        for r0 in range(0, tm, rc):
            acc = init_of(cs, rc)
            for k, row in enumerate(tap_rows):
                acc = acc + weight_of(k, cs) * src_ref[r0 + row:r0 + row + rc, ss]
            store(r0, rc, cs, acc)


def _mixer_fwd(l, x, mod, gains, w_in_t, conv_w, vec, pool_w, w_out, *, seq, tm, act_dtype, payload=()):
    t, d = x.shape
    d_in = w_in_t.shape[0]
    dc = conv_w.shape[-1]
    d_mix = w_out.shape[0]
    n_taps = 31
    nt, tps = t // tm, seq // tm
    p = CONV_PREFIX
    n_pay = len(payload)

    def body(*refs):
        ins, outs, scr, pay_refs, got_refs, sems = _carried_refs(refs, 8, 7, 2, n_pay, aliased=False)
        x_ref, mod_ref, g_ref, win_ref, cw_ref, v_ref, pw_ref, wout_ref = ins
        x1_ref, h1_ref, u_ref, a1_ref, ap_ref, o_ref, tail_ref = outs
        ext_ref, car_ref = scr
        i = pl.program_id(0)
        first = (i % tps) == 0

        if n_pay:
            @pl.when(i == 0)
            def _():
                _gather_phase(pay_refs, got_refs, *sems, 0)

        xv = x_ref[...]
        r = lax.rsqrt(_mean(xv * xv) + EPS)
        hv = (xv * r) * g_ref[0:1, :] * (1.0 + mod_ref[1:2, :]) + mod_ref[0:1, :]
        hb = hv.astype(act_dtype)
        h1_ref[...] = hb
        u = _mm_nt(hb, win_ref[...])
        u_ref[...] = u
        a0 = u[:, :dc] * _sigmoid(u[:, dc:2 * dc])

        @pl.when(first)
        def _():
            car_ref[...] = jnp.zeros_like(car_ref)

        @pl.when(i == 0)
        def _():
            ext_ref[p + tm:p + tm + 8, :] = jnp.zeros((8, ext_ref.shape[1]), F32)

        ext_ref[p:p + tm, 0:dc] = a0
        ext_ref[p:p + tm, dc:] = u[:, 2 * dc:]
        ext_ref[0:p, :] = jnp.where(_sublane_is(p, 0), car_ref[...], ext_ref[tm - 1:tm - 1 + p, :])

        def store(r0, rc, cs, acc):
            a1_ref[r0:r0 + rc, cs] = acc

        _conv_taps(ext_ref, 0, dc, tm, [p - 8 * (n_taps - 1 - k) for k in range(n_taps)],
                   lambda k, cs: cw_ref[k:k + 1, cs],
                   lambda cs, rc: jnp.broadcast_to(v_ref[0:1, cs], (rc, LANES)), store)
        a1 = a1_ref[...]
        mu = _mean(a1)
        xc = a1 - mu
        rstd = lax.rsqrt(_mean(xc * xc) + EPS)
        a2 = (xc * rstd) * v_ref[1:2, :] + v_ref[2:3, :]
        ap_ref[:, 0:dc] = (a2 * _sigmoid(a2)).astype(act_dtype)

        pos = (i % tps) * tm + _time_of_row(tm)
        for g, w in enumerate(POOL_WINDOWS):
            cs = slice(dc + g * POOL_GROUP, dc + (g + 1) * POOL_GROUP)
            s = ext_ref[p:p + tm, cs]
            for j in range(1, w):
                s = s + ext_ref[p - 8 * j:p - 8 * j + tm, cs]
            cnt = jnp.minimum(pos + 1, w).astype(F32)
            dv = s / cnt - ext_ref[p:p + tm, cs]
            q = _mm(dv, pw_ref[g])
            ap_ref[:, cs] = (q * v_ref[3:4, g * POOL_GROUP:(g + 1) * POOL_GROUP]).astype(act_dtype)

        o = _mm(ap_ref[...], wout_ref[...])
        o_ref[...] = o
        ro = lax.rsqrt(_mean(o * o) + EPS)
        x1_ref[...] = xv + (1.0 + mod_ref[2:3, :]) * ((o * ro) * g_ref[1:2, :])

        nxt = ext_ref[tm + 7:tm + 7 + p, :]
        car_ref[...] = nxt
        tail_ref[...] = nxt

        if n_pay:
            @pl.when(i == max(nt - 2, 0))
            def _():
                _gather_phase(pay_refs, got_refs, *sems, 1)

            @pl.when(i == nt - 1)
            def _():
                _gather_phase(pay_refs, got_refs, *sems, 2)

    row = lambda width: pl.BlockSpec((tm, width), lambda i: (i, 0))
    return _carrier_call(
        body, f"mixer_fwd_{l}", (nt,),
        out_shape=[jax.ShapeDtypeStruct((t, d), F32), jax.ShapeDtypeStruct((t, d), act_dtype),
                   jax.ShapeDtypeStruct((t, d_in), F32), jax.ShapeDtypeStruct((t, dc), F32),
                   jax.ShapeDtypeStruct((t, d_mix), act_dtype), jax.ShapeDtypeStruct((t, d), F32),
                   jax.ShapeDtypeStruct((nt, p, d_mix), F32)],
        in_specs=[row(d),
                  pl.BlockSpec((None, None, 8, d), lambda i: (l, i // tps, 0, 0)),
                  _const_spec((None, 8, d), (l, 0, 0)),
                  _const_spec((d_in, d), (0, 0)),
                  _const_spec((32, dc), (0, 0)),
                  _const_spec((None, 8, dc), (l, 0, 0)),
                  _const_spec((None, len(POOL_WINDOWS), POOL_GROUP, POOL_GROUP), (l, 0, 0, 0)),
                  _const_spec((d_mix, d), (0, 0))],
        out_specs=[row(d), row(d), row(d_in), row(dc), row(d_mix), row(d),
                   pl.BlockSpec((None, p, d_mix), lambda i: (i, 0, 0))],
        scratch=[pltpu.VMEM((p + tm + 8, d_mix), F32), pltpu.VMEM((p, d_mix), F32)],
        operands=(x, mod, gains, w_in_t, conv_w, vec, pool_w, w_out),
        payload=payload)


def _mixer_bwd(l, dx1, x, mod, gains, u, tails, a1, o, w_in_t, conv_w, vec, pool_w, w_out, *, seq, tm, act_dtype,
               payload=(), recvs=(), rows=(), pay_layer=0):
    t, d = x.shape
    d_in = w_in_t.shape[0]
    dc = conv_w.shape[-1]
    d_mix = w_out.shape[0]
    n_taps = 31
    nt, tps = t // tm, seq // tm
    p = CONV_PREFIX
    n_groups = len(POOL_WINDOWS)
    n_pay = len(payload)

    def body(*refs):
        ins, outs, scr, pay_refs, recv_refs, sems = _carried_refs(refs, 13, 8, 3, n_pay, aliased=True)
        dx1_ref, x_ref, mod_ref, g_ref, u_ref, tail_ref, a1_ref, o_ref, win_ref, cw_ref, v_ref, pw_ref, wout_ref = ins
        dx_ref, du_ref, do_ref, dmod_ref, gd_ref, gv_ref, dcw_ref, dpw_ref = outs
        ext_ref, fext_ref, fcar_ref = scr
        i = pl.program_id(0)
        j = nt - 1 - i
        first_in_seq = (j % tps) == 0
        last_in_seq = (j % tps) == tps - 1

        if n_pay:
            @pl.when(i == 0)
            def _():
                _exchange(pay_refs, recv_refs, rows, pay_layer, *sems, start=True)

        @pl.when(i == 0)
        def _():
            gd_ref[...] = jnp.zeros_like(gd_ref)
            gv_ref[...] = jnp.zeros_like(gv_ref)
            dcw_ref[...] = jnp.zeros_like(dcw_ref)
            dpw_ref[...] = jnp.zeros_like(dpw_ref)
            fext_ref[0:8, :] = jnp.zeros((8, fext_ref.shape[1]), F32)

        @pl.when(last_in_seq)
        def _():
            dmod_ref[...] = jnp.zeros_like(dmod_ref)
            fcar_ref[...] = jnp.zeros_like(fcar_ref)

        xv = x_ref[...]
        dx1v = dx1_ref[...]
        pre_g, post_g = g_ref[0:1, :], g_ref[1:2, :]

        ov = o_ref[...]
        ro = lax.rsqrt(_mean(ov * ov) + EPS)
        yo = ov * ro
        dmod_ref[2:3, :] += _colsum(dx1v * (yo * post_g))
        dn = dx1v * (1.0 + mod_ref[2:3, :])
        gd_ref[1:2, :] += _colsum(dn * yo)
        dyo = dn * post_g
        do = ro * (dyo - yo * _mean(dyo * yo))
        dob = do.astype(act_dtype)
        do_ref[...] = dob
        dap = _mm_nt(dob, wout_ref[...])

        uv = u_ref[...]
        val, gate = uv[:, :dc], uv[:, dc:2 * dc]
        sg = _sigmoid(gate)
        ext_ref[p:p + tm, 0:dc] = val * sg
        ext_ref[p:p + tm, dc:] = uv[:, 2 * dc:]
        keep = jnp.where(first_in_seq, 0.0, 1.0).astype(F32)
        ext_ref[0:p, :] = jnp.where(_sublane_is(p, 0), tail_ref[...] * keep, ext_ref[tm - 1:tm - 1 + p, :])

        a1v = a1_ref[...]
        mu = _mean(a1v)
        xc = a1v - mu
        rstd = lax.rsqrt(_mean(xc * xc) + EPS)
        xh = xc * rstd
        ln_g = v_ref[1:2, :]
        a2 = xh * ln_g + v_ref[2:3, :]
        s2 = _sigmoid(a2)
        da2 = dap[:, :dc] * (s2 * (1.0 + a2 * (1.0 - s2)))
        gv_ref[1:2, :] += _colsum(da2 * xh)
        gv_ref[2:3, :] += _colsum(da2)
        dxh = da2 * ln_g
        da1 = rstd * (dxh - _mean(dxh) - xh * _mean(dxh * xh))
        gv_ref[0:1, :] += _colsum(da1)
        last_sublane = _sublane_is(p, 7)

        def put(cs, value):
            fext_ref[8:8 + tm, cs] = value
            fext_ref[8 + tm:8 + tm + p, cs] = jnp.where(last_sublane, fcar_ref[:, cs], fext_ref[9:9 + p, cs])

        put(slice(0, dc), da1)

        for k in range(n_taps):
            row = p - 8 * (n_taps - 1 - k)
            dcw_ref[k:k + 1, :] += _colsum(fext_ref[8:8 + tm, 0:dc] * ext_ref[row:row + tm, 0:dc])

        def store(r0, rc, cs, acc):
            sgc = _sigmoid(u_ref[r0:r0 + rc, dc + cs.start:dc + cs.stop])
            vc = u_ref[r0:r0 + rc, cs]
            du_ref[r0:r0 + rc, cs] = (acc * sgc).astype(act_dtype)
            du_ref[r0:r0 + rc, dc + cs.start:dc + cs.stop] = (acc * vc * sgc * (1.0 - sgc)).astype(act_dtype)

        _conv_taps(fext_ref, 0, dc, tm, [8 + 8 * (n_taps - 1 - k) for k in range(n_taps)],
                   lambda k, cs: cw_ref[k:k + 1, cs],
                   lambda cs, rc: jnp.zeros((rc, LANES), F32), store)

        pos = (j % tps) * tm + _time_of_row(tm)
        for g, w in enumerate(POOL_WINDOWS):
            cs = slice(dc + g * POOL_GROUP, dc + (g + 1) * POOL_GROUP)
            gs = slice(g * POOL_GROUP, (g + 1) * POOL_GROUP)
            s = ext_ref[p:p + tm, cs]
            for jj in range(1, w):
                s = s + ext_ref[p - 8 * jj:p - 8 * jj + tm, cs]
            cnt = jnp.minimum(pos + 1, w).astype(F32)
            dv = (s / cnt - ext_ref[p:p + tm, cs]).astype(MXU_DTYPE)
            q = _mm(dv, pw_ref[g])
            dp = dap[:, cs]
            gv_ref[3:4, gs] += _colsum(dp * q)
            dq = (dp * v_ref[3:4, gs]).astype(MXU_DTYPE)
            dpw_ref[g] += _mm_tn(dv, dq)
            dd = _mm_nt(dq, pw_ref[g])
            put(cs, dd / cnt)
            dhp = fext_ref[8:8 + tm, cs]
            for jj in range(1, w):
                dhp = dhp + fext_ref[8 + 8 * jj:8 + 8 * jj + tm, cs]
            du_ref[:, dc + cs.start:dc + cs.stop] = (dhp - dd).astype(act_dtype)

        fcar_ref[...] = fext_ref[1:1 + p, :]

        dh = _mm(du_ref[...], win_ref[...])

        r = lax.rsqrt(_mean(xv * xv) + EPS)
        xn = xv * r
        dmod_ref[0:1, :] += _colsum(dh)
        dmod_ref[1:2, :] += _colsum(dh * (xn * pre_g))
        dy = dh * (1.0 + mod_ref[1:2, :])
        gd_ref[0:1, :] += _colsum(dy * xn)
        dxn = dy * pre_g
        dx_ref[...] = dx1v + r * (dxn - xn * _mean(dxn * xn))

        if n_pay:
            @pl.when(i == nt - 1)
            def _():
                _exchange(pay_refs, recv_refs, rows, pay_layer, *sems, start=False)

    rev = lambda width: pl.BlockSpec((tm, width), lambda i: (nt - 1 - i, 0))
    return _carrier_call(
        body, f"mixer_bwd_{l}", (nt,),
        out_shape=[jax.ShapeDtypeStruct((t, d), F32), jax.ShapeDtypeStruct((t, d_in), act_dtype),
                   jax.ShapeDtypeStruct((t, d), act_dtype),
                   jax.ShapeDtypeStruct((t // seq, 8, d), F32), jax.ShapeDtypeStruct((8, d), F32),
                   jax.ShapeDtypeStruct((8, dc), F32), jax.ShapeDtypeStruct((32, dc), F32),
                   jax.ShapeDtypeStruct((n_groups, POOL_GROUP, POOL_GROUP), F32)],
        in_specs=[rev(d), rev(d),
                  pl.BlockSpec((None, None, 8, d), lambda i: (l, (nt - 1 - i) // tps, 0, 0)),
                  _const_spec((None, 8, d), (l, 0, 0)),
                  rev(d_in),
                  pl.BlockSpec((None, p, d_mix), lambda i: (jnp.maximum(nt - 2 - i, 0), 0, 0)),
                  rev(dc), rev(d),
                  _const_spec((d_in, d), (0, 0)),
                  _const_spec((32, dc), (0, 0)),
                  _const_spec((None, 8, dc), (l, 0, 0)),
                  _const_spec((None, n_groups, POOL_GROUP, POOL_GROUP), (l, 0, 0, 0)),
                  _const_spec((d_mix, d), (0, 0))],
        out_specs=[rev(d), rev(d_in), rev(d),
                   pl.BlockSpec((None, 8, d), lambda i: ((nt - 1 - i) // tps, 0, 0)),
                   _acc_spec((8, d), (0, 0)), _acc_spec((8, dc), (0, 0)), _acc_spec((32, dc), (0, 0)),
                   _acc_spec((n_groups, POOL_GROUP, POOL_GROUP), (0, 0, 0))],
        scratch=[pltpu.VMEM((p + tm, d_mix), F32), pltpu.VMEM((8 + tm + p, d_mix), F32),
                 pltpu.VMEM((p, d_mix), F32)],
        operands=(dx1, x, mod, gains, u, tails, a1, o, w_in_t, conv_w, vec, pool_w, w_out),
        payload=payload, recvs=recvs)


def _ffn_fwd(l, x, mod, gains, up, fcw, down, *, seq, tm, act_dtype, payload=()):
    t, d = x.shape
    n_chunks, _, fc = up.shape
    half = n_chunks // 2
    nt, tps = t // tm, seq // tm
    p = FFN_PREFIX
    n_pay = len(payload)

    def body(*refs):
        ins, outs, scr, pay_refs, got_refs, sems = _carried_refs(refs, 6, 6, 2, n_pay, aliased=False)
        x_ref, mod_ref, g_ref, up_ref, fcw_ref, down_ref = ins
        x2_ref, h2_ref, u2_ref, u3_ref, hid_ref, o2_ref = outs
        ext_ref, car_ref = scr
        i = pl.program_id(0)
        first = (i % tps) == 0

        if n_pay:
            @pl.when(i == 0)
            def _():
                _gather_phase(pay_refs, got_refs, *sems, 0)

        xv = x_ref[...]
        r = lax.rsqrt(_mean(xv * xv) + EPS)
        hv = (xv * r) * g_ref[0:1, :] * (1.0 + mod_ref[4:5, :]) + mod_ref[3:4, :]
        hb = hv.astype(act_dtype)
        h2_ref[...] = hb

        @pl.when(first)
        def _():
            car_ref[...] = jnp.zeros_like(car_ref)

        @pl.when(i == 0)
        def _():
            for n in range(n_chunks):
                ext_ref[n, p + tm:p + tm + 8, :] = jnp.zeros((8, fc), F32)

        first_sublane = _sublane_is(p, 0)
        for n in range(n_chunks):
            un = _mm(hb, up_ref[n])
            ext_ref[n, p:p + tm, :] = un
            u2_ref[n] = un.astype(act_dtype)
            ext_ref[n, 0:p, :] = jnp.where(first_sublane, car_ref[n], ext_ref[n, tm - 1:tm - 1 + p, :])
            car_ref[n] = ext_ref[n, tm + 7:tm + 7 + p, :]

        def conv(n):
            return (fcw_ref[n, 3:4, :] + fcw_ref[n, 0:1, :] * ext_ref[n, p - 16:p - 16 + tm, :]
                    + fcw_ref[n, 1:2, :] * ext_ref[n, p - 8:p - 8 + tm, :]
                    + fcw_ref[n, 2:3, :] * ext_ref[n, p:p + tm, :])

        o2 = jnp.zeros((tm, d), F32)
        for n in range(half):
            gt = conv(n + half)
            v = conv(n)
            u3_ref[n] = v.astype(act_dtype)
            u3_ref[n + half] = gt.astype(act_dtype)
            hid = ((gt * _sigmoid(gt)) * v).astype(act_dtype)
            hid_ref[n] = hid
            o2 = o2 + _mm(hid, down_ref[n])
        o2_ref[...] = o2
        ro = lax.rsqrt(_mean(o2 * o2) + EPS)
        x2_ref[...] = xv + (1.0 + mod_ref[5:6, :]) * ((o2 * ro) * g_ref[1:2, :])

        if n_pay:
            @pl.when(i == max(nt - 2, 0))
            def _():
                _gather_phase(pay_refs, got_refs, *sems, 1)

            @pl.when(i == nt - 1)
            def _():
                _gather_phase(pay_refs, got_refs, *sems, 2)

    row = lambda width: pl.BlockSpec((tm, width), lambda i: (i, 0))
    chunked = lambda n: pl.BlockSpec((n, tm, fc), lambda i: (0, i, 0))
    return _carrier_call(
        body, f"ffn_fwd_{l}", (nt,),
        out_shape=[jax.ShapeDtypeStruct((t, d), F32), jax.ShapeDtypeStruct((t, d), act_dtype),
                   jax.ShapeDtypeStruct((n_chunks, t, fc), act_dtype),
                   jax.ShapeDtypeStruct((n_chunks, t, fc), act_dtype),
                   jax.ShapeDtypeStruct((half, t, fc), act_dtype), jax.ShapeDtypeStruct((t, d), F32)],
        in_specs=[row(d),
                  pl.BlockSpec((None, None, 8, d), lambda i: (l, i // tps, 0, 0)),
                  _const_spec((None, 8, d), (l, 0, 0)),
                  _const_spec((n_chunks, d, fc), (0, 0, 0)),
                  _const_spec((n_chunks, 8, fc), (0, 0, 0)),
                  _const_spec((half, fc, d), (0, 0, 0))],
        out_specs=[row(d), row(d), chunked(n_chunks), chunked(n_chunks), chunked(half), row(d)],
        scratch=[pltpu.VMEM((n_chunks, p + tm + 8, fc), F32), pltpu.VMEM((n_chunks, p, fc), F32)],
        operands=(x, mod, gains, up, fcw, down),
        payload=payload)


def _ffn_bwd(l, dx2, x, mod, gains, u2, u3, o2, up, fcw, down, *, seq, tm, act_dtype,
             payload=(), recvs=(), rows=(), pay_layer=0):
    t, d = x.shape
    n_chunks, _, fc = up.shape
    half = n_chunks // 2
    nt, tps = t // tm, seq // tm
    p = FFN_PREFIX
    n_pay = len(payload)

    def body(*refs):
        ins, outs, scr, pay_refs, recv_refs, sems = _carried_refs(refs, 10, 6, 2, n_pay, aliased=True)
        dx2_ref, x_ref, mod_ref, g_ref, u2_ref, u3_ref, o2_ref, up_ref, fcw_ref, down_ref = ins
        dx1_ref, du2_ref, do2_ref, dmod_ref, gd_ref, dfcw_ref = outs
        fext_ref, fcar_ref = scr
        i = pl.program_id(0)
        j = nt - 1 - i
        last_in_seq = (j % tps) == tps - 1

        if n_pay:
            @pl.when(i == 0)
            def _():
                _exchange(pay_refs, recv_refs, rows, pay_layer, *sems, start=True)

        @pl.when(i == 0)
        def _():
            gd_ref[...] = jnp.zeros_like(gd_ref)
            dfcw_ref[...] = jnp.zeros_like(dfcw_ref)
            for n in range(n_chunks):
                fext_ref[n, 0:8, :] = jnp.zeros((8, fc), F32)

        @pl.when(last_in_seq)
        def _():
            dmod_ref[...] = jnp.zeros_like(dmod_ref)
            fcar_ref[...] = jnp.zeros_like(fcar_ref)

        xv = x_ref[...]
        dx2v = dx2_ref[...]
        pre_g, post_g = g_ref[0:1, :], g_ref[1:2, :]

        ov = o2_ref[...]
        ro = lax.rsqrt(_mean(ov * ov) + EPS)
        yo = ov * ro
        dmod_ref[2:3, :] += _colsum(dx2v * (yo * post_g))
        dn = dx2v * (1.0 + mod_ref[5:6, :])
        gd_ref[1:2, :] += _colsum(dn * yo)
        dyo = dn * post_g
        do = ro * (dyo - yo * _mean(dyo * yo))
        dob = do.astype(act_dtype)
        do2_ref[...] = dob

        for n in range(half):
            v = u3_ref[n].astype(F32)
            gt = u3_ref[n + half].astype(F32)
            sg = _sigmoid(gt)
            dhid = _mm_nt(dob, down_ref[n])
            fext_ref[n, 8:8 + tm, :] = dhid * (gt * sg)
            fext_ref[n + half, 8:8 + tm, :] = dhid * v * (sg * (1.0 + gt * (1.0 - sg)))

        last_sublane = _sublane_is(p, 7)
        dh = jnp.zeros((tm, d), F32)
        for n in range(n_chunks):
            fext_ref[n, 8 + tm:8 + tm + p, :] = jnp.where(last_sublane, fcar_ref[n], fext_ref[n, 9:9 + p, :])
            fcar_ref[n] = fext_ref[n, 1:1 + p, :]
            d2 = fext_ref[n, 8:8 + tm, :]
            d1 = fext_ref[n, 16:16 + tm, :]
            d0 = fext_ref[n, 24:24 + tm, :]
            u2v = u2_ref[n].astype(F32)
            dfcw_ref[n, 3:4, :] += _colsum(d2)
            dfcw_ref[n, 0:1, :] += _colsum(d0 * u2v)
            dfcw_ref[n, 1:2, :] += _colsum(d1 * u2v)
            dfcw_ref[n, 2:3, :] += _colsum(d2 * u2v)
            du2 = (fcw_ref[n, 0:1, :] * d0 + fcw_ref[n, 1:2, :] * d1 + fcw_ref[n, 2:3, :] * d2).astype(act_dtype)
            du2_ref[n] = du2
            dh = dh + _mm_nt(du2, up_ref[n])

        r = lax.rsqrt(_mean(xv * xv) + EPS)
        xn = xv * r
        dmod_ref[0:1, :] += _colsum(dh)
        dmod_ref[1:2, :] += _colsum(dh * (xn * pre_g))
        dy = dh * (1.0 + mod_ref[4:5, :])
        gd_ref[0:1, :] += _colsum(dy * xn)
        dxn = dy * pre_g
        dx1_ref[...] = dx2v + r * (dxn - xn * _mean(dxn * xn))

        if n_pay:
            @pl.when(i == nt - 1)
            def _():
                _exchange(pay_refs, recv_refs, rows, pay_layer, *sems, start=False)

    rev = lambda width: pl.BlockSpec((tm, width), lambda i: (nt - 1 - i, 0))
    chunked = pl.BlockSpec((n_chunks, tm, fc), lambda i: (0, nt - 1 - i, 0))
    return _carrier_call(
        body, f"ffn_bwd_{l}", (nt,),
        out_shape=[jax.ShapeDtypeStruct((t, d), F32), jax.ShapeDtypeStruct((n_chunks, t, fc), act_dtype),
                   jax.ShapeDtypeStruct((t, d), act_dtype),
                   jax.ShapeDtypeStruct((t // seq, 8, d), F32), jax.ShapeDtypeStruct((8, d), F32),
                   jax.ShapeDtypeStruct((n_chunks, 8, fc), F32)],
        in_specs=[rev(d), rev(d),
                  pl.BlockSpec((None, None, 8, d), lambda i: (l, (nt - 1 - i) // tps, 0, 0)),
                  _const_spec((None, 8, d), (l, 0, 0)),
                  chunked, chunked, rev(d),
                  _const_spec((n_chunks, d, fc), (0, 0, 0)),
                  _const_spec((n_chunks, 8, fc), (0, 0, 0)),
                  _const_spec((half, fc, d), (0, 0, 0))],
        out_specs=[rev(d), chunked, rev(d),
                   pl.BlockSpec((None, 8, d), lambda i: ((nt - 1 - i) // tps, 0, 0)),
                   _acc_spec((8, d), (0, 0)), _acc_spec((n_chunks, 8, fc), (0, 0, 0))],
        scratch=[pltpu.VMEM((n_chunks, 8 + tm + p, fc), F32), pltpu.VMEM((n_chunks, p, fc), F32)],
        operands=(dx2, x, mod, gains, u2, u3, o2, up, fcw, down),
        payload=payload, recvs=recvs)


def _pad_rows(a, rows):
    pad = [(0, 0)] * a.ndim
    pad[-2] = (0, rows - a.shape[-2])
    return jnp.pad(a, pad)


def kernel(x, c, ada_w, ada_b, pre_mix_g, post_mix_g, w_in, conv_w, conv_b, conv_ln_g, conv_ln_b, pool_w, pool_scale, w_out, pre_ffn_g, post_ffn_g, ffn_up, ffn_conv_w, ffn_conv_b, ffn_down, loss_target, m_ada_w, m_ada_b, m_pre_mix_g, m_post_mix_g, m_w_in, m_conv_w, m_conv_b, m_conv_ln_g, m_conv_ln_b, m_pool_w, m_pool_scale, m_w_out, m_pre_ffn_g, m_post_ffn_g, m_ffn_up, m_ffn_conv_w, m_ffn_conv_b, m_ffn_down, v_ada_w, v_ada_b, v_pre_mix_g, v_post_mix_g, v_w_in, v_conv_w, v_conv_b, v_conv_ln_g, v_conv_ln_b, v_pool_w, v_pool_scale, v_w_out, v_pre_ffn_g, v_post_ffn_g, v_ffn_up, v_ffn_conv_w, v_ffn_conv_b, v_ffn_down):
    bl, seq, d = x.shape
    n_layers = ada_w.shape[0]
    t = bl * seq
    dc = conv_b.shape[1]
    d_in = w_in.shape[2] * N_DEV
    d_mix = w_out.shape[1] * N_DEV
    n_taps = conv_w.shape[1]
    fc = ffn_up.shape[2]
    half = N_DEV // 2
    ada_cols = ada_w.shape[2]
    n_mod = ada_cols * N_DEV // d
    assert pool_scale.shape[1] == dc and n_taps == 31 and n_mod == 6 and ffn_conv_w.shape[1] == 3
    assert pool_w.shape[1:] == (len(POOL_WINDOWS), POOL_GROUP, POOL_GROUP)
    tm = TILE_TOKENS
    assert seq % tm == 0 and CONV_PREFIX <= tm - 8
    tk = 2048 if t % 2048 == 0 else tm
    act = MXU_DTYPE

    def tile_order(a, inverse=False):
        shape = (t // tm, tm // 8, 8, d) if inverse else (t // tm, 8, tm // 8, d)
        return a.reshape(shape).transpose(0, 2, 1, 3).reshape(t, d)

    ax = lax.axis_index
    me = 4 * ax("x") + 2 * ax("y") + ax("c")

    (c_all,) = _all_gather([c], "gather_c")
    c_all = c_all.reshape(N_DEV * bl, d)
    ada_b_cols = lax.dynamic_slice_in_dim(ada_b, me * ada_cols, ada_cols, axis=1)
    mod_cols = _ada_fwd(c_all, ada_w, ada_b_cols)

    w_in_s = w_in.astype(act).transpose(0, 2, 1)
    w_out_s, up_s, down_s = w_out.astype(act), ffn_up.astype(act), ffn_down.astype(act)
    conv_w_s = _pad_rows(conv_w, 32)
    fcw_s = _pad_rows(jnp.concatenate(
        [ffn_conv_w, lax.dynamic_slice_in_dim(ffn_conv_b, me * fc, fc, axis=1)[:, None, :]], axis=1), 8)

    def early_shards(l):
        return (w_in_s[l], w_out_s[l], conv_w_s[l], down_s[l], fcw_s[l])

    def early_weights(g_w_in, g_w_out, g_conv_w, g_down, g_fcw):
        mixer = (g_w_in.reshape(d_in, d), g_conv_w.transpose(1, 0, 2).reshape(32, dc), g_w_out.reshape(d_mix, d))
        return mixer, (g_fcw, g_down.reshape(half, fc, d))

    g0 = _all_gather([mod_cols, *early_shards(0)], "gather_w0")
    mod_all = g0[0].transpose(1, 2, 0, 3).reshape(n_layers, N_DEV * bl, n_mod, d)
    mod = _pad_rows(lax.dynamic_slice_in_dim(mod_all, me * bl, bl, axis=1), 8)
    early = [early_weights(*g0[1:])]
    ups = []
    gains_mix = _pad_rows(jnp.stack([pre_mix_g, post_mix_g], axis=1), 8)
    gains_ffn = _pad_rows(jnp.stack([pre_ffn_g, post_ffn_g], axis=1), 8)
    vec = _pad_rows(jnp.stack([conv_b, conv_ln_g, conv_ln_b, pool_scale], axis=1), 8)
    pool_w_b = pool_w.astype(act)

    kw = dict(seq=seq, tm=tm, act_dtype=act)
    xs = tile_order(x.reshape(t, d))
    saved = []
    for l in range(n_layers):
        (w_in_t, cw, w_o), (fcw_l, down_l) = early[l]
        (x1, h1, u, a1, ap, o, tails), (up_l,) = _mixer_fwd(
            l, xs, mod, gains_mix, w_in_t, cw, vec, pool_w_b, w_o, **kw, payload=(up_s[l],))
        ups.append(up_l)
        (x2, h2, u2, u3, hid, o2), got = _ffn_fwd(l, x1, mod, gains_ffn, up_l, fcw_l, down_l, **kw,
                                                  payload=early_shards(l + 1) if l + 1 < n_layers else ())
        if got:
            early.append(early_weights(*got))
        saved.append((xs, h1, u, tails, a1, ap, o, x1, h2, u2, u3, hid, o2))
        xs = x2

    dx, loss_part = _loss_grad(xs, tile_order(loss_target.reshape(t, d)), tm)

    def landing(shard, dtype):
        return lax.empty((N_DEV, n_layers) + shard.shape[1:], dtype)

    r_up, r_down = landing(ffn_up, act), landing(ffn_down, act)
    r_w_in, r_w_out = landing(w_in, act), landing(w_out, act)
    r_conv_w, r_fcw = landing(conv_w, F32), landing(ffn_conv_w, F32)
    split = (2 * d // 5) // 16 * 16
    late_rows = ((split, d - split), None, None, None, None)
    dmods, smalls = [], []
    pending = ()
    for l in reversed(range(n_layers)):
        xin, h1, u, tails, a1, ap, o, x1, h2, u2, u3, hid, o2 = saved[l]
        (w_in_t, cw, w_o), (fcw_l, down_l) = early[l]
        up_l = ups[l]
        (dx1, du2, do2, dmod_b, gd_b, dfcw), got = _ffn_bwd(
            l, dx, x1, mod, gains_ffn, u2, u3, o2, up_l, fcw_l, down_l, **kw,
            payload=pending, recvs=(r_up, r_w_in, r_w_out, r_conv_w, r_fcw) if pending else (),
            rows=late_rows, pay_layer=l + 1)
        if pending:
            r_up, r_w_in, r_w_out, r_conv_w, r_fcw = got
        p_up = _matmul_tn(f"dw_up_{l}", h2[None], du2, tk, act)
        p_down = _matmul_tn(f"dw_down_{l}", hid, do2[None], tk, act).reshape(N_DEV, fc // 2, d)
        (dx, du, do, dmod_a, gd_a, gv, dcw, dpw), (r_down, r_up) = _mixer_bwd(
            l, dx1, xin, mod, gains_mix, u, tails, a1, o, w_in_t, cw, vec, pool_w_b, w_o, **kw,
            payload=(p_down, p_up), recvs=(r_down, r_up), rows=(None, (0, split)), pay_layer=l)
        dw_in = _matmul_tn(f"dw_in_{l}", h1[None], du[None], tk, act)[0]
        p_w_in = dw_in.reshape(d, N_DEV, d_in // N_DEV).transpose(1, 0, 2)
        p_w_out = _matmul_tn(f"dw_out_{l}", ap[None], do[None], tk, act)[0].reshape(N_DEV, d_mix // N_DEV, d)
        p_conv_w = dcw[:n_taps].reshape(n_taps, N_DEV, dc // N_DEV).transpose(1, 0, 2)
        pending = (p_up, p_w_in, p_w_out, p_conv_w, dfcw[:, :3, :])
        dmods.append(jnp.concatenate([dmod_a[:, 0:3], dmod_b[:, 0:3]], axis=1).reshape(bl, n_mod * d))
        smalls.append(jnp.concatenate(
            [gd_a[0], gd_a[1], gv[0], gv[1], gv[2], gv[3], gd_b[0], gd_b[1], dfcw[:, 3, :].reshape(-1),
             dpw.reshape(-1)]))
    dmods.reverse()
    smalls.reverse()

    dmod_loc = jnp.stack(dmods)
    small_loc = jnp.stack(smalls)
    n_small = small_loc.shape[1]
    small_cols = 8 * LANES if (n_layers * n_small) % (8 * LANES) == 0 else LANES
    (r_up, r_w_in, r_w_out, r_conv_w, r_fcw), (g_dmod, g_small, g_loss) = _exchange_and_gather(
        pending, (r_up, r_w_in, r_w_out, r_conv_w, r_fcw), late_rows, 0,
        [dmod_loc, small_loc.reshape(-1, small_cols), loss_part], "exchange_tail")

    loss = _sum_parts(g_loss)[0, 0]

    def flat2(a):
        return a.reshape(-1, a.shape[-1])

    def update(name, parts, w, m, v):
        outs = _adam_reduce(name, parts.reshape(parts.shape[0], -1, w.shape[-1]), flat2(w), flat2(m), flat2(v))
        return [o_.reshape(w.shape) for o_ in outs]

    res = {}
    res["w_in"] = update("adam_w_in", r_w_in, w_in, m_w_in, v_w_in)
    res["w_out"] = update("adam_w_out", r_w_out, w_out, m_w_out, v_w_out)
    res["ffn_up"] = update("adam_ffn_up", r_up, ffn_up, m_ffn_up, v_ffn_up)
    res["ffn_down"] = update("adam_ffn_down", r_down, ffn_down, m_ffn_down, v_ffn_down)
    res["conv_w"] = update("adam_conv_w", r_conv_w, conv_w, m_conv_w, v_conv_w)
    res["ffn_conv_w"] = update("adam_ffn_conv_w", r_fcw, ffn_conv_w, m_ffn_conv_w, v_ffn_conv_w)

    dmod_all = g_dmod.transpose(1, 0, 2, 3).reshape(n_layers, N_DEV * bl, n_mod * d)
    dmod_cols = lax.dynamic_slice_in_dim(dmod_all, me * ada_cols, ada_cols, axis=2)
    res["ada_w"] = list(_ada_bwd(c_all.T, dmod_cols, ada_w, m_ada_w, v_ada_w))
    res["ada_b"] = update("adam_ada_b", dmod_all.transpose(1, 0, 2), ada_b, m_ada_b, v_ada_b)

    small_names = ["pre_mix_g", "post_mix_g", "conv_b", "conv_ln_g", "conv_ln_b", "pool_scale", "pre_ffn_g",
                   "post_ffn_g", "ffn_conv_b", "pool_w"]
    small_w = [pre_mix_g, post_mix_g, conv_b, conv_ln_g, conv_ln_b, pool_scale, pre_ffn_g, post_ffn_g,
               ffn_conv_b, pool_w]
    small_m = [m_pre_mix_g, m_post_mix_g, m_conv_b, m_conv_ln_g, m_conv_ln_b, m_pool_scale, m_pre_ffn_g,
               m_post_ffn_g, m_ffn_conv_b, m_pool_w]
    small_v = [v_pre_mix_g, v_post_mix_g, v_conv_b, v_conv_ln_g, v_conv_ln_b, v_pool_scale, v_pre_ffn_g,
               v_post_ffn_g, v_ffn_conv_b, v_pool_w]

    def pack(arrs):
        return jnp.concatenate([a.reshape(n_layers, -1) for a in arrs], axis=1).reshape(-1, small_cols)

    outs = _adam_reduce("adam_small", g_small, pack(small_w), pack(small_m), pack(small_v))
    outs = [o_.reshape(n_layers, n_small) for o_ in outs]
    off = 0
    for name, w in zip(small_names, small_w):
        size = w[0].size
        res[name] = [o_[:, off:off + size].reshape(w.shape) for o_ in outs]
        off += size

    order = ["ada_w", "ada_b", "pre_mix_g", "post_mix_g", "w_in", "conv_w", "conv_b", "conv_ln_g", "conv_ln_b",
             "pool_w", "pool_scale", "w_out", "pre_ffn_g", "post_ffn_g", "ffn_up", "ffn_conv_w", "ffn_conv_b",
             "ffn_down"]
    grad_x = tile_order(dx, inverse=True).reshape(bl, seq, d)
    return (loss, grad_x, *[res[n][0] for n in order], *[res[n][1] for n in order],
            *[res[n][2] for n in order], *[res[n][3] for n in order])
```

```python
import jax
import jax.numpy as jnp
from jax import lax
from jax.experimental import pallas as pl
from jax.experimental.pallas import tpu as pltpu

N_DEV = 8
EPS = 1e-6
POOL_WINDOWS = (2, 4, 8, 16)
POOL_GROUP = 128
TILE_TOKENS = 256
CONV_PREFIX = 8 * 30
FFN_PREFIX = 8 * 2
LANES = 128
ROW_CHUNK = 128
MXU_DTYPE = jnp.bfloat16
VMEM_LIMIT = 60 * 1024 * 1024

ADAM_LR = 0.001
ADAM_B1 = 0.9
ADAM_B2 = 0.999
ADAM_EPS = 1e-08
ADAM_WD = 0.01
ADAM_STEP = 10

MESH = pl.DeviceIdType.MESH
F32 = jnp.float32


def _mm(a, b):
    return jnp.dot(a.astype(MXU_DTYPE), b.astype(MXU_DTYPE), preferred_element_type=F32)


def _mm_nt(a, b):
    return lax.dot_general(a.astype(MXU_DTYPE), b.astype(MXU_DTYPE), (((1,), (1,)), ((), ())),
                           preferred_element_type=F32)


def _mm_tn(a, b):
    return lax.dot_general(a.astype(MXU_DTYPE), b.astype(MXU_DTYPE), (((0,), (0,)), ((), ())),
                           preferred_element_type=F32)


def _mean(v):
    return jnp.mean(v, axis=-1, keepdims=True)


def _colsum(v):
    return jnp.sum(v, axis=0, keepdims=True)


def _sigmoid(v):
    return jax.nn.sigmoid(v)


def _params(n_grid=1):
    return pltpu.CompilerParams(dimension_semantics=("arbitrary",) * n_grid, vmem_limit_bytes=VMEM_LIMIT)


def _const_spec(shape, index):
    return pl.BlockSpec(shape, lambda *_: index, pipeline_mode=pl.Buffered(1))


def _acc_spec(shape, index):
    return pl.BlockSpec(shape, lambda *_: index)


def _position():
    x, y, c = lax.axis_index("x"), lax.axis_index("y"), lax.axis_index("c")
    return x, y, c


def _gather_phase(ins, outs, send_sems, recv_sems, local_sems, phase):
    n = len(ins)
    x, y, c = _position()
    me, sibling = (x, y, c), (x, y, 1 - c)
    chips = [(1 - x, y), (x, 1 - y), (1 - x, 1 - y)]

    def slot(k, px, py, pc):
        return outs[k].at[4 * px + 2 * py + pc]

    def copy(k, s, block, to, src=None):
        return pltpu.make_async_remote_copy(
            src_ref=slot(k, *block) if src is None else src, dst_ref=slot(k, *block),
            send_sem=send_sems.at[k, s], recv_sem=recv_sems.at[k, s], device_id=to, device_id_type=MESH)

    mine = [pltpu.make_async_copy(ins[k], slot(k, *me), local_sems.at[k]) for k in range(n)]
    first = []
    for k in range(n):
        first.append(copy(k, 0, me, sibling, src=ins[k]))
        first += [copy(k, 1 + j, me, (*chip, c), src=ins[k]) for j, chip in enumerate(chips)]
    passed = [copy(k, 4 + j, (*chip, c), sibling) for j, chip in enumerate(chips) for k in range(n)]
    if phase == 0:
        for cp in mine + first:
            cp.start()
    elif phase == 1:
        for j, chip in enumerate(chips):
            for k in range(n):
                copy(k, 1 + j, (*chip, c), me).wait_recv()
                copy(k, 4 + j, (*chip, c), sibling).start()
    else:
        for k in range(n):
            copy(k, 0, sibling, me).wait_recv()
            for j, chip in enumerate(chips):
                copy(k, 4 + j, (*chip, 1 - c), me).wait_recv()
        for cp in first + passed:
            cp.wait_send()
        for cp in mine:
            cp.wait()


def _gather_scratch(n):
    return [pltpu.SemaphoreType.DMA((n, 7)), pltpu.SemaphoreType.DMA((n, 7)), pltpu.SemaphoreType.DMA((n,))]


def _all_gather(arrs, name):
    n = len(arrs)

    def body(*refs):
        for phase in range(3):
            _gather_phase(refs[:n], refs[n:2 * n], *refs[2 * n:], phase)

    any_spec = pl.BlockSpec(memory_space=pl.ANY)
    return pl.pallas_call(
        body, name=name,
        out_shape=[jax.ShapeDtypeStruct((N_DEV,) + a.shape, a.dtype) for a in arrs],
        in_specs=[any_spec] * n, out_specs=[any_spec] * n,
        scratch_shapes=_gather_scratch(n),
    )(*arrs)


def _exchange(payload, recvs, rows, layer, send_sems, recv_sems, local_sems, start):
    x, y, c = _position()
    me = 4 * x + 2 * y + c
    for k, (src, recv) in enumerate(zip(payload, recvs)):
        def block(ref, *index):
            return ref.at[index] if rows[k] is None else ref.at[(*index, pl.ds(*rows[k]))]

        local = pltpu.make_async_copy(block(src, me), block(recv, me, layer), local_sems.at[k])
        if start:
            local.start()
        else:
            local.wait()
        for j in range(1, N_DEV):
            px = (1 - x) if (j & 4) else x
            py = (1 - y) if (j & 2) else y
            pc = (1 - c) if (j & 1) else c
            peer = 4 * px + 2 * py + pc
            landing = block(recv, me, layer) if start else block(recv, peer, layer)
            cp = pltpu.make_async_remote_copy(
                src_ref=block(src, peer), dst_ref=landing, send_sem=send_sems.at[k, j - 1],
                recv_sem=recv_sems.at[k, j - 1], device_id=(px, py, pc), device_id_type=MESH)
            if start:
                cp.start()
            else:
                cp.wait()


def _exchange_scratch(n):
    return [pltpu.SemaphoreType.DMA((n, N_DEV - 1)), pltpu.SemaphoreType.DMA((n, N_DEV - 1)),
            pltpu.SemaphoreType.DMA((n,))]


def _exchange_and_gather(payload, recvs, rows, layer, arrs, name):
    n, m = len(payload), len(arrs)

    def body(*refs):
        pay, srcs = refs[:n], refs[2 * n:2 * n + m]
        outs, got = refs[2 * n + m:3 * n + m], refs[3 * n + m:3 * n + 2 * m]
        xsems, gsems = refs[3 * n + 2 * m:3 * n + 2 * m + 3], refs[3 * n + 2 * m + 3:]
        _exchange(pay, outs, rows, layer, *xsems, start=True)
        for phase in range(3):
            _gather_phase(srcs, got, *gsems, phase)
        _exchange(pay, outs, rows, layer, *xsems, start=False)

    any_spec = pl.BlockSpec(memory_space=pl.ANY)
    res = pl.pallas_call(
        body, name=name,
        out_shape=[jax.ShapeDtypeStruct(r.shape, r.dtype) for r in recvs]
        + [jax.ShapeDtypeStruct((N_DEV,) + a.shape, a.dtype) for a in arrs],
        in_specs=[any_spec] * (2 * n + m), out_specs=[any_spec] * (n + m),
        input_output_aliases={n + k: k for k in range(n)},
        scratch_shapes=_exchange_scratch(n) + _gather_scratch(m),
    )(*payload, *recvs, *arrs)
    return res[:n], res[n:]


def _carried_refs(refs, n_in, n_out, n_scratch, n_pay, aliased):
    ins = refs[:n_in]
    pay = refs[n_in:n_in + n_pay]
    o0 = n_in + (2 * n_pay if aliased else n_pay)
    outs = refs[o0:o0 + n_out]
    recvs = refs[o0 + n_out:o0 + n_out + n_pay]
    s0 = o0 + n_out + n_pay
    return ins, outs, refs[s0:s0 + n_scratch], pay, recvs, refs[s0 + n_scratch:]


def _carrier_call(body, name, grid, in_specs, out_specs, out_shape, scratch, operands, payload, recvs=None):
    n_in, n_out, n_pay = len(in_specs), len(out_specs), len(payload)
    any_spec = pl.BlockSpec(memory_space=pl.ANY)
    if recvs is None:
        landing = [jax.ShapeDtypeStruct((N_DEV,) + a.shape, a.dtype) for a in payload]
        extra_in, aliases = list(payload), {}
        sems = _gather_scratch(n_pay) if n_pay else []
    else:
        landing = [jax.ShapeDtypeStruct(r.shape, r.dtype) for r in recvs]
        extra_in = list(payload) + list(recvs)
        aliases = {n_in + n_pay + k: n_out + k for k in range(n_pay)}
        sems = _exchange_scratch(n_pay) if n_pay else []
    res = pl.pallas_call(
        body, name=name, grid=grid,
        out_shape=list(out_shape) + landing,
        in_specs=list(in_specs) + [any_spec] * len(extra_in),
        out_specs=list(out_specs) + [any_spec] * n_pay,
        input_output_aliases=aliases,
        scratch_shapes=list(scratch) + sems,
        compiler_params=_params(len(grid)),
    )(*operands, *extra_in)
    return res[:n_out], res[n_out:]


def _adamw(w, g, m, v):
    m = ADAM_B1 * m + (1.0 - ADAM_B1) * g
    v = ADAM_B2 * v + (1.0 - ADAM_B2) * jnp.square(g)
    m_hat = m / (1.0 - ADAM_B1 ** ADAM_STEP)
    v_hat = v / (1.0 - ADAM_B2 ** ADAM_STEP)
    delta = -ADAM_LR * (m_hat / (jnp.sqrt(v_hat) + ADAM_EPS) + ADAM_WD * w)
    return delta, m, v


def _ada_fwd(c_all, ada_w, ada_b_cols):
    n_layers, d, cols = ada_w.shape
    b = c_all.shape[0]

    def body(c_ref, w_ref, b_ref, o_ref):
        cv = c_ref[...]
        act = cv * _sigmoid(cv)
        o_ref[...] = jnp.dot(act, w_ref[...], preferred_element_type=F32,
                             precision=lax.Precision.HIGHEST) + b_ref[...]

    return pl.pallas_call(
        body, name="ada_fwd", grid=(n_layers,),
        out_shape=jax.ShapeDtypeStruct((n_layers, b, cols), F32),
        in_specs=[pl.BlockSpec((b, d), lambda l: (0, 0)),
                  pl.BlockSpec((None, d, cols), lambda l: (l, 0, 0)),
                  pl.BlockSpec((None, 1, cols), lambda l: (l, 0, 0))],
        out_specs=pl.BlockSpec((None, b, cols), lambda l: (l, 0, 0)),
        compiler_params=_params(1),
    )(c_all, ada_w, ada_b_cols.reshape(n_layers, 1, cols))


def _ada_bwd(c_all_t, dmod_cols, w, m, v):
    n_layers, d, cols = w.shape
    b = c_all_t.shape[1]
    td = 256 if d % 256 == 0 else d

    def body(c_ref, dm_ref, w_ref, m_ref, v_ref, g_ref, dl_ref, nm_ref, nv_ref):
        cv = c_ref[...]
        act = cv * _sigmoid(cv)
        g = jnp.dot(act, dm_ref[...], preferred_element_type=F32, precision=lax.Precision.HIGHEST)
        delta, nm, nv = _adamw(w_ref[...], g, m_ref[...], v_ref[...])
        g_ref[...] = g
        dl_ref[...] = delta
        nm_ref[...] = nm
        nv_ref[...] = nv

    blk = pl.BlockSpec((None, td, cols), lambda l, i: (l, i, 0))
    shp = jax.ShapeDtypeStruct(w.shape, F32)
    return pl.pallas_call(
        body, name="ada_bwd", grid=(n_layers, d // td),
        out_shape=[shp] * 4,
        in_specs=[pl.BlockSpec((td, b), lambda l, i: (i, 0)),
                  pl.BlockSpec((None, b, cols), lambda l, i: (l, 0, 0)), blk, blk, blk],
        out_specs=[blk] * 4,
        compiler_params=_params(2),
    )(c_all_t, dmod_cols, w, m, v)


def _row_tile(rows, cols, budget=128 * 1024, step=8):
    best = None
    for t in range(step, rows + 1, step):
        if rows % t == 0 and t * cols <= budget:
            best = t
    return best if best is not None else rows


def _adam_reduce(name, parts, w, m, v):
    p, rows, cols = parts.shape
    tr = _row_tile(rows, cols, budget=(256 * 1024) // max(1, p // 4), step=8 if parts.dtype == F32 else 16)

    def body(p_ref, w_ref, m_ref, v_ref, g_ref, dl_ref, nm_ref, nv_ref):
        g = p_ref[0].astype(F32)
        for k in range(1, p):
            g = g + p_ref[k].astype(F32)
        delta, nm, nv = _adamw(w_ref[...], g, m_ref[...], v_ref[...])
        g_ref[...] = g
        dl_ref[...] = delta
        nm_ref[...] = nm
        nv_ref[...] = nv

    blk = pl.BlockSpec((tr, cols), lambda i: (i, 0))
    shp = jax.ShapeDtypeStruct((rows, cols), F32)
    return pl.pallas_call(
        body, name=name, grid=(rows // tr,),
        out_shape=[shp] * 4,
        in_specs=[pl.BlockSpec((p, tr, cols), lambda i: (0, i, 0)), blk, blk, blk],
        out_specs=[blk] * 4,
        compiler_params=_params(1),
    )(parts, w, m, v)


def _sum_parts(parts):
    p = parts.shape[0]

    def body(p_ref, o_ref):
        acc = p_ref[0]
        for k in range(1, p):
            acc = acc + p_ref[k]
        o_ref[...] = acc

    return pl.pallas_call(body, name="loss_sum", out_shape=jax.ShapeDtypeStruct(parts.shape[1:], F32))(parts)


def _loss_grad(y, target, tm):
    t, d = y.shape

    def body(y_ref, t_ref, dy_ref, loss_ref):
        @pl.when(pl.program_id(0) == 0)
        def _():
            loss_ref[...] = jnp.zeros_like(loss_ref)

        diff = y_ref[...] - t_ref[...]
        dy_ref[...] = diff / d
        part = 0.5 * jnp.sum(_mean(diff * diff), axis=0, keepdims=True)
        loss_ref[...] += jnp.broadcast_to(part, loss_ref.shape)

    blk = pl.BlockSpec((tm, d), lambda i: (i, 0))
    return pl.pallas_call(
        body, name="loss_grad", grid=(t // tm,),
        out_shape=[jax.ShapeDtypeStruct((t, d), F32), jax.ShapeDtypeStruct((8, LANES), F32)],
        in_specs=[blk, blk], out_specs=[blk, pl.BlockSpec((8, LANES), lambda i: (0, 0))],
        compiler_params=_params(1),
    )(y, target)


def _matmul_tn(name, a, b, tk, out_dtype):
    ga, t, m = a.shape
    gb, _, n = b.shape
    g = max(ga, gb)
    n_k = t // tk

    def body(a_ref, b_ref, o_ref, acc_ref):
        k = pl.program_id(1)

        @pl.when(k == 0)
        def _():
            acc_ref[...] = jnp.zeros_like(acc_ref)

        acc_ref[...] += _mm_tn(a_ref[...], b_ref[...])

        @pl.when(k == n_k - 1)
        def _():
            o_ref[...] = acc_ref[...].astype(out_dtype)

    return pl.pallas_call(
        body, name=name, grid=(g, n_k),
        out_shape=jax.ShapeDtypeStruct((g, m, n), out_dtype),
        in_specs=[pl.BlockSpec((None, tk, m), (lambda gi, k: (gi, k, 0)) if ga > 1 else (lambda gi, k: (0, k, 0))),
                  pl.BlockSpec((None, tk, n), (lambda gi, k: (gi, k, 0)) if gb > 1 else (lambda gi, k: (0, k, 0)))],
        out_specs=pl.BlockSpec((None, m, n), lambda gi, k: (gi, 0, 0)),
        scratch_shapes=[pltpu.VMEM((m, n), F32)],
        compiler_params=_params(2),
    )(a, b)


def _time_of_row(tm):
    i = lax.broadcasted_iota(jnp.int32, (tm, 1), 0)
    return (i % 8) * (tm // 8) + i // 8


def _sublane_is(rows, s):
    return lax.broadcasted_iota(jnp.int32, (rows, 1), 0) % 8 == s


def _conv_taps(src_ref, col0, ncols, tm, tap_rows, w_ref, init_of, store):
    rc = min(ROW_CHUNK, tm)
    for cb in range(ncols // LANES):
        cs = slice(cb * LANES, (cb + 1) * LANES)
        ss = slice(col0 + cb * LANES, col0 + (cb + 1) * LANES)
        for r0 in range(0, tm, rc):
            acc = init_of(cs, rc)
            for k, row in enumerate(tap_rows):
                acc = acc + w_ref[k:k + 1, cs] * src_ref[r0 + row:r0 + row + rc, ss]
            store(r0, rc, cs, acc)


def _mixer_fwd(l, x, mod, gains, w_in_t, conv_w, vec, pool_w, w_out, *, seq, tm, act_dtype, payload=()):
    t, d = x.shape
    d_in = w_in_t.shape[0]
    dc = conv_w.shape[-1]
    d_mix = w_out.shape[0]
    n_taps = 31
    nt, tps = t // tm, seq // tm
    p = CONV_PREFIX
    n_pay = len(payload)

    def body(*refs):
        ins, outs, scr, pay_refs, got_refs, sems = _carried_refs(refs, 8, 7, 2, n_pay, aliased=False)
        x_ref, mod_ref, g_ref, win_ref, cw_ref, v_ref, pw_ref, wout_ref = ins
        x1_ref, h1_ref, u_ref, a1_ref, ap_ref, o_ref, tail_ref = outs
        ext_ref, car_ref = scr
        i = pl.program_id(0)
        first = (i % tps) == 0

        if n_pay:
            @pl.when(i == 0)
            def _():
                _gather_phase(pay_refs, got_refs, *sems, 0)

        xv = x_ref[...]
        r = lax.rsqrt(_mean(xv * xv) + EPS)
        hv = (xv * r) * g_ref[0:1, :] * (1.0 + mod_ref[1:2, :]) + mod_ref[0:1, :]
        hb = hv.astype(act_dtype)
        h1_ref[...] = hb
        u = _mm_nt(hb, win_ref[...])
        u_ref[...] = u
        a0 = u[:, :dc] * _sigmoid(u[:, dc:2 * dc])

        @pl.when(first)
        def _():
            car_ref[...] = jnp.zeros_like(car_ref)

        @pl.when(i == 0)
        def _():
            ext_ref[p + tm:p + tm + 8, :] = jnp.zeros((8, ext_ref.shape[1]), F32)

        ext_ref[p:p + tm, 0:dc] = a0
        ext_ref[p:p + tm, dc:] = u[:, 2 * dc:]
        ext_ref[0:p, :] = jnp.where(_sublane_is(p, 0), car_ref[...], ext_ref[tm - 1:tm - 1 + p, :])

        def store(r0, rc, cs, acc):
            a1_ref[r0:r0 + rc, cs] = acc

        _conv_taps(ext_ref, 0, dc, tm, [p - 8 * (n_taps - 1 - k) for k in range(n_taps)],
                   cw_ref,
                   lambda cs, rc: jnp.broadcast_to(v_ref[0:1, cs], (rc, LANES)), store)
        a1 = a1_ref[...]
        mu = _mean(a1)
        xc = a1 - mu
        rstd = lax.rsqrt(_mean(xc * xc) + EPS)
        a2 = (xc * rstd) * v_ref[1:2, :] + v_ref[2:3, :]
        ap_ref[:, 0:dc] = (a2 * _sigmoid(a2)).astype(act_dtype)

        pos = (i % tps) * tm + _time_of_row(tm)
        for g, w in enumerate(POOL_WINDOWS):
            cs = slice(dc + g * POOL_GROUP, dc + (g + 1) * POOL_GROUP)
            s = ext_ref[p:p + tm, cs]
            for j in range(1, w):
                s = s + ext_ref[p - 8 * j:p - 8 * j + tm, cs]
            cnt = jnp.minimum(pos + 1, w).astype(F32)
            dv = s / cnt - ext_ref[p:p + tm, cs]
            q = _mm(dv, pw_ref[g])
            ap_ref[:, cs] = (q * v_ref[3:4, g * POOL_GROUP:(g + 1) * POOL_GROUP]).astype(act_dtype)

        o = _mm(ap_ref[...], wout_ref[...])
        o_ref[...] = o
        ro = lax.rsqrt(_mean(o * o) + EPS)
        x1_ref[...] = xv + (1.0 + mod_ref[2:3, :]) * ((o * ro) * g_ref[1:2, :])

        nxt = ext_ref[tm + 7:tm + 7 + p, :]
        car_ref[...] = nxt
        tail_ref[...] = nxt

        if n_pay:
            @pl.when(i == max(nt - 2, 0))
            def _():
                _gather_phase(pay_refs, got_refs, *sems, 1)

            @pl.when(i == nt - 1)
            def _():
                _gather_phase(pay_refs, got_refs, *sems, 2)

    row = lambda width: pl.BlockSpec((tm, width), lambda i: (i, 0))
    return _carrier_call(
        body, f"mixer_fwd_{l}", (nt,),
        out_shape=[jax.ShapeDtypeStruct((t, d), F32), jax.ShapeDtypeStruct((t, d), act_dtype),
                   jax.ShapeDtypeStruct((t, d_in), F32), jax.ShapeDtypeStruct((t, dc), F32),
                   jax.ShapeDtypeStruct((t, d_mix), act_dtype), jax.ShapeDtypeStruct((t, d), F32),
                   jax.ShapeDtypeStruct((nt, p, d_mix), F32)],
        in_specs=[row(d),
                  pl.BlockSpec((None, None, 8, d), lambda i: (l, i // tps, 0, 0)),
                  _const_spec((None, 8, d), (l, 0, 0)),
                  _const_spec((d_in, d), (0, 0)),
                  _const_spec((32, dc), (0, 0)),
                  _const_spec((None, 8, dc), (l, 0, 0)),
                  _const_spec((None, len(POOL_WINDOWS), POOL_GROUP, POOL_GROUP), (l, 0, 0, 0)),
                  _const_spec((d_mix, d), (0, 0))],
        out_specs=[row(d), row(d), row(d_in), row(dc), row(d_mix), row(d),
                   pl.BlockSpec((None, p, d_mix), lambda i: (i, 0, 0))],
        scratch=[pltpu.VMEM((p + tm + 8, d_mix), F32), pltpu.VMEM((p, d_mix), F32)],
        operands=(x, mod, gains, w_in_t, conv_w, vec, pool_w, w_out),
        payload=payload)


def _mixer_bwd(l, dx1, x, mod, gains, u, tails, a1, o, w_in_t, conv_w, vec, pool_w, w_out, *, seq, tm, act_dtype,
               payload=(), recvs=(), rows=(), pay_layer=0):
    t, d = x.shape
    d_in = w_in_t.shape[0]
    dc = conv_w.shape[-1]
    d_mix = w_out.shape[0]
    n_taps = 31
    nt, tps = t // tm, seq // tm
    p = CONV_PREFIX
    n_groups = len(POOL_WINDOWS)
    n_pay = len(payload)

    def body(*refs):
        ins, outs, scr, pay_refs, recv_refs, sems = _carried_refs(refs, 13, 8, 3, n_pay, aliased=True)
        dx1_ref, x_ref, mod_ref, g_ref, u_ref, tail_ref, a1_ref, o_ref, win_ref, cw_ref, v_ref, pw_ref, wout_ref = ins
        dx_ref, du_ref, do_ref, dmod_ref, gd_ref, gv_ref, dcw_ref, dpw_ref = outs
        ext_ref, fext_ref, fcar_ref = scr
        i = pl.program_id(0)
        j = nt - 1 - i
        first_in_seq = (j % tps) == 0
        last_in_seq = (j % tps) == tps - 1

        if n_pay:
            @pl.when(i == 0)
            def _():
                _exchange(pay_refs, recv_refs, rows, pay_layer, *sems, start=True)

        @pl.when(i == 0)
        def _():
            gd_ref[...] = jnp.zeros_like(gd_ref)
            gv_ref[...] = jnp.zeros_like(gv_ref)
            dcw_ref[...] = jnp.zeros_like(dcw_ref)
            dpw_ref[...] = jnp.zeros_like(dpw_ref)
            fext_ref[0:8, :] = jnp.zeros((8, fext_ref.shape[1]), F32)

        @pl.when(last_in_seq)
        def _():
            dmod_ref[...] = jnp.zeros_like(dmod_ref)
            fcar_ref[...] = jnp.zeros_like(fcar_ref)

        xv = x_ref[...]
        dx1v = dx1_ref[...]
        pre_g, post_g = g_ref[0:1, :], g_ref[1:2, :]

        ov = o_ref[...]
        ro = lax.rsqrt(_mean(ov * ov) + EPS)
        yo = ov * ro
        dmod_ref[2:3, :] += _colsum(dx1v * (yo * post_g))
        dn = dx1v * (1.0 + mod_ref[2:3, :])
        gd_ref[1:2, :] += _colsum(dn * yo)
        dyo = dn * post_g
        do = ro * (dyo - yo * _mean(dyo * yo))
        dob = do.astype(act_dtype)
        do_ref[...] = dob
        dap = _mm_nt(dob, wout_ref[...])

        uv = u_ref[...]
        val, gate = uv[:, :dc], uv[:, dc:2 * dc]
        sg = _sigmoid(gate)
        ext_ref[p:p + tm, 0:dc] = val * sg
        ext_ref[p:p + tm, dc:] = uv[:, 2 * dc:]
        keep = jnp.where(first_in_seq, 0.0, 1.0).astype(F32)
        ext_ref[0:p, :] = jnp.where(_sublane_is(p, 0), tail_ref[...] * keep, ext_ref[tm - 1:tm - 1 + p, :])

        a1v = a1_ref[...]
        mu = _mean(a1v)
        xc = a1v - mu
        rstd = lax.rsqrt(_mean(xc * xc) + EPS)
        xh = xc * rstd
        ln_g = v_ref[1:2, :]
        a2 = xh * ln_g + v_ref[2:3, :]
        s2 = _sigmoid(a2)
        da2 = dap[:, :dc] * (s2 * (1.0 + a2 * (1.0 - s2)))
        gv_ref[1:2, :] += _colsum(da2 * xh)
        gv_ref[2:3, :] += _colsum(da2)
        dxh = da2 * ln_g
        da1 = rstd * (dxh - _mean(dxh) - xh * _mean(dxh * xh))
        gv_ref[0:1, :] += _colsum(da1)
        last_sublane = _sublane_is(p, 7)

        def put(cs, value):
            fext_ref[8:8 + tm, cs] = value
            fext_ref[8 + tm:8 + tm + p, cs] = jnp.where(last_sublane, fcar_ref[:, cs], fext_ref[9:9 + p, cs])

        put(slice(0, dc), da1)

        for k in range(n_taps):
            row = p - 8 * (n_taps - 1 - k)
            dcw_ref[k:k + 1, :] += _colsum(fext_ref[8:8 + tm, 0:dc] * ext_ref[row:row + tm, 0:dc])

        def store(r0, rc, cs, acc):
            sgc = _sigmoid(u_ref[r0:r0 + rc, dc + cs.start:dc + cs.stop])
            vc = u_ref[r0:r0 + rc, cs]
            du_ref[r0:r0 + rc, cs] = (acc * sgc).astype(act_dtype)
            du_ref[r0:r0 + rc, dc + cs.start:dc + cs.stop] = (acc * vc * sgc * (1.0 - sgc)).astype(act_dtype)

        _conv_taps(fext_ref, 0, dc, tm, [8 + 8 * (n_taps - 1 - k) for k in range(n_taps)],
                   cw_ref,
                   lambda cs, rc: jnp.zeros((rc, LANES), F32), store)

        pos = (j % tps) * tm + _time_of_row(tm)
        for g, w in enumerate(POOL_WINDOWS):
            cs = slice(dc + g * POOL_GROUP, dc + (g + 1) * POOL_GROUP)
            gs = slice(g * POOL_GROUP, (g + 1) * POOL_GROUP)
            s = ext_ref[p:p + tm, cs]
            for jj in range(1, w):
                s = s + ext_ref[p - 8 * jj:p - 8 * jj + tm, cs]
            cnt = jnp.minimum(pos + 1, w).astype(F32)
            dv = (s / cnt - ext_ref[p:p + tm, cs]).astype(MXU_DTYPE)
            q = _mm(dv, pw_ref[g])
            dp = dap[:, cs]
            gv_ref[3:4, gs] += _colsum(dp * q)
            dq = (dp * v_ref[3:4, gs]).astype(MXU_DTYPE)
            dpw_ref[g] += _mm_tn(dv, dq)
            dd = _mm_nt(dq, pw_ref[g])
            put(cs, dd / cnt)
            dhp = fext_ref[8:8 + tm, cs]
            for jj in range(1, w):
                dhp = dhp + fext_ref[8 + 8 * jj:8 + 8 * jj + tm, cs]
            du_ref[:, dc + cs.start:dc + cs.stop] = (dhp - dd).astype(act_dtype)

        fcar_ref[...] = fext_ref[1:1 + p, :]

        dh = _mm(du_ref[...], win_ref[...])

        r = lax.rsqrt(_mean(xv * xv) + EPS)
        xn = xv * r
        dmod_ref[0:1, :] += _colsum(dh)
        dmod_ref[1:2, :] += _colsum(dh * (xn * pre_g))
        dy = dh * (1.0 + mod_ref[1:2, :])
        gd_ref[0:1, :] += _colsum(dy * xn)
        dxn = dy * pre_g
        dx_ref[...] = dx1v + r * (dxn - xn * _mean(dxn * xn))

        if n_pay:
            @pl.when(i == nt - 1)
            def _():
                _exchange(pay_refs, recv_refs, rows, pay_layer, *sems, start=False)

    rev = lambda width: pl.BlockSpec((tm, width), lambda i: (nt - 1 - i, 0))
    return _carrier_call(
        body, f"mixer_bwd_{l}", (nt,),
        out_shape=[jax.ShapeDtypeStruct((t, d), F32), jax.ShapeDtypeStruct((t, d_in), act_dtype),
                   jax.ShapeDtypeStruct((t, d), act_dtype),
                   jax.ShapeDtypeStruct((t // seq, 8, d), F32), jax.ShapeDtypeStruct((8, d), F32),
                   jax.ShapeDtypeStruct((8, dc), F32), jax.ShapeDtypeStruct((32, dc), F32),
                   jax.ShapeDtypeStruct((n_groups, POOL_GROUP, POOL_GROUP), F32)],
        in_specs=[rev(d), rev(d),
                  pl.BlockSpec((None, None, 8, d), lambda i: (l, (nt - 1 - i) // tps, 0, 0)),
                  _const_spec((None, 8, d), (l, 0, 0)),
                  rev(d_in),
                  pl.BlockSpec((None, p, d_mix), lambda i: (jnp.maximum(nt - 2 - i, 0), 0, 0)),
                  rev(dc), rev(d),
                  _const_spec((d_in, d), (0, 0)),
                  _const_spec((32, dc), (0, 0)),
                  _const_spec((None, 8, dc), (l, 0, 0)),
                  _const_spec((None, n_groups, POOL_GROUP, POOL_GROUP), (l, 0, 0, 0)),
                  _const_spec((d_mix, d), (0, 0))],
        out_specs=[rev(d), rev(d_in), rev(d),
                   pl.BlockSpec((None, 8, d), lambda i: ((nt - 1 - i) // tps, 0, 0)),
                   _acc_spec((8, d), (0, 0)), _acc_spec((8, dc), (0, 0)), _acc_spec((32, dc), (0, 0)),
                   _acc_spec((n_groups, POOL_GROUP, POOL_GROUP), (0, 0, 0))],
        scratch=[pltpu.VMEM((p + tm, d_mix), F32), pltpu.VMEM((8 + tm + p, d_mix), F32),
                 pltpu.VMEM((p, d_mix), F32)],
        operands=(dx1, x, mod, gains, u, tails, a1, o, w_in_t, conv_w, vec, pool_w, w_out),
        payload=payload, recvs=recvs)


def _ffn_fwd(l, x, mod, gains, up, fcw, down, *, seq, tm, act_dtype, payload=()):
    t, d = x.shape
    n_chunks, _, fc = up.shape
    half = n_chunks // 2
    nt, tps = t // tm, seq // tm
    p = FFN_PREFIX
    n_pay = len(payload)

    def body(*refs):
        ins, outs, scr, pay_refs, got_refs, sems = _carried_refs(refs, 6, 6, 2, n_pay, aliased=False)
        x_ref, mod_ref, g_ref, up_ref, fcw_ref, down_ref = ins
        x2_ref, h2_ref, u2_ref, u3_ref, hid_ref, o2_ref = outs
        ext_ref, car_ref = scr
        i = pl.program_id(0)
        first = (i % tps) == 0

        if n_pay:
            @pl.when(i == 0)
            def _():
                _gather_phase(pay_refs, got_refs, *sems, 0)

        xv = x_ref[...]
        r = lax.rsqrt(_mean(xv * xv) + EPS)
        hv = (xv * r) * g_ref[0:1, :] * (1.0 + mod_ref[4:5, :]) + mod_ref[3:4, :]
        hb = hv.astype(act_dtype)
        h2_ref[...] = hb

        @pl.when(first)
        def _():
            car_ref[...] = jnp.zeros_like(car_ref)

        @pl.when(i == 0)
        def _():
            for n in range(n_chunks):
                ext_ref[n, p + tm:p + tm + 8, :] = jnp.zeros((8, fc), F32)

        first_sublane = _sublane_is(p, 0)

        def project(n):
            un = _mm(hb, up_ref[n])
            ext_ref[n, p:p + tm, :] = un
            u2_ref[n] = un.astype(act_dtype)
            ext_ref[n, 0:p, :] = jnp.where(first_sublane, car_ref[n], ext_ref[n, tm - 1:tm - 1 + p, :])
            car_ref[n] = ext_ref[n, tm + 7:tm + 7 + p, :]

        def conv(n):
            return (fcw_ref[n, 3:4, :] + fcw_ref[n, 0:1, :] * ext_ref[n, p - 16:p - 16 + tm, :]
                    + fcw_ref[n, 1:2, :] * ext_ref[n, p - 8:p - 8 + tm, :]
                    + fcw_ref[n, 2:3, :] * ext_ref[n, p:p + tm, :])

        for n in range(n_chunks):
            project(n)
        o2 = jnp.zeros((tm, d), F32)
        for n in range(half):
            gt = conv(n + half)
            v = conv(n)
            u3_ref[n] = v.astype(act_dtype)
            u3_ref[n + half] = gt.astype(act_dtype)
            hid = ((gt * _sigmoid(gt)) * v).astype(act_dtype)
            hid_ref[n] = hid
            o2 = o2 + _mm(hid, down_ref[n])
        o2_ref[...] = o2
        ro = lax.rsqrt(_mean(o2 * o2) + EPS)
        x2_ref[...] = xv + (1.0 + mod_ref[5:6, :]) * ((o2 * ro) * g_ref[1:2, :])

        if n_pay:
            @pl.when(i == max(nt - 2, 0))
            def _():
                _gather_phase(pay_refs, got_refs, *sems, 1)

            @pl.when(i == nt - 1)
            def _():
                _gather_phase(pay_refs, got_refs, *sems, 2)

    row = lambda width: pl.BlockSpec((tm, width), lambda i: (i, 0))
    chunked = lambda n: pl.BlockSpec((n, tm, fc), lambda i: (0, i, 0))
    return _carrier_call(
        body, f"ffn_fwd_{l}", (nt,),
        out_shape=[jax.ShapeDtypeStruct((t, d), F32), jax.ShapeDtypeStruct((t, d), act_dtype),
                   jax.ShapeDtypeStruct((n_chunks, t, fc), act_dtype),
                   jax.ShapeDtypeStruct((n_chunks, t, fc), act_dtype),
                   jax.ShapeDtypeStruct((half, t, fc), act_dtype), jax.ShapeDtypeStruct((t, d), F32)],
        in_specs=[row(d),
                  pl.BlockSpec((None, None, 8, d), lambda i: (l, i // tps, 0, 0)),
                  _const_spec((None, 8, d), (l, 0, 0)),
                  _const_spec((n_chunks, d, fc), (0, 0, 0)),
                  _const_spec((n_chunks, 8, fc), (0, 0, 0)),
                  _const_spec((half, fc, d), (0, 0, 0))],
        out_specs=[row(d), row(d), chunked(n_chunks), chunked(n_chunks), chunked(half), row(d)],
        scratch=[pltpu.VMEM((n_chunks, p + tm + 8, fc), F32), pltpu.VMEM((n_chunks, p, fc), F32)],
        operands=(x, mod, gains, up, fcw, down),
        payload=payload)


def _ffn_bwd(l, dx2, x, mod, gains, u2, u3, o2, up, fcw, down, *, seq, tm, act_dtype,
             payload=(), recvs=(), rows=(), pay_layer=0):
    t, d = x.shape
    n_chunks, _, fc = up.shape
    half = n_chunks // 2
    nt, tps = t // tm, seq // tm
    p = FFN_PREFIX
    n_pay = len(payload)

    def body(*refs):
        ins, outs, scr, pay_refs, recv_refs, sems = _carried_refs(refs, 10, 6, 2, n_pay, aliased=True)
        dx2_ref, x_ref, mod_ref, g_ref, u2_ref, u3_ref, o2_ref, up_ref, fcw_ref, down_ref = ins
        dx1_ref, du2_ref, do2_ref, dmod_ref, gd_ref, dfcw_ref = outs
        fext_ref, fcar_ref = scr
        i = pl.program_id(0)
        j = nt - 1 - i
        last_in_seq = (j % tps) == tps - 1

        if n_pay:
            @pl.when(i == 0)
            def _():
                _exchange(pay_refs, recv_refs, rows, pay_layer, *sems, start=True)

        @pl.when(i == 0)
        def _():
            gd_ref[...] = jnp.zeros_like(gd_ref)
            dfcw_ref[...] = jnp.zeros_like(dfcw_ref)
            for n in range(n_chunks):
                fext_ref[n, 0:8, :] = jnp.zeros((8, fc), F32)

        @pl.when(last_in_seq)
        def _():
            dmod_ref[...] = jnp.zeros_like(dmod_ref)
            fcar_ref[...] = jnp.zeros_like(fcar_ref)

        xv = x_ref[...]
        dx2v = dx2_ref[...]
        pre_g, post_g = g_ref[0:1, :], g_ref[1:2, :]

        ov = o2_ref[...]
        ro = lax.rsqrt(_mean(ov * ov) + EPS)
        yo = ov * ro
        dmod_ref[2:3, :] += _colsum(dx2v * (yo * post_g))
        dn = dx2v * (1.0 + mod_ref[5:6, :])
        gd_ref[1:2, :] += _colsum(dn * yo)
        dyo = dn * post_g
        do = ro * (dyo - yo * _mean(dyo * yo))
        dob = do.astype(act_dtype)
        do2_ref[...] = dob

        for n in range(half):
            v = u3_ref[n].astype(F32)
            gt = u3_ref[n + half].astype(F32)
            sg = _sigmoid(gt)
            dhid = _mm_nt(dob, down_ref[n])
            fext_ref[n, 8:8 + tm, :] = dhid * (gt * sg)
            fext_ref[n + half, 8:8 + tm, :] = dhid * v * (sg * (1.0 + gt * (1.0 - sg)))

        last_sublane = _sublane_is(p, 7)
        dh = jnp.zeros((tm, d), F32)
        for n in range(n_chunks):
            fext_ref[n, 8 + tm:8 + tm + p, :] = jnp.where(last_sublane, fcar_ref[n], fext_ref[n, 9:9 + p, :])
            fcar_ref[n] = fext_ref[n, 1:1 + p, :]
            d2 = fext_ref[n, 8:8 + tm, :]
            d1 = fext_ref[n, 16:16 + tm, :]
            d0 = fext_ref[n, 24:24 + tm, :]
            u2v = u2_ref[n].astype(F32)
            dfcw_ref[n, 3:4, :] += _colsum(d2)
            dfcw_ref[n, 0:1, :] += _colsum(d0 * u2v)
            dfcw_ref[n, 1:2, :] += _colsum(d1 * u2v)
            dfcw_ref[n, 2:3, :] += _colsum(d2 * u2v)
            du2 = (fcw_ref[n, 0:1, :] * d0 + fcw_ref[n, 1:2, :] * d1 + fcw_ref[n, 2:3, :] * d2).astype(act_dtype)
            du2_ref[n] = du2
            dh = dh + _mm_nt(du2, up_ref[n])

        r = lax.rsqrt(_mean(xv * xv) + EPS)
        xn = xv * r
        dmod_ref[0:1, :] += _colsum(dh)
        dmod_ref[1:2, :] += _colsum(dh * (xn * pre_g))
        dy = dh * (1.0 + mod_ref[4:5, :])
        gd_ref[0:1, :] += _colsum(dy * xn)
        dxn = dy * pre_g
        dx1_ref[...] = dx2v + r * (dxn - xn * _mean(dxn * xn))

        if n_pay:
            @pl.when(i == nt - 1)
            def _():
                _exchange(pay_refs, recv_refs, rows, pay_layer, *sems, start=False)

    rev = lambda width: pl.BlockSpec((tm, width), lambda i: (nt - 1 - i, 0))
    chunked = pl.BlockSpec((n_chunks, tm, fc), lambda i: (0, nt - 1 - i, 0))
    return _carrier_call(
        body, f"ffn_bwd_{l}", (nt,),
        out_shape=[jax.ShapeDtypeStruct((t, d), F32), jax.ShapeDtypeStruct((n_chunks, t, fc), act_dtype),
                   jax.ShapeDtypeStruct((t, d), act_dtype),
                   jax.ShapeDtypeStruct((t // seq, 8, d), F32), jax.ShapeDtypeStruct((8, d), F32),
                   jax.ShapeDtypeStruct((n_chunks, 8, fc), F32)],
        in_specs=[rev(d), rev(d),
                  pl.BlockSpec((None, None, 8, d), lambda i: (l, (nt - 1 - i) // tps, 0, 0)),
                  _const_spec((None, 8, d), (l, 0, 0)),
                  chunked, chunked, rev(d),
                  _const_spec((n_chunks, d, fc), (0, 0, 0)),
                  _const_spec((n_chunks, 8, fc), (0, 0, 0)),
                  _const_spec((half, fc, d), (0, 0, 0))],
        out_specs=[rev(d), chunked, rev(d),
                   pl.BlockSpec((None, 8, d), lambda i: ((nt - 1 - i) // tps, 0, 0)),
                   _acc_spec((8, d), (0, 0)), _acc_spec((n_chunks, 8, fc), (0, 0, 0))],
        scratch=[pltpu.VMEM((n_chunks, 8 + tm + p, fc), F32), pltpu.VMEM((n_chunks, p, fc), F32)],
        operands=(dx2, x, mod, gains, u2, u3, o2, up, fcw, down),
        payload=payload, recvs=recvs)


def _pad_rows(a, rows):
    pad = [(0, 0)] * a.ndim
    pad[-2] = (0, rows - a.shape[-2])
    return jnp.pad(a, pad)


def kernel(x, c, ada_w, ada_b, pre_mix_g, post_mix_g, w_in, conv_w, conv_b, conv_ln_g, conv_ln_b, pool_w, pool_scale, w_out, pre_ffn_g, post_ffn_g, ffn_up, ffn_conv_w, ffn_conv_b, ffn_down, loss_target, m_ada_w, m_ada_b, m_pre_mix_g, m_post_mix_g, m_w_in, m_conv_w, m_conv_b, m_conv_ln_g, m_conv_ln_b, m_pool_w, m_pool_scale, m_w_out, m_pre_ffn_g, m_post_ffn_g, m_ffn_up, m_ffn_conv_w, m_ffn_conv_b, m_ffn_down, v_ada_w, v_ada_b, v_pre_mix_g, v_post_mix_g, v_w_in, v_conv_w, v_conv_b, v_conv_ln_g, v_conv_ln_b, v_pool_w, v_pool_scale, v_w_out, v_pre_ffn_g, v_post_ffn_g, v_ffn_up, v_ffn_conv_w, v_ffn_conv_b, v_ffn_down):
    bl, seq, d = x.shape
    n_layers = ada_w.shape[0]
    t = bl * seq
    dc = conv_b.shape[1]
    d_in = w_in.shape[2] * N_DEV
    d_mix = w_out.shape[1] * N_DEV
    n_taps = conv_w.shape[1]
    fc = ffn_up.shape[2]
    half = N_DEV // 2
    ada_cols = ada_w.shape[2]
    n_mod = ada_cols * N_DEV // d
    assert pool_scale.shape[1] == dc and n_taps == 31 and n_mod == 6 and ffn_conv_w.shape[1] == 3
    assert pool_w.shape[1:] == (len(POOL_WINDOWS), POOL_GROUP, POOL_GROUP)
    tm = TILE_TOKENS
    assert seq % tm == 0 and CONV_PREFIX <= tm - 8
    tk = 2048 if t % 2048 == 0 else tm
    act = MXU_DTYPE

    def tile_order(a, inverse=False):
        shape = (t // tm, tm // 8, 8, d) if inverse else (t // tm, 8, tm // 8, d)
        return a.reshape(shape).transpose(0, 2, 1, 3).reshape(t, d)

    ax = lax.axis_index
    me = 4 * ax("x") + 2 * ax("y") + ax("c")

    (c_all,) = _all_gather([c], "gather_c")
    c_all = c_all.reshape(N_DEV * bl, d)
    ada_b_cols = lax.dynamic_slice_in_dim(ada_b, me * ada_cols, ada_cols, axis=1)
    mod_cols = _ada_fwd(c_all, ada_w, ada_b_cols)

    w_in_s = w_in.astype(act).transpose(0, 2, 1)
    w_out_s, up_s, down_s = w_out.astype(act), ffn_up.astype(act), ffn_down.astype(act)
    conv_w_s = _pad_rows(conv_w, 32)
    fcw_s = _pad_rows(jnp.concatenate(
        [ffn_conv_w, lax.dynamic_slice_in_dim(ffn_conv_b, me * fc, fc, axis=1)[:, None, :]], axis=1), 8)

    def early_shards(l):
        return (w_in_s[l], w_out_s[l], conv_w_s[l], down_s[l], fcw_s[l])

    def early_weights(g_w_in, g_w_out, g_conv_w, g_down, g_fcw):
        mixer = (g_w_in.reshape(d_in, d), g_conv_w.transpose(1, 0, 2).reshape(32, dc), g_w_out.reshape(d_mix, d))
        return mixer, (g_fcw, g_down.reshape(half, fc, d))

    g0 = _all_gather([mod_cols, *early_shards(0)], "gather_w0")
    mod_all = g0[0].transpose(1, 2, 0, 3).reshape(n_layers, N_DEV * bl, n_mod, d)
    mod = _pad_rows(lax.dynamic_slice_in_dim(mod_all, me * bl, bl, axis=1), 8)
    early = [early_weights(*g0[1:])]
    ups = []
    gains_mix = _pad_rows(jnp.stack([pre_mix_g, post_mix_g], axis=1), 8)
    gains_ffn = _pad_rows(jnp.stack([pre_ffn_g, post_ffn_g], axis=1), 8)
    vec = _pad_rows(jnp.stack([conv_b, conv_ln_g, conv_ln_b, pool_scale], axis=1), 8)
    pool_w_b = pool_w.astype(act)

    kw = dict(seq=seq, tm=tm, act_dtype=act)
    xs = tile_order(x.reshape(t, d))
    saved = []
    for l in range(n_layers):
        (w_in_t, cw, w_o), (fcw_l, down_l) = early[l]
        (x1, h1, u, a1, ap, o, tails), (up_l,) = _mixer_fwd(
            l, xs, mod, gains_mix, w_in_t, cw, vec, pool_w_b, w_o, **kw, payload=(up_s[l],))
        ups.append(up_l)
        (x2, h2, u2, u3, hid, o2), got = _ffn_fwd(l, x1, mod, gains_ffn, up_l, fcw_l, down_l, **kw,
                                                  payload=early_shards(l + 1) if l + 1 < n_layers else ())
        if got:
            early.append(early_weights(*got))
        saved.append((xs, h1, u, tails, a1, ap, o, x1, h2, u2, u3, hid, o2))
        xs = x2

    dx, loss_part = _loss_grad(xs, tile_order(loss_target.reshape(t, d)), tm)

    def landing(shard, dtype):
        return lax.empty((N_DEV, n_layers) + shard.shape[1:], dtype)

    r_up, r_down = landing(ffn_up, act), landing(ffn_down, act)
    r_w_in, r_w_out = landing(w_in, act), landing(w_out, act)
    r_conv_w, r_fcw = landing(conv_w, F32), landing(ffn_conv_w, F32)
    def early_rows(l):
        return (7 * d // 10 if l == 0 else 2 * d // 5) // 16 * 16

    def late_rows(l):
        return ((early_rows(l), d - early_rows(l)), None, None, None, None)
    dmods, smalls = [], []
    pending = ()
    for l in reversed(range(n_layers)):
        xin, h1, u, tails, a1, ap, o, x1, h2, u2, u3, hid, o2 = saved[l]
        (w_in_t, cw, w_o), (fcw_l, down_l) = early[l]
        up_l = ups[l]
        (dx1, du2, do2, dmod_b, gd_b, dfcw), got = _ffn_bwd(
            l, dx, x1, mod, gains_ffn, u2, u3, o2, up_l, fcw_l, down_l, **kw,
            payload=pending, recvs=(r_up, r_w_in, r_w_out, r_conv_w, r_fcw) if pending else (),
            rows=late_rows(l + 1), pay_layer=l + 1)
        if pending:
            r_up, r_w_in, r_w_out, r_conv_w, r_fcw = got
        p_up = _matmul_tn(f"dw_up_{l}", h2[None], du2, tk, act)
        p_down = _matmul_tn(f"dw_down_{l}", hid, do2[None], tk, act).reshape(N_DEV, fc // 2, d)
        (dx, du, do, dmod_a, gd_a, gv, dcw, dpw), (r_down, r_up) = _mixer_bwd(
            l, dx1, xin, mod, gains_mix, u, tails, a1, o, w_in_t, cw, vec, pool_w_b, w_o, **kw,
            payload=(p_down, p_up), recvs=(r_down, r_up), rows=(None, (0, early_rows(l))), pay_layer=l)
        dw_in = _matmul_tn(f"dw_in_{l}", h1[None], du[None], tk, act)[0]
        p_w_in = dw_in.reshape(d, N_DEV, d_in // N_DEV).transpose(1, 0, 2)
        p_w_out = _matmul_tn(f"dw_out_{l}", ap[None], do[None], tk, act)[0].reshape(N_DEV, d_mix // N_DEV, d)
        p_conv_w = dcw[:n_taps].reshape(n_taps, N_DEV, dc // N_DEV).transpose(1, 0, 2)
        pending = (p_up, p_w_in, p_w_out, p_conv_w, dfcw[:, :3, :])
        dmods.append(jnp.concatenate([dmod_a[:, 0:3], dmod_b[:, 0:3]], axis=1).reshape(bl, n_mod * d))
        smalls.append(jnp.concatenate(
            [gd_a[0], gd_a[1], gv[0], gv[1], gv[2], gv[3], gd_b[0], gd_b[1], dfcw[:, 3, :].reshape(-1),
             dpw.reshape(-1)]))
    dmods.reverse()
    smalls.reverse()

    dmod_loc = jnp.stack(dmods)
    small_loc = jnp.stack(smalls)
    n_small = small_loc.shape[1]
    small_cols = 8 * LANES if (n_layers * n_small) % (8 * LANES) == 0 else LANES
    (r_up, r_w_in, r_w_out, r_conv_w, r_fcw), (g_dmod, g_small, g_loss) = _exchange_and_gather(
        pending, (r_up, r_w_in, r_w_out, r_conv_w, r_fcw), late_rows(0), 0,
        [dmod_loc, small_loc.reshape(-1, small_cols), loss_part], "exchange_tail")

    loss = _sum_parts(g_loss)[0, 0]

    def flat2(a):
        return a.reshape(-1, a.shape[-1])

    def update(name, parts, w, m, v):
        outs = _adam_reduce(name, parts.reshape(parts.shape[0], -1, w.shape[-1]), flat2(w), flat2(m), flat2(v))
        return [o_.reshape(w.shape) for o_ in outs]

    res = {}
    res["w_in"] = update("adam_w_in", r_w_in, w_in, m_w_in, v_w_in)
    res["w_out"] = update("adam_w_out", r_w_out, w_out, m_w_out, v_w_out)
    res["ffn_up"] = update("adam_ffn_up", r_up, ffn_up, m_ffn_up, v_ffn_up)
    res["ffn_down"] = update("adam_ffn_down", r_down, ffn_down, m_ffn_down, v_ffn_down)
    res["conv_w"] = update("adam_conv_w", r_conv_w, conv_w, m_conv_w, v_conv_w)
    res["ffn_conv_w"] = update("adam_ffn_conv_w", r_fcw, ffn_conv_w, m_ffn_conv_w, v_ffn_conv_w)

    dmod_all = g_dmod.transpose(1, 0, 2, 3).reshape(n_layers, N_DEV * bl, n_mod * d)
    dmod_cols = lax.dynamic_slice_in_dim(dmod_all, me * ada_cols, ada_cols, axis=2)
    res["ada_w"] = list(_ada_bwd(c_all.T, dmod_cols, ada_w, m_ada_w, v_ada_w))
    res["ada_b"] = update("adam_ada_b", dmod_all.transpose(1, 0, 2), ada_b, m_ada_b, v_ada_b)

    small_names = ["pre_mix_g", "post_mix_g", "conv_b", "conv_ln_g", "conv_ln_b", "pool_scale", "pre_ffn_g",
                   "post_ffn_g", "ffn_conv_b", "pool_w"]
    small_w = [pre_mix_g, post_mix_g, conv_b, conv_ln_g, conv_ln_b, pool_scale, pre_ffn_g, post_ffn_g,
               ffn_conv_b, pool_w]
    small_m = [m_pre_mix_g, m_post_mix_g, m_conv_b, m_conv_ln_g, m_conv_ln_b, m_pool_scale, m_pre_ffn_g,
               m_post_ffn_g, m_ffn_conv_b, m_pool_w]
    small_v = [v_pre_mix_g, v_post_mix_g, v_conv_b, v_conv_ln_g, v_conv_ln_b, v_pool_scale, v_pre_ffn_g,
               v_post_ffn_g, v_ffn_conv_b, v_pool_w]

    def pack(arrs):
        return jnp.concatenate([a.reshape(n_layers, -1) for a in arrs], axis=1).reshape(-1, small_cols)

    outs = _adam_reduce("adam_small", g_small, pack(small_w), pack(small_m), pack(small_v))
    outs = [o_.reshape(n_layers, n_small) for o_ in outs]
    off = 0
    for name, w in zip(small_names, small_w):
        size = w[0].size
        res[name] = [o_[:, off:off + size].reshape(w.shape) for o_ in outs]
        off += size

    order = ["ada_w", "ada_b", "pre_mix_g", "post_mix_g", "w_in", "conv_w", "conv_b", "conv_ln_g", "conv_ln_b",
             "pool_w", "pool_scale", "w_out", "pre_ffn_g", "post_ffn_g", "ffn_up", "ffn_conv_w", "ffn_conv_b",
             "ffn_down"]
    grad_x = tile_order(dx, inverse=True).reshape(bl, seq, d)
    return (loss, grad_x, *[res[n][0] for n in order], *[res[n][1] for n in order],
            *[res[n][2] for n in order], *[res[n][3] for n in order])
```

```python
import jax
import jax.numpy as jnp
from jax import lax
from jax.experimental import pallas as pl
from jax.experimental.pallas import tpu as pltpu

N_DEV = 8
EPS = 1e-6
POOL_WINDOWS = (2, 4, 8, 16)
POOL_GROUP = 128
TILE_TOKENS = 256
CONV_PREFIX = 8 * 30
FFN_PREFIX = 8 * 2
LANES = 128
ROW_CHUNK = 128
MXU_DTYPE = jnp.bfloat16
VMEM_LIMIT = 60 * 1024 * 1024

ADAM_LR = 0.001
ADAM_B1 = 0.9
ADAM_B2 = 0.999
ADAM_EPS = 1e-08
ADAM_WD = 0.01
ADAM_STEP = 10

MESH = pl.DeviceIdType.MESH
F32 = jnp.float32


def _mm(a, b):
    return jnp.dot(a.astype(MXU_DTYPE), b.astype(MXU_DTYPE), preferred_element_type=F32)


def _mm_nt(a, b):
    return lax.dot_general(a.astype(MXU_DTYPE), b.astype(MXU_DTYPE), (((1,), (1,)), ((), ())),
                           preferred_element_type=F32)


def _mm_tn(a, b):
    return lax.dot_general(a.astype(MXU_DTYPE), b.astype(MXU_DTYPE), (((0,), (0,)), ((), ())),
                           preferred_element_type=F32)


def _mean(v):
    return jnp.mean(v, axis=-1, keepdims=True)


def _colsum(v):
    return jnp.sum(v, axis=0, keepdims=True)


def _sigmoid(v):
    return jax.nn.sigmoid(v)


def _params(n_grid=1):
    return pltpu.CompilerParams(dimension_semantics=("arbitrary",) * n_grid, vmem_limit_bytes=VMEM_LIMIT)


def _const_spec(shape, index):
    return pl.BlockSpec(shape, lambda *_: index, pipeline_mode=pl.Buffered(1))


def _acc_spec(shape, index):
    return pl.BlockSpec(shape, lambda *_: index)


def _position():
    x, y, c = lax.axis_index("x"), lax.axis_index("y"), lax.axis_index("c")
    return x, y, c


def _gather_phase(ins, outs, send_sems, recv_sems, local_sems, phase):
    n = len(ins)
    x, y, c = _position()
    me, sibling = (x, y, c), (x, y, 1 - c)
    chips = [(1 - x, y), (x, 1 - y), (1 - x, 1 - y)]

    def slot(k, px, py, pc):
        return outs[k].at[4 * px + 2 * py + pc]

    def copy(k, s, block, to, src=None):
        return pltpu.make_async_remote_copy(
            src_ref=slot(k, *block) if src is None else src, dst_ref=slot(k, *block),
            send_sem=send_sems.at[k, s], recv_sem=recv_sems.at[k, s], device_id=to, device_id_type=MESH)

    mine = [pltpu.make_async_copy(ins[k], slot(k, *me), local_sems.at[k]) for k in range(n)]
    first = []
    for k in range(n):
        first.append(copy(k, 0, me, sibling, src=ins[k]))
        first += [copy(k, 1 + j, me, (*chip, c), src=ins[k]) for j, chip in enumerate(chips)]
    passed = [copy(k, 4 + j, (*chip, c), sibling) for j, chip in enumerate(chips) for k in range(n)]
    if phase == 0:
        for cp in mine + first:
            cp.start()
    elif phase == 1:
        for j, chip in enumerate(chips):
            for k in range(n):
                copy(k, 1 + j, (*chip, c), me).wait_recv()
                copy(k, 4 + j, (*chip, c), sibling).start()
    else:
        for k in range(n):
            copy(k, 0, sibling, me).wait_recv()
            for j, chip in enumerate(chips):
                copy(k, 4 + j, (*chip, 1 - c), me).wait_recv()
        for cp in first + passed:
            cp.wait_send()
        for cp in mine:
            cp.wait()


def _gather_scratch(n):
    return [pltpu.SemaphoreType.DMA((n, 7)), pltpu.SemaphoreType.DMA((n, 7)), pltpu.SemaphoreType.DMA((n,))]


def _layer_of(refs, layers):
    return [r if lay is None else r.at[lay] for r, lay in zip(refs, layers)]


def _gathered_shapes(arrs, layers):
    return [jax.ShapeDtypeStruct((N_DEV,) + (a.shape if lay is None else a.shape[1:]), a.dtype)
            for a, lay in zip(arrs, layers)]


def _all_gather(arrs, name, layers=None):
    n = len(arrs)
    layers = [None] * n if layers is None else layers

    def body(*refs):
        for phase in range(3):
            _gather_phase(_layer_of(refs[:n], layers), refs[n:2 * n], *refs[2 * n:], phase)

    any_spec = pl.BlockSpec(memory_space=pl.ANY)
    return pl.pallas_call(
        body, name=name,
        out_shape=_gathered_shapes(arrs, layers),
        in_specs=[any_spec] * n, out_specs=[any_spec] * n,
        scratch_shapes=_gather_scratch(n),
    )(*arrs)


def _exchange(payload, recvs, rows, layer, send_sems, recv_sems, local_sems, start):
    x, y, c = _position()
    me = 4 * x + 2 * y + c
    for k, (src, recv) in enumerate(zip(payload, recvs)):
        def block(ref, *index):
            return ref.at[index] if rows[k] is None else ref.at[(*index, pl.ds(*rows[k]))]

        local = pltpu.make_async_copy(block(src, me), block(recv, me, layer), local_sems.at[k])
        if start:
            local.start()
        else:
            local.wait()
        for j in range(1, N_DEV):
            px = (1 - x) if (j & 4) else x
            py = (1 - y) if (j & 2) else y
            pc = (1 - c) if (j & 1) else c
            peer = 4 * px + 2 * py + pc
            landing = block(recv, me, layer) if start else block(recv, peer, layer)
            cp = pltpu.make_async_remote_copy(
                src_ref=block(src, peer), dst_ref=landing, send_sem=send_sems.at[k, j - 1],
                recv_sem=recv_sems.at[k, j - 1], device_id=(px, py, pc), device_id_type=MESH)
            if start:
                cp.start()
            else:
                cp.wait()


def _exchange_scratch(n):
    return [pltpu.SemaphoreType.DMA((n, N_DEV - 1)), pltpu.SemaphoreType.DMA((n, N_DEV - 1)),
            pltpu.SemaphoreType.DMA((n,))]


def _exchange_and_gather(payload, recvs, rows, layer, arrs, name):
    n, m = len(payload), len(arrs)

    def body(*refs):
        pay, srcs = refs[:n], refs[2 * n:2 * n + m]
        outs, got = refs[2 * n + m:3 * n + m], refs[3 * n + m:3 * n + 2 * m]
        xsems, gsems = refs[3 * n + 2 * m:3 * n + 2 * m + 3], refs[3 * n + 2 * m + 3:]
        _exchange(pay, outs, rows, layer, *xsems, start=True)
        for phase in range(3):
            _gather_phase(srcs, got, *gsems, phase)
        _exchange(pay, outs, rows, layer, *xsems, start=False)

    any_spec = pl.BlockSpec(memory_space=pl.ANY)
    res = pl.pallas_call(
        body, name=name,
        out_shape=[jax.ShapeDtypeStruct(r.shape, r.dtype) for r in recvs]
        + [jax.ShapeDtypeStruct((N_DEV,) + a.shape, a.dtype) for a in arrs],
        in_specs=[any_spec] * (2 * n + m), out_specs=[any_spec] * (n + m),
        input_output_aliases={n + k: k for k in range(n)},
        scratch_shapes=_exchange_scratch(n) + _gather_scratch(m),
    )(*payload, *recvs, *arrs)
    return res[:n], res[n:]


def _carried_refs(refs, n_in, n_out, n_scratch, n_pay, aliased):
    ins = refs[:n_in]
    pay = refs[n_in:n_in + n_pay]
    o0 = n_in + (2 * n_pay if aliased else n_pay)
    outs = refs[o0:o0 + n_out]
    recvs = refs[o0 + n_out:o0 + n_out + n_pay]
    s0 = o0 + n_out + n_pay
    return ins, outs, refs[s0:s0 + n_scratch], pay, recvs, refs[s0 + n_scratch:]


def _carrier_call(body, name, grid, in_specs, out_specs, out_shape, scratch, operands, payload, recvs=None,
                  gather_layer=None):
    n_in, n_out, n_pay = len(in_specs), len(out_specs), len(payload)
    any_spec = pl.BlockSpec(memory_space=pl.ANY)
    if recvs is None:
        landing = _gathered_shapes(payload, [gather_layer] * n_pay)
        extra_in, aliases = list(payload), {}
        sems = _gather_scratch(n_pay) if n_pay else []
    else:
        landing = [jax.ShapeDtypeStruct(r.shape, r.dtype) for r in recvs]
        extra_in = list(payload) + list(recvs)
        aliases = {n_in + n_pay + k: n_out + k for k in range(n_pay)}
        sems = _exchange_scratch(n_pay) if n_pay else []
    res = pl.pallas_call(
        body, name=name, grid=grid,
        out_shape=list(out_shape) + landing,
        in_specs=list(in_specs) + [any_spec] * len(extra_in),
        out_specs=list(out_specs) + [any_spec] * n_pay,
        input_output_aliases=aliases,
        scratch_shapes=list(scratch) + sems,
        compiler_params=_params(len(grid)),
    )(*operands, *extra_in)
    return res[:n_out], res[n_out:]


def _adamw(w, g, m, v):
    m = ADAM_B1 * m + (1.0 - ADAM_B1) * g
    v = ADAM_B2 * v + (1.0 - ADAM_B2) * jnp.square(g)
    m_hat = m / (1.0 - ADAM_B1 ** ADAM_STEP)
    v_hat = v / (1.0 - ADAM_B2 ** ADAM_STEP)
    delta = -ADAM_LR * (m_hat / (jnp.sqrt(v_hat) + ADAM_EPS) + ADAM_WD * w)
    return delta, m, v


def _ada_fwd(c_all, ada_w, ada_b_cols):
    n_layers, d, cols = ada_w.shape
    b = c_all.shape[0]

    def body(c_ref, w_ref, b_ref, o_ref):
        cv = c_ref[...]
        act = cv * _sigmoid(cv)
        o_ref[...] = jnp.dot(act, w_ref[...], preferred_element_type=F32,
                             precision=lax.Precision.HIGHEST) + b_ref[...]

    return pl.pallas_call(
        body, name="ada_fwd", grid=(n_layers,),
        out_shape=jax.ShapeDtypeStruct((n_layers, b, cols), F32),
        in_specs=[pl.BlockSpec((b, d), lambda l: (0, 0)),
                  pl.BlockSpec((None, d, cols), lambda l: (l, 0, 0)),
                  pl.BlockSpec((None, 1, cols), lambda l: (l, 0, 0))],
        out_specs=pl.BlockSpec((None, b, cols), lambda l: (l, 0, 0)),
        compiler_params=_params(1),
    )(c_all, ada_w, ada_b_cols.reshape(n_layers, 1, cols))


def _ada_bwd(c_all_t, dmod_cols, w, m, v):
    n_layers, d, cols = w.shape
    b = c_all_t.shape[1]
    td = 256 if d % 256 == 0 else d

    def body(c_ref, dm_ref, w_ref, m_ref, v_ref, g_ref, dl_ref, nm_ref, nv_ref):
        cv = c_ref[...]
        act = cv * _sigmoid(cv)
        g = jnp.dot(act, dm_ref[...], preferred_element_type=F32, precision=lax.Precision.HIGHEST)
        delta, nm, nv = _adamw(w_ref[...], g, m_ref[...], v_ref[...])
        g_ref[...] = g
        dl_ref[...] = delta
        nm_ref[...] = nm
        nv_ref[...] = nv

    blk = pl.BlockSpec((None, td, cols), lambda l, i: (l, i, 0))
    shp = jax.ShapeDtypeStruct(w.shape, F32)
    return pl.pallas_call(
        body, name="ada_bwd", grid=(n_layers, d // td),
        out_shape=[shp] * 4,
        in_specs=[pl.BlockSpec((td, b), lambda l, i: (i, 0)),
                  pl.BlockSpec((None, b, cols), lambda l, i: (l, 0, 0)), blk, blk, blk],
        out_specs=[blk] * 4,
        compiler_params=_params(2),
    )(c_all_t, dmod_cols, w, m, v)


def _row_tile(rows, cols, budget=128 * 1024, step=8):
    best = None
    for t in range(step, rows + 1, step):
        if rows % t == 0 and t * cols <= budget:
            best = t
    return best if best is not None else rows


def _adam_reduce(name, parts, w, m, v):
    p, rows, cols = parts.shape
    tr = _row_tile(rows, cols, budget=(256 * 1024) // max(1, p // 4), step=8 if parts.dtype == F32 else 16)

    def body(p_ref, w_ref, m_ref, v_ref, g_ref, dl_ref, nm_ref, nv_ref):
        g = p_ref[0].astype(F32)
        for k in range(1, p):
            g = g + p_ref[k].astype(F32)
        delta, nm, nv = _adamw(w_ref[...], g, m_ref[...], v_ref[...])
        g_ref[...] = g
        dl_ref[...] = delta
        nm_ref[...] = nm
        nv_ref[...] = nv

    blk = pl.BlockSpec((tr, cols), lambda i: (i, 0))
    shp = jax.ShapeDtypeStruct((rows, cols), F32)
    return pl.pallas_call(
        body, name=name, grid=(rows // tr,),
        out_shape=[shp] * 4,
        in_specs=[pl.BlockSpec((p, tr, cols), lambda i: (0, i, 0)), blk, blk, blk],
        out_specs=[blk] * 4,
        compiler_params=_params(1),
    )(parts, w, m, v)


def _sum_parts(parts):
    p = parts.shape[0]

    def body(p_ref, o_ref):
        acc = p_ref[0]
        for k in range(1, p):
            acc = acc + p_ref[k]
        o_ref[...] = acc

    return pl.pallas_call(body, name="loss_sum", out_shape=jax.ShapeDtypeStruct(parts.shape[1:], F32))(parts)


def _loss_grad(y, target, tm):
    t, d = y.shape

    def body(y_ref, t_ref, dy_ref, loss_ref):
        @pl.when(pl.program_id(0) == 0)
        def _():
            loss_ref[...] = jnp.zeros_like(loss_ref)

        diff = y_ref[...] - t_ref[...]
        dy_ref[...] = diff / d
        part = 0.5 * jnp.sum(_mean(diff * diff), axis=0, keepdims=True)
        loss_ref[...] += jnp.broadcast_to(part, loss_ref.shape)

    blk = pl.BlockSpec((tm, d), lambda i: (i, 0))
    return pl.pallas_call(
        body, name="loss_grad", grid=(t // tm,),
        out_shape=[jax.ShapeDtypeStruct((t, d), F32), jax.ShapeDtypeStruct((8, LANES), F32)],
        in_specs=[blk, blk], out_specs=[blk, pl.BlockSpec((8, LANES), lambda i: (0, 0))],
        compiler_params=_params(1),
    )(y, target)


def _matmul_tn(name, a, b, tk, out_dtype):
    ga, t, m = a.shape
    gb, _, n = b.shape
    g = max(ga, gb)
    n_k = t // tk

    def body(a_ref, b_ref, o_ref, acc_ref):
        k = pl.program_id(1)

        @pl.when(k == 0)
        def _():
            acc_ref[...] = jnp.zeros_like(acc_ref)

        acc_ref[...] += _mm_tn(a_ref[...], b_ref[...])

        @pl.when(k == n_k - 1)
        def _():
            o_ref[...] = acc_ref[...].astype(out_dtype)

    return pl.pallas_call(
        body, name=name, grid=(g, n_k),
        out_shape=jax.ShapeDtypeStruct((g, m, n), out_dtype),
        in_specs=[pl.BlockSpec((None, tk, m), (lambda gi, k: (gi, k, 0)) if ga > 1 else (lambda gi, k: (0, k, 0))),
                  pl.BlockSpec((None, tk, n), (lambda gi, k: (gi, k, 0)) if gb > 1 else (lambda gi, k: (0, k, 0)))],
        out_specs=pl.BlockSpec((None, m, n), lambda gi, k: (gi, 0, 0)),
        scratch_shapes=[pltpu.VMEM((m, n), F32)],
        compiler_params=_params(2),
    )(a, b)


def _time_of_row(tm):
    i = lax.broadcasted_iota(jnp.int32, (tm, 1), 0)
    return (i % 8) * (tm // 8) + i // 8


def _sublane_is(rows, s):
    return lax.broadcasted_iota(jnp.int32, (rows, 1), 0) % 8 == s


def _conv_taps(src_ref, col0, ncols, tm, tap_rows, w_ref, init_of, store):
    rc = min(ROW_CHUNK, tm)
    for cb in range(ncols // LANES):
        cs = slice(cb * LANES, (cb + 1) * LANES)
        ss = slice(col0 + cb * LANES, col0 + (cb + 1) * LANES)
        for r0 in range(0, tm, rc):
            acc = init_of(cs, rc)
            for k, row in enumerate(tap_rows):
                acc = acc + w_ref[k:k + 1, cs] * src_ref[r0 + row:r0 + row + rc, ss]
            store(r0, rc, cs, acc)


def _mixer_fwd(l, x, mod, gains, w_in_t, conv_w, vec, pool_w, w_out, *, seq, tm, act_dtype, payload=(),
               gather_layer=None):
    t, d = x.shape
    d_in = w_in_t.shape[0]
    dc = conv_w.shape[-1]
    d_mix = w_out.shape[0]
    n_taps = 31
    nt, tps = t // tm, seq // tm
    p = CONV_PREFIX
    n_pay = len(payload)

    def body(*refs):
        ins, outs, scr, pay_refs, got_refs, sems = _carried_refs(refs, 8, 7, 2, n_pay, aliased=False)
        x_ref, mod_ref, g_ref, win_ref, cw_ref, v_ref, pw_ref, wout_ref = ins
        x1_ref, h1_ref, u_ref, a1_ref, ap_ref, o_ref, tail_ref = outs
        ext_ref, car_ref = scr
        i = pl.program_id(0)
        first = (i % tps) == 0

        if n_pay:
            @pl.when(i == 0)
            def _():
                _gather_phase(_layer_of(pay_refs, [gather_layer] * n_pay), got_refs, *sems, 0)

        xv = x_ref[...]
        r = lax.rsqrt(_mean(xv * xv) + EPS)
        hv = (xv * r) * g_ref[0:1, :] * (1.0 + mod_ref[1:2, :]) + mod_ref[0:1, :]
        hb = hv.astype(act_dtype)
        h1_ref[...] = hb
        u = _mm_nt(hb, win_ref[...])
        u_ref[...] = u
        a0 = u[:, :dc] * _sigmoid(u[:, dc:2 * dc])

        @pl.when(first)
        def _():
            car_ref[...] = jnp.zeros_like(car_ref)

        @pl.when(i == 0)
        def _():
            ext_ref[p + tm:p + tm + 8, :] = jnp.zeros((8, ext_ref.shape[1]), F32)

        ext_ref[p:p + tm, 0:dc] = a0
        ext_ref[p:p + tm, dc:] = u[:, 2 * dc:]
        ext_ref[0:p, :] = jnp.where(_sublane_is(p, 0), car_ref[...], ext_ref[tm - 1:tm - 1 + p, :])

        def store(r0, rc, cs, acc):
            a1_ref[r0:r0 + rc, cs] = acc

        _conv_taps(ext_ref, 0, dc, tm, [p - 8 * (n_taps - 1 - k) for k in range(n_taps)],
                   cw_ref,
                   lambda cs, rc: jnp.broadcast_to(v_ref[0:1, cs], (rc, LANES)), store)
        a1 = a1_ref[...]
        mu = _mean(a1)
        xc = a1 - mu
        rstd = lax.rsqrt(_mean(xc * xc) + EPS)
        a2 = (xc * rstd) * v_ref[1:2, :] + v_ref[2:3, :]
        ap_ref[:, 0:dc] = (a2 * _sigmoid(a2)).astype(act_dtype)

        pos = (i % tps) * tm + _time_of_row(tm)
        for g, w in enumerate(POOL_WINDOWS):
            cs = slice(dc + g * POOL_GROUP, dc + (g + 1) * POOL_GROUP)
            s = ext_ref[p:p + tm, cs]
            for j in range(1, w):
                s = s + ext_ref[p - 8 * j:p - 8 * j + tm, cs]
            cnt = jnp.minimum(pos + 1, w).astype(F32)
            dv = s / cnt - ext_ref[p:p + tm, cs]
            q = _mm(dv, pw_ref[g])
            ap_ref[:, cs] = (q * v_ref[3:4, g * POOL_GROUP:(g + 1) * POOL_GROUP]).astype(act_dtype)

        o = _mm(ap_ref[...], wout_ref[...])
        o_ref[...] = o
        ro = lax.rsqrt(_mean(o * o) + EPS)
        x1_ref[...] = xv + (1.0 + mod_ref[2:3, :]) * ((o * ro) * g_ref[1:2, :])

        nxt = ext_ref[tm + 7:tm + 7 + p, :]
        car_ref[...] = nxt
        tail_ref[...] = nxt

        if n_pay:
            @pl.when(i == max(nt - 2, 0))
            def _():
                _gather_phase(_layer_of(pay_refs, [gather_layer] * n_pay), got_refs, *sems, 1)

            @pl.when(i == nt - 1)
            def _():
                _gather_phase(_layer_of(pay_refs, [gather_layer] * n_pay), got_refs, *sems, 2)

    row = lambda width: pl.BlockSpec((tm, width), lambda i: (i, 0))
    return _carrier_call(
        body, f"mixer_fwd_{l}", (nt,),
        out_shape=[jax.ShapeDtypeStruct((t, d), F32), jax.ShapeDtypeStruct((t, d), act_dtype),
                   jax.ShapeDtypeStruct((t, d_in), F32), jax.ShapeDtypeStruct((t, dc), F32),
                   jax.ShapeDtypeStruct((t, d_mix), act_dtype), jax.ShapeDtypeStruct((t, d), F32),
                   jax.ShapeDtypeStruct((nt, p, d_mix), F32)],
        in_specs=[row(d),
                  pl.BlockSpec((None, None, 8, d), lambda i: (l, i // tps, 0, 0)),
                  _const_spec((None, 8, d), (l, 0, 0)),
                  _const_spec((d_in, d), (0, 0)),
                  _const_spec((32, dc), (0, 0)),
                  _const_spec((None, 8, dc), (l, 0, 0)),
                  _const_spec((None, len(POOL_WINDOWS), POOL_GROUP, POOL_GROUP), (l, 0, 0, 0)),
                  _const_spec((d_mix, d), (0, 0))],
        out_specs=[row(d), row(d), row(d_in), row(dc), row(d_mix), row(d),
                   pl.BlockSpec((None, p, d_mix), lambda i: (i, 0, 0))],
        scratch=[pltpu.VMEM((p + tm + 8, d_mix), F32), pltpu.VMEM((p, d_mix), F32)],
        operands=(x, mod, gains, w_in_t, conv_w, vec, pool_w, w_out),
        payload=payload, gather_layer=gather_layer)


def _mixer_bwd(l, dx1, x, mod, gains, u, tails, a1, o, w_in_t, conv_w, vec, pool_w, w_out, *, seq, tm, act_dtype,
               payload=(), recvs=(), rows=(), pay_layer=0):
    t, d = x.shape
    d_in = w_in_t.shape[0]
    dc = conv_w.shape[-1]
    d_mix = w_out.shape[0]
    n_taps = 31
    nt, tps = t // tm, seq // tm
    p = CONV_PREFIX
    n_groups = len(POOL_WINDOWS)
    n_pay = len(payload)

    def body(*refs):
        ins, outs, scr, pay_refs, recv_refs, sems = _carried_refs(refs, 13, 8, 3, n_pay, aliased=True)
        dx1_ref, x_ref, mod_ref, g_ref, u_ref, tail_ref, a1_ref, o_ref, win_ref, cw_ref, v_ref, pw_ref, wout_ref = ins
        dx_ref, du_ref, do_ref, dmod_ref, gd_ref, gv_ref, dcw_ref, dpw_ref = outs
        ext_ref, fext_ref, fcar_ref = scr
        i = pl.program_id(0)
        j = nt - 1 - i
        first_in_seq = (j % tps) == 0
        last_in_seq = (j % tps) == tps - 1

        if n_pay:
            @pl.when(i == 0)
            def _():
                _exchange(pay_refs, recv_refs, rows, pay_layer, *sems, start=True)

        @pl.when(i == 0)
        def _():
            gd_ref[...] = jnp.zeros_like(gd_ref)
            gv_ref[...] = jnp.zeros_like(gv_ref)
            dcw_ref[...] = jnp.zeros_like(dcw_ref)
            dpw_ref[...] = jnp.zeros_like(dpw_ref)
            fext_ref[0:8, :] = jnp.zeros((8, fext_ref.shape[1]), F32)

        @pl.when(last_in_seq)
        def _():
            dmod_ref[...] = jnp.zeros_like(dmod_ref)
            fcar_ref[...] = jnp.zeros_like(fcar_ref)

        xv = x_ref[...]
        dx1v = dx1_ref[...]
        pre_g, post_g = g_ref[0:1, :], g_ref[1:2, :]

        ov = o_ref[...]
        ro = lax.rsqrt(_mean(ov * ov) + EPS)
        yo = ov * ro
        dmod_ref[2:3, :] += _colsum(dx1v * (yo * post_g))
        dn = dx1v * (1.0 + mod_ref[2:3, :])
        gd_ref[1:2, :] += _colsum(dn * yo)
        dyo = dn * post_g
        do = ro * (dyo - yo * _mean(dyo * yo))
        dob = do.astype(act_dtype)
        do_ref[...] = dob
        dap = _mm_nt(dob, wout_ref[...])

        uv = u_ref[...]
        val, gate = uv[:, :dc], uv[:, dc:2 * dc]
        sg = _sigmoid(gate)
        ext_ref[p:p + tm, 0:dc] = val * sg
        ext_ref[p:p + tm, dc:] = uv[:, 2 * dc:]
        keep = jnp.where(first_in_seq, 0.0, 1.0).astype(F32)
        ext_ref[0:p, :] = jnp.where(_sublane_is(p, 0), tail_ref[...] * keep, ext_ref[tm - 1:tm - 1 + p, :])

        a1v = a1_ref[...]
        mu = _mean(a1v)
        xc = a1v - mu
        rstd = lax.rsqrt(_mean(xc * xc) + EPS)
        xh = xc * rstd
        ln_g = v_ref[1:2, :]
        a2 = xh * ln_g + v_ref[2:3, :]
        s2 = _sigmoid(a2)
        da2 = dap[:, :dc] * (s2 * (1.0 + a2 * (1.0 - s2)))
        gv_ref[1:2, :] += _colsum(da2 * xh)
        gv_ref[2:3, :] += _colsum(da2)
        dxh = da2 * ln_g
        da1 = rstd * (dxh - _mean(dxh) - xh * _mean(dxh * xh))
        gv_ref[0:1, :] += _colsum(da1)
        last_sublane = _sublane_is(p, 7)

        def put(cs, value):
            fext_ref[8:8 + tm, cs] = value
            fext_ref[8 + tm:8 + tm + p, cs] = jnp.where(last_sublane, fcar_ref[:, cs], fext_ref[9:9 + p, cs])

        put(slice(0, dc), da1)

        for k in range(n_taps):
            row = p - 8 * (n_taps - 1 - k)
            dcw_ref[k:k + 1, :] += _colsum(fext_ref[8:8 + tm, 0:dc] * ext_ref[row:row + tm, 0:dc])

        def store(r0, rc, cs, acc):
            sgc = _sigmoid(u_ref[r0:r0 + rc, dc + cs.start:dc + cs.stop])
            vc = u_ref[r0:r0 + rc, cs]
            du_ref[r0:r0 + rc, cs] = (acc * sgc).astype(act_dtype)
            du_ref[r0:r0 + rc, dc + cs.start:dc + cs.stop] = (acc * vc * sgc * (1.0 - sgc)).astype(act_dtype)

        _conv_taps(fext_ref, 0, dc, tm, [8 + 8 * (n_taps - 1 - k) for k in range(n_taps)],
                   cw_ref,
                   lambda cs, rc: jnp.zeros((rc, LANES), F32), store)

        pos = (j % tps) * tm + _time_of_row(tm)
        for g, w in enumerate(POOL_WINDOWS):
            cs = slice(dc + g * POOL_GROUP, dc + (g + 1) * POOL_GROUP)
            gs = slice(g * POOL_GROUP, (g + 1) * POOL_GROUP)
            s = ext_ref[p:p + tm, cs]
            for jj in range(1, w):
                s = s + ext_ref[p - 8 * jj:p - 8 * jj + tm, cs]
            cnt = jnp.minimum(pos + 1, w).astype(F32)
            dv = (s / cnt - ext_ref[p:p + tm, cs]).astype(MXU_DTYPE)
            q = _mm(dv, pw_ref[g])
            dp = dap[:, cs]
            gv_ref[3:4, gs] += _colsum(dp * q)
            dq = (dp * v_ref[3:4, gs]).astype(MXU_DTYPE)
            dpw_ref[g] += _mm_tn(dv, dq)
            dd = _mm_nt(dq, pw_ref[g])
            put(cs, dd / cnt)
            dhp = fext_ref[8:8 + tm, cs]
            for jj in range(1, w):
                dhp = dhp + fext_ref[8 + 8 * jj:8 + 8 * jj + tm, cs]
            du_ref[:, dc + cs.start:dc + cs.stop] = (dhp - dd).astype(act_dtype)

        fcar_ref[...] = fext_ref[1:1 + p, :]

        dh = _mm(du_ref[...], win_ref[...])

        r = lax.rsqrt(_mean(xv * xv) + EPS)
        xn = xv * r
        dmod_ref[0:1, :] += _colsum(dh)
        dmod_ref[1:2, :] += _colsum(dh * (xn * pre_g))
        dy = dh * (1.0 + mod_ref[1:2, :])
        gd_ref[0:1, :] += _colsum(dy * xn)
        dxn = dy * pre_g
        dx_ref[...] = dx1v + r * (dxn - xn * _mean(dxn * xn))

        if n_pay:
            @pl.when(i == nt - 1)
            def _():
                _exchange(pay_refs, recv_refs, rows, pay_layer, *sems, start=False)

    rev = lambda width: pl.BlockSpec((tm, width), lambda i: (nt - 1 - i, 0))
    return _carrier_call(
        body, f"mixer_bwd_{l}", (nt,),
        out_shape=[jax.ShapeDtypeStruct((t, d), F32), jax.ShapeDtypeStruct((t, d_in), act_dtype),
                   jax.ShapeDtypeStruct((t, d), act_dtype),
                   jax.ShapeDtypeStruct((t // seq, 8, d), F32), jax.ShapeDtypeStruct((8, d), F32),
                   jax.ShapeDtypeStruct((8, dc), F32), jax.ShapeDtypeStruct((32, dc), F32),
                   jax.ShapeDtypeStruct((n_groups, POOL_GROUP, POOL_GROUP), F32)],
        in_specs=[rev(d), rev(d),
                  pl.BlockSpec((None, None, 8, d), lambda i: (l, (nt - 1 - i) // tps, 0, 0)),
                  _const_spec((None, 8, d), (l, 0, 0)),
                  rev(d_in),
                  pl.BlockSpec((None, p, d_mix), lambda i: (jnp.maximum(nt - 2 - i, 0), 0, 0)),
                  rev(dc), rev(d),
                  _const_spec((d_in, d), (0, 0)),
                  _const_spec((32, dc), (0, 0)),
                  _const_spec((None, 8, dc), (l, 0, 0)),
                  _const_spec((None, n_groups, POOL_GROUP, POOL_GROUP), (l, 0, 0, 0)),
                  _const_spec((d_mix, d), (0, 0))],
        out_specs=[rev(d), rev(d_in), rev(d),
                   pl.BlockSpec((None, 8, d), lambda i: ((nt - 1 - i) // tps, 0, 0)),
                   _acc_spec((8, d), (0, 0)), _acc_spec((8, dc), (0, 0)), _acc_spec((32, dc), (0, 0)),
                   _acc_spec((n_groups, POOL_GROUP, POOL_GROUP), (0, 0, 0))],
        scratch=[pltpu.VMEM((p + tm, d_mix), F32), pltpu.VMEM((8 + tm + p, d_mix), F32),
                 pltpu.VMEM((p, d_mix), F32)],
        operands=(dx1, x, mod, gains, u, tails, a1, o, w_in_t, conv_w, vec, pool_w, w_out),
        payload=payload, recvs=recvs)


def _ffn_fwd(l, x, mod, gains, up, fcw, down, *, seq, tm, act_dtype, payload=(), gather_layer=None):
    t, d = x.shape
    n_chunks, _, fc = up.shape
    half = n_chunks // 2
    nt, tps = t // tm, seq // tm
    p = FFN_PREFIX
    n_pay = len(payload)

    def body(*refs):
        ins, outs, scr, pay_refs, got_refs, sems = _carried_refs(refs, 6, 6, 2, n_pay, aliased=False)
        x_ref, mod_ref, g_ref, up_ref, fcw_ref, down_ref = ins
        x2_ref, h2_ref, u2_ref, u3_ref, hid_ref, o2_ref = outs
        ext_ref, car_ref = scr
        i = pl.program_id(0)
        first = (i % tps) == 0

        if n_pay:
            @pl.when(i == 0)
            def _():
                _gather_phase(_layer_of(pay_refs, [gather_layer] * n_pay), got_refs, *sems, 0)

        xv = x_ref[...]
        r = lax.rsqrt(_mean(xv * xv) + EPS)
        hv = (xv * r) * g_ref[0:1, :] * (1.0 + mod_ref[4:5, :]) + mod_ref[3:4, :]
        hb = hv.astype(act_dtype)
        h2_ref[...] = hb

        @pl.when(first)
        def _():
            car_ref[...] = jnp.zeros_like(car_ref)

        @pl.when(i == 0)
        def _():
            for n in range(n_chunks):
                ext_ref[n, p + tm:p + tm + 8, :] = jnp.zeros((8, fc), F32)

        first_sublane = _sublane_is(p, 0)

        def project(n):
            un = _mm(hb, up_ref[n])
            ext_ref[n, p:p + tm, :] = un
            u2_ref[n] = un.astype(act_dtype)
            ext_ref[n, 0:p, :] = jnp.where(first_sublane, car_ref[n], ext_ref[n, tm - 1:tm - 1 + p, :])
            car_ref[n] = ext_ref[n, tm + 7:tm + 7 + p, :]

        def conv(n):
            return (fcw_ref[n, 3:4, :] + fcw_ref[n, 0:1, :] * ext_ref[n, p - 16:p - 16 + tm, :]
                    + fcw_ref[n, 1:2, :] * ext_ref[n, p - 8:p - 8 + tm, :]
                    + fcw_ref[n, 2:3, :] * ext_ref[n, p:p + tm, :])

        for n in range(n_chunks):
            project(n)
        o2 = jnp.zeros((tm, d), F32)
        for n in range(half):
            gt = conv(n + half)
            v = conv(n)
            u3_ref[n] = v.astype(act_dtype)
            u3_ref[n + half] = gt.astype(act_dtype)
            hid = ((gt * _sigmoid(gt)) * v).astype(act_dtype)
            hid_ref[n] = hid
            o2 = o2 + _mm(hid, down_ref[n])
        o2_ref[...] = o2
        ro = lax.rsqrt(_mean(o2 * o2) + EPS)
        x2_ref[...] = xv + (1.0 + mod_ref[5:6, :]) * ((o2 * ro) * g_ref[1:2, :])

        if n_pay:
            @pl.when(i == max(nt - 2, 0))
            def _():
                _gather_phase(_layer_of(pay_refs, [gather_layer] * n_pay), got_refs, *sems, 1)

            @pl.when(i == nt - 1)
            def _():
                _gather_phase(_layer_of(pay_refs, [gather_layer] * n_pay), got_refs, *sems, 2)

    row = lambda width: pl.BlockSpec((tm, width), lambda i: (i, 0))
    chunked = lambda n: pl.BlockSpec((n, tm, fc), lambda i: (0, i, 0))
    return _carrier_call(
        body, f"ffn_fwd_{l}", (nt,),
        out_shape=[jax.ShapeDtypeStruct((t, d), F32), jax.ShapeDtypeStruct((t, d), act_dtype),
                   jax.ShapeDtypeStruct((n_chunks, t, fc), act_dtype),
                   jax.ShapeDtypeStruct((n_chunks, t, fc), act_dtype),
                   jax.ShapeDtypeStruct((half, t, fc), act_dtype), jax.ShapeDtypeStruct((t, d), F32)],
        in_specs=[row(d),
                  pl.BlockSpec((None, None, 8, d), lambda i: (l, i // tps, 0, 0)),
                  _const_spec((None, 8, d), (l, 0, 0)),
                  _const_spec((n_chunks, d, fc), (0, 0, 0)),
                  _const_spec((n_chunks, 8, fc), (0, 0, 0)),
                  _const_spec((half, fc, d), (0, 0, 0))],
        out_specs=[row(d), row(d), chunked(n_chunks), chunked(n_chunks), chunked(half), row(d)],
        scratch=[pltpu.VMEM((n_chunks, p + tm + 8, fc), F32), pltpu.VMEM((n_chunks, p, fc), F32)],
        operands=(x, mod, gains, up, fcw, down),
        payload=payload, gather_layer=gather_layer)


def _ffn_bwd(l, dx2, x, mod, gains, u2, u3, o2, up, fcw, down, *, seq, tm, act_dtype,
             payload=(), recvs=(), rows=(), pay_layer=0):
    t, d = x.shape
    n_chunks, _, fc = up.shape
    half = n_chunks // 2
    nt, tps = t // tm, seq // tm
    p = FFN_PREFIX
    n_pay = len(payload)

    def body(*refs):
        ins, outs, scr, pay_refs, recv_refs, sems = _carried_refs(refs, 10, 6, 2, n_pay, aliased=True)
        dx2_ref, x_ref, mod_ref, g_ref, u2_ref, u3_ref, o2_ref, up_ref, fcw_ref, down_ref = ins
        dx1_ref, du2_ref, do2_ref, dmod_ref, gd_ref, dfcw_ref = outs
        fext_ref, fcar_ref = scr
        i = pl.program_id(0)
        j = nt - 1 - i
        last_in_seq = (j % tps) == tps - 1

        if n_pay:
            @pl.when(i == 0)
            def _():
                _exchange(pay_refs, recv_refs, rows, pay_layer, *sems, start=True)

        @pl.when(i == 0)
        def _():
            gd_ref[...] = jnp.zeros_like(gd_ref)
            dfcw_ref[...] = jnp.zeros_like(dfcw_ref)
            for n in range(n_chunks):
                fext_ref[n, 0:8, :] = jnp.zeros((8, fc), F32)

        @pl.when(last_in_seq)
        def _():
            dmod_ref[...] = jnp.zeros_like(dmod_ref)
            fcar_ref[...] = jnp.zeros_like(fcar_ref)

        xv = x_ref[...]
        dx2v = dx2_ref[...]
        pre_g, post_g = g_ref[0:1, :], g_ref[1:2, :]

        ov = o2_ref[...]
        ro = lax.rsqrt(_mean(ov * ov) + EPS)
        yo = ov * ro
        dmod_ref[2:3, :] += _colsum(dx2v * (yo * post_g))
        dn = dx2v * (1.0 + mod_ref[5:6, :])
        gd_ref[1:2, :] += _colsum(dn * yo)
        dyo = dn * post_g
        do = ro * (dyo - yo * _mean(dyo * yo))
        dob = do.astype(act_dtype)
        do2_ref[...] = dob

        for n in range(half):
            v = u3_ref[n].astype(F32)
            gt = u3_ref[n + half].astype(F32)
            sg = _sigmoid(gt)
            dhid = _mm_nt(dob, down_ref[n])
            fext_ref[n, 8:8 + tm, :] = dhid * (gt * sg)
            fext_ref[n + half, 8:8 + tm, :] = dhid * v * (sg * (1.0 + gt * (1.0 - sg)))

        last_sublane = _sublane_is(p, 7)
        dh = jnp.zeros((tm, d), F32)
        for n in range(n_chunks):
            fext_ref[n, 8 + tm:8 + tm + p, :] = jnp.where(last_sublane, fcar_ref[n], fext_ref[n, 9:9 + p, :])
            fcar_ref[n] = fext_ref[n, 1:1 + p, :]
            d2 = fext_ref[n, 8:8 + tm, :]
            d1 = fext_ref[n, 16:16 + tm, :]
            d0 = fext_ref[n, 24:24 + tm, :]
            u2v = u2_ref[n].astype(F32)
            dfcw_ref[n, 3:4, :] += _colsum(d2)
            dfcw_ref[n, 0:1, :] += _colsum(d0 * u2v)
            dfcw_ref[n, 1:2, :] += _colsum(d1 * u2v)
            dfcw_ref[n, 2:3, :] += _colsum(d2 * u2v)
            du2 = (fcw_ref[n, 0:1, :] * d0 + fcw_ref[n, 1:2, :] * d1 + fcw_ref[n, 2:3, :] * d2).astype(act_dtype)
            du2_ref[n] = du2
            dh = dh + _mm_nt(du2, up_ref[n])

        r = lax.rsqrt(_mean(xv * xv) + EPS)
        xn = xv * r
        dmod_ref[0:1, :] += _colsum(dh)
        dmod_ref[1:2, :] += _colsum(dh * (xn * pre_g))
        dy = dh * (1.0 + mod_ref[4:5, :])
        gd_ref[0:1, :] += _colsum(dy * xn)
        dxn = dy * pre_g
        dx1_ref[...] = dx2v + r * (dxn - xn * _mean(dxn * xn))

        if n_pay:
            @pl.when(i == nt - 1)
            def _():
                _exchange(pay_refs, recv_refs, rows, pay_layer, *sems, start=False)

    rev = lambda width: pl.BlockSpec((tm, width), lambda i: (nt - 1 - i, 0))
    chunked = pl.BlockSpec((n_chunks, tm, fc), lambda i: (0, nt - 1 - i, 0))
    return _carrier_call(
        body, f"ffn_bwd_{l}", (nt,),
        out_shape=[jax.ShapeDtypeStruct((t, d), F32), jax.ShapeDtypeStruct((n_chunks, t, fc), act_dtype),
                   jax.ShapeDtypeStruct((t, d), act_dtype),
                   jax.ShapeDtypeStruct((t // seq, 8, d), F32), jax.ShapeDtypeStruct((8, d), F32),
                   jax.ShapeDtypeStruct((n_chunks, 8, fc), F32)],
        in_specs=[rev(d), rev(d),
                  pl.BlockSpec((None, None, 8, d), lambda i: (l, (nt - 1 - i) // tps, 0, 0)),
                  _const_spec((None, 8, d), (l, 0, 0)),
                  chunked, chunked, rev(d),
                  _const_spec((n_chunks, d, fc), (0, 0, 0)),
                  _const_spec((n_chunks, 8, fc), (0, 0, 0)),
                  _const_spec((half, fc, d), (0, 0, 0))],
        out_specs=[rev(d), chunked, rev(d),
                   pl.BlockSpec((None, 8, d), lambda i: ((nt - 1 - i) // tps, 0, 0)),
                   _acc_spec((8, d), (0, 0)), _acc_spec((n_chunks, 8, fc), (0, 0, 0))],
        scratch=[pltpu.VMEM((n_chunks, 8 + tm + p, fc), F32), pltpu.VMEM((n_chunks, p, fc), F32)],
        operands=(dx2, x, mod, gains, u2, u3, o2, up, fcw, down),
        payload=payload, recvs=recvs)


def _pad_rows(a, rows):
    pad = [(0, 0)] * a.ndim
    pad[-2] = (0, rows - a.shape[-2])
    return jnp.pad(a, pad)


def kernel(x, c, ada_w, ada_b, pre_mix_g, post_mix_g, w_in, conv_w, conv_b, conv_ln_g, conv_ln_b, pool_w, pool_scale, w_out, pre_ffn_g, post_ffn_g, ffn_up, ffn_conv_w, ffn_conv_b, ffn_down, loss_target, m_ada_w, m_ada_b, m_pre_mix_g, m_post_mix_g, m_w_in, m_conv_w, m_conv_b, m_conv_ln_g, m_conv_ln_b, m_pool_w, m_pool_scale, m_w_out, m_pre_ffn_g, m_post_ffn_g, m_ffn_up, m_ffn_conv_w, m_ffn_conv_b, m_ffn_down, v_ada_w, v_ada_b, v_pre_mix_g, v_post_mix_g, v_w_in, v_conv_w, v_conv_b, v_conv_ln_g, v_conv_ln_b, v_pool_w, v_pool_scale, v_w_out, v_pre_ffn_g, v_post_ffn_g, v_ffn_up, v_ffn_conv_w, v_ffn_conv_b, v_ffn_down):
    bl, seq, d = x.shape
    n_layers = ada_w.shape[0]
    t = bl * seq
    dc = conv_b.shape[1]
    d_in = w_in.shape[2] * N_DEV
    d_mix = w_out.shape[1] * N_DEV
    n_taps = conv_w.shape[1]
    fc = ffn_up.shape[2]
    half = N_DEV // 2
    ada_cols = ada_w.shape[2]
    n_mod = ada_cols * N_DEV // d
    assert pool_scale.shape[1] == dc and n_taps == 31 and n_mod == 6 and ffn_conv_w.shape[1] == 3
    assert pool_w.shape[1:] == (len(POOL_WINDOWS), POOL_GROUP, POOL_GROUP)
    tm = TILE_TOKENS
    assert seq % tm == 0 and CONV_PREFIX <= tm - 8
    tk = 2048 if t % 2048 == 0 else tm
    act = MXU_DTYPE

    def tile_order(a, inverse=False):
        shape = (t // tm, tm // 8, 8, d) if inverse else (t // tm, 8, tm // 8, d)
        return a.reshape(shape).transpose(0, 2, 1, 3).reshape(t, d)

    ax = lax.axis_index
    me = 4 * ax("x") + 2 * ax("y") + ax("c")

    (c_all,) = _all_gather([c], "gather_c")
    c_all = c_all.reshape(N_DEV * bl, d)
    ada_b_cols = lax.dynamic_slice_in_dim(ada_b, me * ada_cols, ada_cols, axis=1)
    mod_cols = _ada_fwd(c_all, ada_w, ada_b_cols)

    w_in_s = w_in.astype(act).transpose(0, 2, 1)
    w_out_s, up_s, down_s = w_out.astype(act), ffn_up.astype(act), ffn_down.astype(act)
    conv_w_s = _pad_rows(conv_w, 32)
    fcw_s = _pad_rows(jnp.concatenate(
        [ffn_conv_w, lax.dynamic_slice_in_dim(ffn_conv_b, me * fc, fc, axis=1)[:, None, :]], axis=1), 8)

    early_shards = (w_in_s, w_out_s, conv_w_s, down_s, fcw_s)

    def early_weights(g_w_in, g_w_out, g_conv_w, g_down, g_fcw):
        mixer = (g_w_in.reshape(d_in, d), g_conv_w.transpose(1, 0, 2).reshape(32, dc), g_w_out.reshape(d_mix, d))
        return mixer, (g_fcw, g_down.reshape(half, fc, d))

    g0 = _all_gather([mod_cols, *early_shards], "gather_w0", layers=[None, 0, 0, 0, 0, 0])
    mod_all = g0[0].transpose(1, 2, 0, 3).reshape(n_layers, N_DEV * bl, n_mod, d)
    mod = _pad_rows(lax.dynamic_slice_in_dim(mod_all, me * bl, bl, axis=1), 8)
    early = [early_weights(*g0[1:])]
    ups = []
    gains_mix = _pad_rows(jnp.stack([pre_mix_g, post_mix_g], axis=1), 8)
    gains_ffn = _pad_rows(jnp.stack([pre_ffn_g, post_ffn_g], axis=1), 8)
    vec = _pad_rows(jnp.stack([conv_b, conv_ln_g, conv_ln_b, pool_scale], axis=1), 8)
    pool_w_b = pool_w.astype(act)

    kw = dict(seq=seq, tm=tm, act_dtype=act)
    xs = tile_order(x.reshape(t, d))
    saved = []
    for l in range(n_layers):
        (w_in_t, cw, w_o), (fcw_l, down_l) = early[l]
        (x1, h1, u, a1, ap, o, tails), (up_l,) = _mixer_fwd(
            l, xs, mod, gains_mix, w_in_t, cw, vec, pool_w_b, w_o, **kw, payload=(up_s,), gather_layer=l)
        ups.append(up_l)
        (x2, h2, u2, u3, hid, o2), got = _ffn_fwd(l, x1, mod, gains_ffn, up_l, fcw_l, down_l, **kw,
                                                  payload=early_shards if l + 1 < n_layers else (),
                                                  gather_layer=l + 1)
        if got:
            early.append(early_weights(*got))
        saved.append((xs, h1, u, tails, a1, ap, o, x1, h2, u2, u3, hid, o2))
        xs = x2

    dx, loss_part = _loss_grad(xs, tile_order(loss_target.reshape(t, d)), tm)

    def landing(shard, dtype):
        return lax.empty((N_DEV, n_layers) + shard.shape[1:], dtype)

    r_up, r_down = landing(ffn_up, act), landing(ffn_down, act)
    r_w_in, r_w_out = landing(w_in, act), landing(w_out, act)
    r_conv_w, r_fcw = landing(conv_w, F32), landing(ffn_conv_w, F32)
    def early_rows(l):
        return (7 * d // 10 if l == 0 else 2 * d // 5) // 16 * 16

    def late_rows(l):
        return ((early_rows(l), d - early_rows(l)), None, None, None, None)
    dmods, smalls = [], []
    pending = ()
    for l in reversed(range(n_layers)):
        xin, h1, u, tails, a1, ap, o, x1, h2, u2, u3, hid, o2 = saved[l]
        (w_in_t, cw, w_o), (fcw_l, down_l) = early[l]
        up_l = ups[l]
        (dx1, du2, do2, dmod_b, gd_b, dfcw), got = _ffn_bwd(
            l, dx, x1, mod, gains_ffn, u2, u3, o2, up_l, fcw_l, down_l, **kw,
            payload=pending, recvs=(r_up, r_w_in, r_w_out, r_conv_w, r_fcw) if pending else (),
            rows=late_rows(l + 1), pay_layer=l + 1)
        if pending:
            r_up, r_w_in, r_w_out, r_conv_w, r_fcw = got
        p_up = _matmul_tn(f"dw_up_{l}", h2[None], du2, tk, act)
        p_down = _matmul_tn(f"dw_down_{l}", hid, do2[None], tk, act).reshape(N_DEV, fc // 2, d)
        (dx, du, do, dmod_a, gd_a, gv, dcw, dpw), (r_down, r_up) = _mixer_bwd(
            l, dx1, xin, mod, gains_mix, u, tails, a1, o, w_in_t, cw, vec, pool_w_b, w_o, **kw,
            payload=(p_down, p_up), recvs=(r_down, r_up), rows=(None, (0, early_rows(l))), pay_layer=l)
        dw_in = _matmul_tn(f"dw_in_{l}", h1[None], du[None], tk, act)[0]
        p_w_in = dw_in.reshape(d, N_DEV, d_in // N_DEV).transpose(1, 0, 2)
        p_w_out = _matmul_tn(f"dw_out_{l}", ap[None], do[None], tk, act)[0].reshape(N_DEV, d_mix // N_DEV, d)
        p_conv_w = dcw[:n_taps].reshape(n_taps, N_DEV, dc // N_DEV).transpose(1, 0, 2)
        pending = (p_up, p_w_in, p_w_out, p_conv_w, dfcw[:, :3, :])
        dmods.append(jnp.concatenate([dmod_a[:, 0:3], dmod_b[:, 0:3]], axis=1).reshape(bl, n_mod * d))
        smalls.append(jnp.concatenate(
            [gd_a[0], gd_a[1], gv[0], gv[1], gv[2], gv[3], gd_b[0], gd_b[1], dfcw[:, 3, :].reshape(-1),
             dpw.reshape(-1)]))
    dmods.reverse()
    smalls.reverse()

    dmod_loc = jnp.stack(dmods)
    small_loc = jnp.stack(smalls)
    n_small = small_loc.shape[1]
    small_cols = 8 * LANES if (n_layers * n_small) % (8 * LANES) == 0 else LANES
    (r_up, r_w_in, r_w_out, r_conv_w, r_fcw), (g_dmod, g_small, g_loss) = _exchange_and_gather(
        pending, (r_up, r_w_in, r_w_out, r_conv_w, r_fcw), late_rows(0), 0,
        [dmod_loc.astype(act), small_loc.reshape(-1, small_cols).astype(act), loss_part], "exchange_tail")
    g_dmod = g_dmod.astype(F32)

    loss = _sum_parts(g_loss)[0, 0]

    def flat2(a):
        return a.reshape(-1, a.shape[-1])

    def update(name, parts, w, m, v):
        outs = _adam_reduce(name, parts.reshape(parts.shape[0], -1, w.shape[-1]), flat2(w), flat2(m), flat2(v))
        return [o_.reshape(w.shape) for o_ in outs]

    res = {}
    res["w_in"] = update("adam_w_in", r_w_in, w_in, m_w_in, v_w_in)
    res["w_out"] = update("adam_w_out", r_w_out, w_out, m_w_out, v_w_out)
    res["ffn_up"] = update("adam_ffn_up", r_up, ffn_up, m_ffn_up, v_ffn_up)
    res["ffn_down"] = update("adam_ffn_down", r_down, ffn_down, m_ffn_down, v_ffn_down)
    res["conv_w"] = update("adam_conv_w", r_conv_w, conv_w, m_conv_w, v_conv_w)
    res["ffn_conv_w"] = update("adam_ffn_conv_w", r_fcw, ffn_conv_w, m_ffn_conv_w, v_ffn_conv_w)

    dmod_all = g_dmod.transpose(1, 0, 2, 3).reshape(n_layers, N_DEV * bl, n_mod * d)
    dmod_cols = lax.dynamic_slice_in_dim(dmod_all, me * ada_cols, ada_cols, axis=2)
    res["ada_w"] = list(_ada_bwd(c_all.T, dmod_cols, ada_w, m_ada_w, v_ada_w))
    res["ada_b"] = update("adam_ada_b", dmod_all.transpose(1, 0, 2), ada_b, m_ada_b, v_ada_b)

    small_names = ["pre_mix_g", "post_mix_g", "conv_b", "conv_ln_g", "conv_ln_b", "pool_scale", "pre_ffn_g",
                   "post_ffn_g", "ffn_conv_b", "pool_w"]
    small_w = [pre_mix_g, post_mix_g, conv_b, conv_ln_g, conv_ln_b, pool_scale, pre_ffn_g, post_ffn_g,
               ffn_conv_b, pool_w]
    small_m = [m_pre_mix_g, m_post_mix_g, m_conv_b, m_conv_ln_g, m_conv_ln_b, m_pool_scale, m_pre_ffn_g,
               m_post_ffn_g, m_ffn_conv_b, m_pool_w]
    small_v = [v_pre_mix_g, v_post_mix_g, v_conv_b, v_conv_ln_g, v_conv_ln_b, v_pool_scale, v_pre_ffn_g,
               v_post_ffn_g, v_ffn_conv_b, v_pool_w]

    def pack(arrs):
        return jnp.concatenate([a.reshape(n_layers, -1) for a in arrs], axis=1).reshape(-1, small_cols)

    outs = _adam_reduce("adam_small", g_small, pack(small_w), pack(small_m), pack(small_v))
    outs = [o_.reshape(n_layers, n_small) for o_ in outs]
    off = 0
    for name, w in zip(small_names, small_w):
        size = w[0].size
        res[name] = [o_[:, off:off + size].reshape(w.shape) for o_ in outs]
        off += size

    order = ["ada_w", "ada_b", "pre_mix_g", "post_mix_g", "w_in", "conv_w", "conv_b", "conv_ln_g", "conv_ln_b",
             "pool_w", "pool_scale", "w_out", "pre_ffn_g", "post_ffn_g", "ffn_up", "ffn_conv_w", "ffn_conv_b",
             "ffn_down"]
    grad_x = tile_order(dx, inverse=True).reshape(bl, seq, d)
    return (loss, grad_x, *[res[n][0] for n in order], *[res[n][1] for n in order],
            *[res[n][2] for n in order], *[res[n][3] for n in order])
```

```python
import jax
import jax.numpy as jnp
from jax import lax
from jax.experimental import pallas as pl
from jax.experimental.pallas import tpu as pltpu

N_DEV = 8
EPS = 1e-6
POOL_WINDOWS = (2, 4, 8, 16)
POOL_GROUP = 128
TILE_TOKENS = 256
CONV_PREFIX = 8 * 30
FFN_PREFIX = 8 * 2
LANES = 128
ROW_CHUNK = 128
MXU_DTYPE = jnp.bfloat16
VMEM_LIMIT = 60 * 1024 * 1024

ADAM_LR = 0.001
ADAM_B1 = 0.9
ADAM_B2 = 0.999
ADAM_EPS = 1e-08
ADAM_WD = 0.01
ADAM_STEP = 10

MESH = pl.DeviceIdType.MESH
F32 = jnp.float32


def _mm(a, b):
    return jnp.dot(a.astype(MXU_DTYPE), b.astype(MXU_DTYPE), preferred_element_type=F32)


def _mm_nt(a, b):
    return lax.dot_general(a.astype(MXU_DTYPE), b.astype(MXU_DTYPE), (((1,), (1,)), ((), ())),
                           preferred_element_type=F32)


def _mm_tn(a, b):
    return lax.dot_general(a.astype(MXU_DTYPE), b.astype(MXU_DTYPE), (((0,), (0,)), ((), ())),
                           preferred_element_type=F32)


def _mean(v):
    return jnp.mean(v, axis=-1, keepdims=True)


def _colsum(v):
    return jnp.sum(v, axis=0, keepdims=True)


def _sigmoid(v):
    return jax.nn.sigmoid(v)


def _params(n_grid=1):
    return pltpu.CompilerParams(dimension_semantics=("arbitrary",) * n_grid, vmem_limit_bytes=VMEM_LIMIT)


def _const_spec(shape, index):
    return pl.BlockSpec(shape, lambda *_: index, pipeline_mode=pl.Buffered(1))


def _acc_spec(shape, index):
    return pl.BlockSpec(shape, lambda *_: index)


def _position():
    x, y, c = lax.axis_index("x"), lax.axis_index("y"), lax.axis_index("c")
    return x, y, c


def _gather_phase(ins, outs, send_sems, recv_sems, local_sems, phase):
    n = len(ins)
    x, y, c = _position()
    me, sibling = (x, y, c), (x, y, 1 - c)
    chips = [(1 - x, y), (x, 1 - y), (1 - x, 1 - y)]

    def slot(k, px, py, pc):
        return outs[k].at[4 * px + 2 * py + pc]

    def copy(k, s, block, to, src=None):
        return pltpu.make_async_remote_copy(
            src_ref=slot(k, *block) if src is None else src, dst_ref=slot(k, *block),
            send_sem=send_sems.at[k, s], recv_sem=recv_sems.at[k, s], device_id=to, device_id_type=MESH)

    mine = [pltpu.make_async_copy(ins[k], slot(k, *me), local_sems.at[k]) for k in range(n)]
    first = []
    for k in range(n):
        first.append(copy(k, 0, me, sibling, src=ins[k]))
        first += [copy(k, 1 + j, me, (*chip, c), src=ins[k]) for j, chip in enumerate(chips)]
    passed = [copy(k, 4 + j, (*chip, c), sibling) for j, chip in enumerate(chips) for k in range(n)]
    if phase == 0:
        for cp in mine + first:
            cp.start()
    elif phase == 1:
        for j, chip in enumerate(chips):
            for k in range(n):
                copy(k, 1 + j, (*chip, c), me).wait_recv()
                copy(k, 4 + j, (*chip, c), sibling).start()
    else:
        for k in range(n):
            copy(k, 0, sibling, me).wait_recv()
            for j, chip in enumerate(chips):
                copy(k, 4 + j, (*chip, 1 - c), me).wait_recv()
        for cp in first + passed:
            cp.wait_send()
        for cp in mine:
            cp.wait()


def _gather_scratch(n):
    return [pltpu.SemaphoreType.DMA((n, 7)), pltpu.SemaphoreType.DMA((n, 7)), pltpu.SemaphoreType.DMA((n,))]


def _layer_of(refs, layers):
    return [r if lay is None else r.at[lay] for r, lay in zip(refs, layers)]


def _gathered_shapes(arrs, layers):
    return [jax.ShapeDtypeStruct((N_DEV,) + (a.shape if lay is None else a.shape[1:]), a.dtype)
            for a, lay in zip(arrs, layers)]


def _all_gather(arrs, name, layers=None):
    n = len(arrs)
    layers = [None] * n if layers is None else layers

    def body(*refs):
        for phase in range(3):
            _gather_phase(_layer_of(refs[:n], layers), refs[n:2 * n], *refs[2 * n:], phase)

    any_spec = pl.BlockSpec(memory_space=pl.ANY)
    return pl.pallas_call(
        body, name=name,
        out_shape=_gathered_shapes(arrs, layers),
        in_specs=[any_spec] * n, out_specs=[any_spec] * n,
        scratch_shapes=_gather_scratch(n),
    )(*arrs)


def _exchange(payload, recvs, rows, layer, send_sems, recv_sems, local_sems, start):
    x, y, c = _position()
    me = 4 * x + 2 * y + c
    for k, (src, recv) in enumerate(zip(payload, recvs)):
        def block(ref, *index):
            return ref.at[index] if rows[k] is None else ref.at[(*index, pl.ds(*rows[k]))]

        local = pltpu.make_async_copy(block(src, me), block(recv, me, layer), local_sems.at[k])
        if start:
            local.start()
        else:
            local.wait()
        for j in range(1, N_DEV):
            px = (1 - x) if (j & 4) else x
            py = (1 - y) if (j & 2) else y
            pc = (1 - c) if (j & 1) else c
            peer = 4 * px + 2 * py + pc
            landing = block(recv, me, layer) if start else block(recv, peer, layer)
            cp = pltpu.make_async_remote_copy(
                src_ref=block(src, peer), dst_ref=landing, send_sem=send_sems.at[k, j - 1],
                recv_sem=recv_sems.at[k, j - 1], device_id=(px, py, pc), device_id_type=MESH)
            if start:
                cp.start()
            else:
                cp.wait()


def _exchange_scratch(n):
    return [pltpu.SemaphoreType.DMA((n, N_DEV - 1)), pltpu.SemaphoreType.DMA((n, N_DEV - 1)),
            pltpu.SemaphoreType.DMA((n,))]


def _exchange_and_gather(payload, recvs, rows, layer, arrs, name):
    n, m = len(payload), len(arrs)

    def body(*refs):
        pay, srcs = refs[:n], refs[2 * n:2 * n + m]
        outs, got = refs[2 * n + m:3 * n + m], refs[3 * n + m:3 * n + 2 * m]
        xsems, gsems = refs[3 * n + 2 * m:3 * n + 2 * m + 3], refs[3 * n + 2 * m + 3:]
        _exchange(pay, outs, rows, layer, *xsems, start=True)
        for phase in range(3):
            _gather_phase(srcs, got, *gsems, phase)
        _exchange(pay, outs, rows, layer, *xsems, start=False)

    any_spec = pl.BlockSpec(memory_space=pl.ANY)
    res = pl.pallas_call(
        body, name=name,
        out_shape=[jax.ShapeDtypeStruct(r.shape, r.dtype) for r in recvs]
        + [jax.ShapeDtypeStruct((N_DEV,) + a.shape, a.dtype) for a in arrs],
        in_specs=[any_spec] * (2 * n + m), out_specs=[any_spec] * (n + m),
        input_output_aliases={n + k: k for k in range(n)},
        scratch_shapes=_exchange_scratch(n) + _gather_scratch(m),
    )(*payload, *recvs, *arrs)
    return res[:n], res[n:]


def _carried_refs(refs, n_in, n_out, n_scratch, n_pay, aliased):
    ins = refs[:n_in]
    pay = refs[n_in:n_in + n_pay]
    o0 = n_in + (2 * n_pay if aliased else n_pay)
    outs = refs[o0:o0 + n_out]
    recvs = refs[o0 + n_out:o0 + n_out + n_pay]
    s0 = o0 + n_out + n_pay
    return ins, outs, refs[s0:s0 + n_scratch], pay, recvs, refs[s0 + n_scratch:]


def _carrier_call(body, name, grid, in_specs, out_specs, out_shape, scratch, operands, payload, recvs=None,
                  gather_layer=None):
    n_in, n_out, n_pay = len(in_specs), len(out_specs), len(payload)
    any_spec = pl.BlockSpec(memory_space=pl.ANY)
    if recvs is None:
        landing = _gathered_shapes(payload, [gather_layer] * n_pay)
        extra_in, aliases = list(payload), {}
        sems = _gather_scratch(n_pay) if n_pay else []
    else:
        landing = [jax.ShapeDtypeStruct(r.shape, r.dtype) for r in recvs]
        extra_in = list(payload) + list(recvs)
        aliases = {n_in + n_pay + k: n_out + k for k in range(n_pay)}
        sems = _exchange_scratch(n_pay) if n_pay else []
    res = pl.pallas_call(
        body, name=name, grid=grid,
        out_shape=list(out_shape) + landing,
        in_specs=list(in_specs) + [any_spec] * len(extra_in),
        out_specs=list(out_specs) + [any_spec] * n_pay,
        input_output_aliases=aliases,
        scratch_shapes=list(scratch) + sems,
        compiler_params=_params(len(grid)),
    )(*operands, *extra_in)
    return res[:n_out], res[n_out:]


def _adamw(w, g, m, v):
    m = ADAM_B1 * m + (1.0 - ADAM_B1) * g
    v = ADAM_B2 * v + (1.0 - ADAM_B2) * jnp.square(g)
    m_hat = m / (1.0 - ADAM_B1 ** ADAM_STEP)
    v_hat = v / (1.0 - ADAM_B2 ** ADAM_STEP)
    delta = -ADAM_LR * (m_hat / (jnp.sqrt(v_hat) + ADAM_EPS) + ADAM_WD * w)
    return delta, m, v


def _ada_fwd(c_all, ada_w, ada_b_cols):
    n_layers, d, cols = ada_w.shape
    b = c_all.shape[0]

    def body(c_ref, w_ref, b_ref, o_ref):
        cv = c_ref[...]
        act = cv * _sigmoid(cv)
        o_ref[...] = jnp.dot(act, w_ref[...], preferred_element_type=F32,
                             precision=lax.Precision.HIGHEST) + b_ref[...]

    return pl.pallas_call(
        body, name="ada_fwd", grid=(n_layers,),
        out_shape=jax.ShapeDtypeStruct((n_layers, b, cols), F32),
        in_specs=[pl.BlockSpec((b, d), lambda l: (0, 0)),
                  pl.BlockSpec((None, d, cols), lambda l: (l, 0, 0)),
                  pl.BlockSpec((None, 1, cols), lambda l: (l, 0, 0))],
        out_specs=pl.BlockSpec((None, b, cols), lambda l: (l, 0, 0)),
        compiler_params=_params(1),
    )(c_all, ada_w, ada_b_cols.reshape(n_layers, 1, cols))


def _ada_bwd(c_all_t, dmod_cols, w, m, v):
    n_layers, d, cols = w.shape
    b = c_all_t.shape[1]
    td = 256 if d % 256 == 0 else d

    def body(c_ref, dm_ref, w_ref, m_ref, v_ref, g_ref, dl_ref, nm_ref, nv_ref):
        cv = c_ref[...]
        act = cv * _sigmoid(cv)
        g = jnp.dot(act, dm_ref[...], preferred_element_type=F32, precision=lax.Precision.HIGHEST)
        delta, nm, nv = _adamw(w_ref[...], g, m_ref[...], v_ref[...])
        g_ref[...] = g
        dl_ref[...] = delta
        nm_ref[...] = nm
        nv_ref[...] = nv

    blk = pl.BlockSpec((None, td, cols), lambda l, i: (l, i, 0))
    shp = jax.ShapeDtypeStruct(w.shape, F32)
    return pl.pallas_call(
        body, name="ada_bwd", grid=(n_layers, d // td),
        out_shape=[shp] * 4,
        in_specs=[pl.BlockSpec((td, b), lambda l, i: (i, 0)),
                  pl.BlockSpec((None, b, cols), lambda l, i: (l, 0, 0)), blk, blk, blk],
        out_specs=[blk] * 4,
        compiler_params=_params(2),
    )(c_all_t, dmod_cols, w, m, v)


def _row_tile(rows, cols, budget=128 * 1024, step=8):
    best = None
    for t in range(step, rows + 1, step):
        if rows % t == 0 and t * cols <= budget:
            best = t
    return best if best is not None else rows


def _adam_reduce(name, parts, w, m, v):
    p, rows, cols = parts.shape
    tr = _row_tile(rows, cols, budget=(256 * 1024) // max(1, p // 4), step=8 if parts.dtype == F32 else 16)

    def body(p_ref, w_ref, m_ref, v_ref, g_ref, dl_ref, nm_ref, nv_ref):
        g = p_ref[0].astype(F32)
        for k in range(1, p):
            g = g + p_ref[k].astype(F32)
        delta, nm, nv = _adamw(w_ref[...], g, m_ref[...], v_ref[...])
        g_ref[...] = g
        dl_ref[...] = delta
        nm_ref[...] = nm
        nv_ref[...] = nv

    blk = pl.BlockSpec((tr, cols), lambda i: (i, 0))
    shp = jax.ShapeDtypeStruct((rows, cols), F32)
    return pl.pallas_call(
        body, name=name, grid=(rows // tr,),
        out_shape=[shp] * 4,
        in_specs=[pl.BlockSpec((p, tr, cols), lambda i: (0, i, 0)), blk, blk, blk],
        out_specs=[blk] * 4,
        compiler_params=_params(1),
    )(parts, w, m, v)


def _sum_parts(parts):
    p = parts.shape[0]

    def body(p_ref, o_ref):
        acc = p_ref[0]
        for k in range(1, p):
            acc = acc + p_ref[k]
        o_ref[...] = acc

    return pl.pallas_call(body, name="loss_sum", out_shape=jax.ShapeDtypeStruct(parts.shape[1:], F32))(parts)


def _loss_grad(y, target, tm):
    t, d = y.shape

    def body(y_ref, t_ref, dy_ref, loss_ref):
        @pl.when(pl.program_id(0) == 0)
        def _():
            loss_ref[...] = jnp.zeros_like(loss_ref)

        diff = y_ref[...] - t_ref[...]
        dy_ref[...] = diff / d
        part = 0.5 * jnp.sum(_mean(diff * diff), axis=0, keepdims=True)
        loss_ref[...] += jnp.broadcast_to(part, loss_ref.shape)

    blk = pl.BlockSpec((tm, d), lambda i: (i, 0))
    return pl.pallas_call(
        body, name="loss_grad", grid=(t // tm,),
        out_shape=[jax.ShapeDtypeStruct((t, d), F32), jax.ShapeDtypeStruct((8, LANES), F32)],
        in_specs=[blk, blk], out_specs=[blk, pl.BlockSpec((8, LANES), lambda i: (0, 0))],
        compiler_params=_params(1),
    )(y, target)


def _matmul_tn(name, a, b, tk, out_dtype, payload=(), recvs=(), rows=(), pay_layer=0):
    ga, t, m = a.shape
    gb, _, n = b.shape
    g = max(ga, gb)
    n_k = t // tk
    n_pay = len(payload)

    def body(*refs):
        (a_ref, b_ref), (o_ref,), (acc_ref,), pay_refs, recv_refs, sems = _carried_refs(
            refs, 2, 1, 1, n_pay, aliased=True)
        gi, k = pl.program_id(0), pl.program_id(1)

        if n_pay:
            @pl.when(jnp.logical_and(gi == 0, k == 0))
            def _():
                _exchange(pay_refs, recv_refs, rows, pay_layer, *sems, start=True)

        @pl.when(k == 0)
        def _():
            acc_ref[...] = jnp.zeros_like(acc_ref)

        acc_ref[...] += _mm_tn(a_ref[...], b_ref[...])

        @pl.when(k == n_k - 1)
        def _():
            o_ref[...] = acc_ref[...].astype(out_dtype)

        if n_pay:
            @pl.when(jnp.logical_and(gi == g - 1, k == n_k - 1))
            def _():
                _exchange(pay_refs, recv_refs, rows, pay_layer, *sems, start=False)

    (out,), got = _carrier_call(
        body, name, (g, n_k),
        out_shape=[jax.ShapeDtypeStruct((g, m, n), out_dtype)],
        in_specs=[pl.BlockSpec((None, tk, m), (lambda gi, k: (gi, k, 0)) if ga > 1 else (lambda gi, k: (0, k, 0))),
                  pl.BlockSpec((None, tk, n), (lambda gi, k: (gi, k, 0)) if gb > 1 else (lambda gi, k: (0, k, 0)))],
        out_specs=[pl.BlockSpec((None, m, n), lambda gi, k: (gi, 0, 0))],
        scratch=[pltpu.VMEM((m, n), F32)],
        operands=(a, b), payload=payload, recvs=recvs)
    return out, got


def _time_of_row(tm):
    i = lax.broadcasted_iota(jnp.int32, (tm, 1), 0)
    return (i % 8) * (tm // 8) + i // 8


def _sublane_is(rows, s):
    return lax.broadcasted_iota(jnp.int32, (rows, 1), 0) % 8 == s


def _conv_taps(src_ref, col0, ncols, tm, tap_rows, w_ref, init_of, store):
    rc = min(ROW_CHUNK, tm)
    for cb in range(ncols // LANES):
        cs = slice(cb * LANES, (cb + 1) * LANES)
        ss = slice(col0 + cb * LANES, col0 + (cb + 1) * LANES)
        for r0 in range(0, tm, rc):
            acc = init_of(cs, rc)
            for k, row in enumerate(tap_rows):
                acc = acc + w_ref[k:k + 1, cs] * src_ref[r0 + row:r0 + row + rc, ss]
            store(r0, rc, cs, acc)


def _mixer_fwd(l, x, mod, gains, w_in_t, conv_w, vec, pool_w, w_out, *, seq, tm, act_dtype, payload=(),
               gather_layer=None):
    t, d = x.shape
    d_in = w_in_t.shape[0]
    dc = conv_w.shape[-1]
    d_mix = w_out.shape[0]
    n_taps = 31
    nt, tps = t // tm, seq // tm
    p = CONV_PREFIX
    n_pay = len(payload)

    def body(*refs):
        ins, outs, scr, pay_refs, got_refs, sems = _carried_refs(refs, 8, 7, 2, n_pay, aliased=False)
        x_ref, mod_ref, g_ref, win_ref, cw_ref, v_ref, pw_ref, wout_ref = ins
        x1_ref, h1_ref, u_ref, a1_ref, ap_ref, o_ref, tail_ref = outs
        ext_ref, car_ref = scr
        i = pl.program_id(0)
        first = (i % tps) == 0

        if n_pay:
            @pl.when(i == 0)
            def _():
                _gather_phase(_layer_of(pay_refs, [gather_layer] * n_pay), got_refs, *sems, 0)

        xv = x_ref[...]
        r = lax.rsqrt(_mean(xv * xv) + EPS)
        hv = (xv * r) * g_ref[0:1, :] * (1.0 + mod_ref[1:2, :]) + mod_ref[0:1, :]
        hb = hv.astype(act_dtype)
        h1_ref[...] = hb
        u = _mm_nt(hb, win_ref[...])
        u_ref[...] = u
        a0 = u[:, :dc] * _sigmoid(u[:, dc:2 * dc])

        @pl.when(first)
        def _():
            car_ref[...] = jnp.zeros_like(car_ref)

        @pl.when(i == 0)
        def _():
            ext_ref[p + tm:p + tm + 8, :] = jnp.zeros((8, ext_ref.shape[1]), F32)

        ext_ref[p:p + tm, 0:dc] = a0
        ext_ref[p:p + tm, dc:] = u[:, 2 * dc:]
        ext_ref[0:p, :] = jnp.where(_sublane_is(p, 0), car_ref[...], ext_ref[tm - 1:tm - 1 + p, :])

        def store(r0, rc, cs, acc):
            a1_ref[r0:r0 + rc, cs] = acc

        _conv_taps(ext_ref, 0, dc, tm, [p - 8 * (n_taps - 1 - k) for k in range(n_taps)],
                   cw_ref,
                   lambda cs, rc: jnp.broadcast_to(v_ref[0:1, cs], (rc, LANES)), store)
        a1 = a1_ref[...]
        mu = _mean(a1)
        xc = a1 - mu
        rstd = lax.rsqrt(_mean(xc * xc) + EPS)
        a2 = (xc * rstd) * v_ref[1:2, :] + v_ref[2:3, :]
        ap_ref[:, 0:dc] = (a2 * _sigmoid(a2)).astype(act_dtype)

        pos = (i % tps) * tm + _time_of_row(tm)
        for g, w in enumerate(POOL_WINDOWS):
            cs = slice(dc + g * POOL_GROUP, dc + (g + 1) * POOL_GROUP)
            s = ext_ref[p:p + tm, cs]
            for j in range(1, w):
                s = s + ext_ref[p - 8 * j:p - 8 * j + tm, cs]
            cnt = jnp.minimum(pos + 1, w).astype(F32)
            dv = s / cnt - ext_ref[p:p + tm, cs]
            q = _mm(dv, pw_ref[g])
            ap_ref[:, cs] = (q * v_ref[3:4, g * POOL_GROUP:(g + 1) * POOL_GROUP]).astype(act_dtype)

        o = _mm(ap_ref[...], wout_ref[...])
        o_ref[...] = o
        ro = lax.rsqrt(_mean(o * o) + EPS)
        x1_ref[...] = xv + (1.0 + mod_ref[2:3, :]) * ((o * ro) * g_ref[1:2, :])

        nxt = ext_ref[tm + 7:tm + 7 + p, :]
        car_ref[...] = nxt
        tail_ref[...] = nxt

        if n_pay:
            @pl.when(i == max(nt - 2, 0))
            def _():
                _gather_phase(_layer_of(pay_refs, [gather_layer] * n_pay), got_refs, *sems, 1)

            @pl.when(i == nt - 1)
            def _():
                _gather_phase(_layer_of(pay_refs, [gather_layer] * n_pay), got_refs, *sems, 2)

    row = lambda width: pl.BlockSpec((tm, width), lambda i: (i, 0))
    return _carrier_call(
        body, f"mixer_fwd_{l}", (nt,),
        out_shape=[jax.ShapeDtypeStruct((t, d), F32), jax.ShapeDtypeStruct((t, d), act_dtype),
                   jax.ShapeDtypeStruct((t, d_in), F32), jax.ShapeDtypeStruct((t, dc), F32),
                   jax.ShapeDtypeStruct((t, d_mix), act_dtype), jax.ShapeDtypeStruct((t, d), F32),
                   jax.ShapeDtypeStruct((nt, p, d_mix), F32)],
        in_specs=[row(d),
                  pl.BlockSpec((None, None, 8, d), lambda i: (l, i // tps, 0, 0)),
                  _const_spec((None, 8, d), (l, 0, 0)),
                  _const_spec((d_in, d), (0, 0)),
                  _const_spec((32, dc), (0, 0)),
                  _const_spec((None, 8, dc), (l, 0, 0)),
                  _const_spec((None, len(POOL_WINDOWS), POOL_GROUP, POOL_GROUP), (l, 0, 0, 0)),
                  _const_spec((d_mix, d), (0, 0))],
        out_specs=[row(d), row(d), row(d_in), row(dc), row(d_mix), row(d),
                   pl.BlockSpec((None, p, d_mix), lambda i: (i, 0, 0))],
        scratch=[pltpu.VMEM((p + tm + 8, d_mix), F32), pltpu.VMEM((p, d_mix), F32)],
        operands=(x, mod, gains, w_in_t, conv_w, vec, pool_w, w_out),
        payload=payload, gather_layer=gather_layer)


def _mixer_bwd(l, dx1, x, mod, gains, u, tails, a1, o, w_in_t, conv_w, vec, pool_w, w_out, *, seq, tm, act_dtype,
               payload=(), recvs=(), rows=(), pay_layer=0):
    t, d = x.shape
    d_in = w_in_t.shape[0]
    dc = conv_w.shape[-1]
    d_mix = w_out.shape[0]
    n_taps = 31
    nt, tps = t // tm, seq // tm
    p = CONV_PREFIX
    n_groups = len(POOL_WINDOWS)
    n_pay = len(payload)

    def body(*refs):
        ins, outs, scr, pay_refs, recv_refs, sems = _carried_refs(refs, 13, 8, 3, n_pay, aliased=True)
        dx1_ref, x_ref, mod_ref, g_ref, u_ref, tail_ref, a1_ref, o_ref, win_ref, cw_ref, v_ref, pw_ref, wout_ref = ins
        dx_ref, du_ref, do_ref, dmod_ref, gd_ref, gv_ref, dcw_ref, dpw_ref = outs
        ext_ref, fext_ref, fcar_ref = scr
        i = pl.program_id(0)
        j = nt - 1 - i
        first_in_seq = (j % tps) == 0
        last_in_seq = (j % tps) == tps - 1

        if n_pay:
            @pl.when(i == 0)
            def _():
                _exchange(pay_refs, recv_refs, rows, pay_layer, *sems, start=True)

        @pl.when(i == 0)
        def _():
            gd_ref[...] = jnp.zeros_like(gd_ref)
            gv_ref[...] = jnp.zeros_like(gv_ref)
            dcw_ref[...] = jnp.zeros_like(dcw_ref)
            dpw_ref[...] = jnp.zeros_like(dpw_ref)
            fext_ref[0:8, :] = jnp.zeros((8, fext_ref.shape[1]), F32)

        @pl.when(last_in_seq)
        def _():
            dmod_ref[...] = jnp.zeros_like(dmod_ref)
            fcar_ref[...] = jnp.zeros_like(fcar_ref)

        xv = x_ref[...]
        dx1v = dx1_ref[...]
        pre_g, post_g = g_ref[0:1, :], g_ref[1:2, :]

        ov = o_ref[...]
        ro = lax.rsqrt(_mean(ov * ov) + EPS)
        yo = ov * ro
        dmod_ref[2:3, :] += _colsum(dx1v * (yo * post_g))
        dn = dx1v * (1.0 + mod_ref[2:3, :])
        gd_ref[1:2, :] += _colsum(dn * yo)
        dyo = dn * post_g
        do = ro * (dyo - yo * _mean(dyo * yo))
        dob = do.astype(act_dtype)
        do_ref[...] = dob
        dap = _mm_nt(dob, wout_ref[...])

        uv = u_ref[...]
        val, gate = uv[:, :dc], uv[:, dc:2 * dc]
        sg = _sigmoid(gate)
        ext_ref[p:p + tm, 0:dc] = val * sg
        ext_ref[p:p + tm, dc:] = uv[:, 2 * dc:]
        keep = jnp.where(first_in_seq, 0.0, 1.0).astype(F32)
        ext_ref[0:p, :] = jnp.where(_sublane_is(p, 0), tail_ref[...] * keep, ext_ref[tm - 1:tm - 1 + p, :])

        a1v = a1_ref[...]
        mu = _mean(a1v)
        xc = a1v - mu
        rstd = lax.rsqrt(_mean(xc * xc) + EPS)
        xh = xc * rstd
        ln_g = v_ref[1:2, :]
        a2 = xh * ln_g + v_ref[2:3, :]
        s2 = _sigmoid(a2)
        da2 = dap[:, :dc] * (s2 * (1.0 + a2 * (1.0 - s2)))
        gv_ref[1:2, :] += _colsum(da2 * xh)
        gv_ref[2:3, :] += _colsum(da2)
        dxh = da2 * ln_g
        da1 = rstd * (dxh - _mean(dxh) - xh * _mean(dxh * xh))
        gv_ref[0:1, :] += _colsum(da1)
        last_sublane = _sublane_is(p, 7)

        def put(cs, value):
            fext_ref[8:8 + tm, cs] = value
            fext_ref[8 + tm:8 + tm + p, cs] = jnp.where(last_sublane, fcar_ref[:, cs], fext_ref[9:9 + p, cs])

        put(slice(0, dc), da1)

        for k in range(n_taps):
            row = p - 8 * (n_taps - 1 - k)
            dcw_ref[k:k + 1, :] += _colsum(fext_ref[8:8 + tm, 0:dc] * ext_ref[row:row + tm, 0:dc])

        def store(r0, rc, cs, acc):
            sgc = _sigmoid(u_ref[r0:r0 + rc, dc + cs.start:dc + cs.stop])
            vc = u_ref[r0:r0 + rc, cs]
            du_ref[r0:r0 + rc, cs] = (acc * sgc).astype(act_dtype)
            du_ref[r0:r0 + rc, dc + cs.start:dc + cs.stop] = (acc * vc * sgc * (1.0 - sgc)).astype(act_dtype)

        _conv_taps(fext_ref, 0, dc, tm, [8 + 8 * (n_taps - 1 - k) for k in range(n_taps)],
                   cw_ref,
                   lambda cs, rc: jnp.zeros((rc, LANES), F32), store)

        pos = (j % tps) * tm + _time_of_row(tm)
        for g, w in enumerate(POOL_WINDOWS):
            cs = slice(dc + g * POOL_GROUP, dc + (g + 1) * POOL_GROUP)
            gs = slice(g * POOL_GROUP, (g + 1) * POOL_GROUP)
            s = ext_ref[p:p + tm, cs]
            for jj in range(1, w):
                s = s + ext_ref[p - 8 * jj:p - 8 * jj + tm, cs]
            cnt = jnp.minimum(pos + 1, w).astype(F32)
            dv = (s / cnt - ext_ref[p:p + tm, cs]).astype(MXU_DTYPE)
            q = _mm(dv, pw_ref[g])
            dp = dap[:, cs]
            gv_ref[3:4, gs] += _colsum(dp * q)
            dq = (dp * v_ref[3:4, gs]).astype(MXU_DTYPE)
            dpw_ref[g] += _mm_tn(dv, dq)
            dd = _mm_nt(dq, pw_ref[g])
            put(cs, dd / cnt)
            dhp = fext_ref[8:8 + tm, cs]
            for jj in range(1, w):
                dhp = dhp + fext_ref[8 + 8 * jj:8 + 8 * jj + tm, cs]
            du_ref[:, dc + cs.start:dc + cs.stop] = (dhp - dd).astype(act_dtype)

        fcar_ref[...] = fext_ref[1:1 + p, :]

        dh = _mm(du_ref[...], win_ref[...])

        r = lax.rsqrt(_mean(xv * xv) + EPS)
        xn = xv * r
        dmod_ref[0:1, :] += _colsum(dh)
        dmod_ref[1:2, :] += _colsum(dh * (xn * pre_g))
        dy = dh * (1.0 + mod_ref[1:2, :])
        gd_ref[0:1, :] += _colsum(dy * xn)
        dxn = dy * pre_g
        dx_ref[...] = dx1v + r * (dxn - xn * _mean(dxn * xn))

        if n_pay:
            @pl.when(i == nt - 1)
            def _():
                _exchange(pay_refs, recv_refs, rows, pay_layer, *sems, start=False)

    rev = lambda width: pl.BlockSpec((tm, width), lambda i: (nt - 1 - i, 0))
    return _carrier_call(
        body, f"mixer_bwd_{l}", (nt,),
        out_shape=[jax.ShapeDtypeStruct((t, d), F32), jax.ShapeDtypeStruct((t, d_in), act_dtype),
                   jax.ShapeDtypeStruct((t, d), act_dtype),
                   jax.ShapeDtypeStruct((t // seq, 8, d), F32), jax.ShapeDtypeStruct((8, d), F32),
                   jax.ShapeDtypeStruct((8, dc), F32), jax.ShapeDtypeStruct((32, dc), F32),
                   jax.ShapeDtypeStruct((n_groups, POOL_GROUP, POOL_GROUP), F32)],
        in_specs=[rev(d), rev(d),
                  pl.BlockSpec((None, None, 8, d), lambda i: (l, (nt - 1 - i) // tps, 0, 0)),
                  _const_spec((None, 8, d), (l, 0, 0)),
                  rev(d_in),
                  pl.BlockSpec((None, p, d_mix), lambda i: (jnp.maximum(nt - 2 - i, 0), 0, 0)),
                  rev(dc), rev(d),
                  _const_spec((d_in, d), (0, 0)),
                  _const_spec((32, dc), (0, 0)),
                  _const_spec((None, 8, dc), (l, 0, 0)),
                  _const_spec((None, n_groups, POOL_GROUP, POOL_GROUP), (l, 0, 0, 0)),
                  _const_spec((d_mix, d), (0, 0))],
        out_specs=[rev(d), rev(d_in), rev(d),
                   pl.BlockSpec((None, 8, d), lambda i: ((nt - 1 - i) // tps, 0, 0)),
                   _acc_spec((8, d), (0, 0)), _acc_spec((8, dc), (0, 0)), _acc_spec((32, dc), (0, 0)),
                   _acc_spec((n_groups, POOL_GROUP, POOL_GROUP), (0, 0, 0))],
        scratch=[pltpu.VMEM((p + tm, d_mix), F32), pltpu.VMEM((8 + tm + p, d_mix), F32),
                 pltpu.VMEM((p, d_mix), F32)],
        operands=(dx1, x, mod, gains, u, tails, a1, o, w_in_t, conv_w, vec, pool_w, w_out),
        payload=payload, recvs=recvs)


def _ffn_fwd(l, x, mod, gains, up, fcw, down, *, seq, tm, act_dtype, payload=(), gather_layer=None):
    t, d = x.shape
    n_chunks, _, fc = up.shape
    half = n_chunks // 2
    nt, tps = t // tm, seq // tm
    p = FFN_PREFIX
    n_pay = len(payload)

    def body(*refs):
        ins, outs, scr, pay_refs, got_refs, sems = _carried_refs(refs, 6, 6, 2, n_pay, aliased=False)
        x_ref, mod_ref, g_ref, up_ref, fcw_ref, down_ref = ins
        x2_ref, h2_ref, u2_ref, u3_ref, hid_ref, o2_ref = outs
        ext_ref, car_ref = scr
        i = pl.program_id(0)
        first = (i % tps) == 0

        if n_pay:
            @pl.when(i == 0)
            def _():
                _gather_phase(_layer_of(pay_refs, [gather_layer] * n_pay), got_refs, *sems, 0)

        xv = x_ref[...]
        r = lax.rsqrt(_mean(xv * xv) + EPS)
        hv = (xv * r) * g_ref[0:1, :] * (1.0 + mod_ref[4:5, :]) + mod_ref[3:4, :]
        hb = hv.astype(act_dtype)
        h2_ref[...] = hb

        @pl.when(first)
        def _():
            car_ref[...] = jnp.zeros_like(car_ref)

        @pl.when(i == 0)
        def _():
            for n in range(n_chunks):
                ext_ref[n, p + tm:p + tm + 8, :] = jnp.zeros((8, fc), F32)

        first_sublane = _sublane_is(p, 0)

        def project(n):
            un = _mm(hb, up_ref[n])
            ext_ref[n, p:p + tm, :] = un
            u2_ref[n] = un.astype(act_dtype)
            ext_ref[n, 0:p, :] = jnp.where(first_sublane, car_ref[n], ext_ref[n, tm - 1:tm - 1 + p, :])
            car_ref[n] = ext_ref[n, tm + 7:tm + 7 + p, :]

        def conv(n):
            return (fcw_ref[n, 3:4, :] + fcw_ref[n, 0:1, :] * ext_ref[n, p - 16:p - 16 + tm, :]
                    + fcw_ref[n, 1:2, :] * ext_ref[n, p - 8:p - 8 + tm, :]
                    + fcw_ref[n, 2:3, :] * ext_ref[n, p:p + tm, :])

        for n in range(n_chunks):
            project(n)
        o2 = jnp.zeros((tm, d), F32)
        for n in range(half):
            gt = conv(n + half)
            v = conv(n)
            u3_ref[n] = v.astype(act_dtype)
            u3_ref[n + half] = gt.astype(act_dtype)
            hid = ((gt * _sigmoid(gt)) * v).astype(act_dtype)
            hid_ref[n] = hid
            o2 = o2 + _mm(hid, down_ref[n])
        o2_ref[...] = o2
        ro = lax.rsqrt(_mean(o2 * o2) + EPS)
        x2_ref[...] = xv + (1.0 + mod_ref[5:6, :]) * ((o2 * ro) * g_ref[1:2, :])

        if n_pay:
            @pl.when(i == max(nt - 2, 0))
            def _():
                _gather_phase(_layer_of(pay_refs, [gather_layer] * n_pay), got_refs, *sems, 1)

            @pl.when(i == nt - 1)
            def _():
                _gather_phase(_layer_of(pay_refs, [gather_layer] * n_pay), got_refs, *sems, 2)

    row = lambda width: pl.BlockSpec((tm, width), lambda i: (i, 0))
    chunked = lambda n: pl.BlockSpec((n, tm, fc), lambda i: (0, i, 0))
    return _carrier_call(
        body, f"ffn_fwd_{l}", (nt,),
        out_shape=[jax.ShapeDtypeStruct((t, d), F32), jax.ShapeDtypeStruct((t, d), act_dtype),
                   jax.ShapeDtypeStruct((n_chunks, t, fc), act_dtype),
                   jax.ShapeDtypeStruct((n_chunks, t, fc), act_dtype),
                   jax.ShapeDtypeStruct((half, t, fc), act_dtype), jax.ShapeDtypeStruct((t, d), F32)],
        in_specs=[row(d),
                  pl.BlockSpec((None, None, 8, d), lambda i: (l, i // tps, 0, 0)),
                  _const_spec((None, 8, d), (l, 0, 0)),
                  _const_spec((n_chunks, d, fc), (0, 0, 0)),
                  _const_spec((n_chunks, 8, fc), (0, 0, 0)),
                  _const_spec((half, fc, d), (0, 0, 0))],
        out_specs=[row(d), row(d), chunked(n_chunks), chunked(n_chunks), chunked(half), row(d)],
        scratch=[pltpu.VMEM((n_chunks, p + tm + 8, fc), F32), pltpu.VMEM((n_chunks, p, fc), F32)],
        operands=(x, mod, gains, up, fcw, down),
        payload=payload, gather_layer=gather_layer)


def _ffn_bwd(l, dx2, x, mod, gains, u2, u3, o2, up, fcw, down, *, seq, tm, act_dtype,
             payload=(), recvs=(), rows=(), pay_layer=0):
    t, d = x.shape
    n_chunks, _, fc = up.shape
    half = n_chunks // 2
    nt, tps = t // tm, seq // tm
    p = FFN_PREFIX
    n_pay = len(payload)

    def body(*refs):
        ins, outs, scr, pay_refs, recv_refs, sems = _carried_refs(refs, 10, 6, 2, n_pay, aliased=True)
        dx2_ref, x_ref, mod_ref, g_ref, u2_ref, u3_ref, o2_ref, up_ref, fcw_ref, down_ref = ins
        dx1_ref, du2_ref, do2_ref, dmod_ref, gd_ref, dfcw_ref = outs
        fext_ref, fcar_ref = scr
        i = pl.program_id(0)
        j = nt - 1 - i
        last_in_seq = (j % tps) == tps - 1

        if n_pay:
            @pl.when(i == 0)
            def _():
                _exchange(pay_refs, recv_refs, rows, pay_layer, *sems, start=True)

        @pl.when(i == 0)
        def _():
            gd_ref[...] = jnp.zeros_like(gd_ref)
            dfcw_ref[...] = jnp.zeros_like(dfcw_ref)
            for n in range(n_chunks):
                fext_ref[n, 0:8, :] = jnp.zeros((8, fc), F32)

        @pl.when(last_in_seq)
        def _():
            dmod_ref[...] = jnp.zeros_like(dmod_ref)
            fcar_ref[...] = jnp.zeros_like(fcar_ref)

        xv = x_ref[...]
        dx2v = dx2_ref[...]
        pre_g, post_g = g_ref[0:1, :], g_ref[1:2, :]

        ov = o2_ref[...]
        ro = lax.rsqrt(_mean(ov * ov) + EPS)
        yo = ov * ro
        dmod_ref[2:3, :] += _colsum(dx2v * (yo * post_g))
        dn = dx2v * (1.0 + mod_ref[5:6, :])
        gd_ref[1:2, :] += _colsum(dn * yo)
        dyo = dn * post_g
        do = ro * (dyo - yo * _mean(dyo * yo))
        dob = do.astype(act_dtype)
        do2_ref[...] = dob

        for n in range(half):
            v = u3_ref[n].astype(F32)
            gt = u3_ref[n + half].astype(F32)
            sg = _sigmoid(gt)
            dhid = _mm_nt(dob, down_ref[n])
            fext_ref[n, 8:8 + tm, :] = dhid * (gt * sg)
            fext_ref[n + half, 8:8 + tm, :] = dhid * v * (sg * (1.0 + gt * (1.0 - sg)))

        last_sublane = _sublane_is(p, 7)
        dh = jnp.zeros((tm, d), F32)
        for n in range(n_chunks):
            fext_ref[n, 8 + tm:8 + tm + p, :] = jnp.where(last_sublane, fcar_ref[n], fext_ref[n, 9:9 + p, :])
            fcar_ref[n] = fext_ref[n, 1:1 + p, :]
            d2 = fext_ref[n, 8:8 + tm, :]
            d1 = fext_ref[n, 16:16 + tm, :]
            d0 = fext_ref[n, 24:24 + tm, :]
            u2v = u2_ref[n].astype(F32)
            dfcw_ref[n, 3:4, :] += _colsum(d2)
            dfcw_ref[n, 0:1, :] += _colsum(d0 * u2v)
            dfcw_ref[n, 1:2, :] += _colsum(d1 * u2v)
            dfcw_ref[n, 2:3, :] += _colsum(d2 * u2v)
            du2 = (fcw_ref[n, 0:1, :] * d0 + fcw_ref[n, 1:2, :] * d1 + fcw_ref[n, 2:3, :] * d2).astype(act_dtype)
            du2_ref[n] = du2
            dh = dh + _mm_nt(du2, up_ref[n])

        r = lax.rsqrt(_mean(xv * xv) + EPS)
        xn = xv * r
        dmod_ref[0:1, :] += _colsum(dh)
        dmod_ref[1:2, :] += _colsum(dh * (xn * pre_g))
        dy = dh * (1.0 + mod_ref[4:5, :])
        gd_ref[0:1, :] += _colsum(dy * xn)
        dxn = dy * pre_g
        dx1_ref[...] = dx2v + r * (dxn - xn * _mean(dxn * xn))

        if n_pay:
            @pl.when(i == nt - 1)
            def _():
                _exchange(pay_refs, recv_refs, rows, pay_layer, *sems, start=False)

    rev = lambda width: pl.BlockSpec((tm, width), lambda i: (nt - 1 - i, 0))
    chunked = pl.BlockSpec((n_chunks, tm, fc), lambda i: (0, nt - 1 - i, 0))
    return _carrier_call(
        body, f"ffn_bwd_{l}", (nt,),
        out_shape=[jax.ShapeDtypeStruct((t, d), F32), jax.ShapeDtypeStruct((n_chunks, t, fc), act_dtype),
                   jax.ShapeDtypeStruct((t, d), act_dtype),
                   jax.ShapeDtypeStruct((t // seq, 8, d), F32), jax.ShapeDtypeStruct((8, d), F32),
                   jax.ShapeDtypeStruct((n_chunks, 8, fc), F32)],
        in_specs=[rev(d), rev(d),
                  pl.BlockSpec((None, None, 8, d), lambda i: (l, (nt - 1 - i) // tps, 0, 0)),
                  _const_spec((None, 8, d), (l, 0, 0)),
                  chunked, chunked, rev(d),
                  _const_spec((n_chunks, d, fc), (0, 0, 0)),
                  _const_spec((n_chunks, 8, fc), (0, 0, 0)),
                  _const_spec((half, fc, d), (0, 0, 0))],
        out_specs=[rev(d), chunked, rev(d),
                   pl.BlockSpec((None, 8, d), lambda i: ((nt - 1 - i) // tps, 0, 0)),
                   _acc_spec((8, d), (0, 0)), _acc_spec((n_chunks, 8, fc), (0, 0, 0))],
        scratch=[pltpu.VMEM((n_chunks, 8 + tm + p, fc), F32), pltpu.VMEM((n_chunks, p, fc), F32)],
        operands=(dx2, x, mod, gains, u2, u3, o2, up, fcw, down),
        payload=payload, recvs=recvs)


def _pad_rows(a, rows):
    pad = [(0, 0)] * a.ndim
    pad[-2] = (0, rows - a.shape[-2])
    return jnp.pad(a, pad)


def kernel(x, c, ada_w, ada_b, pre_mix_g, post_mix_g, w_in, conv_w, conv_b, conv_ln_g, conv_ln_b, pool_w, pool_scale, w_out, pre_ffn_g, post_ffn_g, ffn_up, ffn_conv_w, ffn_conv_b, ffn_down, loss_target, m_ada_w, m_ada_b, m_pre_mix_g, m_post_mix_g, m_w_in, m_conv_w, m_conv_b, m_conv_ln_g, m_conv_ln_b, m_pool_w, m_pool_scale, m_w_out, m_pre_ffn_g, m_post_ffn_g, m_ffn_up, m_ffn_conv_w, m_ffn_conv_b, m_ffn_down, v_ada_w, v_ada_b, v_pre_mix_g, v_post_mix_g, v_w_in, v_conv_w, v_conv_b, v_conv_ln_g, v_conv_ln_b, v_pool_w, v_pool_scale, v_w_out, v_pre_ffn_g, v_post_ffn_g, v_ffn_up, v_ffn_conv_w, v_ffn_conv_b, v_ffn_down):
    bl, seq, d = x.shape
    n_layers = ada_w.shape[0]
    t = bl * seq
    dc = conv_b.shape[1]
    d_in = w_in.shape[2] * N_DEV
    d_mix = w_out.shape[1] * N_DEV
    n_taps = conv_w.shape[1]
    fc = ffn_up.shape[2]
    half = N_DEV // 2
    ada_cols = ada_w.shape[2]
    n_mod = ada_cols * N_DEV // d
    assert pool_scale.shape[1] == dc and n_taps == 31 and n_mod == 6 and ffn_conv_w.shape[1] == 3
    assert pool_w.shape[1:] == (len(POOL_WINDOWS), POOL_GROUP, POOL_GROUP)
    tm = TILE_TOKENS
    assert seq % tm == 0 and CONV_PREFIX <= tm - 8
    tk = 2048 if t % 2048 == 0 else tm
    act = MXU_DTYPE

    def tile_order(a, inverse=False):
        shape = (t // tm, tm // 8, 8, d) if inverse else (t // tm, 8, tm // 8, d)
        return a.reshape(shape).transpose(0, 2, 1, 3).reshape(t, d)

    ax = lax.axis_index
    me = 4 * ax("x") + 2 * ax("y") + ax("c")

    (c_all,) = _all_gather([c], "gather_c")
    c_all = c_all.reshape(N_DEV * bl, d)
    ada_b_cols = lax.dynamic_slice_in_dim(ada_b, me * ada_cols, ada_cols, axis=1)
    mod_cols = _ada_fwd(c_all, ada_w, ada_b_cols)

    w_in_s = w_in.astype(act).transpose(0, 2, 1)
    w_out_s, up_s, down_s = w_out.astype(act), ffn_up.astype(act), ffn_down.astype(act)
    conv_w_s = _pad_rows(conv_w, 32)
    fcw_s = _pad_rows(jnp.concatenate(
        [ffn_conv_w, lax.dynamic_slice_in_dim(ffn_conv_b, me * fc, fc, axis=1)[:, None, :]], axis=1), 8)

    mixer_shards = (w_in_s, w_out_s, conv_w_s)
    late_shards = (down_s, fcw_s)

    def mixer_weights(g_w_in, g_w_out, g_conv_w):
        return (g_w_in.reshape(d_in, d), g_conv_w.transpose(1, 0, 2).reshape(32, dc), g_w_out.reshape(d_mix, d))

    def late_weights(g_down, g_fcw):
        return (g_fcw, g_down.reshape(half, fc, d))

    g0 = _all_gather([mod_cols, *mixer_shards], "gather_w0", layers=[None, 0, 0, 0])
    mod_all = g0[0].transpose(1, 2, 0, 3).reshape(n_layers, N_DEV * bl, n_mod, d)
    mod = _pad_rows(lax.dynamic_slice_in_dim(mod_all, me * bl, bl, axis=1), 8)
    mixers, lates, ups = [mixer_weights(*g0[1:])], [], []
    gains_mix = _pad_rows(jnp.stack([pre_mix_g, post_mix_g], axis=1), 8)
    gains_ffn = _pad_rows(jnp.stack([pre_ffn_g, post_ffn_g], axis=1), 8)
    vec = _pad_rows(jnp.stack([conv_b, conv_ln_g, conv_ln_b, pool_scale], axis=1), 8)
    pool_w_b = pool_w.astype(act)

    kw = dict(seq=seq, tm=tm, act_dtype=act)
    xs = tile_order(x.reshape(t, d))
    saved = []
    for l in range(n_layers):
        w_in_t, cw, w_o = mixers[l]
        (x1, h1, u, a1, ap, o, tails), got = _mixer_fwd(
            l, xs, mod, gains_mix, w_in_t, cw, vec, pool_w_b, w_o, **kw,
            payload=(up_s, *late_shards) if l == 0 else (up_s,), gather_layer=l)
        ups.append(got[0])
        if l == 0:
            lates.append(late_weights(*got[1:]))
        up_l, (fcw_l, down_l) = ups[l], lates[l]
        (x2, h2, u2, u3, hid, o2), got = _ffn_fwd(l, x1, mod, gains_ffn, up_l, fcw_l, down_l, **kw,
                                                  payload=(*mixer_shards, *late_shards) if l + 1 < n_layers else (),
                                                  gather_layer=l + 1)
        if got:
            mixers.append(mixer_weights(*got[:3]))
            lates.append(late_weights(*got[3:]))
        saved.append((xs, h1, u, tails, a1, ap, o, x1, h2, u2, u3, hid, o2))
        xs = x2

    dx, loss_part = _loss_grad(xs, tile_order(loss_target.reshape(t, d)), tm)

    def landing(shard, dtype):
        return lax.empty((N_DEV, n_layers) + shard.shape[1:], dtype)

    r_up, r_down = landing(ffn_up, act), landing(ffn_down, act)
    r_w_in, r_w_out = landing(w_in, act), landing(w_out, act)
    r_conv_w, r_fcw = landing(conv_w, F32), landing(ffn_conv_w, F32)
    def cuts(l):
        first, second = (7, 10) if l == 0 else (3, 7.5)
        return int(first * d / 10) // 16 * 16, int(second * d / 10) // 16 * 16

    def late_rows(l):
        cut1, cut2 = cuts(l)
        return ((cut1, cut2 - cut1), None, None, None, None)

    dmods, smalls = [], []
    pending = ()
    for l in reversed(range(n_layers)):
        xin, h1, u, tails, a1, ap, o, x1, h2, u2, u3, hid, o2 = saved[l]
        (w_in_t, cw, w_o), (fcw_l, down_l), up_l = mixers[l], lates[l], ups[l]
        (dx1, du2, do2, dmod_b, gd_b, dfcw), got = _ffn_bwd(
            l, dx, x1, mod, gains_ffn, u2, u3, o2, up_l, fcw_l, down_l, **kw,
            payload=pending, recvs=(r_up, r_w_in, r_w_out, r_conv_w, r_fcw) if pending else (),
            rows=late_rows(l + 1), pay_layer=l + 1)
        if pending:
            r_up, r_w_in, r_w_out, r_conv_w, r_fcw = got
            cut2 = cuts(l + 1)[1]
            p_up, (r_up,) = _matmul_tn(f"dw_up_{l}", h2[None], du2, tk, act, payload=pending[:1], recvs=(r_up,),
                                       rows=((cut2, d - cut2),), pay_layer=l + 1)
        else:
            p_up, _ = _matmul_tn(f"dw_up_{l}", h2[None], du2, tk, act)
        p_down = _matmul_tn(f"dw_down_{l}", hid, do2[None], tk, act)[0].reshape(N_DEV, fc // 2, d)
        (dx, du, do, dmod_a, gd_a, gv, dcw, dpw), (r_down, r_up) = _mixer_bwd(
            l, dx1, xin, mod, gains_mix, u, tails, a1, o, w_in_t, cw, vec, pool_w_b, w_o, **kw,
            payload=(p_down, p_up), recvs=(r_down, r_up), rows=(None, (0, cuts(l)[0])), pay_layer=l)
        dw_in = _matmul_tn(f"dw_in_{l}", h1[None], du[None], tk, act)[0][0]
        p_w_in = dw_in.reshape(d, N_DEV, d_in // N_DEV).transpose(1, 0, 2)
        p_w_out = _matmul_tn(f"dw_out_{l}", ap[None], do[None], tk, act)[0][0].reshape(N_DEV, d_mix // N_DEV, d)
        p_conv_w = dcw[:n_taps].reshape(n_taps, N_DEV, dc // N_DEV).transpose(1, 0, 2)
        pending = (p_up, p_w_in, p_w_out, p_conv_w, dfcw[:, :3, :])
        dmods.append(jnp.concatenate([dmod_a[:, 0:3], dmod_b[:, 0:3]], axis=1).reshape(bl, n_mod * d))
        smalls.append(jnp.concatenate(
            [gd_a[0], gd_a[1], gv[0], gv[1], gv[2], gv[3], gd_b[0], gd_b[1], dfcw[:, 3, :].reshape(-1),
             dpw.reshape(-1)]))
    dmods.reverse()
    smalls.reverse()

    dmod_loc = jnp.stack(dmods)
    small_loc = jnp.stack(smalls)
    n_small = small_loc.shape[1]
    small_cols = 8 * LANES if (n_layers * n_small) % (8 * LANES) == 0 else LANES
    (r_up, r_w_in, r_w_out, r_conv_w, r_fcw), (g_dmod, g_small, g_loss) = _exchange_and_gather(
        pending, (r_up, r_w_in, r_w_out, r_conv_w, r_fcw), late_rows(0), 0,
        [dmod_loc.astype(act), small_loc.reshape(-1, small_cols).astype(act), loss_part], "exchange_tail")
    g_dmod = g_dmod.astype(F32)

    loss = _sum_parts(g_loss)[0, 0]

    def flat2(a):
        return a.reshape(-1, a.shape[-1])

    def update(name, parts, w, m, v):
        outs = _adam_reduce(name, parts.reshape(parts.shape[0], -1, w.shape[-1]), flat2(w), flat2(m), flat2(v))
        return [o_.reshape(w.shape) for o_ in outs]

    res = {}
    res["w_in"] = update("adam_w_in", r_w_in, w_in, m_w_in, v_w_in)
    res["w_out"] = update("adam_w_out", r_w_out, w_out, m_w_out, v_w_out)
    res["ffn_up"] = update("adam_ffn_up", r_up, ffn_up, m_ffn_up, v_ffn_up)
    res["ffn_down"] = update("adam_ffn_down", r_down, ffn_down, m_ffn_down, v_ffn_down)
    res["conv_w"] = update("adam_conv_w", r_conv_w, conv_w, m_conv_w, v_conv_w)
    res["ffn_conv_w"] = update("adam_ffn_conv_w", r_fcw, ffn_conv_w, m_ffn_conv_w, v_ffn_conv_w)

    dmod_all = g_dmod.transpose(1, 0, 2, 3).reshape(n_layers, N_DEV * bl, n_mod * d)
    dmod_cols = lax.dynamic_slice_in_dim(dmod_all, me * ada_cols, ada_cols, axis=2)
    res["ada_w"] = list(_ada_bwd(c_all.T, dmod_cols, ada_w, m_ada_w, v_ada_w))
    res["ada_b"] = update("adam_ada_b", dmod_all.transpose(1, 0, 2), ada_b, m_ada_b, v_ada_b)

    small_names = ["pre_mix_g", "post_mix_g", "conv_b", "conv_ln_g", "conv_ln_b", "pool_scale", "pre_ffn_g",
                   "post_ffn_g", "ffn_conv_b", "pool_w"]
    small_w = [pre_mix_g, post_mix_g, conv_b, conv_ln_g, conv_ln_b, pool_scale, pre_ffn_g, post_ffn_g,
               ffn_conv_b, pool_w]
    small_m = [m_pre_mix_g, m_post_mix_g, m_conv_b, m_conv_ln_g, m_conv_ln_b, m_pool_scale, m_pre_ffn_g,
               m_post_ffn_g, m_ffn_conv_b, m_pool_w]
    small_v = [v_pre_mix_g, v_post_mix_g, v_conv_b, v_conv_ln_g, v_conv_ln_b, v_pool_scale, v_pre_ffn_g,
               v_post_ffn_g, v_ffn_conv_b, v_pool_w]

    def pack(arrs):
        return jnp.concatenate([a.reshape(n_layers, -1) for a in arrs], axis=1).reshape(-1, small_cols)

    outs = _adam_reduce("adam_small", g_small, pack(small_w), pack(small_m), pack(small_v))
    outs = [o_.reshape(n_layers, n_small) for o_ in outs]
    off = 0
    for name, w in zip(small_names, small_w):
        size = w[0].size
        res[name] = [o_[:, off:off + size].reshape(w.shape) for o_ in outs]
        off += size

    order = ["ada_w", "ada_b", "pre_mix_g", "post_mix_g", "w_in", "conv_w", "conv_b", "conv_ln_g", "conv_ln_b",
             "pool_w", "pool_scale", "w_out", "pre_ffn_g", "post_ffn_g", "ffn_up", "ffn_conv_w", "ffn_conv_b",
             "ffn_down"]
    grad_x = tile_order(dx, inverse=True).reshape(bl, seq, d)
    return (loss, grad_x, *[res[n][0] for n in order], *[res[n][1] for n in order],
            *[res[n][2] for n in order], *[res[n][3] for n in order])
```

```python
import jax
import jax.numpy as jnp
from jax import lax
from jax.experimental import pallas as pl
from jax.experimental.pallas import tpu as pltpu

N_DEV = 8
EPS = 1e-6
POOL_WINDOWS = (2, 4, 8, 16)
POOL_GROUP = 128
TILE_TOKENS = 256
CONV_PREFIX = 8 * 30
FFN_PREFIX = 8 * 2
LANES = 128
ROW_CHUNK = 128
MXU_DTYPE = jnp.bfloat16
VMEM_LIMIT = 60 * 1024 * 1024

ADAM_LR = 0.001
ADAM_B1 = 0.9
ADAM_B2 = 0.999
ADAM_EPS = 1e-08
ADAM_WD = 0.01
ADAM_STEP = 10

MESH = pl.DeviceIdType.MESH
F32 = jnp.float32


def _mm(a, b):
    return jnp.dot(a.astype(MXU_DTYPE), b.astype(MXU_DTYPE), preferred_element_type=F32)


def _mm_nt(a, b):
    return lax.dot_general(a.astype(MXU_DTYPE), b.astype(MXU_DTYPE), (((1,), (1,)), ((), ())),
                           preferred_element_type=F32)


def _mm_tn(a, b):
    return lax.dot_general(a.astype(MXU_DTYPE), b.astype(MXU_DTYPE), (((0,), (0,)), ((), ())),
                           preferred_element_type=F32)


def _mean(v):
    return jnp.mean(v, axis=-1, keepdims=True)


def _colsum(v):
    return jnp.sum(v, axis=0, keepdims=True)


def _sigmoid(v):
    return jax.nn.sigmoid(v)


def _params(n_grid=1):
    return pltpu.CompilerParams(dimension_semantics=("arbitrary",) * n_grid, vmem_limit_bytes=VMEM_LIMIT)


def _const_spec(shape, index):
    return pl.BlockSpec(shape, lambda *_: index, pipeline_mode=pl.Buffered(1))


def _acc_spec(shape, index):
    return pl.BlockSpec(shape, lambda *_: index)


def _position():
    x, y, c = lax.axis_index("x"), lax.axis_index("y"), lax.axis_index("c")
    return x, y, c


def _gather_phase(ins, outs, send_sems, recv_sems, local_sems, phase):
    n = len(ins)
    x, y, c = _position()
    me, sibling = (x, y, c), (x, y, 1 - c)
    chips = [(1 - x, y), (x, 1 - y), (1 - x, 1 - y)]

    def slot(k, px, py, pc):
        return outs[k].at[4 * px + 2 * py + pc]

    def copy(k, s, block, to, src=None):
        return pltpu.make_async_remote_copy(
            src_ref=slot(k, *block) if src is None else src, dst_ref=slot(k, *block),
            send_sem=send_sems.at[k, s], recv_sem=recv_sems.at[k, s], device_id=to, device_id_type=MESH)

    mine = [pltpu.make_async_copy(ins[k], slot(k, *me), local_sems.at[k]) for k in range(n)]
    first = []
    for k in range(n):
        first.append(copy(k, 0, me, sibling, src=ins[k]))
        first += [copy(k, 1 + j, me, (*chip, c), src=ins[k]) for j, chip in enumerate(chips)]
    passed = [copy(k, 4 + j, (*chip, c), sibling) for j, chip in enumerate(chips) for k in range(n)]
    if phase == 0:
        for cp in mine + first:
            cp.start()
    elif phase == 1:
        for j, chip in enumerate(chips):
            for k in range(n):
                copy(k, 1 + j, (*chip, c), me).wait_recv()
                copy(k, 4 + j, (*chip, c), sibling).start()
    else:
        for k in range(n):
            copy(k, 0, sibling, me).wait_recv()
            for j, chip in enumerate(chips):
                copy(k, 4 + j, (*chip, 1 - c), me).wait_recv()
        for cp in first + passed:
            cp.wait_send()
        for cp in mine:
            cp.wait()


def _gather_scratch(n):
    return [pltpu.SemaphoreType.DMA((n, 7)), pltpu.SemaphoreType.DMA((n, 7)), pltpu.SemaphoreType.DMA((n,))]


def _layer_of(refs, layers):
    return [r if lay is None else r.at[lay] for r, lay in zip(refs, layers)]


def _gathered_shapes(arrs, layers):
    return [jax.ShapeDtypeStruct((N_DEV,) + (a.shape if lay is None else a.shape[1:]), a.dtype)
            for a, lay in zip(arrs, layers)]


def _all_gather(arrs, name, layers=None):
    n = len(arrs)
    layers = [None] * n if layers is None else layers

    def body(*refs):
        for phase in range(3):
            _gather_phase(_layer_of(refs[:n], layers), refs[n:2 * n], *refs[2 * n:], phase)

    any_spec = pl.BlockSpec(memory_space=pl.ANY)
    return pl.pallas_call(
        body, name=name,
        out_shape=_gathered_shapes(arrs, layers),
        in_specs=[any_spec] * n, out_specs=[any_spec] * n,
        scratch_shapes=_gather_scratch(n),
    )(*arrs)


def _exchange(payload, recvs, rows, layer, send_sems, recv_sems, local_sems, start):
    x, y, c = _position()
    me = 4 * x + 2 * y + c
    for k, (src, recv) in enumerate(zip(payload, recvs)):
        def block(ref, *index):
            return ref.at[index] if rows[k] is None else ref.at[(*index, pl.ds(*rows[k]))]

        local = pltpu.make_async_copy(block(src, me), block(recv, me, layer), local_sems.at[k])
        if start:
            local.start()
        else:
            local.wait()
        for j in range(1, N_DEV):
            px = (1 - x) if (j & 4) else x
            py = (1 - y) if (j & 2) else y
            pc = (1 - c) if (j & 1) else c
            peer = 4 * px + 2 * py + pc
            landing = block(recv, me, layer) if start else block(recv, peer, layer)
            cp = pltpu.make_async_remote_copy(
                src_ref=block(src, peer), dst_ref=landing, send_sem=send_sems.at[k, j - 1],
                recv_sem=recv_sems.at[k, j - 1], device_id=(px, py, pc), device_id_type=MESH)
            if start:
                cp.start()
            else:
                cp.wait()


def _exchange_scratch(n):
    return [pltpu.SemaphoreType.DMA((n, N_DEV - 1)), pltpu.SemaphoreType.DMA((n, N_DEV - 1)),
            pltpu.SemaphoreType.DMA((n,))]


def _exchange_and_gather(payload, recvs, rows, layer, arrs, name):
    n, m = len(payload), len(arrs)

    def body(*refs):
        pay, srcs = refs[:n], refs[2 * n:2 * n + m]
        outs, got = refs[2 * n + m:3 * n + m], refs[3 * n + m:3 * n + 2 * m]
        xsems, gsems = refs[3 * n + 2 * m:3 * n + 2 * m + 3], refs[3 * n + 2 * m + 3:]
        _exchange(pay, outs, rows, layer, *xsems, start=True)
        for phase in range(3):
            _gather_phase(srcs, got, *gsems, phase)
        _exchange(pay, outs, rows, layer, *xsems, start=False)

    any_spec = pl.BlockSpec(memory_space=pl.ANY)
    res = pl.pallas_call(
        body, name=name,
        out_shape=[jax.ShapeDtypeStruct(r.shape, r.dtype) for r in recvs]
        + [jax.ShapeDtypeStruct((N_DEV,) + a.shape, a.dtype) for a in arrs],
        in_specs=[any_spec] * (2 * n + m), out_specs=[any_spec] * (n + m),
        input_output_aliases={n + k: k for k in range(n)},
        scratch_shapes=_exchange_scratch(n) + _gather_scratch(m),
    )(*payload, *recvs, *arrs)
    return res[:n], res[n:]


def _carried_refs(refs, n_in, n_out, n_scratch, n_pay, aliased):
    ins = refs[:n_in]
    pay = refs[n_in:n_in + n_pay]
    o0 = n_in + (2 * n_pay if aliased else n_pay)
    outs = refs[o0:o0 + n_out]
    recvs = refs[o0 + n_out:o0 + n_out + n_pay]
    s0 = o0 + n_out + n_pay
    return ins, outs, refs[s0:s0 + n_scratch], pay, recvs, refs[s0 + n_scratch:]


def _carrier_call(body, name, grid, in_specs, out_specs, out_shape, scratch, operands, payload, recvs=None,
                  gather_layer=None):
    n_in, n_out, n_pay = len(in_specs), len(out_specs), len(payload)
    any_spec = pl.BlockSpec(memory_space=pl.ANY)
    if recvs is None:
        landing = _gathered_shapes(payload, [gather_layer] * n_pay)
        extra_in, aliases = list(payload), {}
        sems = _gather_scratch(n_pay) if n_pay else []
    else:
        landing = [jax.ShapeDtypeStruct(r.shape, r.dtype) for r in recvs]
        extra_in = list(payload) + list(recvs)
        aliases = {n_in + n_pay + k: n_out + k for k in range(n_pay)}
        sems = _exchange_scratch(n_pay) if n_pay else []
    res = pl.pallas_call(
        body, name=name, grid=grid,
        out_shape=list(out_shape) + landing,
        in_specs=list(in_specs) + [any_spec] * len(extra_in),
        out_specs=list(out_specs) + [any_spec] * n_pay,
        input_output_aliases=aliases,
        scratch_shapes=list(scratch) + sems,
        compiler_params=_params(len(grid)),
    )(*operands, *extra_in)
    return res[:n_out], res[n_out:]


def _adamw(w, g, m, v):
    m = ADAM_B1 * m + (1.0 - ADAM_B1) * g
    v = ADAM_B2 * v + (1.0 - ADAM_B2) * jnp.square(g)
    m_hat = m / (1.0 - ADAM_B1 ** ADAM_STEP)
    v_hat = v / (1.0 - ADAM_B2 ** ADAM_STEP)
    delta = -ADAM_LR * (m_hat / (jnp.sqrt(v_hat) + ADAM_EPS) + ADAM_WD * w)
    return delta, m, v


def _ada_fwd(c_all, ada_w, ada_b_cols):
    n_layers, d, cols = ada_w.shape
    b = c_all.shape[0]

    def body(c_ref, w_ref, b_ref, o_ref):
        cv = c_ref[...]
        act = cv * _sigmoid(cv)
        o_ref[...] = jnp.dot(act, w_ref[...], preferred_element_type=F32,
                             precision=lax.Precision.HIGHEST) + b_ref[...]

    return pl.pallas_call(
        body, name="ada_fwd", grid=(n_layers,),
        out_shape=jax.ShapeDtypeStruct((n_layers, b, cols), F32),
        in_specs=[pl.BlockSpec((b, d), lambda l: (0, 0)),
                  pl.BlockSpec((None, d, cols), lambda l: (l, 0, 0)),
                  pl.BlockSpec((None, 1, cols), lambda l: (l, 0, 0))],
        out_specs=pl.BlockSpec((None, b, cols), lambda l: (l, 0, 0)),
        compiler_params=_params(1),
    )(c_all, ada_w, ada_b_cols.reshape(n_layers, 1, cols))


def _ada_bwd(c_all_t, dmod_cols, w, m, v):
    n_layers, d, cols = w.shape
    b = c_all_t.shape[1]
    td = 256 if d % 256 == 0 else d

    def body(c_ref, dm_ref, w_ref, m_ref, v_ref, g_ref, dl_ref, nm_ref, nv_ref):
        cv = c_ref[...]
        act = cv * _sigmoid(cv)
        g = jnp.dot(act, dm_ref[...], preferred_element_type=F32, precision=lax.Precision.HIGHEST)
        delta, nm, nv = _adamw(w_ref[...], g, m_ref[...], v_ref[...])
        g_ref[...] = g
        dl_ref[...] = delta
        nm_ref[...] = nm
        nv_ref[...] = nv

    blk = pl.BlockSpec((None, td, cols), lambda l, i: (l, i, 0))
    shp = jax.ShapeDtypeStruct(w.shape, F32)
    return pl.pallas_call(
        body, name="ada_bwd", grid=(n_layers, d // td),
        out_shape=[shp] * 4,
        in_specs=[pl.BlockSpec((td, b), lambda l, i: (i, 0)),
                  pl.BlockSpec((None, b, cols), lambda l, i: (l, 0, 0)), blk, blk, blk],
        out_specs=[blk] * 4,
        compiler_params=_params(2),
    )(c_all_t, dmod_cols, w, m, v)


def _row_tile(rows, cols, budget=128 * 1024, step=8):
    best = None
    for t in range(step, rows + 1, step):
        if rows % t == 0 and t * cols <= budget:
            best = t
    return best if best is not None else rows


def _adam_reduce(name, parts, w, m, v):
    p, rows, cols = parts.shape
    tr = _row_tile(rows, cols, budget=(256 * 1024) // max(1, p // 4), step=8 if parts.dtype == F32 else 16)

    def body(p_ref, w_ref, m_ref, v_ref, g_ref, dl_ref, nm_ref, nv_ref):
        g = p_ref[0].astype(F32)
        for k in range(1, p):
            g = g + p_ref[k].astype(F32)
        delta, nm, nv = _adamw(w_ref[...], g, m_ref[...], v_ref[...])
        g_ref[...] = g
        dl_ref[...] = delta
        nm_ref[...] = nm
        nv_ref[...] = nv

    blk = pl.BlockSpec((tr, cols), lambda i: (i, 0))
    shp = jax.ShapeDtypeStruct((rows, cols), F32)
    return pl.pallas_call(
        body, name=name, grid=(rows // tr,),
        out_shape=[shp] * 4,
        in_specs=[pl.BlockSpec((p, tr, cols), lambda i: (0, i, 0)), blk, blk, blk],
        out_specs=[blk] * 4,
        compiler_params=_params(1),
    )(parts, w, m, v)


def _sum_parts(parts):
    p = parts.shape[0]

    def body(p_ref, o_ref):
        acc = p_ref[0]
        for k in range(1, p):
            acc = acc + p_ref[k]
        o_ref[...] = acc

    return pl.pallas_call(body, name="loss_sum", out_shape=jax.ShapeDtypeStruct(parts.shape[1:], F32))(parts)


def _loss_grad(y, target, tm):
    t, d = y.shape

    def body(y_ref, t_ref, dy_ref, loss_ref):
        @pl.when(pl.program_id(0) == 0)
        def _():
            loss_ref[...] = jnp.zeros_like(loss_ref)

        diff = y_ref[...] - t_ref[...]
        dy_ref[...] = diff / d
        part = 0.5 * jnp.sum(_mean(diff * diff), axis=0, keepdims=True)
        loss_ref[...] += jnp.broadcast_to(part, loss_ref.shape)

    blk = pl.BlockSpec((tm, d), lambda i: (i, 0))
    return pl.pallas_call(
        body, name="loss_grad", grid=(t // tm,),
        out_shape=[jax.ShapeDtypeStruct((t, d), F32), jax.ShapeDtypeStruct((8, LANES), F32)],
        in_specs=[blk, blk], out_specs=[blk, pl.BlockSpec((8, LANES), lambda i: (0, 0))],
        compiler_params=_params(1),
    )(y, target)


def _matmul_tn(name, a, b, tk, out_dtype, payload=(), recvs=(), rows=(), pay_layer=0):
    ga, t, m = a.shape
    gb, _, n = b.shape
    g = max(ga, gb)
    n_k = t // tk
    n_pay = len(payload)

    def body(*refs):
        (a_ref, b_ref), (o_ref,), (acc_ref,), pay_refs, recv_refs, sems = _carried_refs(
            refs, 2, 1, 1, n_pay, aliased=True)
        gi, k = pl.program_id(0), pl.program_id(1)

        if n_pay:
            @pl.when(jnp.logical_and(gi == 0, k == 0))
            def _():
                _exchange(pay_refs, recv_refs, rows, pay_layer, *sems, start=True)

        @pl.when(k == 0)
        def _():
            acc_ref[...] = jnp.zeros_like(acc_ref)

        acc_ref[...] += _mm_tn(a_ref[...], b_ref[...])

        @pl.when(k == n_k - 1)
        def _():
            o_ref[...] = acc_ref[...].astype(out_dtype)

        if n_pay:
            @pl.when(jnp.logical_and(gi == g - 1, k == n_k - 1))
            def _():
                _exchange(pay_refs, recv_refs, rows, pay_layer, *sems, start=False)

    (out,), got = _carrier_call(
        body, name, (g, n_k),
        out_shape=[jax.ShapeDtypeStruct((g, m, n), out_dtype)],
        in_specs=[pl.BlockSpec((None, tk, m), (lambda gi, k: (gi, k, 0)) if ga > 1 else (lambda gi, k: (0, k, 0))),
                  pl.BlockSpec((None, tk, n), (lambda gi, k: (gi, k, 0)) if gb > 1 else (lambda gi, k: (0, k, 0)))],
        out_specs=[pl.BlockSpec((None, m, n), lambda gi, k: (gi, 0, 0))],
        scratch=[pltpu.VMEM((m, n), F32)],
        operands=(a, b), payload=payload, recvs=recvs)
    return out, got


def _time_of_row(tm):
    i = lax.broadcasted_iota(jnp.int32, (tm, 1), 0)
    return (i % 8) * (tm // 8) + i // 8


def _sublane_is(rows, s):
    return lax.broadcasted_iota(jnp.int32, (rows, 1), 0) % 8 == s


def _conv_taps(src_ref, col0, ncols, tm, tap_rows, w_ref, init_of, store):
    rc = min(ROW_CHUNK, tm)
    for cb in range(ncols // LANES):
        cs = slice(cb * LANES, (cb + 1) * LANES)
        ss = slice(col0 + cb * LANES, col0 + (cb + 1) * LANES)
        for r0 in range(0, tm, rc):
            acc = init_of(cs, rc)
            for k, row in enumerate(tap_rows):
                acc = acc + w_ref[k:k + 1, cs] * src_ref[r0 + row:r0 + row + rc, ss]
            store(r0, rc, cs, acc)


def _mixer_fwd(l, x, mod, gains, w_in_t, conv_w, vec, pool_w, w_out, *, seq, tm, act_dtype, payload=(),
               gather_layer=None):
    t, d = x.shape
    d_in = w_in_t.shape[0]
    dc = conv_w.shape[-1]
    d_mix = w_out.shape[0]
    n_taps = 31
    nt, tps = t // tm, seq // tm
    p = CONV_PREFIX
    n_pay = len(payload)

    def body(*refs):
        ins, outs, scr, pay_refs, got_refs, sems = _carried_refs(refs, 8, 7, 2, n_pay, aliased=False)
        x_ref, mod_ref, g_ref, win_ref, cw_ref, v_ref, pw_ref, wout_ref = ins
        x1_ref, h1_ref, u_ref, a1_ref, ap_ref, o_ref, tail_ref = outs
        ext_ref, car_ref = scr
        i = pl.program_id(0)
        first = (i % tps) == 0

        if n_pay:
            @pl.when(i == 0)
            def _():
                _gather_phase(_layer_of(pay_refs, [gather_layer] * n_pay), got_refs, *sems, 0)

        xv = x_ref[...]
        r = lax.rsqrt(_mean(xv * xv) + EPS)
        hv = (xv * r) * g_ref[0:1, :] * (1.0 + mod_ref[1:2, :]) + mod_ref[0:1, :]
        hb = hv.astype(act_dtype)
        h1_ref[...] = hb
        u = _mm_nt(hb, win_ref[...])
        u_ref[...] = u
        a0 = u[:, :dc] * _sigmoid(u[:, dc:2 * dc])

        @pl.when(first)
        def _():
            car_ref[...] = jnp.zeros_like(car_ref)

        @pl.when(i == 0)
        def _():
            ext_ref[p + tm:p + tm + 8, :] = jnp.zeros((8, ext_ref.shape[1]), F32)

        ext_ref[p:p + tm, 0:dc] = a0
        ext_ref[p:p + tm, dc:] = u[:, 2 * dc:]
        ext_ref[0:p, :] = jnp.where(_sublane_is(p, 0), car_ref[...], ext_ref[tm - 1:tm - 1 + p, :])

        def store(r0, rc, cs, acc):
            a1_ref[r0:r0 + rc, cs] = acc

        _conv_taps(ext_ref, 0, dc, tm, [p - 8 * (n_taps - 1 - k) for k in range(n_taps)],
                   cw_ref,
                   lambda cs, rc: jnp.broadcast_to(v_ref[0:1, cs], (rc, LANES)), store)
        a1 = a1_ref[...]
        mu = _mean(a1)
        xc = a1 - mu
        rstd = lax.rsqrt(_mean(xc * xc) + EPS)
        a2 = (xc * rstd) * v_ref[1:2, :] + v_ref[2:3, :]
        ap_ref[:, 0:dc] = (a2 * _sigmoid(a2)).astype(act_dtype)

        pos = (i % tps) * tm + _time_of_row(tm)
        for g, w in enumerate(POOL_WINDOWS):
            cs = slice(dc + g * POOL_GROUP, dc + (g + 1) * POOL_GROUP)
            s = ext_ref[p:p + tm, cs]
            for j in range(1, w):
                s = s + ext_ref[p - 8 * j:p - 8 * j + tm, cs]
            cnt = jnp.minimum(pos + 1, w).astype(F32)
            dv = s / cnt - ext_ref[p:p + tm, cs]
            q = _mm(dv, pw_ref[g])
            ap_ref[:, cs] = (q * v_ref[3:4, g * POOL_GROUP:(g + 1) * POOL_GROUP]).astype(act_dtype)

        o = _mm(ap_ref[...], wout_ref[...])
        o_ref[...] = o
        ro = lax.rsqrt(_mean(o * o) + EPS)
        x1_ref[...] = xv + (1.0 + mod_ref[2:3, :]) * ((o * ro) * g_ref[1:2, :])

        nxt = ext_ref[tm + 7:tm + 7 + p, :]
        car_ref[...] = nxt
        tail_ref[...] = nxt

        if n_pay:
            @pl.when(i == max(nt - 2, 0))
            def _():
                _gather_phase(_layer_of(pay_refs, [gather_layer] * n_pay), got_refs, *sems, 1)

            @pl.when(i == nt - 1)
            def _():
                _gather_phase(_layer_of(pay_refs, [gather_layer] * n_pay), got_refs, *sems, 2)

    row = lambda width: pl.BlockSpec((tm, width), lambda i: (i, 0))
    return _carrier_call(
        body, f"mixer_fwd_{l}", (nt,),
        out_shape=[jax.ShapeDtypeStruct((t, d), F32), jax.ShapeDtypeStruct((t, d), act_dtype),
                   jax.ShapeDtypeStruct((t, d_in), F32), jax.ShapeDtypeStruct((t, dc), F32),
                   jax.ShapeDtypeStruct((t, d_mix), act_dtype), jax.ShapeDtypeStruct((t, d), F32),
                   jax.ShapeDtypeStruct((nt, p, d_mix), F32)],
        in_specs=[row(d),
                  pl.BlockSpec((None, None, 8, d), lambda i: (l, i // tps, 0, 0)),
                  _const_spec((None, 8, d), (l, 0, 0)),
                  _const_spec((d_in, d), (0, 0)),
                  _const_spec((32, dc), (0, 0)),
                  _const_spec((None, 8, dc), (l, 0, 0)),
                  _const_spec((None, len(POOL_WINDOWS), POOL_GROUP, POOL_GROUP), (l, 0, 0, 0)),
                  _const_spec((d_mix, d), (0, 0))],
        out_specs=[row(d), row(d), row(d_in), row(dc), row(d_mix), row(d),
                   pl.BlockSpec((None, p, d_mix), lambda i: (i, 0, 0))],
        scratch=[pltpu.VMEM((p + tm + 8, d_mix), F32), pltpu.VMEM((p, d_mix), F32)],
        operands=(x, mod, gains, w_in_t, conv_w, vec, pool_w, w_out),
        payload=payload, gather_layer=gather_layer)


def _mixer_bwd(l, dx1, x, mod, gains, u, tails, a1, o, w_in_t, conv_w, vec, pool_w, w_out, *, seq, tm, act_dtype,
               payload=(), recvs=(), rows=(), pay_layer=0):
    t, d = x.shape
    d_in = w_in_t.shape[0]
    dc = conv_w.shape[-1]
    d_mix = w_out.shape[0]
    n_taps = 31
    nt, tps = t // tm, seq // tm
    p = CONV_PREFIX
    n_groups = len(POOL_WINDOWS)
    n_pay = len(payload)

    def body(*refs):
        ins, outs, scr, pay_refs, recv_refs, sems = _carried_refs(refs, 13, 8, 3, n_pay, aliased=True)
        dx1_ref, x_ref, mod_ref, g_ref, u_ref, tail_ref, a1_ref, o_ref, win_ref, cw_ref, v_ref, pw_ref, wout_ref = ins
        dx_ref, du_ref, do_ref, dmod_ref, gd_ref, gv_ref, dcw_ref, dpw_ref = outs
        ext_ref, fext_ref, fcar_ref = scr
        i = pl.program_id(0)
        j = nt - 1 - i
        first_in_seq = (j % tps) == 0
        last_in_seq = (j % tps) == tps - 1

        if n_pay:
            @pl.when(i == 0)
            def _():
                _exchange(pay_refs, recv_refs, rows, pay_layer, *sems, start=True)

        @pl.when(i == 0)
        def _():
            gd_ref[...] = jnp.zeros_like(gd_ref)
            gv_ref[...] = jnp.zeros_like(gv_ref)
            dcw_ref[...] = jnp.zeros_like(dcw_ref)
            dpw_ref[...] = jnp.zeros_like(dpw_ref)
            fext_ref[0:8, :] = jnp.zeros((8, fext_ref.shape[1]), F32)

        @pl.when(last_in_seq)
        def _():
            dmod_ref[...] = jnp.zeros_like(dmod_ref)
            fcar_ref[...] = jnp.zeros_like(fcar_ref)

        xv = x_ref[...]
        dx1v = dx1_ref[...]
        pre_g, post_g = g_ref[0:1, :], g_ref[1:2, :]

        ov = o_ref[...]
        ro = lax.rsqrt(_mean(ov * ov) + EPS)
        yo = ov * ro
        dmod_ref[2:3, :] += _colsum(dx1v * (yo * post_g))
        dn = dx1v * (1.0 + mod_ref[2:3, :])
        gd_ref[1:2, :] += _colsum(dn * yo)
        dyo = dn * post_g
        do = ro * (dyo - yo * _mean(dyo * yo))
        dob = do.astype(act_dtype)
        do_ref[...] = dob
        dap = _mm_nt(dob, wout_ref[...])

        uv = u_ref[...]
        val, gate = uv[:, :dc], uv[:, dc:2 * dc]
        sg = _sigmoid(gate)
        ext_ref[p:p + tm, 0:dc] = val * sg
        ext_ref[p:p + tm, dc:] = uv[:, 2 * dc:]
        keep = jnp.where(first_in_seq, 0.0, 1.0).astype(F32)
        ext_ref[0:p, :] = jnp.where(_sublane_is(p, 0), tail_ref[...] * keep, ext_ref[tm - 1:tm - 1 + p, :])

        a1v = a1_ref[...]
        mu = _mean(a1v)
        xc = a1v - mu
        rstd = lax.rsqrt(_mean(xc * xc) + EPS)
        xh = xc * rstd
        ln_g = v_ref[1:2, :]
        a2 = xh * ln_g + v_ref[2:3, :]
        s2 = _sigmoid(a2)
        da2 = dap[:, :dc] * (s2 * (1.0 + a2 * (1.0 - s2)))
        gv_ref[1:2, :] += _colsum(da2 * xh)
        gv_ref[2:3, :] += _colsum(da2)
        dxh = da2 * ln_g
        da1 = rstd * (dxh - _mean(dxh) - xh * _mean(dxh * xh))
        gv_ref[0:1, :] += _colsum(da1)
        last_sublane = _sublane_is(p, 7)

        def put(cs, value):
            fext_ref[8:8 + tm, cs] = value
            fext_ref[8 + tm:8 + tm + p, cs] = jnp.where(last_sublane, fcar_ref[:, cs], fext_ref[9:9 + p, cs])

        put(slice(0, dc), da1)

        for k in range(n_taps):
            row = p - 8 * (n_taps - 1 - k)
            dcw_ref[k:k + 1, :] += _colsum(fext_ref[8:8 + tm, 0:dc] * ext_ref[row:row + tm, 0:dc])

        def store(r0, rc, cs, acc):
            sgc = _sigmoid(u_ref[r0:r0 + rc, dc + cs.start:dc + cs.stop])
            vc = u_ref[r0:r0 + rc, cs]
            du_ref[r0:r0 + rc, cs] = (acc * sgc).astype(act_dtype)
            du_ref[r0:r0 + rc, dc + cs.start:dc + cs.stop] = (acc * vc * sgc * (1.0 - sgc)).astype(act_dtype)

        _conv_taps(fext_ref, 0, dc, tm, [8 + 8 * (n_taps - 1 - k) for k in range(n_taps)],
                   cw_ref,
                   lambda cs, rc: jnp.zeros((rc, LANES), F32), store)

        pos = (j % tps) * tm + _time_of_row(tm)
        for g, w in enumerate(POOL_WINDOWS):
            cs = slice(dc + g * POOL_GROUP, dc + (g + 1) * POOL_GROUP)
            gs = slice(g * POOL_GROUP, (g + 1) * POOL_GROUP)
            s = ext_ref[p:p + tm, cs]
            for jj in range(1, w):
                s = s + ext_ref[p - 8 * jj:p - 8 * jj + tm, cs]
            cnt = jnp.minimum(pos + 1, w).astype(F32)
            dv = (s / cnt - ext_ref[p:p + tm, cs]).astype(MXU_DTYPE)
            q = _mm(dv, pw_ref[g])
            dp = dap[:, cs]
            gv_ref[3:4, gs] += _colsum(dp * q)
            dq = (dp * v_ref[3:4, gs]).astype(MXU_DTYPE)
            dpw_ref[g] += _mm_tn(dv, dq)
            dd = _mm_nt(dq, pw_ref[g])
            put(cs, dd / cnt)
            dhp = fext_ref[8:8 + tm, cs]
            for jj in range(1, w):
                dhp = dhp + fext_ref[8 + 8 * jj:8 + 8 * jj + tm, cs]
            du_ref[:, dc + cs.start:dc + cs.stop] = (dhp - dd).astype(act_dtype)

        fcar_ref[...] = fext_ref[1:1 + p, :]

        dh = _mm(du_ref[...], win_ref[...])

        r = lax.rsqrt(_mean(xv * xv) + EPS)
        xn = xv * r
        dmod_ref[0:1, :] += _colsum(dh)
        dmod_ref[1:2, :] += _colsum(dh * (xn * pre_g))
        dy = dh * (1.0 + mod_ref[1:2, :])
        gd_ref[0:1, :] += _colsum(dy * xn)
        dxn = dy * pre_g
        dx_ref[...] = dx1v + r * (dxn - xn * _mean(dxn * xn))

        if n_pay:
            @pl.when(i == nt - 1)
            def _():
                _exchange(pay_refs, recv_refs, rows, pay_layer, *sems, start=False)

    rev = lambda width: pl.BlockSpec((tm, width), lambda i: (nt - 1 - i, 0))
    return _carrier_call(
        body, f"mixer_bwd_{l}", (nt,),
        out_shape=[jax.ShapeDtypeStruct((t, d), F32), jax.ShapeDtypeStruct((t, d_in), act_dtype),
                   jax.ShapeDtypeStruct((t, d), act_dtype),
                   jax.ShapeDtypeStruct((t // seq, 8, d), F32), jax.ShapeDtypeStruct((8, d), F32),
                   jax.ShapeDtypeStruct((8, dc), F32), jax.ShapeDtypeStruct((32, dc), F32),
                   jax.ShapeDtypeStruct((n_groups, POOL_GROUP, POOL_GROUP), F32)],
        in_specs=[rev(d), rev(d),
                  pl.BlockSpec((None, None, 8, d), lambda i: (l, (nt - 1 - i) // tps, 0, 0)),
                  _const_spec((None, 8, d), (l, 0, 0)),
                  rev(d_in),
                  pl.BlockSpec((None, p, d_mix), lambda i: (jnp.maximum(nt - 2 - i, 0), 0, 0)),
                  rev(dc), rev(d),
                  _const_spec((d_in, d), (0, 0)),
                  _const_spec((32, dc), (0, 0)),
                  _const_spec((None, 8, dc), (l, 0, 0)),
                  _const_spec((None, n_groups, POOL_GROUP, POOL_GROUP), (l, 0, 0, 0)),
                  _const_spec((d_mix, d), (0, 0))],
        out_specs=[rev(d), rev(d_in), rev(d),
                   pl.BlockSpec((None, 8, d), lambda i: ((nt - 1 - i) // tps, 0, 0)),
                   _acc_spec((8, d), (0, 0)), _acc_spec((8, dc), (0, 0)), _acc_spec((32, dc), (0, 0)),
                   _acc_spec((n_groups, POOL_GROUP, POOL_GROUP), (0, 0, 0))],
        scratch=[pltpu.VMEM((p + tm, d_mix), F32), pltpu.VMEM((8 + tm + p, d_mix), F32),
                 pltpu.VMEM((p, d_mix), F32)],
        operands=(dx1, x, mod, gains, u, tails, a1, o, w_in_t, conv_w, vec, pool_w, w_out),
        payload=payload, recvs=recvs)


def _ffn_fwd(l, x, mod, gains, up, fcw, down, *, seq, tm, act_dtype, payload=(), gather_layer=None):
    t, d = x.shape
    n_chunks, _, fc = up.shape
    half = n_chunks // 2
    nt, tps = t // tm, seq // tm
    p = FFN_PREFIX
    n_pay = len(payload)

    def body(*refs):
        ins, outs, scr, pay_refs, got_refs, sems = _carried_refs(refs, 6, 6, 2, n_pay, aliased=False)
        x_ref, mod_ref, g_ref, up_ref, fcw_ref, down_ref = ins
        x2_ref, h2_ref, u2_ref, u3_ref, hid_ref, o2_ref = outs
        ext_ref, car_ref = scr
        i = pl.program_id(0)
        first = (i % tps) == 0

        if n_pay:
            @pl.when(i == 0)
            def _():
                _gather_phase(_layer_of(pay_refs, [gather_layer] * n_pay), got_refs, *sems, 0)

        xv = x_ref[...]
        r = lax.rsqrt(_mean(xv * xv) + EPS)
        hv = (xv * r) * g_ref[0:1, :] * (1.0 + mod_ref[4:5, :]) + mod_ref[3:4, :]
        hb = hv.astype(act_dtype)
        h2_ref[...] = hb

        @pl.when(first)
        def _():
            car_ref[...] = jnp.zeros_like(car_ref)

        @pl.when(i == 0)
        def _():
            for n in range(n_chunks):
                ext_ref[n, p + tm:p + tm + 8, :] = jnp.zeros((8, fc), F32)

        first_sublane = _sublane_is(p, 0)

        def project(n):
            un = _mm(hb, up_ref[n])
            ext_ref[n, p:p + tm, :] = un
            u2_ref[n] = un.astype(act_dtype)
            ext_ref[n, 0:p, :] = jnp.where(first_sublane, car_ref[n], ext_ref[n, tm - 1:tm - 1 + p, :])
            car_ref[n] = ext_ref[n, tm + 7:tm + 7 + p, :]

        def conv(n):
            return (fcw_ref[n, 3:4, :] + fcw_ref[n, 0:1, :] * ext_ref[n, p - 16:p - 16 + tm, :]
                    + fcw_ref[n, 1:2, :] * ext_ref[n, p - 8:p - 8 + tm, :]
                    + fcw_ref[n, 2:3, :] * ext_ref[n, p:p + tm, :])

        for n in range(n_chunks):
            project(n)
        o2 = jnp.zeros((tm, d), F32)
        for n in range(half):
            gt = conv(n + half)
            v = conv(n)
            sg = _sigmoid(gt)
            silu = gt * sg
            u3_ref[n] = silu.astype(act_dtype)
            u3_ref[n + half] = (v * (sg * (1.0 + gt * (1.0 - sg)))).astype(act_dtype)
            hid = (silu * v).astype(act_dtype)
            hid_ref[n] = hid
            o2 = o2 + _mm(hid, down_ref[n])
        o2_ref[...] = o2
        ro = lax.rsqrt(_mean(o2 * o2) + EPS)
        x2_ref[...] = xv + (1.0 + mod_ref[5:6, :]) * ((o2 * ro) * g_ref[1:2, :])

        if n_pay:
            @pl.when(i == max(nt - 2, 0))
            def _():
                _gather_phase(_layer_of(pay_refs, [gather_layer] * n_pay), got_refs, *sems, 1)

            @pl.when(i == nt - 1)
            def _():
                _gather_phase(_layer_of(pay_refs, [gather_layer] * n_pay), got_refs, *sems, 2)

    row = lambda width: pl.BlockSpec((tm, width), lambda i: (i, 0))
    chunked = lambda n: pl.BlockSpec((n, tm, fc), lambda i: (0, i, 0))
    return _carrier_call(
        body, f"ffn_fwd_{l}", (nt,),
        out_shape=[jax.ShapeDtypeStruct((t, d), F32), jax.ShapeDtypeStruct((t, d), act_dtype),
                   jax.ShapeDtypeStruct((n_chunks, t, fc), act_dtype),
                   jax.ShapeDtypeStruct((n_chunks, t, fc), act_dtype),
                   jax.ShapeDtypeStruct((half, t, fc), act_dtype), jax.ShapeDtypeStruct((t, d), F32)],
        in_specs=[row(d),
                  pl.BlockSpec((None, None, 8, d), lambda i: (l, i // tps, 0, 0)),
                  _const_spec((None, 8, d), (l, 0, 0)),
                  _const_spec((n_chunks, d, fc), (0, 0, 0)),
                  _const_spec((n_chunks, 8, fc), (0, 0, 0)),
                  _const_spec((half, fc, d), (0, 0, 0))],
        out_specs=[row(d), row(d), chunked(n_chunks), chunked(n_chunks), chunked(half), row(d)],
        scratch=[pltpu.VMEM((n_chunks, p + tm + 8, fc), F32), pltpu.VMEM((n_chunks, p, fc), F32)],
        operands=(x, mod, gains, up, fcw, down),
        payload=payload, gather_layer=gather_layer)


def _ffn_bwd(l, dx2, x, mod, gains, u2, u3, o2, up, fcw, down, *, seq, tm, act_dtype,
             payload=(), recvs=(), rows=(), pay_layer=0):
    t, d = x.shape
    n_chunks, _, fc = up.shape
    half = n_chunks // 2
    nt, tps = t // tm, seq // tm
    p = FFN_PREFIX
    n_pay = len(payload)

    def body(*refs):
        ins, outs, scr, pay_refs, recv_refs, sems = _carried_refs(refs, 10, 6, 2, n_pay, aliased=True)
        dx2_ref, x_ref, mod_ref, g_ref, u2_ref, u3_ref, o2_ref, up_ref, fcw_ref, down_ref = ins
        dx1_ref, du2_ref, do2_ref, dmod_ref, gd_ref, dfcw_ref = outs
        fext_ref, fcar_ref = scr
        i = pl.program_id(0)
        j = nt - 1 - i
        last_in_seq = (j % tps) == tps - 1

        if n_pay:
            @pl.when(i == 0)
            def _():
                _exchange(pay_refs, recv_refs, rows, pay_layer, *sems, start=True)

        @pl.when(i == 0)
        def _():
            gd_ref[...] = jnp.zeros_like(gd_ref)
            dfcw_ref[...] = jnp.zeros_like(dfcw_ref)
            for n in range(n_chunks):
                fext_ref[n, 0:8, :] = jnp.zeros((8, fc), F32)

        @pl.when(last_in_seq)
        def _():
            dmod_ref[...] = jnp.zeros_like(dmod_ref)
            fcar_ref[...] = jnp.zeros_like(fcar_ref)

        xv = x_ref[...]
        dx2v = dx2_ref[...]
        pre_g, post_g = g_ref[0:1, :], g_ref[1:2, :]

        ov = o2_ref[...]
        ro = lax.rsqrt(_mean(ov * ov) + EPS)
        yo = ov * ro
        dmod_ref[2:3, :] += _colsum(dx2v * (yo * post_g))
        dn = dx2v * (1.0 + mod_ref[5:6, :])
        gd_ref[1:2, :] += _colsum(dn * yo)
        dyo = dn * post_g
        do = ro * (dyo - yo * _mean(dyo * yo))
        dob = do.astype(act_dtype)
        do2_ref[...] = dob

        for n in range(half):
            dhid = _mm_nt(dob, down_ref[n])
            fext_ref[n, 8:8 + tm, :] = dhid * u3_ref[n].astype(F32)
            fext_ref[n + half, 8:8 + tm, :] = dhid * u3_ref[n + half].astype(F32)

        last_sublane = _sublane_is(p, 7)
        dh = jnp.zeros((tm, d), F32)
        for n in range(n_chunks):
            fext_ref[n, 8 + tm:8 + tm + p, :] = jnp.where(last_sublane, fcar_ref[n], fext_ref[n, 9:9 + p, :])
            fcar_ref[n] = fext_ref[n, 1:1 + p, :]
            d2 = fext_ref[n, 8:8 + tm, :]
            d1 = fext_ref[n, 16:16 + tm, :]
            d0 = fext_ref[n, 24:24 + tm, :]
            u2v = u2_ref[n].astype(F32)
            dfcw_ref[n, 3:4, :] += _colsum(d2)
            dfcw_ref[n, 0:1, :] += _colsum(d0 * u2v)
            dfcw_ref[n, 1:2, :] += _colsum(d1 * u2v)
            dfcw_ref[n, 2:3, :] += _colsum(d2 * u2v)
            du2 = (fcw_ref[n, 0:1, :] * d0 + fcw_ref[n, 1:2, :] * d1 + fcw_ref[n, 2:3, :] * d2).astype(act_dtype)
            du2_ref[n] = du2
            dh = dh + _mm_nt(du2, up_ref[n])

        r = lax.rsqrt(_mean(xv * xv) + EPS)
        xn = xv * r
        dmod_ref[0:1, :] += _colsum(dh)
        dmod_ref[1:2, :] += _colsum(dh * (xn * pre_g))
        dy = dh * (1.0 + mod_ref[4:5, :])
        gd_ref[0:1, :] += _colsum(dy * xn)
        dxn = dy * pre_g
        dx1_ref[...] = dx2v + r * (dxn - xn * _mean(dxn * xn))

        if n_pay:
            @pl.when(i == nt - 1)
            def _():
                _exchange(pay_refs, recv_refs, rows, pay_layer, *sems, start=False)

    rev = lambda width: pl.BlockSpec((tm, width), lambda i: (nt - 1 - i, 0))
    chunked = pl.BlockSpec((n_chunks, tm, fc), lambda i: (0, nt - 1 - i, 0))
    return _carrier_call(
        body, f"ffn_bwd_{l}", (nt,),
        out_shape=[jax.ShapeDtypeStruct((t, d), F32), jax.ShapeDtypeStruct((n_chunks, t, fc), act_dtype),
                   jax.ShapeDtypeStruct((t, d), act_dtype),
                   jax.ShapeDtypeStruct((t // seq, 8, d), F32), jax.ShapeDtypeStruct((8, d), F32),
                   jax.ShapeDtypeStruct((n_chunks, 8, fc), F32)],
        in_specs=[rev(d), rev(d),
                  pl.BlockSpec((None, None, 8, d), lambda i: (l, (nt - 1 - i) // tps, 0, 0)),
                  _const_spec((None, 8, d), (l, 0, 0)),
                  chunked, chunked, rev(d),
                  _const_spec((n_chunks, d, fc), (0, 0, 0)),
                  _const_spec((n_chunks, 8, fc), (0, 0, 0)),
                  _const_spec((half, fc, d), (0, 0, 0))],
        out_specs=[rev(d), chunked, rev(d),
                   pl.BlockSpec((None, 8, d), lambda i: ((nt - 1 - i) // tps, 0, 0)),
                   _acc_spec((8, d), (0, 0)), _acc_spec((n_chunks, 8, fc), (0, 0, 0))],
        scratch=[pltpu.VMEM((n_chunks, 8 + tm + p, fc), F32), pltpu.VMEM((n_chunks, p, fc), F32)],
        operands=(dx2, x, mod, gains, u2, u3, o2, up, fcw, down),
        payload=payload, recvs=recvs)


def _pad_rows(a, rows):
    pad = [(0, 0)] * a.ndim
    pad[-2] = (0, rows - a.shape[-2])
    return jnp.pad(a, pad)


def kernel(x, c, ada_w, ada_b, pre_mix_g, post_mix_g, w_in, conv_w, conv_b, conv_ln_g, conv_ln_b, pool_w, pool_scale, w_out, pre_ffn_g, post_ffn_g, ffn_up, ffn_conv_w, ffn_conv_b, ffn_down, loss_target, m_ada_w, m_ada_b, m_pre_mix_g, m_post_mix_g, m_w_in, m_conv_w, m_conv_b, m_conv_ln_g, m_conv_ln_b, m_pool_w, m_pool_scale, m_w_out, m_pre_ffn_g, m_post_ffn_g, m_ffn_up, m_ffn_conv_w, m_ffn_conv_b, m_ffn_down, v_ada_w, v_ada_b, v_pre_mix_g, v_post_mix_g, v_w_in, v_conv_w, v_conv_b, v_conv_ln_g, v_conv_ln_b, v_pool_w, v_pool_scale, v_w_out, v_pre_ffn_g, v_post_ffn_g, v_ffn_up, v_ffn_conv_w, v_ffn_conv_b, v_ffn_down):
    bl, seq, d = x.shape
    n_layers = ada_w.shape[0]
    t = bl * seq
    dc = conv_b.shape[1]
    d_in = w_in.shape[2] * N_DEV
    d_mix = w_out.shape[1] * N_DEV
    n_taps = conv_w.shape[1]
    fc = ffn_up.shape[2]
    half = N_DEV // 2
    ada_cols = ada_w.shape[2]
    n_mod = ada_cols * N_DEV // d
    assert pool_scale.shape[1] == dc and n_taps == 31 and n_mod == 6 and ffn_conv_w.shape[1] == 3
    assert pool_w.shape[1:] == (len(POOL_WINDOWS), POOL_GROUP, POOL_GROUP)
    tm = TILE_TOKENS
    assert seq % tm == 0 and CONV_PREFIX <= tm - 8
    tk = 2048 if t % 2048 == 0 else tm
    act = MXU_DTYPE

    def tile_order(a, inverse=False):
        shape = (t // tm, tm // 8, 8, d) if inverse else (t // tm, 8, tm // 8, d)
        return a.reshape(shape).transpose(0, 2, 1, 3).reshape(t, d)

    ax = lax.axis_index
    me = 4 * ax("x") + 2 * ax("y") + ax("c")

    (c_all,) = _all_gather([c], "gather_c")
    c_all = c_all.reshape(N_DEV * bl, d)
    ada_b_cols = lax.dynamic_slice_in_dim(ada_b, me * ada_cols, ada_cols, axis=1)
    mod_cols = _ada_fwd(c_all, ada_w, ada_b_cols)

    w_in_s = w_in.astype(act).transpose(0, 2, 1)
    w_out_s, up_s, down_s = w_out.astype(act), ffn_up.astype(act), ffn_down.astype(act)
    conv_w_s = _pad_rows(conv_w, 32)
    fcw_s = _pad_rows(jnp.concatenate(
        [ffn_conv_w, lax.dynamic_slice_in_dim(ffn_conv_b, me * fc, fc, axis=1)[:, None, :]], axis=1), 8)

    mixer_shards = (w_in_s, w_out_s, conv_w_s)
    late_shards = (down_s, fcw_s)

    def mixer_weights(g_w_in, g_w_out, g_conv_w):
        return (g_w_in.reshape(d_in, d), g_conv_w.transpose(1, 0, 2).reshape(32, dc), g_w_out.reshape(d_mix, d))

    def late_weights(g_down, g_fcw):
        return (g_fcw, g_down.reshape(half, fc, d))

    g0 = _all_gather([mod_cols, *mixer_shards], "gather_w0", layers=[None, 0, 0, 0])
    mod_all = g0[0].transpose(1, 2, 0, 3).reshape(n_layers, N_DEV * bl, n_mod, d)
    mod = _pad_rows(lax.dynamic_slice_in_dim(mod_all, me * bl, bl, axis=1), 8)
    mixers, lates, ups = [mixer_weights(*g0[1:])], [], []
    gains_mix = _pad_rows(jnp.stack([pre_mix_g, post_mix_g], axis=1), 8)
    gains_ffn = _pad_rows(jnp.stack([pre_ffn_g, post_ffn_g], axis=1), 8)
    vec = _pad_rows(jnp.stack([conv_b, conv_ln_g, conv_ln_b, pool_scale], axis=1), 8)
    pool_w_b = pool_w.astype(act)

    kw = dict(seq=seq, tm=tm, act_dtype=act)
    xs = tile_order(x.reshape(t, d))
    saved = []
    for l in range(n_layers):
        w_in_t, cw, w_o = mixers[l]
        (x1, h1, u, a1, ap, o, tails), got = _mixer_fwd(
            l, xs, mod, gains_mix, w_in_t, cw, vec, pool_w_b, w_o, **kw,
            payload=(up_s, *late_shards) if l == 0 else (up_s,), gather_layer=l)
        ups.append(got[0])
        if l == 0:
            lates.append(late_weights(*got[1:]))
        up_l, (fcw_l, down_l) = ups[l], lates[l]
        (x2, h2, u2, u3, hid, o2), got = _ffn_fwd(l, x1, mod, gains_ffn, up_l, fcw_l, down_l, **kw,
                                                  payload=(*mixer_shards, *late_shards) if l + 1 < n_layers else (),
                                                  gather_layer=l + 1)
        if got:
            mixers.append(mixer_weights(*got[:3]))
            lates.append(late_weights(*got[3:]))
        saved.append((xs, h1, u, tails, a1, ap, o, x1, h2, u2, u3, hid, o2))
        xs = x2

    dx, loss_part = _loss_grad(xs, tile_order(loss_target.reshape(t, d)), tm)

    def landing(shard, dtype):
        return lax.empty((N_DEV, n_layers) + shard.shape[1:], dtype)

    r_up, r_down = landing(ffn_up, act), landing(ffn_down, act)
    r_w_in, r_w_out = landing(w_in, act), landing(w_out, act)
    r_conv_w, r_fcw = landing(conv_w, F32), landing(ffn_conv_w, F32)
    def cuts(l):
        first, second = (7, 10) if l == 0 else (3, 7.5)
        return int(first * d / 10) // 16 * 16, int(second * d / 10) // 16 * 16

    def late_rows(l):
        cut1, cut2 = cuts(l)
        return ((cut1, cut2 - cut1), None, None, None, None)

    dmods, smalls = [], []
    pending = ()
    for l in reversed(range(n_layers)):
        xin, h1, u, tails, a1, ap, o, x1, h2, u2, u3, hid, o2 = saved[l]
        (w_in_t, cw, w_o), (fcw_l, down_l), up_l = mixers[l], lates[l], ups[l]
        (dx1, du2, do2, dmod_b, gd_b, dfcw), got = _ffn_bwd(
            l, dx, x1, mod, gains_ffn, u2, u3, o2, up_l, fcw_l, down_l, **kw,
            payload=pending, recvs=(r_up, r_w_in, r_w_out, r_conv_w, r_fcw) if pending else (),
            rows=late_rows(l + 1), pay_layer=l + 1)
        if pending:
            r_up, r_w_in, r_w_out, r_conv_w, r_fcw = got
            cut2 = cuts(l + 1)[1]
            p_up, (r_up,) = _matmul_tn(f"dw_up_{l}", h2[None], du2, tk, act, payload=pending[:1], recvs=(r_up,),
                                       rows=((cut2, d - cut2),), pay_layer=l + 1)
        else:
            p_up, _ = _matmul_tn(f"dw_up_{l}", h2[None], du2, tk, act)
        p_down = _matmul_tn(f"dw_down_{l}", hid, do2[None], tk, act)[0].reshape(N_DEV, fc // 2, d)
        (dx, du, do, dmod_a, gd_a, gv, dcw, dpw), (r_down, r_up) = _mixer_bwd(
            l, dx1, xin, mod, gains_mix, u, tails, a1, o, w_in_t, cw, vec, pool_w_b, w_o, **kw,
            payload=(p_down, p_up), recvs=(r_down, r_up), rows=(None, (0, cuts(l)[0])), pay_layer=l)
        dw_in = _matmul_tn(f"dw_in_{l}", h1[None], du[None], tk, act)[0][0]
        p_w_in = dw_in.reshape(d, N_DEV, d_in // N_DEV).transpose(1, 0, 2)
        p_w_out = _matmul_tn(f"dw_out_{l}", ap[None], do[None], tk, act)[0][0].reshape(N_DEV, d_mix // N_DEV, d)
        p_conv_w = dcw[:n_taps].reshape(n_taps, N_DEV, dc // N_DEV).transpose(1, 0, 2)
        pending = (p_up, p_w_in, p_w_out, p_conv_w, dfcw[:, :3, :])
        dmods.append(jnp.concatenate([dmod_a[:, 0:3], dmod_b[:, 0:3]], axis=1).reshape(bl, n_mod * d))
        smalls.append(jnp.concatenate(
            [gd_a[0], gd_a[1], gv[0], gv[1], gv[2], gv[3], gd_b[0], gd_b[1], dfcw[:, 3, :].reshape(-1),
             dpw.reshape(-1)]))
    dmods.reverse()
    smalls.reverse()

    dmod_loc = jnp.stack(dmods)
    small_loc = jnp.stack(smalls)
    n_small = small_loc.shape[1]
    small_cols = 8 * LANES if (n_layers * n_small) % (8 * LANES) == 0 else LANES
    (r_up, r_w_in, r_w_out, r_conv_w, r_fcw), (g_dmod, g_small, g_loss) = _exchange_and_gather(
        pending, (r_up, r_w_in, r_w_out, r_conv_w, r_fcw), late_rows(0), 0,
        [dmod_loc.astype(act), small_loc.reshape(-1, small_cols).astype(act), loss_part], "exchange_tail")
    g_dmod = g_dmod.astype(F32)

    loss = _sum_parts(g_loss)[0, 0]

    def flat2(a):
        return a.reshape(-1, a.shape[-1])

    def update(name, parts, w, m, v):
        outs = _adam_reduce(name, parts.reshape(parts.shape[0], -1, w.shape[-1]), flat2(w), flat2(m), flat2(v))
        return [o_.reshape(w.shape) for o_ in outs]

    res = {}
    res["w_in"] = update("adam_w_in", r_w_in, w_in, m_w_in, v_w_in)
    res["w_out"] = update("adam_w_out", r_w_out, w_out, m_w_out, v_w_out)
    res["ffn_up"] = update("adam_ffn_up", r_up, ffn_up, m_ffn_up, v_ffn_up)
    res["ffn_down"] = update("adam_ffn_down", r_down, ffn_down, m_ffn_down, v_ffn_down)
    res["conv_w"] = update("adam_conv_w", r_conv_w, conv_w, m_conv_w, v_conv_w)
    res["ffn_conv_w"] = update("adam_ffn_conv_w", r_fcw, ffn_conv_w, m_ffn_conv_w, v_ffn_conv_w)

    dmod_all = g_dmod.transpose(1, 0, 2, 3).reshape(n_layers, N_DEV * bl, n_mod * d)
    dmod_cols = lax.dynamic_slice_in_dim(dmod_all, me * ada_cols, ada_cols, axis=2)
    res["ada_w"] = list(_ada_bwd(c_all.T, dmod_cols, ada_w, m_ada_w, v_ada_w))
    res["ada_b"] = update("adam_ada_b", dmod_all.transpose(1, 0, 2), ada_b, m_ada_b, v_ada_b)

    small_names = ["pre_mix_g", "post_mix_g", "conv_b", "conv_ln_g", "conv_ln_b", "pool_scale", "pre_ffn_g",
                   "post_ffn_g", "ffn_conv_b", "pool_w"]
    small_w = [pre_mix_g, post_mix_g, conv_b, conv_ln_g, conv_ln_b, pool_scale, pre_ffn_g, post_ffn_g,
               ffn_conv_b, pool_w]
    small_m = [m_pre_mix_g, m_post_mix_g, m_conv_b, m_conv_ln_g, m_conv_ln_b, m_pool_scale, m_pre_ffn_g,
               m_post_ffn_g, m_ffn_conv_b, m_pool_w]
    small_v = [v_pre_mix_g, v_post_mix_g, v_conv_b, v_conv_ln_g, v_conv_ln_b, v_pool_scale, v_pre_ffn_g,
               v_post_ffn_g, v_ffn_conv_b, v_pool_w]

    def pack(arrs):
        return jnp.concatenate([a.reshape(n_layers, -1) for a in arrs], axis=1).reshape(-1, small_cols)

    outs = _adam_reduce("adam_small", g_small, pack(small_w), pack(small_m), pack(small_v))
    outs = [o_.reshape(n_layers, n_small) for o_ in outs]
    off = 0
    for name, w in zip(small_names, small_w):
        size = w[0].size
        res[name] = [o_[:, off:off + size].reshape(w.shape) for o_ in outs]
        off += size

    order = ["ada_w", "ada_b", "pre_mix_g", "post_mix_g", "w_in", "conv_w", "conv_b", "conv_ln_g", "conv_ln_b",
             "pool_w", "pool_scale", "w_out", "pre_ffn_g", "post_ffn_g", "ffn_up", "ffn_conv_w", "ffn_conv_b",
             "ffn_down"]
    grad_x = tile_order(dx, inverse=True).reshape(bl, seq, d)
    return (loss, grad_x, *[res[n][0] for n in order], *[res[n][1] for n in order],
            *[res[n][2] for n in order], *[res[n][3] for n in order])
```

```python
import jax
import jax.numpy as jnp
from jax import lax
from jax.experimental import pallas as pl
from jax.experimental.pallas import tpu as pltpu

N_DEV = 8
EPS = 1e-6
POOL_WINDOWS = (2, 4, 8, 16)
POOL_GROUP = 128
TILE_TOKENS = 256
CONV_PREFIX = 8 * 30
FFN_PREFIX = 8 * 2
LANES = 128
ROW_CHUNK = 128
MXU_DTYPE = jnp.bfloat16
VMEM_LIMIT = 60 * 1024 * 1024

ADAM_LR = 0.001
ADAM_B1 = 0.9
ADAM_B2 = 0.999
ADAM_EPS = 1e-08
ADAM_WD = 0.01
ADAM_STEP = 10

MESH = pl.DeviceIdType.MESH
F32 = jnp.float32


def _mm(a, b):
    return jnp.dot(a.astype(MXU_DTYPE), b.astype(MXU_DTYPE), preferred_element_type=F32)


def _mm_nt(a, b):
    return lax.dot_general(a.astype(MXU_DTYPE), b.astype(MXU_DTYPE), (((1,), (1,)), ((), ())),
                           preferred_element_type=F32)


def _mm_tn(a, b):
    return lax.dot_general(a.astype(MXU_DTYPE), b.astype(MXU_DTYPE), (((0,), (0,)), ((), ())),
                           preferred_element_type=F32)


def _mean(v):
    return jnp.mean(v, axis=-1, keepdims=True)


def _colsum(v):
    return jnp.sum(v, axis=0, keepdims=True)


def _sigmoid(v):
    return jax.nn.sigmoid(v)


def _params(n_grid=1):
    return pltpu.CompilerParams(dimension_semantics=("arbitrary",) * n_grid, vmem_limit_bytes=VMEM_LIMIT)


def _const_spec(shape, index):
    return pl.BlockSpec(shape, lambda *_: index, pipeline_mode=pl.Buffered(1))


def _acc_spec(shape, index):
    return pl.BlockSpec(shape, lambda *_: index)


def _position():
    x, y, c = lax.axis_index("x"), lax.axis_index("y"), lax.axis_index("c")
    return x, y, c


def _gather_phase(ins, outs, send_sems, recv_sems, local_sems, phase):
    n = len(ins)
    x, y, c = _position()
    me, sibling = (x, y, c), (x, y, 1 - c)
    chips = [(1 - x, y), (x, 1 - y), (1 - x, 1 - y)]

    def slot(k, px, py, pc):
        return outs[k].at[4 * px + 2 * py + pc]

    def copy(k, s, block, to, src=None):
        return pltpu.make_async_remote_copy(
            src_ref=slot(k, *block) if src is None else src, dst_ref=slot(k, *block),
            send_sem=send_sems.at[k, s], recv_sem=recv_sems.at[k, s], device_id=to, device_id_type=MESH)

    mine = [pltpu.make_async_copy(ins[k], slot(k, *me), local_sems.at[k]) for k in range(n)]
    first = []
    for k in range(n):
        first.append(copy(k, 0, me, sibling, src=ins[k]))
        first += [copy(k, 1 + j, me, (*chip, c), src=ins[k]) for j, chip in enumerate(chips)]
    passed = [copy(k, 4 + j, (*chip, c), sibling) for j, chip in enumerate(chips) for k in range(n)]
    if phase == 0:
        for cp in mine + first:
            cp.start()
    elif phase == 1:
        for j, chip in enumerate(chips):
            for k in range(n):
                copy(k, 1 + j, (*chip, c), me).wait_recv()
                copy(k, 4 + j, (*chip, c), sibling).start()
    else:
        for k in range(n):
            copy(k, 0, sibling, me).wait_recv()
            for j, chip in enumerate(chips):
                copy(k, 4 + j, (*chip, 1 - c), me).wait_recv()
        for cp in first + passed:
            cp.wait_send()
        for cp in mine:
            cp.wait()


def _gather_scratch(n):
    return [pltpu.SemaphoreType.DMA((n, 7)), pltpu.SemaphoreType.DMA((n, 7)), pltpu.SemaphoreType.DMA((n,))]


def _layer_of(refs, layers):
    return [r if lay is None else r.at[lay] for r, lay in zip(refs, layers)]


def _gathered_shapes(arrs, layers):
    return [jax.ShapeDtypeStruct((N_DEV,) + (a.shape if lay is None else a.shape[1:]), a.dtype)
            for a, lay in zip(arrs, layers)]


def _all_gather(arrs, name, layers=None):
    n = len(arrs)
    layers = [None] * n if layers is None else layers

    def body(*refs):
        for phase in range(3):
            _gather_phase(_layer_of(refs[:n], layers), refs[n:2 * n], *refs[2 * n:], phase)

    any_spec = pl.BlockSpec(memory_space=pl.ANY)
    return pl.pallas_call(
        body, name=name,
        out_shape=_gathered_shapes(arrs, layers),
        in_specs=[any_spec] * n, out_specs=[any_spec] * n,
        scratch_shapes=_gather_scratch(n),
    )(*arrs)


def _exchange(payload, recvs, rows, layer, send_sems, recv_sems, local_sems, start):
    x, y, c = _position()
    me = 4 * x + 2 * y + c
    for k, (src, recv) in enumerate(zip(payload, recvs)):
        def block(ref, *index):
            return ref.at[index] if rows[k] is None else ref.at[(*index, pl.ds(*rows[k]))]

        local = pltpu.make_async_copy(block(src, me), block(recv, me, layer), local_sems.at[k])
        if start:
            local.start()
        else:
            local.wait()
        for j in range(1, N_DEV):
            px = (1 - x) if (j & 4) else x
            py = (1 - y) if (j & 2) else y
            pc = (1 - c) if (j & 1) else c
            peer = 4 * px + 2 * py + pc
            landing = block(recv, me, layer) if start else block(recv, peer, layer)
            cp = pltpu.make_async_remote_copy(
                src_ref=block(src, peer), dst_ref=landing, send_sem=send_sems.at[k, j - 1],
                recv_sem=recv_sems.at[k, j - 1], device_id=(px, py, pc), device_id_type=MESH)
            if start:
                cp.start()
            else:
                cp.wait()


def _exchange_scratch(n):
    return [pltpu.SemaphoreType.DMA((n, N_DEV - 1)), pltpu.SemaphoreType.DMA((n, N_DEV - 1)),
            pltpu.SemaphoreType.DMA((n,))]


def _exchange_and_gather(payload, recvs, rows, layer, arrs, name):
    n, m = len(payload), len(arrs)

    def body(*refs):
        pay, srcs = refs[:n], refs[2 * n:2 * n + m]
        outs, got = refs[2 * n + m:3 * n + m], refs[3 * n + m:3 * n + 2 * m]
        xsems, gsems = refs[3 * n + 2 * m:3 * n + 2 * m + 3], refs[3 * n + 2 * m + 3:]
        _exchange(pay, outs, rows, layer, *xsems, start=True)
        for phase in range(3):
            _gather_phase(srcs, got, *gsems, phase)
        _exchange(pay, outs, rows, layer, *xsems, start=False)

    any_spec = pl.BlockSpec(memory_space=pl.ANY)
    res = pl.pallas_call(
        body, name=name,
        out_shape=[jax.ShapeDtypeStruct(r.shape, r.dtype) for r in recvs]
        + [jax.ShapeDtypeStruct((N_DEV,) + a.shape, a.dtype) for a in arrs],
        in_specs=[any_spec] * (2 * n + m), out_specs=[any_spec] * (n + m),
        input_output_aliases={n + k: k for k in range(n)},
        scratch_shapes=_exchange_scratch(n) + _gather_scratch(m),
    )(*payload, *recvs, *arrs)
    return res[:n], res[n:]


def _carried_refs(refs, n_in, n_out, n_scratch, n_pay, aliased):
    ins = refs[:n_in]
    pay = refs[n_in:n_in + n_pay]
    o0 = n_in + (2 * n_pay if aliased else n_pay)
    outs = refs[o0:o0 + n_out]
    recvs = refs[o0 + n_out:o0 + n_out + n_pay]
    s0 = o0 + n_out + n_pay
    return ins, outs, refs[s0:s0 + n_scratch], pay, recvs, refs[s0 + n_scratch:]


def _carrier_call(body, name, grid, in_specs, out_specs, out_shape, scratch, operands, payload, recvs=None,
                  gather_layer=None):
    n_in, n_out, n_pay = len(in_specs), len(out_specs), len(payload)
    any_spec = pl.BlockSpec(memory_space=pl.ANY)
    if recvs is None:
        landing = _gathered_shapes(payload, [gather_layer] * n_pay)
        extra_in, aliases = list(payload), {}
        sems = _gather_scratch(n_pay) if n_pay else []
    else:
        landing = [jax.ShapeDtypeStruct(r.shape, r.dtype) for r in recvs]
        extra_in = list(payload) + list(recvs)
        aliases = {n_in + n_pay + k: n_out + k for k in range(n_pay)}
        sems = _exchange_scratch(n_pay) if n_pay else []
    res = pl.pallas_call(
        body, name=name, grid=grid,
        out_shape=list(out_shape) + landing,
        in_specs=list(in_specs) + [any_spec] * len(extra_in),
        out_specs=list(out_specs) + [any_spec] * n_pay,
        input_output_aliases=aliases,
        scratch_shapes=list(scratch) + sems,
        compiler_params=_params(len(grid)),
    )(*operands, *extra_in)
    return res[:n_out], res[n_out:]


def _adamw(w, g, m, v):
    m = ADAM_B1 * m + (1.0 - ADAM_B1) * g
    v = ADAM_B2 * v + (1.0 - ADAM_B2) * jnp.square(g)
    m_hat = m / (1.0 - ADAM_B1 ** ADAM_STEP)
    v_hat = v / (1.0 - ADAM_B2 ** ADAM_STEP)
    delta = -ADAM_LR * (m_hat / (jnp.sqrt(v_hat) + ADAM_EPS) + ADAM_WD * w)
    return delta, m, v


def _ada_fwd(c_all, ada_w, ada_b_cols):
    n_layers, d, cols = ada_w.shape
    b = c_all.shape[0]

    def body(c_ref, w_ref, b_ref, o_ref):
        cv = c_ref[...]
        act = cv * _sigmoid(cv)
        o_ref[...] = jnp.dot(act, w_ref[...], preferred_element_type=F32,
                             precision=lax.Precision.HIGHEST) + b_ref[...]

    return pl.pallas_call(
        body, name="ada_fwd", grid=(n_layers,),
        out_shape=jax.ShapeDtypeStruct((n_layers, b, cols), F32),
        in_specs=[pl.BlockSpec((b, d), lambda l: (0, 0)),
                  pl.BlockSpec((None, d, cols), lambda l: (l, 0, 0)),
                  pl.BlockSpec((None, 1, cols), lambda l: (l, 0, 0))],
        out_specs=pl.BlockSpec((None, b, cols), lambda l: (l, 0, 0)),
        compiler_params=_params(1),
    )(c_all, ada_w, ada_b_cols.reshape(n_layers, 1, cols))


def _ada_bwd(c_all_t, dmod_cols, w, m, v):
    n_layers, d, cols = w.shape
    b = c_all_t.shape[1]
    td = 256 if d % 256 == 0 else d

    def body(c_ref, dm_ref, w_ref, m_ref, v_ref, g_ref, dl_ref, nm_ref, nv_ref):
        cv = c_ref[...]
        act = cv * _sigmoid(cv)
        g = jnp.dot(act, dm_ref[...], preferred_element_type=F32, precision=lax.Precision.HIGHEST)
        delta, nm, nv = _adamw(w_ref[...], g, m_ref[...], v_ref[...])
        g_ref[...] = g
        dl_ref[...] = delta
        nm_ref[...] = nm
        nv_ref[...] = nv

    blk = pl.BlockSpec((None, td, cols), lambda l, i: (l, i, 0))
    shp = jax.ShapeDtypeStruct(w.shape, F32)
    return pl.pallas_call(
        body, name="ada_bwd", grid=(n_layers, d // td),
        out_shape=[shp] * 4,
        in_specs=[pl.BlockSpec((td, b), lambda l, i: (i, 0)),
                  pl.BlockSpec((None, b, cols), lambda l, i: (l, 0, 0)), blk, blk, blk],
        out_specs=[blk] * 4,
        compiler_params=_params(2),
    )(c_all_t, dmod_cols, w, m, v)


def _row_tile(rows, cols, budget=128 * 1024, step=8):
    best = None
    for t in range(step, rows + 1, step):
        if rows % t == 0 and t * cols <= budget:
            best = t
    return best if best is not None else rows


def _adam_reduce(name, parts, w, m, v):
    p, rows, cols = parts.shape
    tr = _row_tile(rows, cols, budget=(256 * 1024) // max(1, p // 4), step=8 if parts.dtype == F32 else 16)

    def body(p_ref, w_ref, m_ref, v_ref, g_ref, dl_ref, nm_ref, nv_ref):
        g = p_ref[0].astype(F32)
        for k in range(1, p):
            g = g + p_ref[k].astype(F32)
        delta, nm, nv = _adamw(w_ref[...], g, m_ref[...], v_ref[...])
        g_ref[...] = g
        dl_ref[...] = delta
        nm_ref[...] = nm
        nv_ref[...] = nv

    blk = pl.BlockSpec((tr, cols), lambda i: (i, 0))
    shp = jax.ShapeDtypeStruct((rows, cols), F32)
    return pl.pallas_call(
        body, name=name, grid=(rows // tr,),
        out_shape=[shp] * 4,
        in_specs=[pl.BlockSpec((p, tr, cols), lambda i: (0, i, 0)), blk, blk, blk],
        out_specs=[blk] * 4,
        compiler_params=_params(1),
    )(parts, w, m, v)


def _sum_parts(parts):
    p = parts.shape[0]

    def body(p_ref, o_ref):
        acc = p_ref[0]
        for k in range(1, p):
            acc = acc + p_ref[k]
        o_ref[...] = acc

    return pl.pallas_call(body, name="loss_sum", out_shape=jax.ShapeDtypeStruct(parts.shape[1:], F32))(parts)


def _loss_grad(y, target, tm):
    t, d = y.shape

    def body(y_ref, t_ref, dy_ref, loss_ref):
        @pl.when(pl.program_id(0) == 0)
        def _():
            loss_ref[...] = jnp.zeros_like(loss_ref)

        diff = y_ref[...] - t_ref[...]
        dy_ref[...] = diff / d
        part = 0.5 * jnp.sum(_mean(diff * diff), axis=0, keepdims=True)
        loss_ref[...] += jnp.broadcast_to(part, loss_ref.shape)

    blk = pl.BlockSpec((tm, d), lambda i: (i, 0))
    return pl.pallas_call(
        body, name="loss_grad", grid=(t // tm,),
        out_shape=[jax.ShapeDtypeStruct((t, d), F32), jax.ShapeDtypeStruct((8, LANES), F32)],
        in_specs=[blk, blk], out_specs=[blk, pl.BlockSpec((8, LANES), lambda i: (0, 0))],
        compiler_params=_params(1),
    )(y, target)


def _matmul_tn(name, a, b, tk, out_dtype, payload=(), recvs=(), rows=(), pay_layer=0):
    ga, t, m = a.shape
    gb, _, n = b.shape
    g = max(ga, gb)
    n_k = t // tk
    n_pay = len(payload)

    def body(*refs):
        (a_ref, b_ref), (o_ref,), (acc_ref,), pay_refs, recv_refs, sems = _carried_refs(
            refs, 2, 1, 1, n_pay, aliased=True)
        gi, k = pl.program_id(0), pl.program_id(1)

        if n_pay:
            @pl.when(jnp.logical_and(gi == 0, k == 0))
            def _():
                _exchange(pay_refs, recv_refs, rows, pay_layer, *sems, start=True)

        @pl.when(k == 0)
        def _():
            acc_ref[...] = jnp.zeros_like(acc_ref)

        acc_ref[...] += _mm_tn(a_ref[...], b_ref[...])

        @pl.when(k == n_k - 1)
        def _():
            o_ref[...] = acc_ref[...].astype(out_dtype)

        if n_pay:
            @pl.when(jnp.logical_and(gi == g - 1, k == n_k - 1))
            def _():
                _exchange(pay_refs, recv_refs, rows, pay_layer, *sems, start=False)

    (out,), got = _carrier_call(
        body, name, (g, n_k),
        out_shape=[jax.ShapeDtypeStruct((g, m, n), out_dtype)],
        in_specs=[pl.BlockSpec((None, tk, m), (lambda gi, k: (gi, k, 0)) if ga > 1 else (lambda gi, k: (0, k, 0))),
                  pl.BlockSpec((None, tk, n), (lambda gi, k: (gi, k, 0)) if gb > 1 else (lambda gi, k: (0, k, 0)))],
        out_specs=[pl.BlockSpec((None, m, n), lambda gi, k: (gi, 0, 0))],
        scratch=[pltpu.VMEM((m, n), F32)],
        operands=(a, b), payload=payload, recvs=recvs)
    return out, got


def _time_of_row(tm):
    i = lax.broadcasted_iota(jnp.int32, (tm, 1), 0)
    return (i % 8) * (tm // 8) + i // 8


def _sublane_is(rows, s):
    return lax.broadcasted_iota(jnp.int32, (rows, 1), 0) % 8 == s


def _conv_taps(src_ref, col0, ncols, tm, tap_rows, w_ref, init_of, store):
    rc = min(ROW_CHUNK, tm)
    for cb in range(ncols // LANES):
        cs = slice(cb * LANES, (cb + 1) * LANES)
        ss = slice(col0 + cb * LANES, col0 + (cb + 1) * LANES)
        for r0 in range(0, tm, rc):
            acc = init_of(cs, rc)
            for k, row in enumerate(tap_rows):
                acc = acc + w_ref[k:k + 1, cs] * src_ref[r0 + row:r0 + row + rc, ss]
            store(r0, rc, cs, acc)


def _mixer_fwd(l, x, mod, gains, w_in_t, conv_w, vec, pool_w, w_out, *, seq, tm, act_dtype, payload=(),
               gather_layer=None):
    t, d = x.shape
    d_in = w_in_t.shape[0]
    dc = conv_w.shape[-1]
    d_mix = w_out.shape[0]
    n_taps = 31
    nt, tps = t // tm, seq // tm
    p = CONV_PREFIX
    n_pay = len(payload)

    def body(*refs):
        ins, outs, scr, pay_refs, got_refs, sems = _carried_refs(refs, 8, 7, 2, n_pay, aliased=False)
        x_ref, mod_ref, g_ref, win_ref, cw_ref, v_ref, pw_ref, wout_ref = ins
        x1_ref, h1_ref, u_ref, a1_ref, ap_ref, o_ref, tail_ref = outs
        ext_ref, car_ref = scr
        i = pl.program_id(0)
        first = (i % tps) == 0

        if n_pay:
            @pl.when(i == 0)
            def _():
                _gather_phase(_layer_of(pay_refs, [gather_layer] * n_pay), got_refs, *sems, 0)

        xv = x_ref[...]
        r = lax.rsqrt(_mean(xv * xv) + EPS)
        hv = (xv * r) * g_ref[0:1, :] * (1.0 + mod_ref[1:2, :]) + mod_ref[0:1, :]
        hb = hv.astype(act_dtype)
        h1_ref[...] = hb
        u = _mm_nt(hb, win_ref[...])
        u_ref[...] = u
        a0 = u[:, :dc] * _sigmoid(u[:, dc:2 * dc])

        @pl.when(first)
        def _():
            car_ref[...] = jnp.zeros_like(car_ref)

        @pl.when(i == 0)
        def _():
            ext_ref[p + tm:p + tm + 8, :] = jnp.zeros((8, ext_ref.shape[1]), F32)

        ext_ref[p:p + tm, 0:dc] = a0
        ext_ref[p:p + tm, dc:] = u[:, 2 * dc:]
        ext_ref[0:p, :] = jnp.where(_sublane_is(p, 0), car_ref[...], ext_ref[tm - 1:tm - 1 + p, :])

        def store(r0, rc, cs, acc):
            a1_ref[r0:r0 + rc, cs] = acc

        _conv_taps(ext_ref, 0, dc, tm, [p - 8 * (n_taps - 1 - k) for k in range(n_taps)],
                   cw_ref,
                   lambda cs, rc: jnp.broadcast_to(v_ref[0:1, cs], (rc, LANES)), store)
        a1 = a1_ref[...]
        mu = _mean(a1)
        xc = a1 - mu
        rstd = lax.rsqrt(_mean(xc * xc) + EPS)
        a2 = (xc * rstd) * v_ref[1:2, :] + v_ref[2:3, :]
        ap_ref[:, 0:dc] = (a2 * _sigmoid(a2)).astype(act_dtype)

        pos = (i % tps) * tm + _time_of_row(tm)
        for g, w in enumerate(POOL_WINDOWS):
            cs = slice(dc + g * POOL_GROUP, dc + (g + 1) * POOL_GROUP)
            s = ext_ref[p:p + tm, cs]
            for j in range(1, w):
                s = s + ext_ref[p - 8 * j:p - 8 * j + tm, cs]
            cnt = jnp.minimum(pos + 1, w).astype(F32)
            dv = s / cnt - ext_ref[p:p + tm, cs]
            q = _mm(dv, pw_ref[g])
            ap_ref[:, cs] = (q * v_ref[3:4, g * POOL_GROUP:(g + 1) * POOL_GROUP]).astype(act_dtype)

        o = _mm(ap_ref[...], wout_ref[...])
        o_ref[...] = o
        ro = lax.rsqrt(_mean(o * o) + EPS)
        x1_ref[...] = xv + (1.0 + mod_ref[2:3, :]) * ((o * ro) * g_ref[1:2, :])

        nxt = ext_ref[tm + 7:tm + 7 + p, :]
        car_ref[...] = nxt
        tail_ref[...] = nxt

        if n_pay:
            @pl.when(i == max(nt - 2, 0))
            def _():
                _gather_phase(_layer_of(pay_refs, [gather_layer] * n_pay), got_refs, *sems, 1)

            @pl.when(i == nt - 1)
            def _():
                _gather_phase(_layer_of(pay_refs, [gather_layer] * n_pay), got_refs, *sems, 2)

    row = lambda width: pl.BlockSpec((tm, width), lambda i: (i, 0))
    return _carrier_call(
        body, f"mixer_fwd_{l}", (nt,),
        out_shape=[jax.ShapeDtypeStruct((t, d), F32), jax.ShapeDtypeStruct((t, d), act_dtype),
                   jax.ShapeDtypeStruct((t, d_in), F32), jax.ShapeDtypeStruct((t, dc), F32),
                   jax.ShapeDtypeStruct((t, d_mix), act_dtype), jax.ShapeDtypeStruct((t, d), F32),
                   jax.ShapeDtypeStruct((nt, p, d_mix), F32)],
        in_specs=[row(d),
                  pl.BlockSpec((None, None, 8, d), lambda i: (l, i // tps, 0, 0)),
                  _const_spec((None, 8, d), (l, 0, 0)),
                  _const_spec((d_in, d), (0, 0)),
                  _const_spec((32, dc), (0, 0)),
                  _const_spec((None, 8, dc), (l, 0, 0)),
                  _const_spec((None, len(POOL_WINDOWS), POOL_GROUP, POOL_GROUP), (l, 0, 0, 0)),
                  _const_spec((d_mix, d), (0, 0))],
        out_specs=[row(d), row(d), row(d_in), row(dc), row(d_mix), row(d),
                   pl.BlockSpec((None, p, d_mix), lambda i: (i, 0, 0))],
        scratch=[pltpu.VMEM((p + tm + 8, d_mix), F32), pltpu.VMEM((p, d_mix), F32)],
        operands=(x, mod, gains, w_in_t, conv_w, vec, pool_w, w_out),
        payload=payload, gather_layer=gather_layer)


def _mixer_bwd(l, dx1, x, mod, gains, u, tails, a1, o, w_in_t, conv_w, vec, pool_w, w_out, *, seq, tm, act_dtype,
               payload=(), recvs=(), rows=(), pay_layer=0):
    t, d = x.shape
    d_in = w_in_t.shape[0]
    dc = conv_w.shape[-1]
    d_mix = w_out.shape[0]
    n_taps = 31
    nt, tps = t // tm, seq // tm
    p = CONV_PREFIX
    n_groups = len(POOL_WINDOWS)
    n_pay = len(payload)

    def body(*refs):
        ins, outs, scr, pay_refs, recv_refs, sems = _carried_refs(refs, 13, 8, 3, n_pay, aliased=True)
        dx1_ref, x_ref, mod_ref, g_ref, u_ref, tail_ref, a1_ref, o_ref, win_ref, cw_ref, v_ref, pw_ref, wout_ref = ins
        dx_ref, du_ref, do_ref, dmod_ref, gd_ref, gv_ref, dcw_ref, dpw_ref = outs
        ext_ref, fext_ref, fcar_ref = scr
        i = pl.program_id(0)
        j = nt - 1 - i
        first_in_seq = (j % tps) == 0
        last_in_seq = (j % tps) == tps - 1

        if n_pay:
            @pl.when(i == 0)
            def _():
                _exchange(pay_refs, recv_refs, rows, pay_layer, *sems, start=True)

        @pl.when(i == 0)
        def _():
            gd_ref[...] = jnp.zeros_like(gd_ref)
            gv_ref[...] = jnp.zeros_like(gv_ref)
            dcw_ref[...] = jnp.zeros_like(dcw_ref)
            dpw_ref[...] = jnp.zeros_like(dpw_ref)
            fext_ref[0:8, :] = jnp.zeros((8, fext_ref.shape[1]), F32)

        @pl.when(last_in_seq)
        def _():
            dmod_ref[...] = jnp.zeros_like(dmod_ref)
            fcar_ref[...] = jnp.zeros_like(fcar_ref)

        xv = x_ref[...]
        dx1v = dx1_ref[...]
        pre_g, post_g = g_ref[0:1, :], g_ref[1:2, :]

        ov = o_ref[...]
        ro = lax.rsqrt(_mean(ov * ov) + EPS)
        yo = ov * ro
        dmod_ref[2:3, :] += _colsum(dx1v * (yo * post_g))
        dn = dx1v * (1.0 + mod_ref[2:3, :])
        gd_ref[1:2, :] += _colsum(dn * yo)
        dyo = dn * post_g
        do = ro * (dyo - yo * _mean(dyo * yo))
        dob = do.astype(act_dtype)
        do_ref[...] = dob
        dap = _mm_nt(dob, wout_ref[...])

        uv = u_ref[...]
        val, gate = uv[:, :dc], uv[:, dc:2 * dc]
        sg = _sigmoid(gate)
        ext_ref[p:p + tm, 0:dc] = val * sg
        ext_ref[p:p + tm, dc:] = uv[:, 2 * dc:]
        keep = jnp.where(first_in_seq, 0.0, 1.0).astype(F32)
        ext_ref[0:p, :] = jnp.where(_sublane_is(p, 0), tail_ref[...] * keep, ext_ref[tm - 1:tm - 1 + p, :])

        a1v = a1_ref[...]
        mu = _mean(a1v)
        xc = a1v - mu
        rstd = lax.rsqrt(_mean(xc * xc) + EPS)
        xh = xc * rstd
        ln_g = v_ref[1:2, :]
        a2 = xh * ln_g + v_ref[2:3, :]
        s2 = _sigmoid(a2)
        da2 = dap[:, :dc] * (s2 * (1.0 + a2 * (1.0 - s2)))
        gv_ref[1:2, :] += _colsum(da2 * xh)
        gv_ref[2:3, :] += _colsum(da2)
        dxh = da2 * ln_g
        da1 = rstd * (dxh - _mean(dxh) - xh * _mean(dxh * xh))
        gv_ref[0:1, :] += _colsum(da1)
        last_sublane = _sublane_is(p, 7)

        def put(cs, value):
            fext_ref[8:8 + tm, cs] = value
            fext_ref[8 + tm:8 + tm + p, cs] = jnp.where(last_sublane, fcar_ref[:, cs], fext_ref[9:9 + p, cs])

        put(slice(0, dc), da1)

        for k in range(n_taps):
            row = p - 8 * (n_taps - 1 - k)
            dcw_ref[k:k + 1, :] += _colsum(fext_ref[8:8 + tm, 0:dc] * ext_ref[row:row + tm, 0:dc])

        def store(r0, rc, cs, acc):
            sgc = _sigmoid(u_ref[r0:r0 + rc, dc + cs.start:dc + cs.stop])
            vc = u_ref[r0:r0 + rc, cs]
            du_ref[r0:r0 + rc, cs] = (acc * sgc).astype(act_dtype)
            du_ref[r0:r0 + rc, dc + cs.start:dc + cs.stop] = (acc * vc * sgc * (1.0 - sgc)).astype(act_dtype)

        _conv_taps(fext_ref, 0, dc, tm, [8 + 8 * (n_taps - 1 - k) for k in range(n_taps)],
                   cw_ref,
                   lambda cs, rc: jnp.zeros((rc, LANES), F32), store)

        pos = (j % tps) * tm + _time_of_row(tm)
        for g, w in enumerate(POOL_WINDOWS):
            cs = slice(dc + g * POOL_GROUP, dc + (g + 1) * POOL_GROUP)
            gs = slice(g * POOL_GROUP, (g + 1) * POOL_GROUP)
            s = ext_ref[p:p + tm, cs]
            for jj in range(1, w):
                s = s + ext_ref[p - 8 * jj:p - 8 * jj + tm, cs]
            cnt = jnp.minimum(pos + 1, w).astype(F32)
            dv = (s / cnt - ext_ref[p:p + tm, cs]).astype(MXU_DTYPE)
            q = _mm(dv, pw_ref[g])
            dp = dap[:, cs]
            gv_ref[3:4, gs] += _colsum(dp * q)
            dq = (dp * v_ref[3:4, gs]).astype(MXU_DTYPE)
            dpw_ref[g] += _mm_tn(dv, dq)
            dd = _mm_nt(dq, pw_ref[g])
            put(cs, dd / cnt)
            dhp = fext_ref[8:8 + tm, cs]
            for jj in range(1, w):
                dhp = dhp + fext_ref[8 + 8 * jj:8 + 8 * jj + tm, cs]
            du_ref[:, dc + cs.start:dc + cs.stop] = (dhp - dd).astype(act_dtype)

        fcar_ref[...] = fext_ref[1:1 + p, :]

        dh = _mm(du_ref[...], win_ref[...])

        r = lax.rsqrt(_mean(xv * xv) + EPS)
        xn = xv * r
        dmod_ref[0:1, :] += _colsum(dh)
        dmod_ref[1:2, :] += _colsum(dh * (xn * pre_g))
        dy = dh * (1.0 + mod_ref[1:2, :])
        gd_ref[0:1, :] += _colsum(dy * xn)
        dxn = dy * pre_g
        dx_ref[...] = dx1v + r * (dxn - xn * _mean(dxn * xn))

        if n_pay:
            @pl.when(i == nt - 1)
            def _():
                _exchange(pay_refs, recv_refs, rows, pay_layer, *sems, start=False)

    rev = lambda width: pl.BlockSpec((tm, width), lambda i: (nt - 1 - i, 0))
    return _carrier_call(
        body, f"mixer_bwd_{l}", (nt,),
        out_shape=[jax.ShapeDtypeStruct((t, d), F32), jax.ShapeDtypeStruct((t, d_in), act_dtype),
                   jax.ShapeDtypeStruct((t, d), act_dtype),
                   jax.ShapeDtypeStruct((t // seq, 8, d), F32), jax.ShapeDtypeStruct((8, d), F32),
                   jax.ShapeDtypeStruct((8, dc), F32), jax.ShapeDtypeStruct((32, dc), F32),
                   jax.ShapeDtypeStruct((n_groups, POOL_GROUP, POOL_GROUP), F32)],
        in_specs=[rev(d), rev(d),
                  pl.BlockSpec((None, None, 8, d), lambda i: (l, (nt - 1 - i) // tps, 0, 0)),
                  _const_spec((None, 8, d), (l, 0, 0)),
                  rev(d_in),
                  pl.BlockSpec((None, p, d_mix), lambda i: (jnp.maximum(nt - 2 - i, 0), 0, 0)),
                  rev(dc), rev(d),
                  _const_spec((d_in, d), (0, 0)),
                  _const_spec((32, dc), (0, 0)),
                  _const_spec((None, 8, dc), (l, 0, 0)),
                  _const_spec((None, n_groups, POOL_GROUP, POOL_GROUP), (l, 0, 0, 0)),
                  _const_spec((d_mix, d), (0, 0))],
        out_specs=[rev(d), rev(d_in), rev(d),
                   pl.BlockSpec((None, 8, d), lambda i: ((nt - 1 - i) // tps, 0, 0)),
                   _acc_spec((8, d), (0, 0)), _acc_spec((8, dc), (0, 0)), _acc_spec((32, dc), (0, 0)),
                   _acc_spec((n_groups, POOL_GROUP, POOL_GROUP), (0, 0, 0))],
        scratch=[pltpu.VMEM((p + tm, d_mix), F32), pltpu.VMEM((8 + tm + p, d_mix), F32),
                 pltpu.VMEM((p, d_mix), F32)],
        operands=(dx1, x, mod, gains, u, tails, a1, o, w_in_t, conv_w, vec, pool_w, w_out),
        payload=payload, recvs=recvs)


def _ffn_fwd(l, x, mod, gains, up, fcw, down, *, seq, tm, act_dtype, payload=(), gather_layer=None):
    t, d = x.shape
    n_chunks, _, fc = up.shape
    half = n_chunks // 2
    nt, tps = t // tm, seq // tm
    p = FFN_PREFIX
    n_pay = len(payload)

    def body(*refs):
        ins, outs, scr, pay_refs, got_refs, sems = _carried_refs(refs, 6, 6, 2, n_pay, aliased=False)
        x_ref, mod_ref, g_ref, up_ref, fcw_ref, down_ref = ins
        x2_ref, h2_ref, u2_ref, u3_ref, hid_ref, o2_ref = outs
        ext_ref, car_ref = scr
        i = pl.program_id(0)
        first = (i % tps) == 0

        if n_pay:
            @pl.when(i == 0)
            def _():
                _gather_phase(_layer_of(pay_refs, [gather_layer] * n_pay), got_refs, *sems, 0)

        xv = x_ref[...]
        r = lax.rsqrt(_mean(xv * xv) + EPS)
        hv = (xv * r) * g_ref[0:1, :] * (1.0 + mod_ref[4:5, :]) + mod_ref[3:4, :]
        hb = hv.astype(act_dtype)
        h2_ref[...] = hb

        @pl.when(first)
        def _():
            car_ref[...] = jnp.zeros_like(car_ref)

        @pl.when(i == 0)
        def _():
            for n in range(n_chunks):
                ext_ref[n, p + tm:p + tm + 8, :] = jnp.zeros((8, fc), F32)

        first_sublane = _sublane_is(p, 0)

        def project(n):
            un = _mm(hb, up_ref[n])
            ext_ref[n, p:p + tm, :] = un
            u2_ref[n] = un.astype(act_dtype)
            ext_ref[n, 0:p, :] = jnp.where(first_sublane, car_ref[n], ext_ref[n, tm - 1:tm - 1 + p, :])
            car_ref[n] = ext_ref[n, tm + 7:tm + 7 + p, :]

        def conv(n):
            return (fcw_ref[n, 3:4, :] + fcw_ref[n, 0:1, :] * ext_ref[n, p - 16:p - 16 + tm, :]
                    + fcw_ref[n, 1:2, :] * ext_ref[n, p - 8:p - 8 + tm, :]
                    + fcw_ref[n, 2:3, :] * ext_ref[n, p:p + tm, :])

        for n in range(n_chunks):
            project(n)
        o2 = jnp.zeros((tm, d), F32)
        for n in range(half):
            gt = conv(n + half)
            v = conv(n)
            sg = _sigmoid(gt)
            silu = gt * sg
            u3_ref[n] = silu.astype(act_dtype)
            u3_ref[n + half] = (v * (sg * (1.0 + gt * (1.0 - sg)))).astype(act_dtype)
            hid = (silu * v).astype(act_dtype)
            hid_ref[n] = hid
            o2 = o2 + _mm(hid, down_ref[n])
        o2_ref[...] = o2
        ro = lax.rsqrt(_mean(o2 * o2) + EPS)
        x2_ref[...] = xv + (1.0 + mod_ref[5:6, :]) * ((o2 * ro) * g_ref[1:2, :])

        if n_pay:
            @pl.when(i == max(nt - 2, 0))
            def _():
                _gather_phase(_layer_of(pay_refs, [gather_layer] * n_pay), got_refs, *sems, 1)

            @pl.when(i == nt - 1)
            def _():
                _gather_phase(_layer_of(pay_refs, [gather_layer] * n_pay), got_refs, *sems, 2)

    row = lambda width: pl.BlockSpec((tm, width), lambda i: (i, 0))
    chunked = lambda n: pl.BlockSpec((n, tm, fc), lambda i: (0, i, 0))
    return _carrier_call(
        body, f"ffn_fwd_{l}", (nt,),
        out_shape=[jax.ShapeDtypeStruct((t, d), F32), jax.ShapeDtypeStruct((t, d), act_dtype),
                   jax.ShapeDtypeStruct((n_chunks, t, fc), act_dtype),
                   jax.ShapeDtypeStruct((n_chunks, t, fc), act_dtype),
                   jax.ShapeDtypeStruct((half, t, fc), act_dtype), jax.ShapeDtypeStruct((t, d), F32)],
        in_specs=[row(d),
                  pl.BlockSpec((None, None, 8, d), lambda i: (l, i // tps, 0, 0)),
                  _const_spec((None, 8, d), (l, 0, 0)),
                  _const_spec((n_chunks, d, fc), (0, 0, 0)),
                  _const_spec((n_chunks, 8, fc), (0, 0, 0)),
                  _const_spec((half, fc, d), (0, 0, 0))],
        out_specs=[row(d), row(d), chunked(n_chunks), chunked(n_chunks), chunked(half), row(d)],
        scratch=[pltpu.VMEM((n_chunks, p + tm + 8, fc), F32), pltpu.VMEM((n_chunks, p, fc), F32)],
        operands=(x, mod, gains, up, fcw, down),
        payload=payload, gather_layer=gather_layer)


def _ffn_bwd(l, dx2, x, mod, gains, u2, u3, o2, up, fcw, down, *, seq, tm, act_dtype,
             payload=(), recvs=(), rows=(), pay_layer=0):
    t, d = x.shape
    n_chunks, _, fc = up.shape
    half = n_chunks // 2
    nt, tps = t // tm, seq // tm
    p = FFN_PREFIX
    n_pay = len(payload)

    def body(*refs):
        ins, outs, scr, pay_refs, recv_refs, sems = _carried_refs(refs, 10, 6, 2, n_pay, aliased=True)
        dx2_ref, x_ref, mod_ref, g_ref, u2_ref, u3_ref, o2_ref, up_ref, fcw_ref, down_ref = ins
        dx1_ref, du2_ref, do2_ref, dmod_ref, gd_ref, dfcw_ref = outs
        fext_ref, fcar_ref = scr
        i = pl.program_id(0)
        j = nt - 1 - i
        last_in_seq = (j % tps) == tps - 1

        if n_pay:
            @pl.when(i == 0)
            def _():
                _exchange(pay_refs, recv_refs, rows, pay_layer, *sems, start=True)

        @pl.when(i == 0)
        def _():
            gd_ref[...] = jnp.zeros_like(gd_ref)
            dfcw_ref[...] = jnp.zeros_like(dfcw_ref)
            for n in range(n_chunks):
                fext_ref[n, 0:8, :] = jnp.zeros((8, fc), F32)

        @pl.when(last_in_seq)
        def _():
            dmod_ref[...] = jnp.zeros_like(dmod_ref)
            fcar_ref[...] = jnp.zeros_like(fcar_ref)

        xv = x_ref[...]
        dx2v = dx2_ref[...]
        pre_g, post_g = g_ref[0:1, :], g_ref[1:2, :]

        ov = o2_ref[...]
        ro = lax.rsqrt(_mean(ov * ov) + EPS)
        yo = ov * ro
        dmod_ref[2:3, :] += _colsum(dx2v * (yo * post_g))
        dn = dx2v * (1.0 + mod_ref[5:6, :])
        gd_ref[1:2, :] += _colsum(dn * yo)
        dyo = dn * post_g
        do = ro * (dyo - yo * _mean(dyo * yo))
        dob = do.astype(act_dtype)
        do2_ref[...] = dob

        for n in range(half):
            dhid = _mm_nt(dob, down_ref[n])
            fext_ref[n, 8:8 + tm, :] = dhid * u3_ref[n].astype(F32)
            fext_ref[n + half, 8:8 + tm, :] = dhid * u3_ref[n + half].astype(F32)

        last_sublane = _sublane_is(p, 7)
        dh = jnp.zeros((tm, d), F32)
        for n in range(n_chunks):
            fext_ref[n, 8 + tm:8 + tm + p, :] = jnp.where(last_sublane, fcar_ref[n], fext_ref[n, 9:9 + p, :])
            fcar_ref[n] = fext_ref[n, 1:1 + p, :]
            d2 = fext_ref[n, 8:8 + tm, :]
            d1 = fext_ref[n, 16:16 + tm, :]
            d0 = fext_ref[n, 24:24 + tm, :]
            u2v = u2_ref[n].astype(F32)
            dfcw_ref[n, 3:4, :] += _colsum(d2)
            dfcw_ref[n, 0:1, :] += _colsum(d0 * u2v)
            dfcw_ref[n, 1:2, :] += _colsum(d1 * u2v)
            dfcw_ref[n, 2:3, :] += _colsum(d2 * u2v)
            du2 = (fcw_ref[n, 0:1, :] * d0 + fcw_ref[n, 1:2, :] * d1 + fcw_ref[n, 2:3, :] * d2).astype(act_dtype)
            du2_ref[n] = du2
            dh = dh + _mm_nt(du2, up_ref[n])

        r = lax.rsqrt(_mean(xv * xv) + EPS)
        xn = xv * r
        dmod_ref[0:1, :] += _colsum(dh)
        dmod_ref[1:2, :] += _colsum(dh * (xn * pre_g))
        dy = dh * (1.0 + mod_ref[4:5, :])
        gd_ref[0:1, :] += _colsum(dy * xn)
        dxn = dy * pre_g
        dx1_ref[...] = dx2v + r * (dxn - xn * _mean(dxn * xn))

        if n_pay:
            @pl.when(i == nt - 1)
            def _():
                _exchange(pay_refs, recv_refs, rows, pay_layer, *sems, start=False)

    rev = lambda width: pl.BlockSpec((tm, width), lambda i: (nt - 1 - i, 0))
    chunked = pl.BlockSpec((n_chunks, tm, fc), lambda i: (0, nt - 1 - i, 0))
    return _carrier_call(
        body, f"ffn_bwd_{l}", (nt,),
        out_shape=[jax.ShapeDtypeStruct((t, d), F32), jax.ShapeDtypeStruct((n_chunks, t, fc), act_dtype),
                   jax.ShapeDtypeStruct((t, d), act_dtype),
                   jax.ShapeDtypeStruct((t // seq, 8, d), F32), jax.ShapeDtypeStruct((8, d), F32),
                   jax.ShapeDtypeStruct((n_chunks, 8, fc), F32)],
        in_specs=[rev(d), rev(d),
                  pl.BlockSpec((None, None, 8, d), lambda i: (l, (nt - 1 - i) // tps, 0, 0)),
                  _const_spec((None, 8, d), (l, 0, 0)),
                  chunked, chunked, rev(d),
                  _const_spec((n_chunks, d, fc), (0, 0, 0)),
                  _const_spec((n_chunks, 8, fc), (0, 0, 0)),
                  _const_spec((half, fc, d), (0, 0, 0))],
        out_specs=[rev(d), chunked, rev(d),
                   pl.BlockSpec((None, 8, d), lambda i: ((nt - 1 - i) // tps, 0, 0)),
                   _acc_spec((8, d), (0, 0)), _acc_spec((n_chunks, 8, fc), (0, 0, 0))],
        scratch=[pltpu.VMEM((n_chunks, 8 + tm + p, fc), F32), pltpu.VMEM((n_chunks, p, fc), F32)],
        operands=(dx2, x, mod, gains, u2, u3, o2, up, fcw, down),
        payload=payload, recvs=recvs)


def _pad_rows(a, rows):
    pad = [(0, 0)] * a.ndim
    pad[-2] = (0, rows - a.shape[-2])
    return jnp.pad(a, pad)


def kernel(x, c, ada_w, ada_b, pre_mix_g, post_mix_g, w_in, conv_w, conv_b, conv_ln_g, conv_ln_b, pool_w, pool_scale, w_out, pre_ffn_g, post_ffn_g, ffn_up, ffn_conv_w, ffn_conv_b, ffn_down, loss_target, m_ada_w, m_ada_b, m_pre_mix_g, m_post_mix_g, m_w_in, m_conv_w, m_conv_b, m_conv_ln_g, m_conv_ln_b, m_pool_w, m_pool_scale, m_w_out, m_pre_ffn_g, m_post_ffn_g, m_ffn_up, m_ffn_conv_w, m_ffn_conv_b, m_ffn_down, v_ada_w, v_ada_b, v_pre_mix_g, v_post_mix_g, v_w_in, v_conv_w, v_conv_b, v_conv_ln_g, v_conv_ln_b, v_pool_w, v_pool_scale, v_w_out, v_pre_ffn_g, v_post_ffn_g, v_ffn_up, v_ffn_conv_w, v_ffn_conv_b, v_ffn_down):
    bl, seq, d = x.shape
    n_layers = ada_w.shape[0]
    t = bl * seq
    dc = conv_b.shape[1]
    d_in = w_in.shape[2] * N_DEV
    d_mix = w_out.shape[1] * N_DEV
    n_taps = conv_w.shape[1]
    fc = ffn_up.shape[2]
    half = N_DEV // 2
    ada_cols = ada_w.shape[2]
    n_mod = ada_cols * N_DEV // d
    assert pool_scale.shape[1] == dc and n_taps == 31 and n_mod == 6 and ffn_conv_w.shape[1] == 3
    assert pool_w.shape[1:] == (len(POOL_WINDOWS), POOL_GROUP, POOL_GROUP)
    tm = TILE_TOKENS
    assert seq % tm == 0 and CONV_PREFIX <= tm - 8
    tk = 2048 if t % 2048 == 0 else tm
    act = MXU_DTYPE

    def tile_order(a, inverse=False):
        shape = (t // tm, tm // 8, 8, d) if inverse else (t // tm, 8, tm // 8, d)
        return a.reshape(shape).transpose(0, 2, 1, 3).reshape(t, d)

    ax = lax.axis_index
    me = 4 * ax("x") + 2 * ax("y") + ax("c")

    (c_all,) = _all_gather([c], "gather_c")
    c_all = c_all.reshape(N_DEV * bl, d)
    ada_b_cols = lax.dynamic_slice_in_dim(ada_b, me * ada_cols, ada_cols, axis=1)
    mod_cols = _ada_fwd(c_all, ada_w, ada_b_cols)

    w_in_s = w_in.astype(act).transpose(0, 2, 1)
    w_out_s, up_s, down_s = w_out.astype(act), ffn_up.astype(act), ffn_down.astype(act)
    conv_w_s = _pad_rows(conv_w, 32)
    fcw_s = _pad_rows(jnp.concatenate(
        [ffn_conv_w, lax.dynamic_slice_in_dim(ffn_conv_b, me * fc, fc, axis=1)[:, None, :]], axis=1), 8)

    mixer_shards = (w_in_s, w_out_s, conv_w_s)
    late_shards = (down_s, fcw_s)

    def mixer_weights(g_w_in, g_w_out, g_conv_w):
        return (g_w_in.reshape(d_in, d), g_conv_w.transpose(1, 0, 2).reshape(32, dc), g_w_out.reshape(d_mix, d))

    def late_weights(g_down, g_fcw):
        return (g_fcw, g_down.reshape(half, fc, d))

    g0 = _all_gather([mod_cols, *mixer_shards], "gather_w0", layers=[None, 0, 0, 0])
    mod_all = g0[0].transpose(1, 2, 0, 3).reshape(n_layers, N_DEV * bl, n_mod, d)
    mod = _pad_rows(lax.dynamic_slice_in_dim(mod_all, me * bl, bl, axis=1), 8)
    mixers, lates, ups = [mixer_weights(*g0[1:])], [], []
    gains_mix = _pad_rows(jnp.stack([pre_mix_g, post_mix_g], axis=1), 8)
    gains_ffn = _pad_rows(jnp.stack([pre_ffn_g, post_ffn_g], axis=1), 8)
    vec = _pad_rows(jnp.stack([conv_b, conv_ln_g, conv_ln_b, pool_scale], axis=1), 8)
    pool_w_b = pool_w.astype(act)

    kw = dict(seq=seq, tm=tm, act_dtype=act)
    xs = tile_order(x.reshape(t, d))
    saved = []
    for l in range(n_layers):
        w_in_t, cw, w_o = mixers[l]
        (x1, h1, u, a1, ap, o, tails), got = _mixer_fwd(
            l, xs, mod, gains_mix, w_in_t, cw, vec, pool_w_b, w_o, **kw,
            payload=(up_s, *late_shards) if l == 0 else (up_s,), gather_layer=l)
        ups.append(got[0])
        if l == 0:
            lates.append(late_weights(*got[1:]))
        up_l, (fcw_l, down_l) = ups[l], lates[l]
        (x2, h2, u2, u3, hid, o2), got = _ffn_fwd(l, x1, mod, gains_ffn, up_l, fcw_l, down_l, **kw,
                                                  payload=(*mixer_shards, *late_shards) if l + 1 < n_layers else (),
                                                  gather_layer=l + 1)
        if got:
            mixers.append(mixer_weights(*got[:3]))
            lates.append(late_weights(*got[3:]))
        saved.append((xs, h1, u, tails, a1, ap, o, x1, h2, u2, u3, hid, o2))
        xs = x2

    dx, loss_part = _loss_grad(xs, tile_order(loss_target.reshape(t, d)), tm)

    def landing(shard, dtype):
        return lax.empty((N_DEV, n_layers) + shard.shape[1:], dtype)

    r_up, r_down = landing(ffn_up, act), landing(ffn_down, act)
    r_w_in, r_w_out = landing(w_in, act), landing(w_out, act)
    r_conv_w, r_fcw = landing(conv_w, F32), landing(ffn_conv_w, F32)
    def cuts(l):
        first, second = (7, 8.8) if l == 0 else (2.5, 6)
        return int(first * d / 10) // 16 * 16, int(second * d / 10) // 16 * 16

    dmods, smalls = [], []
    pending = None
    for l in reversed(range(n_layers)):
        xin, h1, u, tails, a1, ap, o, x1, h2, u2, u3, hid, o2 = saved[l]
        (w_in_t, cw, w_o), (fcw_l, down_l), up_l = mixers[l], lates[l], ups[l]
        a, b = cuts(l + 1) if pending else (0, 0)
        carried = dict(pay_layer=l + 1)
        (dx1, du2, do2, dmod_b, gd_b, dfcw), got = _ffn_bwd(
            l, dx, x1, mod, gains_ffn, u2, u3, o2, up_l, fcw_l, down_l, **kw, **carried,
            payload=(pending["up"], pending["w_in"], pending["conv_w"], pending["fcw"]) if pending else (),
            recvs=(r_up, r_w_in, r_conv_w, r_fcw) if pending else (), rows=((a, b - a), None, None, None))
        if pending:
            r_up, r_w_in, r_conv_w, r_fcw = got
        p_up, got = _matmul_tn(f"dw_up_{l}", h2[None], du2, tk, act, **carried,
                               payload=(pending["up"],) if pending else (), recvs=(r_up,) if pending else (),
                               rows=((b, d - b),))
        if pending:
            (r_up,) = got
        p_down, got = _matmul_tn(f"dw_down_{l}", hid, do2[None], tk, act, **carried,
                                 payload=(pending["w_out"],) if pending else (),
                                 recvs=(r_w_out,) if pending else (), rows=(None,))
        if pending:
            (r_w_out,) = got
        p_down = p_down.reshape(N_DEV, fc // 2, d)
        a, b = cuts(l)
        (dx, du, do, dmod_a, gd_a, gv, dcw, dpw), (r_down, r_up) = _mixer_bwd(
            l, dx1, xin, mod, gains_mix, u, tails, a1, o, w_in_t, cw, vec, pool_w_b, w_o, **kw,
            payload=(p_down, p_up), recvs=(r_down, r_up), rows=(None, (0, a)), pay_layer=l)
        last = dict(payload=(p_up,), recvs=(r_up,), pay_layer=0) if l == 0 else {}
        dw_in, got = _matmul_tn(f"dw_in_{l}", h1[None], du[None], tk, act, **last, rows=((a, b - a),))
        if l == 0:
            (r_up,) = got
            last["recvs"] = (r_up,)
        dw_out, got = _matmul_tn(f"dw_out_{l}", ap[None], do[None], tk, act, **last, rows=((b, d - b),))
        if l == 0:
            (r_up,) = got
        pending = dict(up=p_up, w_in=dw_in[0].reshape(d, N_DEV, d_in // N_DEV).transpose(1, 0, 2),
                       w_out=dw_out[0].reshape(N_DEV, d_mix // N_DEV, d),
                       conv_w=dcw[:n_taps].reshape(n_taps, N_DEV, dc // N_DEV).transpose(1, 0, 2),
                       fcw=dfcw[:, :3, :])
        dmods.append(jnp.concatenate([dmod_a[:, 0:3], dmod_b[:, 0:3]], axis=1).reshape(bl, n_mod * d))
        smalls.append(jnp.concatenate(
            [gd_a[0], gd_a[1], gv[0], gv[1], gv[2], gv[3], gd_b[0], gd_b[1], dfcw[:, 3, :].reshape(-1),
             dpw.reshape(-1)]))
    dmods.reverse()
    smalls.reverse()

    dmod_loc = jnp.stack(dmods)
    small_loc = jnp.stack(smalls)
    n_small = small_loc.shape[1]
    small_cols = 8 * LANES if (n_layers * n_small) % (8 * LANES) == 0 else LANES
    (r_w_in, r_w_out, r_conv_w, r_fcw), (g_dmod, g_small, g_loss) = _exchange_and_gather(
        (pending["w_in"], pending["w_out"], pending["conv_w"], pending["fcw"]),
        (r_w_in, r_w_out, r_conv_w, r_fcw), (None, None, None, None), 0,
        [dmod_loc.astype(act), small_loc.reshape(-1, small_cols).astype(act), loss_part], "exchange_tail")
    g_dmod = g_dmod.astype(F32)

    loss = _sum_parts(g_loss)[0, 0]

    def flat2(a):
        return a.reshape(-1, a.shape[-1])

    def update(name, parts, w, m, v):
        outs = _adam_reduce(name, parts.reshape(parts.shape[0], -1, w.shape[-1]), flat2(w), flat2(m), flat2(v))
        return [o_.reshape(w.shape) for o_ in outs]

    res = {}
    res["w_in"] = update("adam_w_in", r_w_in, w_in, m_w_in, v_w_in)
    res["w_out"] = update("adam_w_out", r_w_out, w_out, m_w_out, v_w_out)
    res["ffn_up"] = update("adam_ffn_up", r_up, ffn_up, m_ffn_up, v_ffn_up)
    res["ffn_down"] = update("adam_ffn_down", r_down, ffn_down, m_ffn_down, v_ffn_down)
    res["conv_w"] = update("adam_conv_w", r_conv_w, conv_w, m_conv_w, v_conv_w)
    res["ffn_conv_w"] = update("adam_ffn_conv_w", r_fcw, ffn_conv_w, m_ffn_conv_w, v_ffn_conv_w)

    dmod_all = g_dmod.transpose(1, 0, 2, 3).reshape(n_layers, N_DEV * bl, n_mod * d)
    dmod_cols = lax.dynamic_slice_in_dim(dmod_all, me * ada_cols, ada_cols, axis=2)
    res["ada_w"] = list(_ada_bwd(c_all.T, dmod_cols, ada_w, m_ada_w, v_ada_w))
    res["ada_b"] = update("adam_ada_b", dmod_all.transpose(1, 0, 2), ada_b, m_ada_b, v_ada_b)

    small_names = ["pre_mix_g", "post_mix_g", "conv_b", "conv_ln_g", "conv_ln_b", "pool_scale", "pre_ffn_g",
                   "post_ffn_g", "ffn_conv_b", "pool_w"]
    small_w = [pre_mix_g, post_mix_g, conv_b, conv_ln_g, conv_ln_b, pool_scale, pre_ffn_g, post_ffn_g,
               ffn_conv_b, pool_w]
    small_m = [m_pre_mix_g, m_post_mix_g, m_conv_b, m_conv_ln_g, m_conv_ln_b, m_pool_scale, m_pre_ffn_g,
               m_post_ffn_g, m_ffn_conv_b, m_pool_w]
    small_v = [v_pre_mix_g, v_post_mix_g, v_conv_b, v_conv_ln_g, v_conv_ln_b, v_pool_scale, v_pre_ffn_g,
               v_post_ffn_g, v_ffn_conv_b, v_pool_w]

    def pack(arrs):
        return jnp.concatenate([a.reshape(n_layers, -1) for a in arrs], axis=1).reshape(-1, small_cols)

    outs = _adam_reduce("adam_small", g_small, pack(small_w), pack(small_m), pack(small_v))
    outs = [o_.reshape(n_layers, n_small) for o_ in outs]
    off = 0
    for name, w in zip(small_names, small_w):
        size = w[0].size
        res[name] = [o_[:, off:off + size].reshape(w.shape) for o_ in outs]
        off += size

    order = ["ada_w", "ada_b", "pre_mix_g", "post_mix_g", "w_in", "conv_w", "conv_b", "conv_ln_g", "conv_ln_b",
             "pool_w", "pool_scale", "w_out", "pre_ffn_g", "post_ffn_g", "ffn_up", "ffn_conv_w", "ffn_conv_b",
             "ffn_down"]
    grad_x = tile_order(dx, inverse=True).reshape(bl, seq, d)
    return (loss, grad_x, *[res[n][0] for n in order], *[res[n][1] for n in order],
            *[res[n][2] for n in order], *[res[n][3] for n in order])
```

```python
import jax
import jax.numpy as jnp
from jax import lax
from jax.experimental import pallas as pl
from jax.experimental.pallas import tpu as pltpu

N_DEV = 8
EPS = 1e-6
POOL_WINDOWS = (2, 4, 8, 16)
POOL_GROUP = 128
TILE_TOKENS = 256
CONV_PREFIX = 8 * 30
FFN_PREFIX = 8 * 2
LANES = 128
ROW_CHUNK = 128
MXU_DTYPE = jnp.bfloat16
VMEM_LIMIT = 60 * 1024 * 1024

ADAM_LR = 0.001
ADAM_B1 = 0.9
ADAM_B2 = 0.999
ADAM_EPS = 1e-08
ADAM_WD = 0.01
ADAM_STEP = 10

MESH = pl.DeviceIdType.MESH
F32 = jnp.float32


def _mm(a, b):
    return jnp.dot(a.astype(MXU_DTYPE), b.astype(MXU_DTYPE), preferred_element_type=F32)


def _mm_nt(a, b):
    return lax.dot_general(a.astype(MXU_DTYPE), b.astype(MXU_DTYPE), (((1,), (1,)), ((), ())),
                           preferred_element_type=F32)


def _mm_tn(a, b):
    return lax.dot_general(a.astype(MXU_DTYPE), b.astype(MXU_DTYPE), (((0,), (0,)), ((), ())),
                           preferred_element_type=F32)


def _mean(v):
    return jnp.mean(v, axis=-1, keepdims=True)


def _colsum(v):
    return jnp.sum(v, axis=0, keepdims=True)


def _sigmoid(v):
    return jax.nn.sigmoid(v)


def _params(n_grid=1):
    return pltpu.CompilerParams(dimension_semantics=("arbitrary",) * n_grid, vmem_limit_bytes=VMEM_LIMIT)


def _const_spec(shape, index):
    return pl.BlockSpec(shape, lambda *_: index, pipeline_mode=pl.Buffered(1))


def _acc_spec(shape, index):
    return pl.BlockSpec(shape, lambda *_: index)


def _position():
    x, y, c = lax.axis_index("x"), lax.axis_index("y"), lax.axis_index("c")
    return x, y, c


def _gather_phase(ins, outs, send_sems, recv_sems, local_sems, phase):
    n = len(ins)
    x, y, c = _position()
    me, sibling = (x, y, c), (x, y, 1 - c)
    chips = [(1 - x, y), (x, 1 - y), (1 - x, 1 - y)]

    def slot(k, px, py, pc):
        return outs[k].at[4 * px + 2 * py + pc]

    def copy(k, s, block, to, src=None):
        return pltpu.make_async_remote_copy(
            src_ref=slot(k, *block) if src is None else src, dst_ref=slot(k, *block),
            send_sem=send_sems.at[k, s], recv_sem=recv_sems.at[k, s], device_id=to, device_id_type=MESH)

    mine = [pltpu.make_async_copy(ins[k], slot(k, *me), local_sems.at[k]) for k in range(n)]
    first = []
    for k in range(n):
        first.append(copy(k, 0, me, sibling, src=ins[k]))
        first += [copy(k, 1 + j, me, (*chip, c), src=ins[k]) for j, chip in enumerate(chips)]
    passed = [copy(k, 4 + j, (*chip, c), sibling) for j, chip in enumerate(chips) for k in range(n)]
    if phase == 0:
        for cp in mine + first:
            cp.start()
    elif phase == 1:
        for j, chip in enumerate(chips):
            for k in range(n):
                copy(k, 1 + j, (*chip, c), me).wait_recv()
                copy(k, 4 + j, (*chip, c), sibling).start()
    else:
        for k in range(n):
            copy(k, 0, sibling, me).wait_recv()
            for j, chip in enumerate(chips):
                copy(k, 4 + j, (*chip, 1 - c), me).wait_recv()
        for cp in first + passed:
            cp.wait_send()
        for cp in mine:
            cp.wait()


def _gather_scratch(n):
    return [pltpu.SemaphoreType.DMA((n, 7)), pltpu.SemaphoreType.DMA((n, 7)), pltpu.SemaphoreType.DMA((n,))]


def _layer_of(refs, layers):
    return [r if lay is None else r.at[lay] for r, lay in zip(refs, layers)]


def _gathered_shapes(arrs, layers):
    return [jax.ShapeDtypeStruct((N_DEV,) + (a.shape if lay is None else a.shape[1:]), a.dtype)
            for a, lay in zip(arrs, layers)]


def _all_gather(arrs, name, layers=None):
    n = len(arrs)
    layers = [None] * n if layers is None else layers

    def body(*refs):
        for phase in range(3):
            _gather_phase(_layer_of(refs[:n], layers), refs[n:2 * n], *refs[2 * n:], phase)

    any_spec = pl.BlockSpec(memory_space=pl.ANY)
    return pl.pallas_call(
        body, name=name,
        out_shape=_gathered_shapes(arrs, layers),
        in_specs=[any_spec] * n, out_specs=[any_spec] * n,
        scratch_shapes=_gather_scratch(n),
    )(*arrs)


def _exchange(payload, recvs, rows, layer, send_sems, recv_sems, local_sems, start):
    x, y, c = _position()
    me = 4 * x + 2 * y + c
    for k, (src, recv) in enumerate(zip(payload, recvs)):
        def block(ref, *index):
            return ref.at[index] if rows[k] is None else ref.at[(*index, pl.ds(*rows[k]))]

        local = pltpu.make_async_copy(block(src, me), block(recv, me, layer), local_sems.at[k])
        if start:
            local.start()
        else:
            local.wait()
        for j in range(1, N_DEV):
            px = (1 - x) if (j & 4) else x
            py = (1 - y) if (j & 2) else y
            pc = (1 - c) if (j & 1) else c
            peer = 4 * px + 2 * py + pc
            landing = block(recv, me, layer) if start else block(recv, peer, layer)
            cp = pltpu.make_async_remote_copy(
                src_ref=block(src, peer), dst_ref=landing, send_sem=send_sems.at[k, j - 1],
                recv_sem=recv_sems.at[k, j - 1], device_id=(px, py, pc), device_id_type=MESH)
            if start:
                cp.start()
            else:
                cp.wait()


def _exchange_scratch(n):
    return [pltpu.SemaphoreType.DMA((n, N_DEV - 1)), pltpu.SemaphoreType.DMA((n, N_DEV - 1)),
            pltpu.SemaphoreType.DMA((n,))]


def _carried_refs(refs, n_in, n_out, n_scratch, n_pay, aliased):
    ins = refs[:n_in]
    pay = refs[n_in:n_in + n_pay]
    o0 = n_in + (2 * n_pay if aliased else n_pay)
    outs = refs[o0:o0 + n_out]
    recvs = refs[o0 + n_out:o0 + n_out + n_pay]
    s0 = o0 + n_out + n_pay
    return ins, outs, refs[s0:s0 + n_scratch], pay, recvs, refs[s0 + n_scratch:]


def _carrier_call(body, name, grid, in_specs, out_specs, out_shape, scratch, operands, payload, recvs=None,
                  gather_layer=None):
    n_in, n_out, n_pay = len(in_specs), len(out_specs), len(payload)
    any_spec = pl.BlockSpec(memory_space=pl.ANY)
    if recvs is None:
        landing = _gathered_shapes(payload, [gather_layer] * n_pay)
        extra_in, aliases = list(payload), {}
        sems = _gather_scratch(n_pay) if n_pay else []
    else:
        landing = [jax.ShapeDtypeStruct(r.shape, r.dtype) for r in recvs]
        extra_in = list(payload) + list(recvs)
        aliases = {n_in + n_pay + k: n_out + k for k in range(n_pay)}
        sems = _exchange_scratch(n_pay) if n_pay else []
    res = pl.pallas_call(
        body, name=name, grid=grid,
        out_shape=list(out_shape) + landing,
        in_specs=list(in_specs) + [any_spec] * len(extra_in),
        out_specs=list(out_specs) + [any_spec] * n_pay,
        input_output_aliases=aliases,
        scratch_shapes=list(scratch) + sems,
        compiler_params=_params(len(grid)),
    )(*operands, *extra_in)
    return res[:n_out], res[n_out:]


def _adamw(w, g, m, v):
    m = ADAM_B1 * m + (1.0 - ADAM_B1) * g
    v = ADAM_B2 * v + (1.0 - ADAM_B2) * jnp.square(g)
    m_hat = m / (1.0 - ADAM_B1 ** ADAM_STEP)
    v_hat = v / (1.0 - ADAM_B2 ** ADAM_STEP)
    delta = -ADAM_LR * (m_hat / (jnp.sqrt(v_hat) + ADAM_EPS) + ADAM_WD * w)
    return delta, m, v


def _ada_fwd(c_all, ada_w, ada_b_cols):
    n_layers, d, cols = ada_w.shape
    b = c_all.shape[0]

    def body(c_ref, w_ref, b_ref, o_ref):
        cv = c_ref[...]
        act = cv * _sigmoid(cv)
        o_ref[...] = jnp.dot(act, w_ref[...], preferred_element_type=F32,
                             precision=lax.Precision.HIGHEST) + b_ref[...]

    return pl.pallas_call(
        body, name="ada_fwd", grid=(n_layers,),
        out_shape=jax.ShapeDtypeStruct((n_layers, b, cols), F32),
        in_specs=[pl.BlockSpec((b, d), lambda l: (0, 0)),
                  pl.BlockSpec((None, d, cols), lambda l: (l, 0, 0)),
                  pl.BlockSpec((None, 1, cols), lambda l: (l, 0, 0))],
        out_specs=pl.BlockSpec((None, b, cols), lambda l: (l, 0, 0)),
        compiler_params=_params(1),
    )(c_all, ada_w, ada_b_cols.reshape(n_layers, 1, cols))


def _carry_exchange(step_is_first, step_is_last, pay_refs, recv_refs, rows, pay_layer, sems):
    def start():
        if pay_refs:
            @pl.when(step_is_first)
            def _():
                _exchange(pay_refs, recv_refs, rows, pay_layer, *sems, start=True)

    def finish():
        if pay_refs:
            @pl.when(step_is_last)
            def _():
                _exchange(pay_refs, recv_refs, rows, pay_layer, *sems, start=False)

    return start, finish


def _ada_bwd(c_all_t, dmod_cols, w, m, v, payload=(), recvs=(), rows=(), pay_layer=0):
    n_layers, d, cols = w.shape
    b = c_all_t.shape[1]
    td = 256 if d % 256 == 0 else d
    n_td = d // td
    n_pay = len(payload)

    def body(*refs):
        ins, outs, _, pay_refs, recv_refs, sems = _carried_refs(refs, 5, 4, 0, n_pay, aliased=True)
        c_ref, dm_ref, w_ref, m_ref, v_ref = ins
        g_ref, dl_ref, nm_ref, nv_ref = outs
        li, i = pl.program_id(0), pl.program_id(1)
        start, finish = _carry_exchange(jnp.logical_and(li == 0, i == 0),
                                        jnp.logical_and(li == n_layers - 1, i == n_td - 1),
                                        pay_refs, recv_refs, rows, pay_layer, sems)
        start()
        cv = c_ref[...]
        act = cv * _sigmoid(cv)
        g = jnp.dot(act, dm_ref[...], preferred_element_type=F32, precision=lax.Precision.HIGHEST)
        delta, nm, nv = _adamw(w_ref[...], g, m_ref[...], v_ref[...])
        g_ref[...] = g
        dl_ref[...] = delta
        nm_ref[...] = nm
        nv_ref[...] = nv
        finish()

    blk = pl.BlockSpec((None, td, cols), lambda l, i: (l, i, 0))
    shp = jax.ShapeDtypeStruct(w.shape, F32)
    return _carrier_call(
        body, "ada_bwd", (n_layers, n_td),
        out_shape=[shp] * 4,
        in_specs=[pl.BlockSpec((td, b), lambda l, i: (i, 0)),
                  pl.BlockSpec((None, b, cols), lambda l, i: (l, 0, 0)), blk, blk, blk],
        out_specs=[blk] * 4, scratch=[],
        operands=(c_all_t, dmod_cols, w, m, v), payload=payload, recvs=recvs)


def _row_tile(rows, cols, budget=128 * 1024, step=8):
    best = None
    for t in range(step, rows + 1, step):
        if rows % t == 0 and t * cols <= budget:
            best = t
    return best if best is not None else rows


def _adam_reduce(name, parts, w, m, v, payload=(), recvs=(), rows=(), pay_layer=0):
    p, n_rows, cols = parts.shape
    tr = _row_tile(n_rows, cols, budget=(256 * 1024) // max(1, p // 4), step=8 if parts.dtype == F32 else 16)
    n_steps = n_rows // tr
    n_pay = len(payload)

    def body(*refs):
        ins, outs, _, pay_refs, recv_refs, sems = _carried_refs(refs, 4, 4, 0, n_pay, aliased=True)
        p_ref, w_ref, m_ref, v_ref = ins
        g_ref, dl_ref, nm_ref, nv_ref = outs
        i = pl.program_id(0)
        start, finish = _carry_exchange(i == 0, i == n_steps - 1, pay_refs, recv_refs, rows, pay_layer, sems)
        start()
        g = p_ref[0].astype(F32)
        for k in range(1, p):
            g = g + p_ref[k].astype(F32)
        delta, nm, nv = _adamw(w_ref[...], g, m_ref[...], v_ref[...])
        g_ref[...] = g
        dl_ref[...] = delta
        nm_ref[...] = nm
        nv_ref[...] = nv
        finish()

    blk = pl.BlockSpec((tr, cols), lambda i: (i, 0))
    shp = jax.ShapeDtypeStruct((n_rows, cols), F32)
    return _carrier_call(
        body, name, (n_steps,),
        out_shape=[shp] * 4,
        in_specs=[pl.BlockSpec((p, tr, cols), lambda i: (0, i, 0)), blk, blk, blk],
        out_specs=[blk] * 4, scratch=[],
        operands=(parts, w, m, v), payload=payload, recvs=recvs)


def _sum_parts(parts):
    p = parts.shape[0]

    def body(p_ref, o_ref):
        acc = p_ref[0]
        for k in range(1, p):
            acc = acc + p_ref[k]
        o_ref[...] = acc

    return pl.pallas_call(body, name="loss_sum", out_shape=jax.ShapeDtypeStruct(parts.shape[1:], F32))(parts)


def _loss_grad(y, target, tm):
    t, d = y.shape

    def body(y_ref, t_ref, dy_ref, loss_ref):
        @pl.when(pl.program_id(0) == 0)
        def _():
            loss_ref[...] = jnp.zeros_like(loss_ref)

        diff = y_ref[...] - t_ref[...]
        dy_ref[...] = diff / d
        part = 0.5 * jnp.sum(_mean(diff * diff), axis=0, keepdims=True)
        loss_ref[...] += jnp.broadcast_to(part, loss_ref.shape)

    blk = pl.BlockSpec((tm, d), lambda i: (i, 0))
    return pl.pallas_call(
        body, name="loss_grad", grid=(t // tm,),
        out_shape=[jax.ShapeDtypeStruct((t, d), F32), jax.ShapeDtypeStruct((8, LANES), F32)],
        in_specs=[blk, blk], out_specs=[blk, pl.BlockSpec((8, LANES), lambda i: (0, 0))],
        compiler_params=_params(1),
    )(y, target)


def _matmul_tn(name, a, b, tk, out_dtype, payload=(), recvs=(), rows=(), pay_layer=0, gather=()):
    ga, t, m = a.shape
    gb, _, n = b.shape
    g = max(ga, gb)
    n_k = t // tk
    n_gather = len(gather)
    assert not (payload and gather)

    def body(*refs):
        (a_ref, b_ref), (o_ref,), (acc_ref,), pay_refs, recv_refs, sems = _carried_refs(
            refs, 2, 1, 1, n_gather or len(payload), aliased=not n_gather)
        gi, k = pl.program_id(0), pl.program_id(1)
        first = jnp.logical_and(gi == 0, k == 0)
        last = jnp.logical_and(gi == g - 1, k == n_k - 1)
        start, finish = _carry_exchange(first, last, () if n_gather else pay_refs, recv_refs, rows, pay_layer, sems)
        start()
        if n_gather:
            @pl.when(first)
            def _():
                _gather_phase(pay_refs, recv_refs, *sems, 0)

        @pl.when(k == 0)
        def _():
            acc_ref[...] = jnp.zeros_like(acc_ref)

        acc_ref[...] += _mm_tn(a_ref[...], b_ref[...])

        @pl.when(k == n_k - 1)
        def _():
            o_ref[...] = acc_ref[...].astype(out_dtype)

        finish()
        if n_gather:
            @pl.when(last)
            def _():
                _gather_phase(pay_refs, recv_refs, *sems, 1)
                _gather_phase(pay_refs, recv_refs, *sems, 2)

    (out,), got = _carrier_call(
        body, name, (g, n_k),
        out_shape=[jax.ShapeDtypeStruct((g, m, n), out_dtype)],
        in_specs=[pl.BlockSpec((None, tk, m), (lambda gi, k: (gi, k, 0)) if ga > 1 else (lambda gi, k: (0, k, 0))),
                  pl.BlockSpec((None, tk, n), (lambda gi, k: (gi, k, 0)) if gb > 1 else (lambda gi, k: (0, k, 0)))],
        out_specs=[pl.BlockSpec((None, m, n), lambda gi, k: (gi, 0, 0))],
        scratch=[pltpu.VMEM((m, n), F32)],
        operands=(a, b), payload=gather or payload, recvs=None if n_gather else recvs)
    return out, got


def _time_of_row(tm):
    i = lax.broadcasted_iota(jnp.int32, (tm, 1), 0)
    return (i % 8) * (tm // 8) + i // 8


def _sublane_is(rows, s):
    return lax.broadcasted_iota(jnp.int32, (rows, 1), 0) % 8 == s


def _conv_taps(src_ref, col0, ncols, tm, tap_rows, w_ref, init_of, store):
    rc = min(ROW_CHUNK, tm)
    for cb in range(ncols // LANES):
        cs = slice(cb * LANES, (cb + 1) * LANES)
        ss = slice(col0 + cb * LANES, col0 + (cb + 1) * LANES)
        for r0 in range(0, tm, rc):
            acc = init_of(cs, rc)
            for k, row in enumerate(tap_rows):
                acc = acc + w_ref[k:k + 1, cs] * src_ref[r0 + row:r0 + row + rc, ss]
            store(r0, rc, cs, acc)


def _mixer_fwd(l, x, mod, gains, w_in_t, conv_w, vec, pool_w, w_out, *, seq, tm, act_dtype, payload=(),
               gather_layer=None):
    t, d = x.shape
    d_in = w_in_t.shape[0]
    dc = conv_w.shape[-1]
    d_mix = w_out.shape[0]
    n_taps = 31
    nt, tps = t // tm, seq // tm
    p = CONV_PREFIX
    n_pay = len(payload)

    def body(*refs):
        ins, outs, scr, pay_refs, got_refs, sems = _carried_refs(refs, 8, 7, 2, n_pay, aliased=False)
        x_ref, mod_ref, g_ref, win_ref, cw_ref, v_ref, pw_ref, wout_ref = ins
        x1_ref, h1_ref, u_ref, a1_ref, ap_ref, o_ref, tail_ref = outs
        ext_ref, car_ref = scr
        i = pl.program_id(0)
        first = (i % tps) == 0

        if n_pay:
            @pl.when(i == 0)
            def _():
                _gather_phase(_layer_of(pay_refs, [gather_layer] * n_pay), got_refs, *sems, 0)

        xv = x_ref[...]
        r = lax.rsqrt(_mean(xv * xv) + EPS)
        hv = (xv * r) * g_ref[0:1, :] * (1.0 + mod_ref[1:2, :]) + mod_ref[0:1, :]
        hb = hv.astype(act_dtype)
        h1_ref[...] = hb
        u = _mm_nt(hb, win_ref[...])
        u_ref[...] = u
        a0 = u[:, :dc] * _sigmoid(u[:, dc:2 * dc])

        @pl.when(first)
        def _():
            car_ref[...] = jnp.zeros_like(car_ref)

        @pl.when(i == 0)
        def _():
            ext_ref[p + tm:p + tm + 8, :] = jnp.zeros((8, ext_ref.shape[1]), F32)

        ext_ref[p:p + tm, 0:dc] = a0
        ext_ref[p:p + tm, dc:] = u[:, 2 * dc:]
        ext_ref[0:p, :] = jnp.where(_sublane_is(p, 0), car_ref[...], ext_ref[tm - 1:tm - 1 + p, :])

        def store(r0, rc, cs, acc):
            a1_ref[r0:r0 + rc, cs] = acc

        _conv_taps(ext_ref, 0, dc, tm, [p - 8 * (n_taps - 1 - k) for k in range(n_taps)],
                   cw_ref,
                   lambda cs, rc: jnp.broadcast_to(v_ref[0:1, cs], (rc, LANES)), store)
        a1 = a1_ref[...]
        mu = _mean(a1)
        xc = a1 - mu
        rstd = lax.rsqrt(_mean(xc * xc) + EPS)
        a2 = (xc * rstd) * v_ref[1:2, :] + v_ref[2:3, :]
        ap_ref[:, 0:dc] = (a2 * _sigmoid(a2)).astype(act_dtype)

        pos = (i % tps) * tm + _time_of_row(tm)
        for g, w in enumerate(POOL_WINDOWS):
            cs = slice(dc + g * POOL_GROUP, dc + (g + 1) * POOL_GROUP)
            s = ext_ref[p:p + tm, cs]
            for j in range(1, w):
                s = s + ext_ref[p - 8 * j:p - 8 * j + tm, cs]
            cnt = jnp.minimum(pos + 1, w).astype(F32)
            dv = s / cnt - ext_ref[p:p + tm, cs]
            q = _mm(dv, pw_ref[g])
            ap_ref[:, cs] = (q * v_ref[3:4, g * POOL_GROUP:(g + 1) * POOL_GROUP]).astype(act_dtype)

        o = _mm(ap_ref[...], wout_ref[...])
        o_ref[...] = o
        ro = lax.rsqrt(_mean(o * o) + EPS)
        x1_ref[...] = xv + (1.0 + mod_ref[2:3, :]) * ((o * ro) * g_ref[1:2, :])

        nxt = ext_ref[tm + 7:tm + 7 + p, :]
        car_ref[...] = nxt
        tail_ref[...] = nxt

        if n_pay:
            @pl.when(i == max(nt - 2, 0))
            def _():
                _gather_phase(_layer_of(pay_refs, [gather_layer] * n_pay), got_refs, *sems, 1)

            @pl.when(i == nt - 1)
            def _():
                _gather_phase(_layer_of(pay_refs, [gather_layer] * n_pay), got_refs, *sems, 2)

    row = lambda width: pl.BlockSpec((tm, width), lambda i: (i, 0))
    return _carrier_call(
        body, f"mixer_fwd_{l}", (nt,),
        out_shape=[jax.ShapeDtypeStruct((t, d), F32), jax.ShapeDtypeStruct((t, d), act_dtype),
                   jax.ShapeDtypeStruct((t, d_in), F32), jax.ShapeDtypeStruct((t, dc), F32),
                   jax.ShapeDtypeStruct((t, d_mix), act_dtype), jax.ShapeDtypeStruct((t, d), F32),
                   jax.ShapeDtypeStruct((nt, p, d_mix), F32)],
        in_specs=[row(d),
                  pl.BlockSpec((None, None, 8, d), lambda i: (l, i // tps, 0, 0)),
                  _const_spec((None, 8, d), (l, 0, 0)),
                  _const_spec((d_in, d), (0, 0)),
                  _const_spec((32, dc), (0, 0)),
                  _const_spec((None, 8, dc), (l, 0, 0)),
                  _const_spec((None, len(POOL_WINDOWS), POOL_GROUP, POOL_GROUP), (l, 0, 0, 0)),
                  _const_spec((d_mix, d), (0, 0))],
        out_specs=[row(d), row(d), row(d_in), row(dc), row(d_mix), row(d),
                   pl.BlockSpec((None, p, d_mix), lambda i: (i, 0, 0))],
        scratch=[pltpu.VMEM((p + tm + 8, d_mix), F32), pltpu.VMEM((p, d_mix), F32)],
        operands=(x, mod, gains, w_in_t, conv_w, vec, pool_w, w_out),
        payload=payload, gather_layer=gather_layer)


def _mixer_bwd(l, dx1, x, mod, gains, u, tails, a1, o, w_in_t, conv_w, vec, pool_w, w_out, *, seq, tm, act_dtype,
               payload=(), recvs=(), rows=(), pay_layer=0):
    t, d = x.shape
    d_in = w_in_t.shape[0]
    dc = conv_w.shape[-1]
    d_mix = w_out.shape[0]
    n_taps = 31
    nt, tps = t // tm, seq // tm
    p = CONV_PREFIX
    n_groups = len(POOL_WINDOWS)
    n_pay = len(payload)

    def body(*refs):
        ins, outs, scr, pay_refs, recv_refs, sems = _carried_refs(refs, 13, 8, 3, n_pay, aliased=True)
        dx1_ref, x_ref, mod_ref, g_ref, u_ref, tail_ref, a1_ref, o_ref, win_ref, cw_ref, v_ref, pw_ref, wout_ref = ins
        dx_ref, du_ref, do_ref, dmod_ref, gd_ref, gv_ref, dcw_ref, dpw_ref = outs
        ext_ref, fext_ref, fcar_ref = scr
        i = pl.program_id(0)
        j = nt - 1 - i
        first_in_seq = (j % tps) == 0
        last_in_seq = (j % tps) == tps - 1

        if n_pay:
            @pl.when(i == 0)
            def _():
                _exchange(pay_refs, recv_refs, rows, pay_layer, *sems, start=True)

        @pl.when(i == 0)
        def _():
            gd_ref[...] = jnp.zeros_like(gd_ref)
            gv_ref[...] = jnp.zeros_like(gv_ref)
            dcw_ref[...] = jnp.zeros_like(dcw_ref)
            dpw_ref[...] = jnp.zeros_like(dpw_ref)
            fext_ref[0:8, :] = jnp.zeros((8, fext_ref.shape[1]), F32)

        @pl.when(last_in_seq)
        def _():
            dmod_ref[...] = jnp.zeros_like(dmod_ref)
            fcar_ref[...] = jnp.zeros_like(fcar_ref)

        xv = x_ref[...]
        dx1v = dx1_ref[...]
        pre_g, post_g = g_ref[0:1, :], g_ref[1:2, :]

        ov = o_ref[...]
        ro = lax.rsqrt(_mean(ov * ov) + EPS)
        yo = ov * ro
        dmod_ref[2:3, :] += _colsum(dx1v * (yo * post_g))
        dn = dx1v * (1.0 + mod_ref[2:3, :])
        gd_ref[1:2, :] += _colsum(dn * yo)
        dyo = dn * post_g
        do = ro * (dyo - yo * _mean(dyo * yo))
        dob = do.astype(act_dtype)
        do_ref[...] = dob
        dap = _mm_nt(dob, wout_ref[...])

        uv = u_ref[...]
        val, gate = uv[:, :dc], uv[:, dc:2 * dc]
        sg = _sigmoid(gate)
        ext_ref[p:p + tm, 0:dc] = val * sg
        ext_ref[p:p + tm, dc:] = uv[:, 2 * dc:]
        keep = jnp.where(first_in_seq, 0.0, 1.0).astype(F32)
        ext_ref[0:p, :] = jnp.where(_sublane_is(p, 0), tail_ref[...] * keep, ext_ref[tm - 1:tm - 1 + p, :])

        a1v = a1_ref[...]
        mu = _mean(a1v)
        xc = a1v - mu
        rstd = lax.rsqrt(_mean(xc * xc) + EPS)
        xh = xc * rstd
        ln_g = v_ref[1:2, :]
        a2 = xh * ln_g + v_ref[2:3, :]
        s2 = _sigmoid(a2)
        da2 = dap[:, :dc] * (s2 * (1.0 + a2 * (1.0 - s2)))
        gv_ref[1:2, :] += _colsum(da2 * xh)
        gv_ref[2:3, :] += _colsum(da2)
        dxh = da2 * ln_g
        da1 = rstd * (dxh - _mean(dxh) - xh * _mean(dxh * xh))
        gv_ref[0:1, :] += _colsum(da1)
        last_sublane = _sublane_is(p, 7)

        def put(cs, value):
            fext_ref[8:8 + tm, cs] = value
            fext_ref[8 + tm:8 + tm + p, cs] = jnp.where(last_sublane, fcar_ref[:, cs], fext_ref[9:9 + p, cs])

        put(slice(0, dc), da1)

        for k in range(n_taps):
            row = p - 8 * (n_taps - 1 - k)
            dcw_ref[k:k + 1, :] += _colsum(fext_ref[8:8 + tm, 0:dc] * ext_ref[row:row + tm, 0:dc])

        def store(r0, rc, cs, acc):
            sgc = _sigmoid(u_ref[r0:r0 + rc, dc + cs.start:dc + cs.stop])
            vc = u_ref[r0:r0 + rc, cs]
            du_ref[r0:r0 + rc, cs] = (acc * sgc).astype(act_dtype)
            du_ref[r0:r0 + rc, dc + cs.start:dc + cs.stop] = (acc * vc * sgc * (1.0 - sgc)).astype(act_dtype)

        _conv_taps(fext_ref, 0, dc, tm, [8 + 8 * (n_taps - 1 - k) for k in range(n_taps)],
                   cw_ref,
                   lambda cs, rc: jnp.zeros((rc, LANES), F32), store)

        pos = (j % tps) * tm + _time_of_row(tm)
        for g, w in enumerate(POOL_WINDOWS):
            cs = slice(dc + g * POOL_GROUP, dc + (g + 1) * POOL_GROUP)
            gs = slice(g * POOL_GROUP, (g + 1) * POOL_GROUP)
            s = ext_ref[p:p + tm, cs]
            for jj in range(1, w):
                s = s + ext_ref[p - 8 * jj:p - 8 * jj + tm, cs]
            cnt = jnp.minimum(pos + 1, w).astype(F32)
            dv = (s / cnt - ext_ref[p:p + tm, cs]).astype(MXU_DTYPE)
            q = _mm(dv, pw_ref[g])
            dp = dap[:, cs]
            gv_ref[3:4, gs] += _colsum(dp * q)
            dq = (dp * v_ref[3:4, gs]).astype(MXU_DTYPE)
            dpw_ref[g] += _mm_tn(dv, dq)
            dd = _mm_nt(dq, pw_ref[g])
            put(cs, dd / cnt)
            dhp = fext_ref[8:8 + tm, cs]
            for jj in range(1, w):
                dhp = dhp + fext_ref[8 + 8 * jj:8 + 8 * jj + tm, cs]
            du_ref[:, dc + cs.start:dc + cs.stop] = (dhp - dd).astype(act_dtype)

        fcar_ref[...] = fext_ref[1:1 + p, :]

        dh = _mm(du_ref[...], win_ref[...])

        r = lax.rsqrt(_mean(xv * xv) + EPS)
        xn = xv * r
        dmod_ref[0:1, :] += _colsum(dh)
        dmod_ref[1:2, :] += _colsum(dh * (xn * pre_g))
        dy = dh * (1.0 + mod_ref[1:2, :])
        gd_ref[0:1, :] += _colsum(dy * xn)
        dxn = dy * pre_g
        dx_ref[...] = dx1v + r * (dxn - xn * _mean(dxn * xn))

        if n_pay:
            @pl.when(i == nt - 1)
            def _():
                _exchange(pay_refs, recv_refs, rows, pay_layer, *sems, start=False)

    rev = lambda width: pl.BlockSpec((tm, width), lambda i: (nt - 1 - i, 0))
    return _carrier_call(
        body, f"mixer_bwd_{l}", (nt,),
        out_shape=[jax.ShapeDtypeStruct((t, d), F32), jax.ShapeDtypeStruct((t, d_in), act_dtype),
                   jax.ShapeDtypeStruct((t, d), act_dtype),
                   jax.ShapeDtypeStruct((t // seq, 8, d), F32), jax.ShapeDtypeStruct((8, d), F32),
                   jax.ShapeDtypeStruct((8, dc), F32), jax.ShapeDtypeStruct((32, dc), F32),
                   jax.ShapeDtypeStruct((n_groups, POOL_GROUP, POOL_GROUP), F32)],
        in_specs=[rev(d), rev(d),
                  pl.BlockSpec((None, None, 8, d), lambda i: (l, (nt - 1 - i) // tps, 0, 0)),
                  _const_spec((None, 8, d), (l, 0, 0)),
                  rev(d_in),
                  pl.BlockSpec((None, p, d_mix), lambda i: (jnp.maximum(nt - 2 - i, 0), 0, 0)),
                  rev(dc), rev(d),
                  _const_spec((d_in, d), (0, 0)),
                  _const_spec((32, dc), (0, 0)),
                  _const_spec((None, 8, dc), (l, 0, 0)),
                  _const_spec((None, n_groups, POOL_GROUP, POOL_GROUP), (l, 0, 0, 0)),
                  _const_spec((d_mix, d), (0, 0))],
        out_specs=[rev(d), rev(d_in), rev(d),
                   pl.BlockSpec((None, 8, d), lambda i: ((nt - 1 - i) // tps, 0, 0)),
                   _acc_spec((8, d), (0, 0)), _acc_spec((8, dc), (0, 0)), _acc_spec((32, dc), (0, 0)),
                   _acc_spec((n_groups, POOL_GROUP, POOL_GROUP), (0, 0, 0))],
        scratch=[pltpu.VMEM((p + tm, d_mix), F32), pltpu.VMEM((8 + tm + p, d_mix), F32),
                 pltpu.VMEM((p, d_mix), F32)],
        operands=(dx1, x, mod, gains, u, tails, a1, o, w_in_t, conv_w, vec, pool_w, w_out),
        payload=payload, recvs=recvs)


def _ffn_fwd(l, x, mod, gains, up, fcw, down, *, seq, tm, act_dtype, payload=(), gather_layer=None):
    t, d = x.shape
    n_chunks, _, fc = up.shape
    half = n_chunks // 2
    nt, tps = t // tm, seq // tm
    p = FFN_PREFIX
    n_pay = len(payload)

    def body(*refs):
        ins, outs, scr, pay_refs, got_refs, sems = _carried_refs(refs, 6, 6, 2, n_pay, aliased=False)
        x_ref, mod_ref, g_ref, up_ref, fcw_ref, down_ref = ins
        x2_ref, h2_ref, u2_ref, u3_ref, hid_ref, o2_ref = outs
        ext_ref, car_ref = scr
        i = pl.program_id(0)
        first = (i % tps) == 0

        if n_pay:
            @pl.when(i == 0)
            def _():
                _gather_phase(_layer_of(pay_refs, [gather_layer] * n_pay), got_refs, *sems, 0)

        xv = x_ref[...]
        r = lax.rsqrt(_mean(xv * xv) + EPS)
        hv = (xv * r) * g_ref[0:1, :] * (1.0 + mod_ref[4:5, :]) + mod_ref[3:4, :]
        hb = hv.astype(act_dtype)
        h2_ref[...] = hb

        @pl.when(first)
        def _():
            car_ref[...] = jnp.zeros_like(car_ref)

        @pl.when(i == 0)
        def _():
            for n in range(n_chunks):
                ext_ref[n, p + tm:p + tm + 8, :] = jnp.zeros((8, fc), F32)

        first_sublane = _sublane_is(p, 0)

        def project(n):
            un = _mm(hb, up_ref[n])
            ext_ref[n, p:p + tm, :] = un
            u2_ref[n] = un.astype(act_dtype)
            ext_ref[n, 0:p, :] = jnp.where(first_sublane, car_ref[n], ext_ref[n, tm - 1:tm - 1 + p, :])
            car_ref[n] = ext_ref[n, tm + 7:tm + 7 + p, :]

        def conv(n):
            return (fcw_ref[n, 3:4, :] + fcw_ref[n, 0:1, :] * ext_ref[n, p - 16:p - 16 + tm, :]
                    + fcw_ref[n, 1:2, :] * ext_ref[n, p - 8:p - 8 + tm, :]
                    + fcw_ref[n, 2:3, :] * ext_ref[n, p:p + tm, :])

        for n in range(n_chunks):
            project(n)
        o2 = jnp.zeros((tm, d), F32)
        for n in range(half):
            gt = conv(n + half)
            v = conv(n)
            sg = _sigmoid(gt)
            silu = gt * sg
            u3_ref[n] = silu.astype(act_dtype)
            u3_ref[n + half] = (v * (sg * (1.0 + gt * (1.0 - sg)))).astype(act_dtype)
            hid = (silu * v).astype(act_dtype)
            hid_ref[n] = hid
            o2 = o2 + _mm(hid, down_ref[n])
        o2_ref[...] = o2
        ro = lax.rsqrt(_mean(o2 * o2) + EPS)
        x2_ref[...] = xv + (1.0 + mod_ref[5:6, :]) * ((o2 * ro) * g_ref[1:2, :])

        if n_pay:
            @pl.when(i == max(nt - 2, 0))
            def _():
                _gather_phase(_layer_of(pay_refs, [gather_layer] * n_pay), got_refs, *sems, 1)

            @pl.when(i == nt - 1)
            def _():
                _gather_phase(_layer_of(pay_refs, [gather_layer] * n_pay), got_refs, *sems, 2)

    row = lambda width: pl.BlockSpec((tm, width), lambda i: (i, 0))
    chunked = lambda n: pl.BlockSpec((n, tm, fc), lambda i: (0, i, 0))
    return _carrier_call(
        body, f"ffn_fwd_{l}", (nt,),
        out_shape=[jax.ShapeDtypeStruct((t, d), F32), jax.ShapeDtypeStruct((t, d), act_dtype),
                   jax.ShapeDtypeStruct((n_chunks, t, fc), act_dtype),
                   jax.ShapeDtypeStruct((n_chunks, t, fc), act_dtype),
                   jax.ShapeDtypeStruct((half, t, fc), act_dtype), jax.ShapeDtypeStruct((t, d), F32)],
        in_specs=[row(d),
                  pl.BlockSpec((None, None, 8, d), lambda i: (l, i // tps, 0, 0)),
                  _const_spec((None, 8, d), (l, 0, 0)),
                  _const_spec((n_chunks, d, fc), (0, 0, 0)),
                  _const_spec((n_chunks, 8, fc), (0, 0, 0)),
                  _const_spec((half, fc, d), (0, 0, 0))],
        out_specs=[row(d), row(d), chunked(n_chunks), chunked(n_chunks), chunked(half), row(d)],
        scratch=[pltpu.VMEM((n_chunks, p + tm + 8, fc), F32), pltpu.VMEM((n_chunks, p, fc), F32)],
        operands=(x, mod, gains, up, fcw, down),
        payload=payload, gather_layer=gather_layer)


def _ffn_bwd(l, dx2, x, mod, gains, u2, u3, o2, up, fcw, down, *, seq, tm, act_dtype,
             payload=(), recvs=(), rows=(), pay_layer=0):
    t, d = x.shape
    n_chunks, _, fc = up.shape
    half = n_chunks // 2
    nt, tps = t // tm, seq // tm
    p = FFN_PREFIX
    n_pay = len(payload)

    def body(*refs):
        ins, outs, scr, pay_refs, recv_refs, sems = _carried_refs(refs, 10, 6, 2, n_pay, aliased=True)
        dx2_ref, x_ref, mod_ref, g_ref, u2_ref, u3_ref, o2_ref, up_ref, fcw_ref, down_ref = ins
        dx1_ref, du2_ref, do2_ref, dmod_ref, gd_ref, dfcw_ref = outs
        fext_ref, fcar_ref = scr
        i = pl.program_id(0)
        j = nt - 1 - i
        last_in_seq = (j % tps) == tps - 1

        if n_pay:
            @pl.when(i == 0)
            def _():
                _exchange(pay_refs, recv_refs, rows, pay_layer, *sems, start=True)

        @pl.when(i == 0)
        def _():
            gd_ref[...] = jnp.zeros_like(gd_ref)
            dfcw_ref[...] = jnp.zeros_like(dfcw_ref)
            for n in range(n_chunks):
                fext_ref[n, 0:8, :] = jnp.zeros((8, fc), F32)

        @pl.when(last_in_seq)
        def _():
            dmod_ref[...] = jnp.zeros_like(dmod_ref)
            fcar_ref[...] = jnp.zeros_like(fcar_ref)

        xv = x_ref[...]
        dx2v = dx2_ref[...]
        pre_g, post_g = g_ref[0:1, :], g_ref[1:2, :]

        ov = o2_ref[...]
        ro = lax.rsqrt(_mean(ov * ov) + EPS)
        yo = ov * ro
        dmod_ref[2:3, :] += _colsum(dx2v * (yo * post_g))
        dn = dx2v * (1.0 + mod_ref[5:6, :])
        gd_ref[1:2, :] += _colsum(dn * yo)
        dyo = dn * post_g
        do = ro * (dyo - yo * _mean(dyo * yo))
        dob = do.astype(act_dtype)
        do2_ref[...] = dob

        for n in range(half):
            dhid = _mm_nt(dob, down_ref[n])
            fext_ref[n, 8:8 + tm, :] = dhid * u3_ref[n].astype(F32)
            fext_ref[n + half, 8:8 + tm, :] = dhid * u3_ref[n + half].astype(F32)

        last_sublane = _sublane_is(p, 7)
        dh = jnp.zeros((tm, d), F32)
        for n in range(n_chunks):
            fext_ref[n, 8 + tm:8 + tm + p, :] = jnp.where(last_sublane, fcar_ref[n], fext_ref[n, 9:9 + p, :])
            fcar_ref[n] = fext_ref[n, 1:1 + p, :]
            d2 = fext_ref[n, 8:8 + tm, :]
            d1 = fext_ref[n, 16:16 + tm, :]
            d0 = fext_ref[n, 24:24 + tm, :]
            u2v = u2_ref[n].astype(F32)
            dfcw_ref[n, 3:4, :] += _colsum(d2)
            dfcw_ref[n, 0:1, :] += _colsum(d0 * u2v)
            dfcw_ref[n, 1:2, :] += _colsum(d1 * u2v)
            dfcw_ref[n, 2:3, :] += _colsum(d2 * u2v)
            du2 = (fcw_ref[n, 0:1, :] * d0 + fcw_ref[n, 1:2, :] * d1 + fcw_ref[n, 2:3, :] * d2).astype(act_dtype)
            du2_ref[n] = du2
            dh = dh + _mm_nt(du2, up_ref[n])

        r = lax.rsqrt(_mean(xv * xv) + EPS)
        xn = xv * r
        dmod_ref[0:1, :] += _colsum(dh)
        dmod_ref[1:2, :] += _colsum(dh * (xn * pre_g))
        dy = dh * (1.0 + mod_ref[4:5, :])
        gd_ref[0:1, :] += _colsum(dy * xn)
        dxn = dy * pre_g
        dx1_ref[...] = dx2v + r * (dxn - xn * _mean(dxn * xn))

        if n_pay:
            @pl.when(i == nt - 1)
            def _():
                _exchange(pay_refs, recv_refs, rows, pay_layer, *sems, start=False)

    rev = lambda width: pl.BlockSpec((tm, width), lambda i: (nt - 1 - i, 0))
    chunked = pl.BlockSpec((n_chunks, tm, fc), lambda i: (0, nt - 1 - i, 0))
    return _carrier_call(
        body, f"ffn_bwd_{l}", (nt,),
        out_shape=[jax.ShapeDtypeStruct((t, d), F32), jax.ShapeDtypeStruct((n_chunks, t, fc), act_dtype),
                   jax.ShapeDtypeStruct((t, d), act_dtype),
                   jax.ShapeDtypeStruct((t // seq, 8, d), F32), jax.ShapeDtypeStruct((8, d), F32),
                   jax.ShapeDtypeStruct((n_chunks, 8, fc), F32)],
        in_specs=[rev(d), rev(d),
                  pl.BlockSpec((None, None, 8, d), lambda i: (l, (nt - 1 - i) // tps, 0, 0)),
                  _const_spec((None, 8, d), (l, 0, 0)),
                  chunked, chunked, rev(d),
                  _const_spec((n_chunks, d, fc), (0, 0, 0)),
                  _const_spec((n_chunks, 8, fc), (0, 0, 0)),
                  _const_spec((half, fc, d), (0, 0, 0))],
        out_specs=[rev(d), chunked, rev(d),
                   pl.BlockSpec((None, 8, d), lambda i: ((nt - 1 - i) // tps, 0, 0)),
                   _acc_spec((8, d), (0, 0)), _acc_spec((n_chunks, 8, fc), (0, 0, 0))],
        scratch=[pltpu.VMEM((n_chunks, 8 + tm + p, fc), F32), pltpu.VMEM((n_chunks, p, fc), F32)],
        operands=(dx2, x, mod, gains, u2, u3, o2, up, fcw, down),
        payload=payload, recvs=recvs)


def _pad_rows(a, rows):
    pad = [(0, 0)] * a.ndim
    pad[-2] = (0, rows - a.shape[-2])
    return jnp.pad(a, pad)


def kernel(x, c, ada_w, ada_b, pre_mix_g, post_mix_g, w_in, conv_w, conv_b, conv_ln_g, conv_ln_b, pool_w, pool_scale, w_out, pre_ffn_g, post_ffn_g, ffn_up, ffn_conv_w, ffn_conv_b, ffn_down, loss_target, m_ada_w, m_ada_b, m_pre_mix_g, m_post_mix_g, m_w_in, m_conv_w, m_conv_b, m_conv_ln_g, m_conv_ln_b, m_pool_w, m_pool_scale, m_w_out, m_pre_ffn_g, m_post_ffn_g, m_ffn_up, m_ffn_conv_w, m_ffn_conv_b, m_ffn_down, v_ada_w, v_ada_b, v_pre_mix_g, v_post_mix_g, v_w_in, v_conv_w, v_conv_b, v_conv_ln_g, v_conv_ln_b, v_pool_w, v_pool_scale, v_w_out, v_pre_ffn_g, v_post_ffn_g, v_ffn_up, v_ffn_conv_w, v_ffn_conv_b, v_ffn_down):
    bl, seq, d = x.shape
    n_layers = ada_w.shape[0]
    t = bl * seq
    dc = conv_b.shape[1]
    d_in = w_in.shape[2] * N_DEV
    d_mix = w_out.shape[1] * N_DEV
    n_taps = conv_w.shape[1]
    fc = ffn_up.shape[2]
    half = N_DEV // 2
    ada_cols = ada_w.shape[2]
    n_mod = ada_cols * N_DEV // d
    assert pool_scale.shape[1] == dc and n_taps == 31 and n_mod == 6 and ffn_conv_w.shape[1] == 3
    assert pool_w.shape[1:] == (len(POOL_WINDOWS), POOL_GROUP, POOL_GROUP)
    tm = TILE_TOKENS
    assert seq % tm == 0 and CONV_PREFIX <= tm - 8
    tk = 2048 if t % 2048 == 0 else tm
    act = MXU_DTYPE

    def tile_order(a, inverse=False):
        shape = (t // tm, tm // 8, 8, d) if inverse else (t // tm, 8, tm // 8, d)
        return a.reshape(shape).transpose(0, 2, 1, 3).reshape(t, d)

    ax = lax.axis_index
    me = 4 * ax("x") + 2 * ax("y") + ax("c")

    (c_all,) = _all_gather([c], "gather_c")
    c_all = c_all.reshape(N_DEV * bl, d)
    ada_b_cols = lax.dynamic_slice_in_dim(ada_b, me * ada_cols, ada_cols, axis=1)
    mod_cols = _ada_fwd(c_all, ada_w, ada_b_cols)

    w_in_s = w_in.astype(act).transpose(0, 2, 1)
    w_out_s, up_s, down_s = w_out.astype(act), ffn_up.astype(act), ffn_down.astype(act)
    conv_w_s = _pad_rows(conv_w, 32)
    fcw_s = _pad_rows(jnp.concatenate(
        [ffn_conv_w, lax.dynamic_slice_in_dim(ffn_conv_b, me * fc, fc, axis=1)[:, None, :]], axis=1), 8)

    mixer_shards = (w_in_s, w_out_s, conv_w_s)
    late_shards = (down_s, fcw_s)

    def mixer_weights(g_w_in, g_w_out, g_conv_w):
        return (g_w_in.reshape(d_in, d), g_conv_w.transpose(1, 0, 2).reshape(32, dc), g_w_out.reshape(d_mix, d))

    def late_weights(g_down, g_fcw):
        return (g_fcw, g_down.reshape(half, fc, d))

    g0 = _all_gather([mod_cols, *mixer_shards], "gather_w0", layers=[None, 0, 0, 0])
    mod_all = g0[0].transpose(1, 2, 0, 3).reshape(n_layers, N_DEV * bl, n_mod, d)
    mod = _pad_rows(lax.dynamic_slice_in_dim(mod_all, me * bl, bl, axis=1), 8)
    mixers, lates, ups = [mixer_weights(*g0[1:])], [], []
    gains_mix = _pad_rows(jnp.stack([pre_mix_g, post_mix_g], axis=1), 8)
    gains_ffn = _pad_rows(jnp.stack([pre_ffn_g, post_ffn_g], axis=1), 8)
    vec = _pad_rows(jnp.stack([conv_b, conv_ln_g, conv_ln_b, pool_scale], axis=1), 8)
    pool_w_b = pool_w.astype(act)

    kw = dict(seq=seq, tm=tm, act_dtype=act)
    xs = tile_order(x.reshape(t, d))
    saved = []
    for l in range(n_layers):
        w_in_t, cw, w_o = mixers[l]
        (x1, h1, u, a1, ap, o, tails), got = _mixer_fwd(
            l, xs, mod, gains_mix, w_in_t, cw, vec, pool_w_b, w_o, **kw,
            payload=(up_s, *late_shards) if l == 0 else (up_s,), gather_layer=l)
        ups.append(got[0])
        if l == 0:
            lates.append(late_weights(*got[1:]))
        up_l, (fcw_l, down_l) = ups[l], lates[l]
        (x2, h2, u2, u3, hid, o2), got = _ffn_fwd(l, x1, mod, gains_ffn, up_l, fcw_l, down_l, **kw,
                                                  payload=(*mixer_shards, *late_shards) if l + 1 < n_layers else (),
                                                  gather_layer=l + 1)
        if got:
            mixers.append(mixer_weights(*got[:3]))
            lates.append(late_weights(*got[3:]))
        saved.append((xs, h1, u, tails, a1, ap, o, x1, h2, u2, u3, hid, o2))
        xs = x2

    dx, loss_part = _loss_grad(xs, tile_order(loss_target.reshape(t, d)), tm)

    def landing(shard, dtype):
        return lax.empty((N_DEV, n_layers) + shard.shape[1:], dtype)

    r_up, r_down = landing(ffn_up, act), landing(ffn_down, act)
    r_w_in, r_w_out = landing(w_in, act), landing(w_out, act)
    r_conv_w, r_fcw = landing(conv_w, F32), landing(ffn_conv_w, F32)
    def cuts(l):
        first, second = (7, 8.8) if l == 0 else (2.5, 6)
        return int(first * d / 10) // 16 * 16, int(second * d / 10) // 16 * 16

    dmods, smalls = [], []
    pending = None
    for l in reversed(range(n_layers)):
        xin, h1, u, tails, a1, ap, o, x1, h2, u2, u3, hid, o2 = saved[l]
        (w_in_t, cw, w_o), (fcw_l, down_l), up_l = mixers[l], lates[l], ups[l]
        a, b = cuts(l + 1) if pending else (0, 0)
        carried = dict(pay_layer=l + 1)
        (dx1, du2, do2, dmod_b, gd_b, dfcw), got = _ffn_bwd(
            l, dx, x1, mod, gains_ffn, u2, u3, o2, up_l, fcw_l, down_l, **kw, **carried,
            payload=(pending["up"], pending["w_in"], pending["conv_w"], pending["fcw"]) if pending else (),
            recvs=(r_up, r_w_in, r_conv_w, r_fcw) if pending else (), rows=((a, b - a), None, None, None))
        if pending:
            r_up, r_w_in, r_conv_w, r_fcw = got
        p_up, got = _matmul_tn(f"dw_up_{l}", h2[None], du2, tk, act, **carried,
                               payload=(pending["up"],) if pending else (), recvs=(r_up,) if pending else (),
                               rows=((b, d - b),))
        if pending:
            (r_up,) = got
        p_down, got = _matmul_tn(f"dw_down_{l}", hid, do2[None], tk, act, **carried,
                                 payload=(pending["w_out"],) if pending else (),
                                 recvs=(r_w_out,) if pending else (), rows=(None,))
        if pending:
            (r_w_out,) = got
        p_down = p_down.reshape(N_DEV, fc // 2, d)
        a, b = cuts(l)
        (dx, du, do, dmod_a, gd_a, gv, dcw, dpw), (r_down, r_up) = _mixer_bwd(
            l, dx1, xin, mod, gains_mix, u, tails, a1, o, w_in_t, cw, vec, pool_w_b, w_o, **kw,
            payload=(p_down, p_up), recvs=(r_down, r_up), rows=(None, (0, a)), pay_layer=l)
        dmods.append(jnp.concatenate([dmod_a[:, 0:3], dmod_b[:, 0:3]], axis=1).reshape(bl, n_mod * d))
        smalls.append(jnp.concatenate(
            [gd_a[0], gd_a[1], gv[0], gv[1], gv[2], gv[3], gd_b[0], gd_b[1], dfcw[:, 3, :].reshape(-1),
             dpw.reshape(-1)]))
        if l > 0:
            dw_in, _ = _matmul_tn(f"dw_in_{l}", h1[None], du[None], tk, act)
            dw_out, _ = _matmul_tn(f"dw_out_{l}", ap[None], do[None], tk, act)
        else:
            dmod_loc = jnp.stack(dmods[::-1])
            small_loc = jnp.stack(smalls[::-1])
            n_small = small_loc.shape[1]
            small_cols = 8 * LANES if (n_layers * n_small) % (8 * LANES) == 0 else LANES
            dw_in, (g_dmod, g_small, g_loss) = _matmul_tn(
                "dw_in_0", h1[None], du[None], tk, act,
                gather=(dmod_loc.astype(act), small_loc.reshape(-1, small_cols).astype(act), loss_part))
            dw_out, (r_up,) = _matmul_tn("dw_out_0", ap[None], do[None], tk, act, payload=(p_up,), recvs=(r_up,),
                                         rows=((a, d - a),), pay_layer=0)
        pending = dict(up=p_up, w_in=dw_in[0].reshape(d, N_DEV, d_in // N_DEV).transpose(1, 0, 2),
                       w_out=dw_out[0].reshape(N_DEV, d_mix // N_DEV, d),
                       conv_w=dcw[:n_taps].reshape(n_taps, N_DEV, dc // N_DEV).transpose(1, 0, 2),
                       fcw=dfcw[:, :3, :])

    g_dmod = g_dmod.astype(F32)
    loss = _sum_parts(g_loss)[0, 0]

    def flat2(a):
        return a.reshape(-1, a.shape[-1])

    def update(name, parts, w, m, v, **carried):
        outs, got = _adam_reduce(name, parts.reshape(parts.shape[0], -1, w.shape[-1]), flat2(w), flat2(m), flat2(v),
                                 **carried)
        return [o_.reshape(w.shape) for o_ in outs], got

    dmod_all = g_dmod.transpose(1, 0, 2, 3).reshape(n_layers, N_DEV * bl, n_mod * d)
    dmod_cols = lax.dynamic_slice_in_dim(dmod_all, me * ada_cols, ada_cols, axis=2)
    res = {}
    res["ada_w"], (r_w_in, r_conv_w, r_fcw) = _ada_bwd(
        c_all.T, dmod_cols, ada_w, m_ada_w, v_ada_w, payload=(pending["w_in"], pending["conv_w"], pending["fcw"]),
        recvs=(r_w_in, r_conv_w, r_fcw), rows=(None, None, None), pay_layer=0)
    res["ffn_down"], (r_w_out,) = update("adam_ffn_down", r_down, ffn_down, m_ffn_down, v_ffn_down,
                                         payload=(pending["w_out"],), recvs=(r_w_out,), rows=(None,), pay_layer=0)
    res["ada_b"], _ = update("adam_ada_b", dmod_all.transpose(1, 0, 2), ada_b, m_ada_b, v_ada_b)
    res["w_in"], _ = update("adam_w_in", r_w_in, w_in, m_w_in, v_w_in)
    res["w_out"], _ = update("adam_w_out", r_w_out, w_out, m_w_out, v_w_out)
    res["ffn_up"], _ = update("adam_ffn_up", r_up, ffn_up, m_ffn_up, v_ffn_up)
    res["conv_w"], _ = update("adam_conv_w", r_conv_w, conv_w, m_conv_w, v_conv_w)
    res["ffn_conv_w"], _ = update("adam_ffn_conv_w", r_fcw, ffn_conv_w, m_ffn_conv_w, v_ffn_conv_w)

    small_names = ["pre_mix_g", "post_mix_g", "conv_b", "conv_ln_g", "conv_ln_b", "pool_scale", "pre_ffn_g",
                   "post_ffn_g", "ffn_conv_b", "pool_w"]
    small_w = [pre_mix_g, post_mix_g, conv_b, conv_ln_g, conv_ln_b, pool_scale, pre_ffn_g, post_ffn_g,
               ffn_conv_b, pool_w]
    small_m = [m_pre_mix_g, m_post_mix_g, m_conv_b, m_conv_ln_g, m_conv_ln_b, m_pool_scale, m_pre_ffn_g,
               m_post_ffn_g, m_ffn_conv_b, m_pool_w]
    small_v = [v_pre_mix_g, v_post_mix_g, v_conv_b, v_conv_ln_g, v_conv_ln_b, v_pool_scale, v_pre_ffn_g,
               v_post_ffn_g, v_ffn_conv_b, v_pool_w]

    def pack(arrs):
        return jnp.concatenate([a.reshape(n_layers, -1) for a in arrs], axis=1).reshape(-1, small_cols)

    outs, _ = _adam_reduce("adam_small", g_small, pack(small_w), pack(small_m), pack(small_v))
    outs = [o_.reshape(n_layers, n_small) for o_ in outs]
    off = 0
    for name, w in zip(small_names, small_w):
        size = w[0].size
        res[name] = [o_[:, off:off + size].reshape(w.shape) for o_ in outs]
        off += size

    order = ["ada_w", "ada_b", "pre_mix_g", "post_mix_g", "w_in", "conv_w", "conv_b", "conv_ln_g", "conv_ln_b",
             "pool_w", "pool_scale", "w_out", "pre_ffn_g", "post_ffn_g", "ffn_up", "ffn_conv_w", "ffn_conv_b",
             "ffn_down"]
    grad_x = tile_order(dx, inverse=True).reshape(bl, seq, d)
    return (loss, grad_x, *[res[n][0] for n in order], *[res[n][1] for n in order],
            *[res[n][2] for n in order], *[res[n][3] for n in order])
```

```python
import jax
import jax.numpy as jnp
from jax import lax
from jax.experimental import pallas as pl
from jax.experimental.pallas import tpu as pltpu

N_DEV = 8
EPS = 1e-6
POOL_WINDOWS = (2, 4, 8, 16)
POOL_GROUP = 128
TILE_TOKENS = 256
CONV_PREFIX = 8 * 30
FFN_PREFIX = 8 * 2
LANES = 128
ROW_CHUNK = 128
MXU_DTYPE = jnp.bfloat16
VMEM_LIMIT = 60 * 1024 * 1024

ADAM_LR = 0.001
ADAM_B1 = 0.9
ADAM_B2 = 0.999
ADAM_EPS = 1e-08
ADAM_WD = 0.01
ADAM_STEP = 10

MESH = pl.DeviceIdType.MESH
F32 = jnp.float32


def _mm(a, b):
    return jnp.dot(a.astype(MXU_DTYPE), b.astype(MXU_DTYPE), preferred_element_type=F32)


def _mm_nt(a, b):
    return lax.dot_general(a.astype(MXU_DTYPE), b.astype(MXU_DTYPE), (((1,), (1,)), ((), ())),
                           preferred_element_type=F32)


def _mm_tn(a, b):
    return lax.dot_general(a.astype(MXU_DTYPE), b.astype(MXU_DTYPE), (((0,), (0,)), ((), ())),
                           preferred_element_type=F32)


def _mean(v):
    return jnp.mean(v, axis=-1, keepdims=True)


def _colsum(v):
    return jnp.sum(v, axis=0, keepdims=True)


def _sigmoid(v):
    return jax.nn.sigmoid(v)


def _params(n_grid=1):
    return pltpu.CompilerParams(dimension_semantics=("arbitrary",) * n_grid, vmem_limit_bytes=VMEM_LIMIT)


def _const_spec(shape, index):
    return pl.BlockSpec(shape, lambda *_: index, pipeline_mode=pl.Buffered(1))


def _acc_spec(shape, index):
    return pl.BlockSpec(shape, lambda *_: index)


def _position():
    x, y, c = lax.axis_index("x"), lax.axis_index("y"), lax.axis_index("c")
    return x, y, c


def _gather_phase(ins, outs, send_sems, recv_sems, local_sems, phase):
    n = len(ins)
    x, y, c = _position()
    me, sibling = (x, y, c), (x, y, 1 - c)
    chips = [(1 - x, y), (x, 1 - y), (1 - x, 1 - y)]

    def slot(k, px, py, pc):
        return outs[k].at[4 * px + 2 * py + pc]

    def copy(k, s, block, to, src=None):
        return pltpu.make_async_remote_copy(
            src_ref=slot(k, *block) if src is None else src, dst_ref=slot(k, *block),
            send_sem=send_sems.at[k, s], recv_sem=recv_sems.at[k, s], device_id=to, device_id_type=MESH)

    mine = [pltpu.make_async_copy(ins[k], slot(k, *me), local_sems.at[k]) for k in range(n)]
    first = []
    for k in range(n):
        first.append(copy(k, 0, me, sibling, src=ins[k]))
        first += [copy(k, 1 + j, me, (*chip, c), src=ins[k]) for j, chip in enumerate(chips)]
    passed = [copy(k, 4 + j, (*chip, c), sibling) for j, chip in enumerate(chips) for k in range(n)]
    if phase == 0:
        for cp in mine + first:
            cp.start()
    elif phase == 1:
        for j, chip in enumerate(chips):
            for k in range(n):
                copy(k, 1 + j, (*chip, c), me).wait_recv()
                copy(k, 4 + j, (*chip, c), sibling).start()
    else:
        for k in range(n):
            copy(k, 0, sibling, me).wait_recv()
            for j, chip in enumerate(chips):
                copy(k, 4 + j, (*chip, 1 - c), me).wait_recv()
        for cp in first + passed:
            cp.wait_send()
        for cp in mine:
            cp.wait()


def _gather_scratch(n):
    return [pltpu.SemaphoreType.DMA((n, 7)), pltpu.SemaphoreType.DMA((n, 7)), pltpu.SemaphoreType.DMA((n,))]


def _layer_of(refs, layers):
    return [r if lay is None else r.at[lay] for r, lay in zip(refs, layers)]


def _gathered_shapes(arrs, layers):
    return [jax.ShapeDtypeStruct((N_DEV,) + (a.shape if lay is None else a.shape[1:]), a.dtype)
            for a, lay in zip(arrs, layers)]


def _all_gather(arrs, name, layers=None):
    n = len(arrs)
    layers = [None] * n if layers is None else layers

    def body(*refs):
        for phase in range(3):
            _gather_phase(_layer_of(refs[:n], layers), refs[n:2 * n], *refs[2 * n:], phase)

    any_spec = pl.BlockSpec(memory_space=pl.ANY)
    return pl.pallas_call(
        body, name=name,
        out_shape=_gathered_shapes(arrs, layers),
        in_specs=[any_spec] * n, out_specs=[any_spec] * n,
        scratch_shapes=_gather_scratch(n),
    )(*arrs)


def _exchange(payload, recvs, rows, layer, send_sems, recv_sems, local_sems, start):
    x, y, c = _position()
    me = 4 * x + 2 * y + c
    for k, (src, recv) in enumerate(zip(payload, recvs)):
        def block(ref, *index):
            return ref.at[index] if rows[k] is None else ref.at[(*index, pl.ds(*rows[k]))]

        local = pltpu.make_async_copy(block(src, me), block(recv, me, layer), local_sems.at[k])
        if start:
            local.start()
        else:
            local.wait()
        for j in range(1, N_DEV):
            px = (1 - x) if (j & 4) else x
            py = (1 - y) if (j & 2) else y
            pc = (1 - c) if (j & 1) else c
            peer = 4 * px + 2 * py + pc
            landing = block(recv, me, layer) if start else block(recv, peer, layer)
            cp = pltpu.make_async_remote_copy(
                src_ref=block(src, peer), dst_ref=landing, send_sem=send_sems.at[k, j - 1],
                recv_sem=recv_sems.at[k, j - 1], device_id=(px, py, pc), device_id_type=MESH)
            if start:
                cp.start()
            else:
                cp.wait()


def _exchange_scratch(n):
    return [pltpu.SemaphoreType.DMA((n, N_DEV - 1)), pltpu.SemaphoreType.DMA((n, N_DEV - 1)),
            pltpu.SemaphoreType.DMA((n,))]


def _exchange_and_gather(payload, recvs, rows, layer, arrs, name):
    n, m = len(payload), len(arrs)

    def body(*refs):
        pay, srcs = refs[:n], refs[2 * n:2 * n + m]
        outs, got = refs[2 * n + m:3 * n + m], refs[3 * n + m:3 * n + 2 * m]
        xsems, gsems = refs[3 * n + 2 * m:3 * n + 2 * m + 3], refs[3 * n + 2 * m + 3:]
        _exchange(pay, outs, rows, layer, *xsems, start=True)
        for phase in range(3):
            _gather_phase(srcs, got, *gsems, phase)
        _exchange(pay, outs, rows, layer, *xsems, start=False)

    any_spec = pl.BlockSpec(memory_space=pl.ANY)
    res = pl.pallas_call(
        body, name=name,
        out_shape=[jax.ShapeDtypeStruct(r.shape, r.dtype) for r in recvs]
        + [jax.ShapeDtypeStruct((N_DEV,) + a.shape, a.dtype) for a in arrs],
        in_specs=[any_spec] * (2 * n + m), out_specs=[any_spec] * (n + m),
        input_output_aliases={n + k: k for k in range(n)},
        scratch_shapes=_exchange_scratch(n) + _gather_scratch(m),
    )(*payload, *recvs, *arrs)
    return res[:n], res[n:]


def _carried_refs(refs, n_in, n_out, n_scratch, n_pay, aliased):
    ins = refs[:n_in]
    pay = refs[n_in:n_in + n_pay]
    o0 = n_in + (2 * n_pay if aliased else n_pay)
    outs = refs[o0:o0 + n_out]
    recvs = refs[o0 + n_out:o0 + n_out + n_pay]
    s0 = o0 + n_out + n_pay
    return ins, outs, refs[s0:s0 + n_scratch], pay, recvs, refs[s0 + n_scratch:]


def _carrier_call(body, name, grid, in_specs, out_specs, out_shape, scratch, operands, payload, recvs=None,
                  gather_layer=None):
    n_in, n_out, n_pay = len(in_specs), len(out_specs), len(payload)
    any_spec = pl.BlockSpec(memory_space=pl.ANY)
    if recvs is None:
        landing = _gathered_shapes(payload, [gather_layer] * n_pay)
        extra_in, aliases = list(payload), {}
        sems = _gather_scratch(n_pay) if n_pay else []
    else:
        landing = [jax.ShapeDtypeStruct(r.shape, r.dtype) for r in recvs]
        extra_in = list(payload) + list(recvs)
        aliases = {n_in + n_pay + k: n_out + k for k in range(n_pay)}
        sems = _exchange_scratch(n_pay) if n_pay else []
    res = pl.pallas_call(
        body, name=name, grid=grid,
        out_shape=list(out_shape) + landing,
        in_specs=list(in_specs) + [any_spec] * len(extra_in),
        out_specs=list(out_specs) + [any_spec] * n_pay,
        input_output_aliases=aliases,
        scratch_shapes=list(scratch) + sems,
        compiler_params=_params(len(grid)),
    )(*operands, *extra_in)
    return res[:n_out], res[n_out:]


def _adamw(w, g, m, v):
    m = ADAM_B1 * m + (1.0 - ADAM_B1) * g
    v = ADAM_B2 * v + (1.0 - ADAM_B2) * jnp.square(g)
    m_hat = m / (1.0 - ADAM_B1 ** ADAM_STEP)
    v_hat = v / (1.0 - ADAM_B2 ** ADAM_STEP)
    delta = -ADAM_LR * (m_hat / (jnp.sqrt(v_hat) + ADAM_EPS) + ADAM_WD * w)
    return delta, m, v


def _ada_fwd(c_all, ada_w, ada_b_cols):
    n_layers, d, cols = ada_w.shape
    b = c_all.shape[0]

    def body(c_ref, w_ref, b_ref, o_ref):
        cv = c_ref[...]
        act = cv * _sigmoid(cv)
        o_ref[...] = jnp.dot(act, w_ref[...], preferred_element_type=F32,
                             precision=lax.Precision.HIGHEST) + b_ref[...]

    return pl.pallas_call(
        body, name="ada_fwd", grid=(n_layers,),
        out_shape=jax.ShapeDtypeStruct((n_layers, b, cols), F32),
        in_specs=[pl.BlockSpec((b, d), lambda l: (0, 0)),
                  pl.BlockSpec((None, d, cols), lambda l: (l, 0, 0)),
                  pl.BlockSpec((None, 1, cols), lambda l: (l, 0, 0))],
        out_specs=pl.BlockSpec((None, b, cols), lambda l: (l, 0, 0)),
        compiler_params=_params(1),
    )(c_all, ada_w, ada_b_cols.reshape(n_layers, 1, cols))


def _ada_bwd(c_all_t, dmod_cols, w, m, v):
    n_layers, d, cols = w.shape
    b = c_all_t.shape[1]
    td = 256 if d % 256 == 0 else d

    def body(c_ref, dm_ref, w_ref, m_ref, v_ref, g_ref, dl_ref, nm_ref, nv_ref):
        cv = c_ref[...]
        act = cv * _sigmoid(cv)
        g = jnp.dot(act, dm_ref[...], preferred_element_type=F32, precision=lax.Precision.HIGHEST)
        delta, nm, nv = _adamw(w_ref[...], g, m_ref[...], v_ref[...])
        g_ref[...] = g
        dl_ref[...] = delta
        nm_ref[...] = nm
        nv_ref[...] = nv

    blk = pl.BlockSpec((None, td, cols), lambda l, i: (l, i, 0))
    shp = jax.ShapeDtypeStruct(w.shape, F32)
    return pl.pallas_call(
        body, name="ada_bwd", grid=(n_layers, d // td),
        out_shape=[shp] * 4,
        in_specs=[pl.BlockSpec((td, b), lambda l, i: (i, 0)),
                  pl.BlockSpec((None, b, cols), lambda l, i: (l, 0, 0)), blk, blk, blk],
        out_specs=[blk] * 4,
        compiler_params=_params(2),
    )(c_all_t, dmod_cols, w, m, v)


def _row_tile(rows, cols, budget=128 * 1024, step=8):
    best = None
    for t in range(step, rows + 1, step):
        if rows % t == 0 and t * cols <= budget:
            best = t
    return best if best is not None else rows


def _adam_reduce(name, parts, w, m, v):
    p, rows, cols = parts.shape
    tr = _row_tile(rows, cols, budget=(256 * 1024) // max(1, p // 4), step=8 if parts.dtype == F32 else 16)

    def body(p_ref, w_ref, m_ref, v_ref, g_ref, dl_ref, nm_ref, nv_ref):
        g = p_ref[0].astype(F32)
        for k in range(1, p):
            g = g + p_ref[k].astype(F32)
        delta, nm, nv = _adamw(w_ref[...], g, m_ref[...], v_ref[...])
        g_ref[...] = g
        dl_ref[...] = delta
        nm_ref[...] = nm
        nv_ref[...] = nv

    blk = pl.BlockSpec((tr, cols), lambda i: (i, 0))
    shp = jax.ShapeDtypeStruct((rows, cols), F32)
    return pl.pallas_call(
        body, name=name, grid=(rows // tr,),
        out_shape=[shp] * 4,
        in_specs=[pl.BlockSpec((p, tr, cols), lambda i: (0, i, 0)), blk, blk, blk],
        out_specs=[blk] * 4,
        compiler_params=_params(1),
    )(parts, w, m, v)


def _sum_parts(parts):
    p = parts.shape[0]

    def body(p_ref, o_ref):
        acc = p_ref[0]
        for k in range(1, p):
            acc = acc + p_ref[k]
        o_ref[...] = acc

    return pl.pallas_call(body, name="loss_sum", out_shape=jax.ShapeDtypeStruct(parts.shape[1:], F32))(parts)


def _loss_grad(y, target, tm):
    t, d = y.shape

    def body(y_ref, t_ref, dy_ref, loss_ref):
        @pl.when(pl.program_id(0) == 0)
        def _():
            loss_ref[...] = jnp.zeros_like(loss_ref)

        diff = y_ref[...] - t_ref[...]
        dy_ref[...] = diff / d
        part = 0.5 * jnp.sum(_mean(diff * diff), axis=0, keepdims=True)
        loss_ref[...] += jnp.broadcast_to(part, loss_ref.shape)

    blk = pl.BlockSpec((tm, d), lambda i: (i, 0))
    return pl.pallas_call(
        body, name="loss_grad", grid=(t // tm,),
        out_shape=[jax.ShapeDtypeStruct((t, d), F32), jax.ShapeDtypeStruct((8, LANES), F32)],
        in_specs=[blk, blk], out_specs=[blk, pl.BlockSpec((8, LANES), lambda i: (0, 0))],
        compiler_params=_params(1),
    )(y, target)


def _matmul_tn(name, a, b, tk, out_dtype, payload=(), recvs=(), rows=(), pay_layer=0):
    ga, t, m = a.shape
    gb, _, n = b.shape
    g = max(ga, gb)
    n_k = t // tk
    n_pay = len(payload)

    def body(*refs):
        (a_ref, b_ref), (o_ref,), (acc_ref,), pay_refs, recv_refs, sems = _carried_refs(
            refs, 2, 1, 1, n_pay, aliased=True)
        gi, k = pl.program_id(0), pl.program_id(1)

        if n_pay:
            @pl.when(jnp.logical_and(gi == 0, k == 0))
            def _():
                _exchange(pay_refs, recv_refs, rows, pay_layer, *sems, start=True)

        @pl.when(k == 0)
        def _():
            acc_ref[...] = jnp.zeros_like(acc_ref)

        acc_ref[...] += _mm_tn(a_ref[...], b_ref[...])

        @pl.when(k == n_k - 1)
        def _():
            o_ref[...] = acc_ref[...].astype(out_dtype)

        if n_pay:
            @pl.when(jnp.logical_and(gi == g - 1, k == n_k - 1))
            def _():
                _exchange(pay_refs, recv_refs, rows, pay_layer, *sems, start=False)

    (out,), got = _carrier_call(
        body, name, (g, n_k),
        out_shape=[jax.ShapeDtypeStruct((g, m, n), out_dtype)],
        in_specs=[pl.BlockSpec((None, tk, m), (lambda gi, k: (gi, k, 0)) if ga > 1 else (lambda gi, k: (0, k, 0))),
                  pl.BlockSpec((None, tk, n), (lambda gi, k: (gi, k, 0)) if gb > 1 else (lambda gi, k: (0, k, 0)))],
        out_specs=[pl.BlockSpec((None, m, n), lambda gi, k: (gi, 0, 0))],
        scratch=[pltpu.VMEM((m, n), F32)],
        operands=(a, b), payload=payload, recvs=recvs)
    return out, got


def _time_of_row(tm):
    i = lax.broadcasted_iota(jnp.int32, (tm, 1), 0)
    return (i % 8) * (tm // 8) + i // 8


def _sublane_is(rows, s):
    return lax.broadcasted_iota(jnp.int32, (rows, 1), 0) % 8 == s


def _conv_taps(src_ref, col0, ncols, tm, tap_rows, w_ref, init_of, store):
    rc = min(ROW_CHUNK, tm)
    for cb in range(ncols // LANES):
        cs = slice(cb * LANES, (cb + 1) * LANES)
        ss = slice(col0 + cb * LANES, col0 + (cb + 1) * LANES)
        for r0 in range(0, tm, rc):
            acc = init_of(cs, rc)
            for k, row in enumerate(tap_rows):
                acc = acc + w_ref[k:k + 1, cs] * src_ref[r0 + row:r0 + row + rc, ss]
            store(r0, rc, cs, acc)


def _mixer_fwd(l, x, mod, gains, w_in_t, conv_w, vec, pool_w, w_out, *, seq, tm, act_dtype, payload=(),
               gather_layer=None):
    t, d = x.shape
    d_in = w_in_t.shape[0]
    dc = conv_w.shape[-1]
    d_mix = w_out.shape[0]
    n_taps = 31
    nt, tps = t // tm, seq // tm
    p = CONV_PREFIX
    n_pay = len(payload)

    def body(*refs):
        ins, outs, scr, pay_refs, got_refs, sems = _carried_refs(refs, 8, 7, 2, n_pay, aliased=False)
        x_ref, mod_ref, g_ref, win_ref, cw_ref, v_ref, pw_ref, wout_ref = ins
        x1_ref, h1_ref, u_ref, a1_ref, ap_ref, o_ref, tail_ref = outs
        ext_ref, car_ref = scr
        i = pl.program_id(0)
        first = (i % tps) == 0

        if n_pay:
            @pl.when(i == 0)
            def _():
                _gather_phase(_layer_of(pay_refs, [gather_layer] * n_pay), got_refs, *sems, 0)

        xv = x_ref[...]
        r = lax.rsqrt(_mean(xv * xv) + EPS)
        hv = (xv * r) * g_ref[0:1, :] * (1.0 + mod_ref[1:2, :]) + mod_ref[0:1, :]
        hb = hv.astype(act_dtype)
        h1_ref[...] = hb
        u = _mm_nt(hb, win_ref[...])
        u_ref[...] = u
        a0 = u[:, :dc] * _sigmoid(u[:, dc:2 * dc])

        @pl.when(first)
        def _():
            car_ref[...] = jnp.zeros_like(car_ref)

        @pl.when(i == 0)
        def _():
            ext_ref[p + tm:p + tm + 8, :] = jnp.zeros((8, ext_ref.shape[1]), F32)

        ext_ref[p:p + tm, 0:dc] = a0
        ext_ref[p:p + tm, dc:] = u[:, 2 * dc:]
        ext_ref[0:p, :] = jnp.where(_sublane_is(p, 0), car_ref[...], ext_ref[tm - 1:tm - 1 + p, :])

        def store(r0, rc, cs, acc):
            a1_ref[r0:r0 + rc, cs] = acc

        _conv_taps(ext_ref, 0, dc, tm, [p - 8 * (n_taps - 1 - k) for k in range(n_taps)],
                   cw_ref,
                   lambda cs, rc: jnp.broadcast_to(v_ref[0:1, cs], (rc, LANES)), store)
        a1 = a1_ref[...]
        mu = _mean(a1)
        xc = a1 - mu
        rstd = lax.rsqrt(_mean(xc * xc) + EPS)
        a2 = (xc * rstd) * v_ref[1:2, :] + v_ref[2:3, :]
        ap_ref[:, 0:dc] = (a2 * _sigmoid(a2)).astype(act_dtype)

        pos = (i % tps) * tm + _time_of_row(tm)
        for g, w in enumerate(POOL_WINDOWS):
            cs = slice(dc + g * POOL_GROUP, dc + (g + 1) * POOL_GROUP)
            s = ext_ref[p:p + tm, cs]
            for j in range(1, w):
                s = s + ext_ref[p - 8 * j:p - 8 * j + tm, cs]
            cnt = jnp.minimum(pos + 1, w).astype(F32)
            dv = s / cnt - ext_ref[p:p + tm, cs]
            q = _mm(dv, pw_ref[g])
            ap_ref[:, cs] = (q * v_ref[3:4, g * POOL_GROUP:(g + 1) * POOL_GROUP]).astype(act_dtype)

        o = _mm(ap_ref[...], wout_ref[...])
        o_ref[...] = o
        ro = lax.rsqrt(_mean(o * o) + EPS)
        x1_ref[...] = xv + (1.0 + mod_ref[2:3, :]) * ((o * ro) * g_ref[1:2, :])

        nxt = ext_ref[tm + 7:tm + 7 + p, :]
        car_ref[...] = nxt
        tail_ref[...] = nxt

        if n_pay:
            @pl.when(i == max(nt - 2, 0))
            def _():
                _gather_phase(_layer_of(pay_refs, [gather_layer] * n_pay), got_refs, *sems, 1)

            @pl.when(i == nt - 1)
            def _():
                _gather_phase(_layer_of(pay_refs, [gather_layer] * n_pay), got_refs, *sems, 2)

    row = lambda width: pl.BlockSpec((tm, width), lambda i: (i, 0))
    return _carrier_call(
        body, f"mixer_fwd_{l}", (nt,),
        out_shape=[jax.ShapeDtypeStruct((t, d), F32), jax.ShapeDtypeStruct((t, d), act_dtype),
                   jax.ShapeDtypeStruct((t, d_in), F32), jax.ShapeDtypeStruct((t, dc), F32),
                   jax.ShapeDtypeStruct((t, d_mix), act_dtype), jax.ShapeDtypeStruct((t, d), F32),
                   jax.ShapeDtypeStruct((nt, p, d_mix), F32)],
        in_specs=[row(d),
                  pl.BlockSpec((None, None, 8, d), lambda i: (l, i // tps, 0, 0)),
                  _const_spec((None, 8, d), (l, 0, 0)),
                  _const_spec((d_in, d), (0, 0)),
                  _const_spec((32, dc), (0, 0)),
                  _const_spec((None, 8, dc), (l, 0, 0)),
                  _const_spec((None, len(POOL_WINDOWS), POOL_GROUP, POOL_GROUP), (l, 0, 0, 0)),
                  _const_spec((d_mix, d), (0, 0))],
        out_specs=[row(d), row(d), row(d_in), row(dc), row(d_mix), row(d),
                   pl.BlockSpec((None, p, d_mix), lambda i: (i, 0, 0))],
        scratch=[pltpu.VMEM((p + tm + 8, d_mix), F32), pltpu.VMEM((p, d_mix), F32)],
        operands=(x, mod, gains, w_in_t, conv_w, vec, pool_w, w_out),
        payload=payload, gather_layer=gather_layer)


def _mixer_bwd(l, dx1, x, mod, gains, u, tails, a1, o, w_in_t, conv_w, vec, pool_w, w_out, *, seq, tm, act_dtype,
               payload=(), recvs=(), rows=(), pay_layer=0):
    t, d = x.shape
    d_in = w_in_t.shape[0]
    dc = conv_w.shape[-1]
    d_mix = w_out.shape[0]
    n_taps = 31
    nt, tps = t // tm, seq // tm
    p = CONV_PREFIX
    n_groups = len(POOL_WINDOWS)
    n_pay = len(payload)

    def body(*refs):
        ins, outs, scr, pay_refs, recv_refs, sems = _carried_refs(refs, 13, 8, 3, n_pay, aliased=True)
        dx1_ref, x_ref, mod_ref, g_ref, u_ref, tail_ref, a1_ref, o_ref, win_ref, cw_ref, v_ref, pw_ref, wout_ref = ins
        dx_ref, du_ref, do_ref, dmod_ref, gd_ref, gv_ref, dcw_ref, dpw_ref = outs
        ext_ref, fext_ref, fcar_ref = scr
        i = pl.program_id(0)
        j = nt - 1 - i
        first_in_seq = (j % tps) == 0
        last_in_seq = (j % tps) == tps - 1

        if n_pay:
            @pl.when(i == 0)
            def _():
                _exchange(pay_refs, recv_refs, rows, pay_layer, *sems, start=True)

        @pl.when(i == 0)
        def _():
            gd_ref[...] = jnp.zeros_like(gd_ref)
            gv_ref[...] = jnp.zeros_like(gv_ref)
            dcw_ref[...] = jnp.zeros_like(dcw_ref)
            dpw_ref[...] = jnp.zeros_like(dpw_ref)
            fext_ref[0:8, :] = jnp.zeros((8, fext_ref.shape[1]), F32)

        @pl.when(last_in_seq)
        def _():
            dmod_ref[...] = jnp.zeros_like(dmod_ref)
            fcar_ref[...] = jnp.zeros_like(fcar_ref)

        xv = x_ref[...]
        dx1v = dx1_ref[...]
        pre_g, post_g = g_ref[0:1, :], g_ref[1:2, :]

        ov = o_ref[...]
        ro = lax.rsqrt(_mean(ov * ov) + EPS)
        yo = ov * ro
        dmod_ref[2:3, :] += _colsum(dx1v * (yo * post_g))
        dn = dx1v * (1.0 + mod_ref[2:3, :])
        gd_ref[1:2, :] += _colsum(dn * yo)
        dyo = dn * post_g
        do = ro * (dyo - yo * _mean(dyo * yo))
        dob = do.astype(act_dtype)
        do_ref[...] = dob
        dap = _mm_nt(dob, wout_ref[...])

        uv = u_ref[...]
        val, gate = uv[:, :dc], uv[:, dc:2 * dc]
        sg = _sigmoid(gate)
        ext_ref[p:p + tm, 0:dc] = val * sg
        ext_ref[p:p + tm, dc:] = uv[:, 2 * dc:]
        keep = jnp.where(first_in_seq, 0.0, 1.0).astype(F32)
        ext_ref[0:p, :] = jnp.where(_sublane_is(p, 0), tail_ref[...] * keep, ext_ref[tm - 1:tm - 1 + p, :])

        a1v = a1_ref[...]
        mu = _mean(a1v)
        xc = a1v - mu
        rstd = lax.rsqrt(_mean(xc * xc) + EPS)
        xh = xc * rstd
        ln_g = v_ref[1:2, :]
        a2 = xh * ln_g + v_ref[2:3, :]
        s2 = _sigmoid(a2)
        da2 = dap[:, :dc] * (s2 * (1.0 + a2 * (1.0 - s2)))
        gv_ref[1:2, :] += _colsum(da2 * xh)
        gv_ref[2:3, :] += _colsum(da2)
        dxh = da2 * ln_g
        da1 = rstd * (dxh - _mean(dxh) - xh * _mean(dxh * xh))
        gv_ref[0:1, :] += _colsum(da1)
        last_sublane = _sublane_is(p, 7)

        def put(cs, value):
            fext_ref[8:8 + tm, cs] = value
            fext_ref[8 + tm:8 + tm + p, cs] = jnp.where(last_sublane, fcar_ref[:, cs], fext_ref[9:9 + p, cs])

        put(slice(0, dc), da1)

        for k in range(n_taps):
            row = p - 8 * (n_taps - 1 - k)
            dcw_ref[k:k + 1, :] += _colsum(fext_ref[8:8 + tm, 0:dc] * ext_ref[row:row + tm, 0:dc])

        def store(r0, rc, cs, acc):
            sgc = _sigmoid(u_ref[r0:r0 + rc, dc + cs.start:dc + cs.stop])
            vc = u_ref[r0:r0 + rc, cs]
            du_ref[r0:r0 + rc, cs] = (acc * sgc).astype(act_dtype)
            du_ref[r0:r0 + rc, dc + cs.start:dc + cs.stop] = (acc * vc * sgc * (1.0 - sgc)).astype(act_dtype)

        _conv_taps(fext_ref, 0, dc, tm, [8 + 8 * (n_taps - 1 - k) for k in range(n_taps)],
                   cw_ref,
                   lambda cs, rc: jnp.zeros((rc, LANES), F32), store)

        pos = (j % tps) * tm + _time_of_row(tm)
        for g, w in enumerate(POOL_WINDOWS):
            cs = slice(dc + g * POOL_GROUP, dc + (g + 1) * POOL_GROUP)
            gs = slice(g * POOL_GROUP, (g + 1) * POOL_GROUP)
            s = ext_ref[p:p + tm, cs]
            for jj in range(1, w):
                s = s + ext_ref[p - 8 * jj:p - 8 * jj + tm, cs]
            cnt = jnp.minimum(pos + 1, w).astype(F32)
            dv = (s / cnt - ext_ref[p:p + tm, cs]).astype(MXU_DTYPE)
            q = _mm(dv, pw_ref[g])
            dp = dap[:, cs]
            gv_ref[3:4, gs] += _colsum(dp * q)
            dq = (dp * v_ref[3:4, gs]).astype(MXU_DTYPE)
            dpw_ref[g] += _mm_tn(dv, dq)
            dd = _mm_nt(dq, pw_ref[g])
            put(cs, dd / cnt)
            dhp = fext_ref[8:8 + tm, cs]
            for jj in range(1, w):
                dhp = dhp + fext_ref[8 + 8 * jj:8 + 8 * jj + tm, cs]
            du_ref[:, dc + cs.start:dc + cs.stop] = (dhp - dd).astype(act_dtype)

        fcar_ref[...] = fext_ref[1:1 + p, :]

        dh = _mm(du_ref[...], win_ref[...])

        r = lax.rsqrt(_mean(xv * xv) + EPS)
        xn = xv * r
        dmod_ref[0:1, :] += _colsum(dh)
        dmod_ref[1:2, :] += _colsum(dh * (xn * pre_g))
        dy = dh * (1.0 + mod_ref[1:2, :])
        gd_ref[0:1, :] += _colsum(dy * xn)
        dxn = dy * pre_g
        dx_ref[...] = dx1v + r * (dxn - xn * _mean(dxn * xn))

        if n_pay:
            @pl.when(i == nt - 1)
            def _():
                _exchange(pay_refs, recv_refs, rows, pay_layer, *sems, start=False)

    rev = lambda width: pl.BlockSpec((tm, width), lambda i: (nt - 1 - i, 0))
    return _carrier_call(
        body, f"mixer_bwd_{l}", (nt,),
        out_shape=[jax.ShapeDtypeStruct((t, d), F32), jax.ShapeDtypeStruct((t, d_in), act_dtype),
                   jax.ShapeDtypeStruct((t, d), act_dtype),
                   jax.ShapeDtypeStruct((t // seq, 8, d), F32), jax.ShapeDtypeStruct((8, d), F32),
                   jax.ShapeDtypeStruct((8, dc), F32), jax.ShapeDtypeStruct((32, dc), F32),
                   jax.ShapeDtypeStruct((n_groups, POOL_GROUP, POOL_GROUP), F32)],
        in_specs=[rev(d), rev(d),
                  pl.BlockSpec((None, None, 8, d), lambda i: (l, (nt - 1 - i) // tps, 0, 0)),
                  _const_spec((None, 8, d), (l, 0, 0)),
                  rev(d_in),
                  pl.BlockSpec((None, p, d_mix), lambda i: (jnp.maximum(nt - 2 - i, 0), 0, 0)),
                  rev(dc), rev(d),
                  _const_spec((d_in, d), (0, 0)),
                  _const_spec((32, dc), (0, 0)),
                  _const_spec((None, 8, dc), (l, 0, 0)),
                  _const_spec((None, n_groups, POOL_GROUP, POOL_GROUP), (l, 0, 0, 0)),
                  _const_spec((d_mix, d), (0, 0))],
        out_specs=[rev(d), rev(d_in), rev(d),
                   pl.BlockSpec((None, 8, d), lambda i: ((nt - 1 - i) // tps, 0, 0)),
                   _acc_spec((8, d), (0, 0)), _acc_spec((8, dc), (0, 0)), _acc_spec((32, dc), (0, 0)),
                   _acc_spec((n_groups, POOL_GROUP, POOL_GROUP), (0, 0, 0))],
        scratch=[pltpu.VMEM((p + tm, d_mix), F32), pltpu.VMEM((8 + tm + p, d_mix), F32),
                 pltpu.VMEM((p, d_mix), F32)],
        operands=(dx1, x, mod, gains, u, tails, a1, o, w_in_t, conv_w, vec, pool_w, w_out),
        payload=payload, recvs=recvs)


def _ffn_fwd(l, x, mod, gains, up, fcw, down, *, seq, tm, act_dtype, payload=(), gather_layer=None):
    t, d = x.shape
    n_chunks, _, fc = up.shape
    half = n_chunks // 2
    nt, tps = t // tm, seq // tm
    p = FFN_PREFIX
    n_pay = len(payload)

    def body(*refs):
        ins, outs, scr, pay_refs, got_refs, sems = _carried_refs(refs, 6, 6, 2, n_pay, aliased=False)
        x_ref, mod_ref, g_ref, up_ref, fcw_ref, down_ref = ins
        x2_ref, h2_ref, u2_ref, u3_ref, hid_ref, o2_ref = outs
        ext_ref, car_ref = scr
        i = pl.program_id(0)
        first = (i % tps) == 0

        if n_pay:
            @pl.when(i == 0)
            def _():
                _gather_phase(_layer_of(pay_refs, [gather_layer] * n_pay), got_refs, *sems, 0)

        xv = x_ref[...]
        r = lax.rsqrt(_mean(xv * xv) + EPS)
        hv = (xv * r) * g_ref[0:1, :] * (1.0 + mod_ref[4:5, :]) + mod_ref[3:4, :]
        hb = hv.astype(act_dtype)
        h2_ref[...] = hb

        @pl.when(first)
        def _():
            car_ref[...] = jnp.zeros_like(car_ref)

        @pl.when(i == 0)
        def _():
            for n in range(n_chunks):
                ext_ref[n, p + tm:p + tm + 8, :] = jnp.zeros((8, fc), F32)

        first_sublane = _sublane_is(p, 0)

        def project(n):
            un = _mm(hb, up_ref[n])
            ext_ref[n, p:p + tm, :] = un
            u2_ref[n] = un.astype(act_dtype)
            ext_ref[n, 0:p, :] = jnp.where(first_sublane, car_ref[n], ext_ref[n, tm - 1:tm - 1 + p, :])
            car_ref[n] = ext_ref[n, tm + 7:tm + 7 + p, :]

        def conv(n):
            return (fcw_ref[n, 3:4, :] + fcw_ref[n, 0:1, :] * ext_ref[n, p - 16:p - 16 + tm, :]
                    + fcw_ref[n, 1:2, :] * ext_ref[n, p - 8:p - 8 + tm, :]
                    + fcw_ref[n, 2:3, :] * ext_ref[n, p:p + tm, :])

        for n in range(n_chunks):
            project(n)
        o2 = jnp.zeros((tm, d), F32)
        for n in range(half):
            gt = conv(n + half)
            v = conv(n)
            sg = _sigmoid(gt)
            silu = gt * sg
            u3_ref[n] = silu.astype(act_dtype)
            u3_ref[n + half] = (v * (sg * (1.0 + gt * (1.0 - sg)))).astype(act_dtype)
            hid = (silu * v).astype(act_dtype)
            hid_ref[n] = hid
            o2 = o2 + _mm(hid, down_ref[n])
        o2_ref[...] = o2
        ro = lax.rsqrt(_mean(o2 * o2) + EPS)
        x2_ref[...] = xv + (1.0 + mod_ref[5:6, :]) * ((o2 * ro) * g_ref[1:2, :])

        if n_pay:
            @pl.when(i == max(nt - 2, 0))
            def _():
                _gather_phase(_layer_of(pay_refs, [gather_layer] * n_pay), got_refs, *sems, 1)

            @pl.when(i == nt - 1)
            def _():
                _gather_phase(_layer_of(pay_refs, [gather_layer] * n_pay), got_refs, *sems, 2)

    row = lambda width: pl.BlockSpec((tm, width), lambda i: (i, 0))
    chunked = lambda n: pl.BlockSpec((n, tm, fc), lambda i: (0, i, 0))
    return _carrier_call(
        body, f"ffn_fwd_{l}", (nt,),
        out_shape=[jax.ShapeDtypeStruct((t, d), F32), jax.ShapeDtypeStruct((t, d), act_dtype),
                   jax.ShapeDtypeStruct((n_chunks, t, fc), act_dtype),
                   jax.ShapeDtypeStruct((n_chunks, t, fc), act_dtype),
                   jax.ShapeDtypeStruct((half, t, fc), act_dtype), jax.ShapeDtypeStruct((t, d), F32)],
        in_specs=[row(d),
                  pl.BlockSpec((None, None, 8, d), lambda i: (l, i // tps, 0, 0)),
                  _const_spec((None, 8, d), (l, 0, 0)),
                  _const_spec((n_chunks, d, fc), (0, 0, 0)),
                  _const_spec((n_chunks, 8, fc), (0, 0, 0)),
                  _const_spec((half, fc, d), (0, 0, 0))],
        out_specs=[row(d), row(d), chunked(n_chunks), chunked(n_chunks), chunked(half), row(d)],
        scratch=[pltpu.VMEM((n_chunks, p + tm + 8, fc), F32), pltpu.VMEM((n_chunks, p, fc), F32)],
        operands=(x, mod, gains, up, fcw, down),
        payload=payload, gather_layer=gather_layer)


def _ffn_bwd(l, dx2, x, mod, gains, u2, u3, o2, up, fcw, down, *, seq, tm, act_dtype,
             payload=(), recvs=(), rows=(), pay_layer=0):
    t, d = x.shape
    n_chunks, _, fc = up.shape
    half = n_chunks // 2
    nt, tps = t // tm, seq // tm
    p = FFN_PREFIX
    n_pay = len(payload)

    def body(*refs):
        ins, outs, scr, pay_refs, recv_refs, sems = _carried_refs(refs, 10, 6, 2, n_pay, aliased=True)
        dx2_ref, x_ref, mod_ref, g_ref, u2_ref, u3_ref, o2_ref, up_ref, fcw_ref, down_ref = ins
        dx1_ref, du2_ref, do2_ref, dmod_ref, gd_ref, dfcw_ref = outs
        fext_ref, fcar_ref = scr
        i = pl.program_id(0)
        j = nt - 1 - i
        last_in_seq = (j % tps) == tps - 1

        if n_pay:
            @pl.when(i == 0)
            def _():
                _exchange(pay_refs, recv_refs, rows, pay_layer, *sems, start=True)

        @pl.when(i == 0)
        def _():
            gd_ref[...] = jnp.zeros_like(gd_ref)
            dfcw_ref[...] = jnp.zeros_like(dfcw_ref)
            for n in range(n_chunks):
                fext_ref[n, 0:8, :] = jnp.zeros((8, fc), F32)

        @pl.when(last_in_seq)
        def _():
            dmod_ref[...] = jnp.zeros_like(dmod_ref)
            fcar_ref[...] = jnp.zeros_like(fcar_ref)

        xv = x_ref[...]
        dx2v = dx2_ref[...]
        pre_g, post_g = g_ref[0:1, :], g_ref[1:2, :]

        ov = o2_ref[...]
        ro = lax.rsqrt(_mean(ov * ov) + EPS)
        yo = ov * ro
        dmod_ref[2:3, :] += _colsum(dx2v * (yo * post_g))
        dn = dx2v * (1.0 + mod_ref[5:6, :])
        gd_ref[1:2, :] += _colsum(dn * yo)
        dyo = dn * post_g
        do = ro * (dyo - yo * _mean(dyo * yo))
        dob = do.astype(act_dtype)
        do2_ref[...] = dob

        for n in range(half):
            dhid = _mm_nt(dob, down_ref[n])
            fext_ref[n, 8:8 + tm, :] = dhid * u3_ref[n].astype(F32)
            fext_ref[n + half, 8:8 + tm, :] = dhid * u3_ref[n + half].astype(F32)

        last_sublane = _sublane_is(p, 7)
        dh = jnp.zeros((tm, d), F32)
        for n in range(n_chunks):
            fext_ref[n, 8 + tm:8 + tm + p, :] = jnp.where(last_sublane, fcar_ref[n], fext_ref[n, 9:9 + p, :])
            fcar_ref[n] = fext_ref[n, 1:1 + p, :]
            d2 = fext_ref[n, 8:8 + tm, :]
            d1 = fext_ref[n, 16:16 + tm, :]
            d0 = fext_ref[n, 24:24 + tm, :]
            u2v = u2_ref[n].astype(F32)
            dfcw_ref[n, 3:4, :] += _colsum(d2)
            dfcw_ref[n, 0:1, :] += _colsum(d0 * u2v)
            dfcw_ref[n, 1:2, :] += _colsum(d1 * u2v)
            dfcw_ref[n, 2:3, :] += _colsum(d2 * u2v)
            du2 = (fcw_ref[n, 0:1, :] * d0 + fcw_ref[n, 1:2, :] * d1 + fcw_ref[n, 2:3, :] * d2).astype(act_dtype)
            du2_ref[n] = du2
            dh = dh + _mm_nt(du2, up_ref[n])

        r = lax.rsqrt(_mean(xv * xv) + EPS)
        xn = xv * r
        dmod_ref[0:1, :] += _colsum(dh)
        dmod_ref[1:2, :] += _colsum(dh * (xn * pre_g))
        dy = dh * (1.0 + mod_ref[4:5, :])
        gd_ref[0:1, :] += _colsum(dy * xn)
        dxn = dy * pre_g
        dx1_ref[...] = dx2v + r * (dxn - xn * _mean(dxn * xn))

        if n_pay:
            @pl.when(i == nt - 1)
            def _():
                _exchange(pay_refs, recv_refs, rows, pay_layer, *sems, start=False)

    rev = lambda width: pl.BlockSpec((tm, width), lambda i: (nt - 1 - i, 0))
    chunked = pl.BlockSpec((n_chunks, tm, fc), lambda i: (0, nt - 1 - i, 0))
    return _carrier_call(
        body, f"ffn_bwd_{l}", (nt,),
        out_shape=[jax.ShapeDtypeStruct((t, d), F32), jax.ShapeDtypeStruct((n_chunks, t, fc), act_dtype),
                   jax.ShapeDtypeStruct((t, d), act_dtype),
                   jax.ShapeDtypeStruct((t // seq, 8, d), F32), jax.ShapeDtypeStruct((8, d), F32),
                   jax.ShapeDtypeStruct((n_chunks, 8, fc), F32)],
        in_specs=[rev(d), rev(d),
                  pl.BlockSpec((None, None, 8, d), lambda i: (l, (nt - 1 - i) // tps, 0, 0)),
                  _const_spec((None, 8, d), (l, 0, 0)),
                  chunked, chunked, rev(d),
                  _const_spec((n_chunks, d, fc), (0, 0, 0)),
                  _const_spec((n_chunks, 8, fc), (0, 0, 0)),
                  _const_spec((half, fc, d), (0, 0, 0))],
        out_specs=[rev(d), chunked, rev(d),
                   pl.BlockSpec((None, 8, d), lambda i: ((nt - 1 - i) // tps, 0, 0)),
                   _acc_spec((8, d), (0, 0)), _acc_spec((n_chunks, 8, fc), (0, 0, 0))],
        scratch=[pltpu.VMEM((n_chunks, 8 + tm + p, fc), F32), pltpu.VMEM((n_chunks, p, fc), F32)],
        operands=(dx2, x, mod, gains, u2, u3, o2, up, fcw, down),
        payload=payload, recvs=recvs)


def _pad_rows(a, rows):
    pad = [(0, 0)] * a.ndim
    pad[-2] = (0, rows - a.shape[-2])
    return jnp.pad(a, pad)


def kernel(x, c, ada_w, ada_b, pre_mix_g, post_mix_g, w_in, conv_w, conv_b, conv_ln_g, conv_ln_b, pool_w, pool_scale, w_out, pre_ffn_g, post_ffn_g, ffn_up, ffn_conv_w, ffn_conv_b, ffn_down, loss_target, m_ada_w, m_ada_b, m_pre_mix_g, m_post_mix_g, m_w_in, m_conv_w, m_conv_b, m_conv_ln_g, m_conv_ln_b, m_pool_w, m_pool_scale, m_w_out, m_pre_ffn_g, m_post_ffn_g, m_ffn_up, m_ffn_conv_w, m_ffn_conv_b, m_ffn_down, v_ada_w, v_ada_b, v_pre_mix_g, v_post_mix_g, v_w_in, v_conv_w, v_conv_b, v_conv_ln_g, v_conv_ln_b, v_pool_w, v_pool_scale, v_w_out, v_pre_ffn_g, v_post_ffn_g, v_ffn_up, v_ffn_conv_w, v_ffn_conv_b, v_ffn_down):
    bl, seq, d = x.shape
    n_layers = ada_w.shape[0]
    t = bl * seq
    dc = conv_b.shape[1]
    d_in = w_in.shape[2] * N_DEV
    d_mix = w_out.shape[1] * N_DEV
    n_taps = conv_w.shape[1]
    fc = ffn_up.shape[2]
    half = N_DEV // 2
    ada_cols = ada_w.shape[2]
    n_mod = ada_cols * N_DEV // d
    assert pool_scale.shape[1] == dc and n_taps == 31 and n_mod == 6 and ffn_conv_w.shape[1] == 3
    assert pool_w.shape[1:] == (len(POOL_WINDOWS), POOL_GROUP, POOL_GROUP)
    tm = TILE_TOKENS
    assert seq % tm == 0 and CONV_PREFIX <= tm - 8
    tk = 2048 if t % 2048 == 0 else tm
    act = MXU_DTYPE

    def tile_order(a, inverse=False):
        shape = (t // tm, tm // 8, 8, d) if inverse else (t // tm, 8, tm // 8, d)
        return a.reshape(shape).transpose(0, 2, 1, 3).reshape(t, d)

    ax = lax.axis_index
    me = 4 * ax("x") + 2 * ax("y") + ax("c")

    (c_all,) = _all_gather([c], "gather_c")
    c_all = c_all.reshape(N_DEV * bl, d)
    ada_b_cols = lax.dynamic_slice_in_dim(ada_b, me * ada_cols, ada_cols, axis=1)
    mod_cols = _ada_fwd(c_all, ada_w, ada_b_cols)

    w_in_s = w_in.astype(act).transpose(0, 2, 1)
    w_out_s, up_s, down_s = w_out.astype(act), ffn_up.astype(act), ffn_down.astype(act)
    conv_w_s = _pad_rows(conv_w, 32)
    fcw_s = _pad_rows(jnp.concatenate(
        [ffn_conv_w, lax.dynamic_slice_in_dim(ffn_conv_b, me * fc, fc, axis=1)[:, None, :]], axis=1), 8)

    mixer_shards = (w_in_s, w_out_s, conv_w_s)
    late_shards = (down_s, fcw_s)

    def mixer_weights(g_w_in, g_w_out, g_conv_w):
        return (g_w_in.reshape(d_in, d), g_conv_w.transpose(1, 0, 2).reshape(32, dc), g_w_out.reshape(d_mix, d))

    def late_weights(g_down, g_fcw):
        return (g_fcw, g_down.reshape(half, fc, d))

    g0 = _all_gather([mod_cols, *mixer_shards], "gather_w0", layers=[None, 0, 0, 0])
    mod_all = g0[0].transpose(1, 2, 0, 3).reshape(n_layers, N_DEV * bl, n_mod, d)
    mod = _pad_rows(lax.dynamic_slice_in_dim(mod_all, me * bl, bl, axis=1), 8)
    mixers, lates, ups = [mixer_weights(*g0[1:])], [], []
    gains_mix = _pad_rows(jnp.stack([pre_mix_g, post_mix_g], axis=1), 8)
    gains_ffn = _pad_rows(jnp.stack([pre_ffn_g, post_ffn_g], axis=1), 8)
    vec = _pad_rows(jnp.stack([conv_b, conv_ln_g, conv_ln_b, pool_scale], axis=1), 8)
    pool_w_b = pool_w.astype(act)

    kw = dict(seq=seq, tm=tm, act_dtype=act)
    xs = tile_order(x.reshape(t, d))
    saved = []
    for l in range(n_layers):
        w_in_t, cw, w_o = mixers[l]
        (x1, h1, u, a1, ap, o, tails), got = _mixer_fwd(
            l, xs, mod, gains_mix, w_in_t, cw, vec, pool_w_b, w_o, **kw,
            payload=(up_s, *late_shards) if l == 0 else (up_s,), gather_layer=l)
        ups.append(got[0])
        if l == 0:
            lates.append(late_weights(*got[1:]))
        up_l, (fcw_l, down_l) = ups[l], lates[l]
        (x2, h2, u2, u3, hid, o2), got = _ffn_fwd(l, x1, mod, gains_ffn, up_l, fcw_l, down_l, **kw,
                                                  payload=(*mixer_shards, *late_shards) if l + 1 < n_layers else (),
                                                  gather_layer=l + 1)
        if got:
            mixers.append(mixer_weights(*got[:3]))
            lates.append(late_weights(*got[3:]))
        saved.append((xs, h1, u, tails, a1, ap, o, x1, h2, u2, u3, hid, o2))
        xs = x2

    dx, loss_part = _loss_grad(xs, tile_order(loss_target.reshape(t, d)), tm)

    def landing(shard, dtype):
        return lax.empty((N_DEV, n_layers) + shard.shape[1:], dtype)

    r_up, r_down = landing(ffn_up, act), landing(ffn_down, act)
    r_w_in, r_w_out = landing(w_in, act), landing(w_out, act)
    r_conv_w, r_fcw = landing(conv_w, F32), landing(ffn_conv_w, F32)
    def cuts(l):
        first, second = (5.5, 7.8) if l == 0 else (2.5, 6)
        return int(first * d / 10) // 16 * 16, int(second * d / 10) // 16 * 16

    dmods, smalls = [], []
    pending = None
    for l in reversed(range(n_layers)):
        xin, h1, u, tails, a1, ap, o, x1, h2, u2, u3, hid, o2 = saved[l]
        (w_in_t, cw, w_o), (fcw_l, down_l), up_l = mixers[l], lates[l], ups[l]
        a, b = cuts(l + 1) if pending else (0, 0)
        carried = dict(pay_layer=l + 1)
        (dx1, du2, do2, dmod_b, gd_b, dfcw), got = _ffn_bwd(
            l, dx, x1, mod, gains_ffn, u2, u3, o2, up_l, fcw_l, down_l, **kw, **carried,
            payload=(pending["up"], pending["w_in"], pending["w_out"], pending["conv_w"], pending["fcw"])
            if pending else (),
            recvs=(r_up, r_w_in, r_w_out, r_conv_w, r_fcw) if pending else (),
            rows=((a, b - a), None, None, None, None))
        if pending:
            r_up, r_w_in, r_w_out, r_conv_w, r_fcw = got
        p_up, got = _matmul_tn(f"dw_up_{l}", h2[None], du2, tk, act, **carried,
                               payload=(pending["up"],) if pending else (), recvs=(r_up,) if pending else (),
                               rows=((b, d - b),))
        if pending:
            (r_up,) = got
        p_down = _matmul_tn(f"dw_down_{l}", hid, do2[None], tk, act)[0].reshape(N_DEV, fc // 2, d)
        a, b = cuts(l)
        (dx, du, do, dmod_a, gd_a, gv, dcw, dpw), (r_down, r_up) = _mixer_bwd(
            l, dx1, xin, mod, gains_mix, u, tails, a1, o, w_in_t, cw, vec, pool_w_b, w_o, **kw,
            payload=(p_down, p_up), recvs=(r_down, r_up), rows=(None, (0, a)), pay_layer=l)
        last = dict(payload=(p_up,), recvs=(r_up,), pay_layer=0) if l == 0 else {}
        dw_in, got = _matmul_tn(f"dw_in_{l}", h1[None], du[None], tk, act, **last, rows=((a, b - a),))
        if l == 0:
            (r_up,) = got
            last["recvs"] = (r_up,)
        dw_out, got = _matmul_tn(f"dw_out_{l}", ap[None], do[None], tk, act, **last, rows=((b, d - b),))
        if l == 0:
            (r_up,) = got
        pending = dict(up=p_up, w_in=dw_in[0].reshape(d, N_DEV, d_in // N_DEV).transpose(1, 0, 2),
                       w_out=dw_out[0].reshape(N_DEV, d_mix // N_DEV, d),
                       conv_w=dcw[:n_taps].reshape(n_taps, N_DEV, dc // N_DEV).transpose(1, 0, 2),
                       fcw=dfcw[:, :3, :])
        dmods.append(jnp.concatenate([dmod_a[:, 0:3], dmod_b[:, 0:3]], axis=1).reshape(bl, n_mod * d))
        smalls.append(jnp.concatenate(
            [gd_a[0], gd_a[1], gv[0], gv[1], gv[2], gv[3], gd_b[0], gd_b[1], dfcw[:, 3, :].reshape(-1),
             dpw.reshape(-1)]))
    dmods.reverse()
    smalls.reverse()

    dmod_loc = jnp.stack(dmods)
    small_loc = jnp.stack(smalls)
    n_small = small_loc.shape[1]
    small_cols = 8 * LANES if (n_layers * n_small) % (8 * LANES) == 0 else LANES
    (r_w_in, r_w_out, r_conv_w, r_fcw), (g_dmod, g_small, g_loss) = _exchange_and_gather(
        (pending["w_in"], pending["w_out"], pending["conv_w"], pending["fcw"]),
        (r_w_in, r_w_out, r_conv_w, r_fcw), (None, None, None, None), 0,
        [dmod_loc.astype(act), small_loc.reshape(-1, small_cols).astype(act), loss_part], "exchange_tail")
    g_dmod = g_dmod.astype(F32)

    loss = _sum_parts(g_loss)[0, 0]

    def flat2(a):
        return a.reshape(-1, a.shape[-1])

    def update(name, parts, w, m, v):
        outs = _adam_reduce(name, parts.reshape(parts.shape[0], -1, w.shape[-1]), flat2(w), flat2(m), flat2(v))
        return [o_.reshape(w.shape) for o_ in outs]

    res = {}
    res["w_in"] = update("adam_w_in", r_w_in, w_in, m_w_in, v_w_in)
    res["w_out"] = update("adam_w_out", r_w_out, w_out, m_w_out, v_w_out)
    res["ffn_up"] = update("adam_ffn_up", r_up, ffn_up, m_ffn_up, v_ffn_up)
    res["ffn_down"] = update("adam_ffn_down", r_down, ffn_down, m_ffn_down, v_ffn_down)
    res["conv_w"] = update("adam_conv_w", r_conv_w, conv_w, m_conv_w, v_conv_w)
    res["ffn_conv_w"] = update("adam_ffn_conv_w", r_fcw, ffn_conv_w, m_ffn_conv_w, v_ffn_conv_w)

    dmod_all = g_dmod.transpose(1, 0, 2, 3).reshape(n_layers, N_DEV * bl, n_mod * d)
    dmod_cols = lax.dynamic_slice_in_dim(dmod_all, me * ada_cols, ada_cols, axis=2)
    res["ada_w"] = list(_ada_bwd(c_all.T, dmod_cols, ada_w, m_ada_w, v_ada_w))
    res["ada_b"] = update("adam_ada_b", dmod_all.transpose(1, 0, 2), ada_b, m_ada_b, v_ada_b)

    small_names = ["pre_mix_g", "post_mix_g", "conv_b", "conv_ln_g", "conv_ln_b", "pool_scale", "pre_ffn_g",
                   "post_ffn_g", "ffn_conv_b", "pool_w"]
    small_w = [pre_mix_g, post_mix_g, conv_b, conv_ln_g, conv_ln_b, pool_scale, pre_ffn_g, post_ffn_g,
               ffn_conv_b, pool_w]
    small_m = [m_pre_mix_g, m_post_mix_g, m_conv_b, m_conv_ln_g, m_conv_ln_b, m_pool_scale, m_pre_ffn_g,
               m_post_ffn_g, m_ffn_conv_b, m_pool_w]
    small_v = [v_pre_mix_g, v_post_mix_g, v_conv_b, v_conv_ln_g, v_conv_ln_b, v_pool_scale, v_pre_ffn_g,
               v_post_ffn_g, v_ffn_conv_b, v_pool_w]

    def pack(arrs):
        return jnp.concatenate([a.reshape(n_layers, -1) for a in arrs], axis=1).reshape(-1, small_cols)

    outs = _adam_reduce("adam_small", g_small, pack(small_w), pack(small_m), pack(small_v))
    outs = [o_.reshape(n_layers, n_small) for o_ in outs]
    off = 0
    for name, w in zip(small_names, small_w):
        size = w[0].size
        res[name] = [o_[:, off:off + size].reshape(w.shape) for o_ in outs]
        off += size

    order = ["ada_w", "ada_b", "pre_mix_g", "post_mix_g", "w_in", "conv_w", "conv_b", "conv_ln_g", "conv_ln_b",
             "pool_w", "pool_scale", "w_out", "pre_ffn_g", "post_ffn_g", "ffn_up", "ffn_conv_w", "ffn_conv_b",
             "ffn_down"]
    grad_x = tile_order(dx, inverse=True).reshape(bl, seq, d)
    return (loss, grad_x, *[res[n][0] for n in order], *[res[n][1] for n in order],
            *[res[n][2] for n in order], *[res[n][3] for n in order])
```

```python
import jax
import jax.numpy as jnp
from jax import lax
from jax.experimental import pallas as pl
from jax.experimental.pallas import tpu as pltpu

N_DEV = 8
EPS = 1e-6
POOL_WINDOWS = (2, 4, 8, 16)
POOL_GROUP = 128
TILE_TOKENS = 256
CONV_PREFIX = 8 * 30
FFN_PREFIX = 8 * 2
LANES = 128
ROW_CHUNK = 128
MXU_DTYPE = jnp.bfloat16
VMEM_LIMIT = 60 * 1024 * 1024

ADAM_LR = 0.001
ADAM_B1 = 0.9
ADAM_B2 = 0.999
ADAM_EPS = 1e-08
ADAM_WD = 0.01
ADAM_STEP = 10

MESH = pl.DeviceIdType.MESH
F32 = jnp.float32


def _mm(a, b):
    return jnp.dot(a.astype(MXU_DTYPE), b.astype(MXU_DTYPE), preferred_element_type=F32)


def _mm_nt(a, b):
    return lax.dot_general(a.astype(MXU_DTYPE), b.astype(MXU_DTYPE), (((1,), (1,)), ((), ())),
                           preferred_element_type=F32)


def _mm_tn(a, b):
    return lax.dot_general(a.astype(MXU_DTYPE), b.astype(MXU_DTYPE), (((0,), (0,)), ((), ())),
                           preferred_element_type=F32)


def _mean(v):
    return jnp.mean(v, axis=-1, keepdims=True)


def _colsum(v):
    return jnp.sum(v, axis=0, keepdims=True)


def _sigmoid(v):
    return jax.nn.sigmoid(v)


def _params(n_grid=1):
    return pltpu.CompilerParams(dimension_semantics=("arbitrary",) * n_grid, vmem_limit_bytes=VMEM_LIMIT)


def _const_spec(shape, index):
    return pl.BlockSpec(shape, lambda *_: index, pipeline_mode=pl.Buffered(1))


def _acc_spec(shape, index):
    return pl.BlockSpec(shape, lambda *_: index)


def _position():
    x, y, c = lax.axis_index("x"), lax.axis_index("y"), lax.axis_index("c")
    return x, y, c


def _gather_phase(ins, outs, send_sems, recv_sems, local_sems, phase):
    n = len(ins)
    x, y, c = _position()
    me, sibling = (x, y, c), (x, y, 1 - c)
    chips = [(1 - x, y), (x, 1 - y), (1 - x, 1 - y)]

    def slot(k, px, py, pc):
        return outs[k].at[4 * px + 2 * py + pc]

    def copy(k, s, block, to, src=None):
        return pltpu.make_async_remote_copy(
            src_ref=slot(k, *block) if src is None else src, dst_ref=slot(k, *block),
            send_sem=send_sems.at[k, s], recv_sem=recv_sems.at[k, s], device_id=to, device_id_type=MESH)

    mine = [pltpu.make_async_copy(ins[k], slot(k, *me), local_sems.at[k]) for k in range(n)]
    first = []
    for k in range(n):
        first.append(copy(k, 0, me, sibling, src=ins[k]))
        first += [copy(k, 1 + j, me, (*chip, c), src=ins[k]) for j, chip in enumerate(chips)]
    passed = [copy(k, 4 + j, (*chip, c), sibling) for j, chip in enumerate(chips) for k in range(n)]
    if phase == 0:
        for cp in mine + first:
            cp.start()
    elif phase == 1:
        for j, chip in enumerate(chips):
            for k in range(n):
                copy(k, 1 + j, (*chip, c), me).wait_recv()
                copy(k, 4 + j, (*chip, c), sibling).start()
    else:
        for k in range(n):
            copy(k, 0, sibling, me).wait_recv()
            for j, chip in enumerate(chips):
                copy(k, 4 + j, (*chip, 1 - c), me).wait_recv()
        for cp in first + passed:
            cp.wait_send()
        for cp in mine:
            cp.wait()


def _gather_scratch(n):
    return [pltpu.SemaphoreType.DMA((n, 7)), pltpu.SemaphoreType.DMA((n, 7)), pltpu.SemaphoreType.DMA((n,))]


def _layer_of(refs, layers):
    return [r if lay is None else r.at[lay] for r, lay in zip(refs, layers)]


def _gathered_shapes(arrs, layers):
    return [jax.ShapeDtypeStruct((N_DEV,) + (a.shape if lay is None else a.shape[1:]), a.dtype)
            for a, lay in zip(arrs, layers)]


def _all_gather(arrs, name, layers=None):
    n = len(arrs)
    layers = [None] * n if layers is None else layers

    def body(*refs):
        for phase in range(3):
            _gather_phase(_layer_of(refs[:n], layers), refs[n:2 * n], *refs[2 * n:], phase)

    any_spec = pl.BlockSpec(memory_space=pl.ANY)
    return pl.pallas_call(
        body, name=name,
        out_shape=_gathered_shapes(arrs, layers),
        in_specs=[any_spec] * n, out_specs=[any_spec] * n,
        scratch_shapes=_gather_scratch(n),
    )(*arrs)


def _exchange(payload, recvs, rows, layer, send_sems, recv_sems, local_sems, start):
    x, y, c = _position()
    me = 4 * x + 2 * y + c
    for k, (src, recv) in enumerate(zip(payload, recvs)):
        def block(ref, *index):
            return ref.at[index] if rows[k] is None else ref.at[(*index, pl.ds(*rows[k]))]

        local = pltpu.make_async_copy(block(src, me), block(recv, me, layer), local_sems.at[k])
        if start:
            local.start()
        else:
            local.wait()
        for j in range(1, N_DEV):
            px = (1 - x) if (j & 4) else x
            py = (1 - y) if (j & 2) else y
            pc = (1 - c) if (j & 1) else c
            peer = 4 * px + 2 * py + pc
            landing = block(recv, me, layer) if start else block(recv, peer, layer)
            cp = pltpu.make_async_remote_copy(
                src_ref=block(src, peer), dst_ref=landing, send_sem=send_sems.at[k, j - 1],
                recv_sem=recv_sems.at[k, j - 1], device_id=(px, py, pc), device_id_type=MESH)
            if start:
                cp.start()
            else:
                cp.wait()


def _exchange_scratch(n):
    return [pltpu.SemaphoreType.DMA((n, N_DEV - 1)), pltpu.SemaphoreType.DMA((n, N_DEV - 1)),
            pltpu.SemaphoreType.DMA((n,))]


def _exchange_and_gather(payload, recvs, rows, layer, arrs, name):
    n, m = len(payload), len(arrs)

    def body(*refs):
        pay, srcs = refs[:n], refs[2 * n:2 * n + m]
        outs, got = refs[2 * n + m:3 * n + m], refs[3 * n + m:3 * n + 2 * m]
        xsems, gsems = refs[3 * n + 2 * m:3 * n + 2 * m + 3], refs[3 * n + 2 * m + 3:]
        _exchange(pay, outs, rows, layer, *xsems, start=True)
        for phase in range(3):
            _gather_phase(srcs, got, *gsems, phase)
        _exchange(pay, outs, rows, layer, *xsems, start=False)

    any_spec = pl.BlockSpec(memory_space=pl.ANY)
    res = pl.pallas_call(
        body, name=name,
        out_shape=[jax.ShapeDtypeStruct(r.shape, r.dtype) for r in recvs]
        + [jax.ShapeDtypeStruct((N_DEV,) + a.shape, a.dtype) for a in arrs],
        in_specs=[any_spec] * (2 * n + m), out_specs=[any_spec] * (n + m),
        input_output_aliases={n + k: k for k in range(n)},
        scratch_shapes=_exchange_scratch(n) + _gather_scratch(m),
    )(*payload, *recvs, *arrs)
    return res[:n], res[n:]


def _carried_refs(refs, n_in, n_out, n_scratch, n_pay, aliased):
    ins = refs[:n_in]
    pay = refs[n_in:n_in + n_pay]
    o0 = n_in + (2 * n_pay if aliased else n_pay)
    outs = refs[o0:o0 + n_out]
    recvs = refs[o0 + n_out:o0 + n_out + n_pay]
    s0 = o0 + n_out + n_pay
    return ins, outs, refs[s0:s0 + n_scratch], pay, recvs, refs[s0 + n_scratch:]


def _carrier_call(body, name, grid, in_specs, out_specs, out_shape, scratch, operands, payload, recvs=None,
                  gather_layer=None):
    n_in, n_out, n_pay = len(in_specs), len(out_specs), len(payload)
    any_spec = pl.BlockSpec(memory_space=pl.ANY)
    if recvs is None:
        landing = _gathered_shapes(payload, [gather_layer] * n_pay)
        extra_in, aliases = list(payload), {}
        sems = _gather_scratch(n_pay) if n_pay else []
    else:
        landing = [jax.ShapeDtypeStruct(r.shape, r.dtype) for r in recvs]
        extra_in = list(payload) + list(recvs)
        aliases = {n_in + n_pay + k: n_out + k for k in range(n_pay)}
        sems = _exchange_scratch(n_pay) if n_pay else []
    res = pl.pallas_call(
        body, name=name, grid=grid,
        out_shape=list(out_shape) + landing,
        in_specs=list(in_specs) + [any_spec] * len(extra_in),
        out_specs=list(out_specs) + [any_spec] * n_pay,
        input_output_aliases=aliases,
        scratch_shapes=list(scratch) + sems,
        compiler_params=_params(len(grid)),
    )(*operands, *extra_in)
    return res[:n_out], res[n_out:]


def _adamw(w, g, m, v):
    m = ADAM_B1 * m + (1.0 - ADAM_B1) * g
    v = ADAM_B2 * v + (1.0 - ADAM_B2) * jnp.square(g)
    m_hat = m / (1.0 - ADAM_B1 ** ADAM_STEP)
    v_hat = v / (1.0 - ADAM_B2 ** ADAM_STEP)
    delta = -ADAM_LR * (m_hat / (jnp.sqrt(v_hat) + ADAM_EPS) + ADAM_WD * w)
    return delta, m, v


def _ada_fwd(c_all, ada_w, ada_b_cols):
    n_layers, d, cols = ada_w.shape
    b = c_all.shape[0]

    def body(c_ref, w_ref, b_ref, o_ref):
        cv = c_ref[...]
        act = cv * _sigmoid(cv)
        o_ref[...] = jnp.dot(act, w_ref[...], preferred_element_type=F32,
                             precision=lax.Precision.HIGHEST) + b_ref[...]

    return pl.pallas_call(
        body, name="ada_fwd", grid=(n_layers,),
        out_shape=jax.ShapeDtypeStruct((n_layers, b, cols), F32),
        in_specs=[pl.BlockSpec((b, d), lambda l: (0, 0)),
                  pl.BlockSpec((None, d, cols), lambda l: (l, 0, 0)),
                  pl.BlockSpec((None, 1, cols), lambda l: (l, 0, 0))],
        out_specs=pl.BlockSpec((None, b, cols), lambda l: (l, 0, 0)),
        compiler_params=_params(1),
    )(c_all, ada_w, ada_b_cols.reshape(n_layers, 1, cols))


def _ada_bwd(c_all_t, dmod_cols, w, m, v):
    n_layers, d, cols = w.shape
    b = c_all_t.shape[1]
    td = 256 if d % 256 == 0 else d

    def body(c_ref, dm_ref, w_ref, m_ref, v_ref, g_ref, dl_ref, nm_ref, nv_ref):
        cv = c_ref[...]
        act = cv * _sigmoid(cv)
        g = jnp.dot(act, dm_ref[...], preferred_element_type=F32, precision=lax.Precision.HIGHEST)
        delta, nm, nv = _adamw(w_ref[...], g, m_ref[...], v_ref[...])
        g_ref[...] = g
        dl_ref[...] = delta
        nm_ref[...] = nm
        nv_ref[...] = nv

    blk = pl.BlockSpec((None, td, cols), lambda l, i: (l, i, 0))
    shp = jax.ShapeDtypeStruct(w.shape, F32)
    return pl.pallas_call(
        body, name="ada_bwd", grid=(n_layers, d // td),
        out_shape=[shp] * 4,
        in_specs=[pl.BlockSpec((td, b), lambda l, i: (i, 0)),
                  pl.BlockSpec((None, b, cols), lambda l, i: (l, 0, 0)), blk, blk, blk],
        out_specs=[blk] * 4,
        compiler_params=_params(2),
    )(c_all_t, dmod_cols, w, m, v)


def _row_tile(rows, cols, budget=128 * 1024, step=8):
    best = None
    for t in range(step, rows + 1, step):
        if rows % t == 0 and t * cols <= budget:
            best = t
    return best if best is not None else rows


def _adam_reduce(name, parts, w, m, v):
    p, rows, cols = parts.shape
    tr = _row_tile(rows, cols, budget=(256 * 1024) // max(1, p // 4), step=8 if parts.dtype == F32 else 16)

    def body(p_ref, w_ref, m_ref, v_ref, g_ref, dl_ref, nm_ref, nv_ref):
        g = p_ref[0].astype(F32)
        for k in range(1, p):
            g = g + p_ref[k].astype(F32)
        delta, nm, nv = _adamw(w_ref[...], g, m_ref[...], v_ref[...])
        g_ref[...] = g
        dl_ref[...] = delta
        nm_ref[...] = nm
        nv_ref[...] = nv

    blk = pl.BlockSpec((tr, cols), lambda i: (i, 0))
    shp = jax.ShapeDtypeStruct((rows, cols), F32)
    return pl.pallas_call(
        body, name=name, grid=(rows // tr,),
        out_shape=[shp] * 4,
        in_specs=[pl.BlockSpec((p, tr, cols), lambda i: (0, i, 0)), blk, blk, blk],
        out_specs=[blk] * 4,
        compiler_params=_params(1),
    )(parts, w, m, v)


def _sum_parts(parts):
    p = parts.shape[0]

    def body(p_ref, o_ref):
        acc = p_ref[0]
        for k in range(1, p):
            acc = acc + p_ref[k]
        o_ref[...] = acc

    return pl.pallas_call(body, name="loss_sum", out_shape=jax.ShapeDtypeStruct(parts.shape[1:], F32))(parts)


def _matmul_tn(name, a, b, tk, out_dtype, payload=(), recvs=(), rows=(), pay_layer=0):
    ga, t, m = a.shape
    gb, _, n = b.shape
    g = max(ga, gb)
    n_k = t // tk
    n_pay = len(payload)

    def body(*refs):
        (a_ref, b_ref), (o_ref,), (acc_ref,), pay_refs, recv_refs, sems = _carried_refs(
            refs, 2, 1, 1, n_pay, aliased=True)
        gi, k = pl.program_id(0), pl.program_id(1)

        if n_pay:
            @pl.when(jnp.logical_and(gi == 0, k == 0))
            def _():
                _exchange(pay_refs, recv_refs, rows, pay_layer, *sems, start=True)

        @pl.when(k == 0)
        def _():
            acc_ref[...] = jnp.zeros_like(acc_ref)

        acc_ref[...] += _mm_tn(a_ref[...], b_ref[...])

        @pl.when(k == n_k - 1)
        def _():
            o_ref[...] = acc_ref[...].astype(out_dtype)

        if n_pay:
            @pl.when(jnp.logical_and(gi == g - 1, k == n_k - 1))
            def _():
                _exchange(pay_refs, recv_refs, rows, pay_layer, *sems, start=False)

    (out,), got = _carrier_call(
        body, name, (g, n_k),
        out_shape=[jax.ShapeDtypeStruct((g, m, n), out_dtype)],
        in_specs=[pl.BlockSpec((None, tk, m), (lambda gi, k: (gi, k, 0)) if ga > 1 else (lambda gi, k: (0, k, 0))),
                  pl.BlockSpec((None, tk, n), (lambda gi, k: (gi, k, 0)) if gb > 1 else (lambda gi, k: (0, k, 0)))],
        out_specs=[pl.BlockSpec((None, m, n), lambda gi, k: (gi, 0, 0))],
        scratch=[pltpu.VMEM((m, n), F32)],
        operands=(a, b), payload=payload, recvs=recvs)
    return out, got


def _time_of_row(tm):
    i = lax.broadcasted_iota(jnp.int32, (tm, 1), 0)
    return (i % 8) * (tm // 8) + i // 8


def _sublane_is(rows, s):
    return lax.broadcasted_iota(jnp.int32, (rows, 1), 0) % 8 == s


def _conv_taps(src_ref, col0, ncols, tm, tap_rows, w_ref, init_of, store):
    rc = min(ROW_CHUNK, tm)
    for cb in range(ncols // LANES):
        cs = slice(cb * LANES, (cb + 1) * LANES)
        ss = slice(col0 + cb * LANES, col0 + (cb + 1) * LANES)
        for r0 in range(0, tm, rc):
            acc = init_of(cs, rc)
            for k, row in enumerate(tap_rows):
                acc = acc + w_ref[k:k + 1, cs] * src_ref[r0 + row:r0 + row + rc, ss]
            store(r0, rc, cs, acc)


def _mixer_fwd(l, x, mod, gains, w_in_t, conv_w, vec, pool_w, w_out, *, seq, tm, act_dtype, payload=(),
               gather_layer=None):
    t, d = x.shape
    d_in = w_in_t.shape[0]
    dc = conv_w.shape[-1]
    d_mix = w_out.shape[0]
    n_taps = 31
    nt, tps = t // tm, seq // tm
    p = CONV_PREFIX
    n_pay = len(payload)

    def body(*refs):
        ins, outs, scr, pay_refs, got_refs, sems = _carried_refs(refs, 8, 7, 2, n_pay, aliased=False)
        x_ref, mod_ref, g_ref, win_ref, cw_ref, v_ref, pw_ref, wout_ref = ins
        x1_ref, h1_ref, u_ref, a1_ref, ap_ref, o_ref, tail_ref = outs
        ext_ref, car_ref = scr
        i = pl.program_id(0)
        first = (i % tps) == 0

        if n_pay:
            @pl.when(i == 0)
            def _():
                _gather_phase(_layer_of(pay_refs, [gather_layer] * n_pay), got_refs, *sems, 0)

        xv = x_ref[...]
        r = lax.rsqrt(_mean(xv * xv) + EPS)
        hv = (xv * r) * g_ref[0:1, :] * (1.0 + mod_ref[1:2, :]) + mod_ref[0:1, :]
        hb = hv.astype(act_dtype)
        h1_ref[...] = hb
        u = _mm_nt(hb, win_ref[...])
        u_ref[...] = u
        a0 = u[:, :dc] * _sigmoid(u[:, dc:2 * dc])

        @pl.when(first)
        def _():
            car_ref[...] = jnp.zeros_like(car_ref)

        @pl.when(i == 0)
        def _():
            ext_ref[p + tm:p + tm + 8, :] = jnp.zeros((8, ext_ref.shape[1]), F32)

        ext_ref[p:p + tm, 0:dc] = a0
        ext_ref[p:p + tm, dc:] = u[:, 2 * dc:]
        ext_ref[0:p, :] = jnp.where(_sublane_is(p, 0), car_ref[...], ext_ref[tm - 1:tm - 1 + p, :])

        def store(r0, rc, cs, acc):
            a1_ref[r0:r0 + rc, cs] = acc

        _conv_taps(ext_ref, 0, dc, tm, [p - 8 * (n_taps - 1 - k) for k in range(n_taps)],
                   cw_ref,
                   lambda cs, rc: jnp.broadcast_to(v_ref[0:1, cs], (rc, LANES)), store)
        a1 = a1_ref[...]
        mu = _mean(a1)
        xc = a1 - mu
        rstd = lax.rsqrt(_mean(xc * xc) + EPS)
        a2 = (xc * rstd) * v_ref[1:2, :] + v_ref[2:3, :]
        ap_ref[:, 0:dc] = (a2 * _sigmoid(a2)).astype(act_dtype)

        pos = (i % tps) * tm + _time_of_row(tm)
        for g, w in enumerate(POOL_WINDOWS):
            cs = slice(dc + g * POOL_GROUP, dc + (g + 1) * POOL_GROUP)
            s = ext_ref[p:p + tm, cs]
            for j in range(1, w):
                s = s + ext_ref[p - 8 * j:p - 8 * j + tm, cs]
            cnt = jnp.minimum(pos + 1, w).astype(F32)
            dv = s / cnt - ext_ref[p:p + tm, cs]
            q = _mm(dv, pw_ref[g])
            ap_ref[:, cs] = (q * v_ref[3:4, g * POOL_GROUP:(g + 1) * POOL_GROUP]).astype(act_dtype)

        o = _mm(ap_ref[...], wout_ref[...])
        o_ref[...] = o
        ro = lax.rsqrt(_mean(o * o) + EPS)
        x1_ref[...] = xv + (1.0 + mod_ref[2:3, :]) * ((o * ro) * g_ref[1:2, :])

        nxt = ext_ref[tm + 7:tm + 7 + p, :]
        car_ref[...] = nxt
        tail_ref[...] = nxt

        if n_pay:
            @pl.when(i == max(nt - 2, 0))
            def _():
                _gather_phase(_layer_of(pay_refs, [gather_layer] * n_pay), got_refs, *sems, 1)

            @pl.when(i == nt - 1)
            def _():
                _gather_phase(_layer_of(pay_refs, [gather_layer] * n_pay), got_refs, *sems, 2)

    row = lambda width: pl.BlockSpec((tm, width), lambda i: (i, 0))
    return _carrier_call(
        body, f"mixer_fwd_{l}", (nt,),
        out_shape=[jax.ShapeDtypeStruct((t, d), F32), jax.ShapeDtypeStruct((t, d), act_dtype),
                   jax.ShapeDtypeStruct((t, d_in), F32), jax.ShapeDtypeStruct((t, dc), F32),
                   jax.ShapeDtypeStruct((t, d_mix), act_dtype), jax.ShapeDtypeStruct((t, d), F32),
                   jax.ShapeDtypeStruct((nt, p, d_mix), F32)],
        in_specs=[row(d),
                  pl.BlockSpec((None, None, 8, d), lambda i: (l, i // tps, 0, 0)),
                  _const_spec((None, 8, d), (l, 0, 0)),
                  _const_spec((d_in, d), (0, 0)),
                  _const_spec((32, dc), (0, 0)),
                  _const_spec((None, 8, dc), (l, 0, 0)),
                  _const_spec((None, len(POOL_WINDOWS), POOL_GROUP, POOL_GROUP), (l, 0, 0, 0)),
                  _const_spec((d_mix, d), (0, 0))],
        out_specs=[row(d), row(d), row(d_in), row(dc), row(d_mix), row(d),
                   pl.BlockSpec((None, p, d_mix), lambda i: (i, 0, 0))],
        scratch=[pltpu.VMEM((p + tm + 8, d_mix), F32), pltpu.VMEM((p, d_mix), F32)],
        operands=(x, mod, gains, w_in_t, conv_w, vec, pool_w, w_out),
        payload=payload, gather_layer=gather_layer)


def _mixer_bwd(l, dx1, x, mod, gains, u, tails, a1, o, w_in_t, conv_w, vec, pool_w, w_out, *, seq, tm, act_dtype,
               payload=(), recvs=(), rows=(), pay_layer=0):
    t, d = x.shape
    d_in = w_in_t.shape[0]
    dc = conv_w.shape[-1]
    d_mix = w_out.shape[0]
    n_taps = 31
    nt, tps = t // tm, seq // tm
    p = CONV_PREFIX
    n_groups = len(POOL_WINDOWS)
    n_pay = len(payload)

    def body(*refs):
        ins, outs, scr, pay_refs, recv_refs, sems = _carried_refs(refs, 13, 8, 3, n_pay, aliased=True)
        dx1_ref, x_ref, mod_ref, g_ref, u_ref, tail_ref, a1_ref, o_ref, win_ref, cw_ref, v_ref, pw_ref, wout_ref = ins
        dx_ref, du_ref, do_ref, dmod_ref, gd_ref, gv_ref, dcw_ref, dpw_ref = outs
        ext_ref, fext_ref, fcar_ref = scr
        i = pl.program_id(0)
        j = nt - 1 - i
        first_in_seq = (j % tps) == 0
        last_in_seq = (j % tps) == tps - 1

        if n_pay:
            @pl.when(i == 0)
            def _():
                _exchange(pay_refs, recv_refs, rows, pay_layer, *sems, start=True)

        @pl.when(i == 0)
        def _():
            gd_ref[...] = jnp.zeros_like(gd_ref)
            gv_ref[...] = jnp.zeros_like(gv_ref)
            dcw_ref[...] = jnp.zeros_like(dcw_ref)
            dpw_ref[...] = jnp.zeros_like(dpw_ref)
            fext_ref[0:8, :] = jnp.zeros((8, fext_ref.shape[1]), F32)

        @pl.when(last_in_seq)
        def _():
            dmod_ref[...] = jnp.zeros_like(dmod_ref)
            fcar_ref[...] = jnp.zeros_like(fcar_ref)

        xv = x_ref[...]
        dx1v = dx1_ref[...]
        pre_g, post_g = g_ref[0:1, :], g_ref[1:2, :]

        ov = o_ref[...]
        ro = lax.rsqrt(_mean(ov * ov) + EPS)
        yo = ov * ro
        dmod_ref[2:3, :] += _colsum(dx1v * (yo * post_g))
        dn = dx1v * (1.0 + mod_ref[2:3, :])
        gd_ref[1:2, :] += _colsum(dn * yo)
        dyo = dn * post_g
        do = ro * (dyo - yo * _mean(dyo * yo))
        dob = do.astype(act_dtype)
        do_ref[...] = dob
        dap = _mm_nt(dob, wout_ref[...])

        uv = u_ref[...]
        val, gate = uv[:, :dc], uv[:, dc:2 * dc]
        sg = _sigmoid(gate)
        ext_ref[p:p + tm, 0:dc] = val * sg
        ext_ref[p:p + tm, dc:] = uv[:, 2 * dc:]
        keep = jnp.where(first_in_seq, 0.0, 1.0).astype(F32)
        ext_ref[0:p, :] = jnp.where(_sublane_is(p, 0), tail_ref[...] * keep, ext_ref[tm - 1:tm - 1 + p, :])

        a1v = a1_ref[...]
        mu = _mean(a1v)
        xc = a1v - mu
        rstd = lax.rsqrt(_mean(xc * xc) + EPS)
        xh = xc * rstd
        ln_g = v_ref[1:2, :]
        a2 = xh * ln_g + v_ref[2:3, :]
        s2 = _sigmoid(a2)
        da2 = dap[:, :dc] * (s2 * (1.0 + a2 * (1.0 - s2)))
        gv_ref[1:2, :] += _colsum(da2 * xh)
        gv_ref[2:3, :] += _colsum(da2)
        dxh = da2 * ln_g
        da1 = rstd * (dxh - _mean(dxh) - xh * _mean(dxh * xh))
        gv_ref[0:1, :] += _colsum(da1)
        last_sublane = _sublane_is(p, 7)

        def put(cs, value):
            fext_ref[8:8 + tm, cs] = value
            fext_ref[8 + tm:8 + tm + p, cs] = jnp.where(last_sublane, fcar_ref[:, cs], fext_ref[9:9 + p, cs])

        put(slice(0, dc), da1)

        for k in range(n_taps):
            row = p - 8 * (n_taps - 1 - k)
            dcw_ref[k:k + 1, :] += _colsum(fext_ref[8:8 + tm, 0:dc] * ext_ref[row:row + tm, 0:dc])

        def store(r0, rc, cs, acc):
            sgc = _sigmoid(u_ref[r0:r0 + rc, dc + cs.start:dc + cs.stop])
            vc = u_ref[r0:r0 + rc, cs]
            du_ref[r0:r0 + rc, cs] = (acc * sgc).astype(act_dtype)
            du_ref[r0:r0 + rc, dc + cs.start:dc + cs.stop] = (acc * vc * sgc * (1.0 - sgc)).astype(act_dtype)

        _conv_taps(fext_ref, 0, dc, tm, [8 + 8 * (n_taps - 1 - k) for k in range(n_taps)],
                   cw_ref,
                   lambda cs, rc: jnp.zeros((rc, LANES), F32), store)

        pos = (j % tps) * tm + _time_of_row(tm)
        for g, w in enumerate(POOL_WINDOWS):
            cs = slice(dc + g * POOL_GROUP, dc + (g + 1) * POOL_GROUP)
            gs = slice(g * POOL_GROUP, (g + 1) * POOL_GROUP)
            s = ext_ref[p:p + tm, cs]
            for jj in range(1, w):
                s = s + ext_ref[p - 8 * jj:p - 8 * jj + tm, cs]
            cnt = jnp.minimum(pos + 1, w).astype(F32)
            dv = (s / cnt - ext_ref[p:p + tm, cs]).astype(MXU_DTYPE)
            q = _mm(dv, pw_ref[g])
            dp = dap[:, cs]
            gv_ref[3:4, gs] += _colsum(dp * q)
            dq = (dp * v_ref[3:4, gs]).astype(MXU_DTYPE)
            dpw_ref[g] += _mm_tn(dv, dq)
            dd = _mm_nt(dq, pw_ref[g])
            put(cs, dd / cnt)
            dhp = fext_ref[8:8 + tm, cs]
            for jj in range(1, w):
                dhp = dhp + fext_ref[8 + 8 * jj:8 + 8 * jj + tm, cs]
            du_ref[:, dc + cs.start:dc + cs.stop] = (dhp - dd).astype(act_dtype)

        fcar_ref[...] = fext_ref[1:1 + p, :]

        dh = _mm(du_ref[...], win_ref[...])

        r = lax.rsqrt(_mean(xv * xv) + EPS)
        xn = xv * r
        dmod_ref[0:1, :] += _colsum(dh)
        dmod_ref[1:2, :] += _colsum(dh * (xn * pre_g))
        dy = dh * (1.0 + mod_ref[1:2, :])
        gd_ref[0:1, :] += _colsum(dy * xn)
        dxn = dy * pre_g
        dx_ref[...] = dx1v + r * (dxn - xn * _mean(dxn * xn))

        if n_pay:
            @pl.when(i == nt - 1)
            def _():
                _exchange(pay_refs, recv_refs, rows, pay_layer, *sems, start=False)

    rev = lambda width: pl.BlockSpec((tm, width), lambda i: (nt - 1 - i, 0))
    return _carrier_call(
        body, f"mixer_bwd_{l}", (nt,),
        out_shape=[jax.ShapeDtypeStruct((t, d), F32), jax.ShapeDtypeStruct((t, d_in), act_dtype),
                   jax.ShapeDtypeStruct((t, d), act_dtype),
                   jax.ShapeDtypeStruct((t // seq, 8, d), F32), jax.ShapeDtypeStruct((8, d), F32),
                   jax.ShapeDtypeStruct((8, dc), F32), jax.ShapeDtypeStruct((32, dc), F32),
                   jax.ShapeDtypeStruct((n_groups, POOL_GROUP, POOL_GROUP), F32)],
        in_specs=[rev(d), rev(d),
                  pl.BlockSpec((None, None, 8, d), lambda i: (l, (nt - 1 - i) // tps, 0, 0)),
                  _const_spec((None, 8, d), (l, 0, 0)),
                  rev(d_in),
                  pl.BlockSpec((None, p, d_mix), lambda i: (jnp.maximum(nt - 2 - i, 0), 0, 0)),
                  rev(dc), rev(d),
                  _const_spec((d_in, d), (0, 0)),
                  _const_spec((32, dc), (0, 0)),
                  _const_spec((None, 8, dc), (l, 0, 0)),
                  _const_spec((None, n_groups, POOL_GROUP, POOL_GROUP), (l, 0, 0, 0)),
                  _const_spec((d_mix, d), (0, 0))],
        out_specs=[rev(d), rev(d_in), rev(d),
                   pl.BlockSpec((None, 8, d), lambda i: ((nt - 1 - i) // tps, 0, 0)),
                   _acc_spec((8, d), (0, 0)), _acc_spec((8, dc), (0, 0)), _acc_spec((32, dc), (0, 0)),
                   _acc_spec((n_groups, POOL_GROUP, POOL_GROUP), (0, 0, 0))],
        scratch=[pltpu.VMEM((p + tm, d_mix), F32), pltpu.VMEM((8 + tm + p, d_mix), F32),
                 pltpu.VMEM((p, d_mix), F32)],
        operands=(dx1, x, mod, gains, u, tails, a1, o, w_in_t, conv_w, vec, pool_w, w_out),
        payload=payload, recvs=recvs)


def _ffn_fwd(l, x, mod, gains, up, fcw, down, *, seq, tm, act_dtype, payload=(), gather_layer=None, target=None):
    t, d = x.shape
    n_chunks, _, fc = up.shape
    half = n_chunks // 2
    nt, tps = t // tm, seq // tm
    p = FFN_PREFIX
    n_pay = len(payload)
    n_head = 0 if target is None else 1

    def body(*refs):
        ins, outs, scr, pay_refs, got_refs, sems = _carried_refs(refs, 6 + n_head, 6 + 2 * n_head, 2, n_pay,
                                                                 aliased=False)
        x_ref, mod_ref, g_ref, up_ref, fcw_ref, down_ref = ins[:6]
        x2_ref, h2_ref, u2_ref, u3_ref, hid_ref, o2_ref = outs[:6]
        ext_ref, car_ref = scr
        i = pl.program_id(0)
        first = (i % tps) == 0

        if n_pay:
            @pl.when(i == 0)
            def _():
                _gather_phase(_layer_of(pay_refs, [gather_layer] * n_pay), got_refs, *sems, 0)

        xv = x_ref[...]
        r = lax.rsqrt(_mean(xv * xv) + EPS)
        hv = (xv * r) * g_ref[0:1, :] * (1.0 + mod_ref[4:5, :]) + mod_ref[3:4, :]
        hb = hv.astype(act_dtype)
        h2_ref[...] = hb

        @pl.when(first)
        def _():
            car_ref[...] = jnp.zeros_like(car_ref)

        @pl.when(i == 0)
        def _():
            for n in range(n_chunks):
                ext_ref[n, p + tm:p + tm + 8, :] = jnp.zeros((8, fc), F32)

        first_sublane = _sublane_is(p, 0)

        def project(n):
            un = _mm(hb, up_ref[n])
            ext_ref[n, p:p + tm, :] = un
            u2_ref[n] = un.astype(act_dtype)
            ext_ref[n, 0:p, :] = jnp.where(first_sublane, car_ref[n], ext_ref[n, tm - 1:tm - 1 + p, :])
            car_ref[n] = ext_ref[n, tm + 7:tm + 7 + p, :]

        def conv(n):
            return (fcw_ref[n, 3:4, :] + fcw_ref[n, 0:1, :] * ext_ref[n, p - 16:p - 16 + tm, :]
                    + fcw_ref[n, 1:2, :] * ext_ref[n, p - 8:p - 8 + tm, :]
                    + fcw_ref[n, 2:3, :] * ext_ref[n, p:p + tm, :])

        for n in range(n_chunks):
            project(n)
        o2 = jnp.zeros((tm, d), F32)
        for n in range(half):
            gt = conv(n + half)
            v = conv(n)
            sg = _sigmoid(gt)
            silu = gt * sg
            u3_ref[n] = silu.astype(act_dtype)
            u3_ref[n + half] = (v * (sg * (1.0 + gt * (1.0 - sg)))).astype(act_dtype)
            hid = (silu * v).astype(act_dtype)
            hid_ref[n] = hid
            o2 = o2 + _mm(hid, down_ref[n])
        o2_ref[...] = o2
        ro = lax.rsqrt(_mean(o2 * o2) + EPS)
        x2 = xv + (1.0 + mod_ref[5:6, :]) * ((o2 * ro) * g_ref[1:2, :])
        x2_ref[...] = x2

        if n_head:
            t_ref, dy_ref, loss_ref = ins[6], outs[6], outs[7]

            @pl.when(i == 0)
            def _():
                loss_ref[...] = jnp.zeros_like(loss_ref)

            diff = x2 - t_ref[...]
            dy_ref[...] = diff / d
            part = 0.5 * jnp.sum(_mean(diff * diff), axis=0, keepdims=True)
            loss_ref[...] += jnp.broadcast_to(part, loss_ref.shape)

        if n_pay:
            @pl.when(i == max(nt - 2, 0))
            def _():
                _gather_phase(_layer_of(pay_refs, [gather_layer] * n_pay), got_refs, *sems, 1)

            @pl.when(i == nt - 1)
            def _():
                _gather_phase(_layer_of(pay_refs, [gather_layer] * n_pay), got_refs, *sems, 2)

    row = lambda width: pl.BlockSpec((tm, width), lambda i: (i, 0))
    chunked = lambda n: pl.BlockSpec((n, tm, fc), lambda i: (0, i, 0))
    head = n_head * [None]
    return _carrier_call(
        body, f"ffn_fwd_{l}", (nt,),
        out_shape=[jax.ShapeDtypeStruct((t, d), F32), jax.ShapeDtypeStruct((t, d), act_dtype),
                   jax.ShapeDtypeStruct((n_chunks, t, fc), act_dtype),
                   jax.ShapeDtypeStruct((n_chunks, t, fc), act_dtype),
                   jax.ShapeDtypeStruct((half, t, fc), act_dtype), jax.ShapeDtypeStruct((t, d), F32)]
        + [jax.ShapeDtypeStruct((t, d), F32) for _ in head] + [jax.ShapeDtypeStruct((8, LANES), F32) for _ in head],
        in_specs=[row(d),
                  pl.BlockSpec((None, None, 8, d), lambda i: (l, i // tps, 0, 0)),
                  _const_spec((None, 8, d), (l, 0, 0)),
                  _const_spec((n_chunks, d, fc), (0, 0, 0)),
                  _const_spec((n_chunks, 8, fc), (0, 0, 0)),
                  _const_spec((half, fc, d), (0, 0, 0))] + [row(d) for _ in head],
        out_specs=[row(d), row(d), chunked(n_chunks), chunked(n_chunks), chunked(half), row(d)]
        + [row(d) for _ in head] + [_acc_spec((8, LANES), (0, 0)) for _ in head],
        scratch=[pltpu.VMEM((n_chunks, p + tm + 8, fc), F32), pltpu.VMEM((n_chunks, p, fc), F32)],
        operands=(x, mod, gains, up, fcw, down) + ((target,) if n_head else ()),
        payload=payload, gather_layer=gather_layer)


def _ffn_bwd(l, dx2, x, mod, gains, u2, u3, o2, up, fcw, down, *, seq, tm, act_dtype,
             payload=(), recvs=(), rows=(), pay_layer=0):
    t, d = x.shape
    n_chunks, _, fc = up.shape
    half = n_chunks // 2
    nt, tps = t // tm, seq // tm
    p = FFN_PREFIX
    n_pay = len(payload)

    def body(*refs):
        ins, outs, scr, pay_refs, recv_refs, sems = _carried_refs(refs, 10, 6, 2, n_pay, aliased=True)
        dx2_ref, x_ref, mod_ref, g_ref, u2_ref, u3_ref, o2_ref, up_ref, fcw_ref, down_ref = ins
        dx1_ref, du2_ref, do2_ref, dmod_ref, gd_ref, dfcw_ref = outs
        fext_ref, fcar_ref = scr
        i = pl.program_id(0)
        j = nt - 1 - i
        last_in_seq = (j % tps) == tps - 1

        if n_pay:
            @pl.when(i == 0)
            def _():
                _exchange(pay_refs, recv_refs, rows, pay_layer, *sems, start=True)

        @pl.when(i == 0)
        def _():
            gd_ref[...] = jnp.zeros_like(gd_ref)
            dfcw_ref[...] = jnp.zeros_like(dfcw_ref)
            for n in range(n_chunks):
                fext_ref[n, 0:8, :] = jnp.zeros((8, fc), F32)

        @pl.when(last_in_seq)
        def _():
            dmod_ref[...] = jnp.zeros_like(dmod_ref)
            fcar_ref[...] = jnp.zeros_like(fcar_ref)

        xv = x_ref[...]
        dx2v = dx2_ref[...]
        pre_g, post_g = g_ref[0:1, :], g_ref[1:2, :]

        ov = o2_ref[...]
        ro = lax.rsqrt(_mean(ov * ov) + EPS)
        yo = ov * ro
        dmod_ref[2:3, :] += _colsum(dx2v * (yo * post_g))
        dn = dx2v * (1.0 + mod_ref[5:6, :])
        gd_ref[1:2, :] += _colsum(dn * yo)
        dyo = dn * post_g
        do = ro * (dyo - yo * _mean(dyo * yo))
        dob = do.astype(act_dtype)
        do2_ref[...] = dob

        for n in range(half):
            dhid = _mm_nt(dob, down_ref[n])
            fext_ref[n, 8:8 + tm, :] = dhid * u3_ref[n].astype(F32)
            fext_ref[n + half, 8:8 + tm, :] = dhid * u3_ref[n + half].astype(F32)

        last_sublane = _sublane_is(p, 7)
        dh = jnp.zeros((tm, d), F32)
        for n in range(n_chunks):
            fext_ref[n, 8 + tm:8 + tm + p, :] = jnp.where(last_sublane, fcar_ref[n], fext_ref[n, 9:9 + p, :])
            fcar_ref[n] = fext_ref[n, 1:1 + p, :]
            d2 = fext_ref[n, 8:8 + tm, :]
            d1 = fext_ref[n, 16:16 + tm, :]
            d0 = fext_ref[n, 24:24 + tm, :]
            u2v = u2_ref[n].astype(F32)
            dfcw_ref[n, 3:4, :] += _colsum(d2)
            dfcw_ref[n, 0:1, :] += _colsum(d0 * u2v)
            dfcw_ref[n, 1:2, :] += _colsum(d1 * u2v)
            dfcw_ref[n, 2:3, :] += _colsum(d2 * u2v)
            du2 = (fcw_ref[n, 0:1, :] * d0 + fcw_ref[n, 1:2, :] * d1 + fcw_ref[n, 2:3, :] * d2).astype(act_dtype)
            du2_ref[n] = du2
            dh = dh + _mm_nt(du2, up_ref[n])

        r = lax.rsqrt(_mean(xv * xv) + EPS)
        xn = xv * r
        dmod_ref[0:1, :] += _colsum(dh)
        dmod_ref[1:2, :] += _colsum(dh * (xn * pre_g))
        dy = dh * (1.0 + mod_ref[4:5, :])
        gd_ref[0:1, :] += _colsum(dy * xn)
        dxn = dy * pre_g
        dx1_ref[...] = dx2v + r * (dxn - xn * _mean(dxn * xn))

        if n_pay:
            @pl.when(i == nt - 1)
            def _():
                _exchange(pay_refs, recv_refs, rows, pay_layer, *sems, start=False)

    rev = lambda width: pl.BlockSpec((tm, width), lambda i: (nt - 1 - i, 0))
    chunked = pl.BlockSpec((n_chunks, tm, fc), lambda i: (0, nt - 1 - i, 0))
    return _carrier_call(
        body, f"ffn_bwd_{l}", (nt,),
        out_shape=[jax.ShapeDtypeStruct((t, d), F32), jax.ShapeDtypeStruct((n_chunks, t, fc), act_dtype),
                   jax.ShapeDtypeStruct((t, d), act_dtype),
                   jax.ShapeDtypeStruct((t // seq, 8, d), F32), jax.ShapeDtypeStruct((8, d), F32),
                   jax.ShapeDtypeStruct((n_chunks, 8, fc), F32)],
        in_specs=[rev(d), rev(d),
                  pl.BlockSpec((None, None, 8, d), lambda i: (l, (nt - 1 - i) // tps, 0, 0)),
                  _const_spec((None, 8, d), (l, 0, 0)),
                  chunked, chunked, rev(d),
                  _const_spec((n_chunks, d, fc), (0, 0, 0)),
                  _const_spec((n_chunks, 8, fc), (0, 0, 0)),
                  _const_spec((half, fc, d), (0, 0, 0))],
        out_specs=[rev(d), chunked, rev(d),
                   pl.BlockSpec((None, 8, d), lambda i: ((nt - 1 - i) // tps, 0, 0)),
                   _acc_spec((8, d), (0, 0)), _acc_spec((n_chunks, 8, fc), (0, 0, 0))],
        scratch=[pltpu.VMEM((n_chunks, 8 + tm + p, fc), F32), pltpu.VMEM((n_chunks, p, fc), F32)],
        operands=(dx2, x, mod, gains, u2, u3, o2, up, fcw, down),
        payload=payload, recvs=recvs)


def _pad_rows(a, rows):
    pad = [(0, 0)] * a.ndim
    pad[-2] = (0, rows - a.shape[-2])
    return jnp.pad(a, pad)


def kernel(x, c, ada_w, ada_b, pre_mix_g, post_mix_g, w_in, conv_w, conv_b, conv_ln_g, conv_ln_b, pool_w, pool_scale, w_out, pre_ffn_g, post_ffn_g, ffn_up, ffn_conv_w, ffn_conv_b, ffn_down, loss_target, m_ada_w, m_ada_b, m_pre_mix_g, m_post_mix_g, m_w_in, m_conv_w, m_conv_b, m_conv_ln_g, m_conv_ln_b, m_pool_w, m_pool_scale, m_w_out, m_pre_ffn_g, m_post_ffn_g, m_ffn_up, m_ffn_conv_w, m_ffn_conv_b, m_ffn_down, v_ada_w, v_ada_b, v_pre_mix_g, v_post_mix_g, v_w_in, v_conv_w, v_conv_b, v_conv_ln_g, v_conv_ln_b, v_pool_w, v_pool_scale, v_w_out, v_pre_ffn_g, v_post_ffn_g, v_ffn_up, v_ffn_conv_w, v_ffn_conv_b, v_ffn_down):
    bl, seq, d = x.shape
    n_layers = ada_w.shape[0]
    t = bl * seq
    dc = conv_b.shape[1]
    d_in = w_in.shape[2] * N_DEV
    d_mix = w_out.shape[1] * N_DEV
    n_taps = conv_w.shape[1]
    fc = ffn_up.shape[2]
    half = N_DEV // 2
    ada_cols = ada_w.shape[2]
    n_mod = ada_cols * N_DEV // d
    assert pool_scale.shape[1] == dc and n_taps == 31 and n_mod == 6 and ffn_conv_w.shape[1] == 3
    assert pool_w.shape[1:] == (len(POOL_WINDOWS), POOL_GROUP, POOL_GROUP)
    tm = TILE_TOKENS
    assert seq % tm == 0 and CONV_PREFIX <= tm - 8
    tk = 2048 if t % 2048 == 0 else tm
    act = MXU_DTYPE

    def tile_order(a, inverse=False):
        shape = (t // tm, tm // 8, 8, d) if inverse else (t // tm, 8, tm // 8, d)
        return a.reshape(shape).transpose(0, 2, 1, 3).reshape(t, d)

    ax = lax.axis_index
    me = 4 * ax("x") + 2 * ax("y") + ax("c")

    (c_all,) = _all_gather([c], "gather_c")
    c_all = c_all.reshape(N_DEV * bl, d)
    ada_b_cols = lax.dynamic_slice_in_dim(ada_b, me * ada_cols, ada_cols, axis=1)
    mod_cols = _ada_fwd(c_all, ada_w, ada_b_cols)

    w_in_s = w_in.astype(act).transpose(0, 2, 1)
    w_out_s, up_s, down_s = w_out.astype(act), ffn_up.astype(act), ffn_down.astype(act)
    conv_w_s = _pad_rows(conv_w, 32)
    fcw_s = _pad_rows(jnp.concatenate(
        [ffn_conv_w, lax.dynamic_slice_in_dim(ffn_conv_b, me * fc, fc, axis=1)[:, None, :]], axis=1), 8)

    mixer_shards = (w_in_s, w_out_s, conv_w_s)
    late_shards = (down_s, fcw_s)

    def mixer_weights(g_w_in, g_w_out, g_conv_w):
        return (g_w_in.reshape(d_in, d), g_conv_w.transpose(1, 0, 2).reshape(32, dc), g_w_out.reshape(d_mix, d))

    def late_weights(g_down, g_fcw):
        return (g_fcw, g_down.reshape(half, fc, d))

    g0 = _all_gather([mod_cols, *mixer_shards], "gather_w0", layers=[None, 0, 0, 0])
    mod_all = g0[0].transpose(1, 2, 0, 3).reshape(n_layers, N_DEV * bl, n_mod, d)
    mod = _pad_rows(lax.dynamic_slice_in_dim(mod_all, me * bl, bl, axis=1), 8)
    mixers, lates, ups = [mixer_weights(*g0[1:])], [], []
    gains_mix = _pad_rows(jnp.stack([pre_mix_g, post_mix_g], axis=1), 8)
    gains_ffn = _pad_rows(jnp.stack([pre_ffn_g, post_ffn_g], axis=1), 8)
    vec = _pad_rows(jnp.stack([conv_b, conv_ln_g, conv_ln_b, pool_scale], axis=1), 8)
    pool_w_b = pool_w.astype(act)

    kw = dict(seq=seq, tm=tm, act_dtype=act)
    xs = tile_order(x.reshape(t, d))
    saved = []
    for l in range(n_layers):
        w_in_t, cw, w_o = mixers[l]
        (x1, h1, u, a1, ap, o, tails), got = _mixer_fwd(
            l, xs, mod, gains_mix, w_in_t, cw, vec, pool_w_b, w_o, **kw,
            payload=(up_s, *late_shards) if l == 0 else (up_s,), gather_layer=l)
        ups.append(got[0])
        if l == 0:
            lates.append(late_weights(*got[1:]))
        up_l, (fcw_l, down_l) = ups[l], lates[l]
        last = l + 1 == n_layers
        (x2, h2, u2, u3, hid, o2, *head), got = _ffn_fwd(
            l, x1, mod, gains_ffn, up_l, fcw_l, down_l, **kw,
            payload=() if last else (*mixer_shards, *late_shards), gather_layer=l + 1,
            target=tile_order(loss_target.reshape(t, d)) if last else None)
        if got:
            mixers.append(mixer_weights(*got[:3]))
            lates.append(late_weights(*got[3:]))
        saved.append((xs, h1, u, tails, a1, ap, o, x1, h2, u2, u3, hid, o2))
        xs = x2

    dx, loss_part = head

    def landing(shard, dtype):
        return lax.empty((N_DEV, n_layers) + shard.shape[1:], dtype)

    r_up, r_down = landing(ffn_up, act), landing(ffn_down, act)
    r_w_in, r_w_out = landing(w_in, act), landing(w_out, act)
    r_conv_w, r_fcw = landing(conv_w, F32), landing(ffn_conv_w, F32)
    def cuts(l):
        first, second = (5.5, 7.8) if l == 0 else (2.5, 6)
        return int(first * d / 10) // 16 * 16, int(second * d / 10) // 16 * 16

    dmods, smalls = [], []
    pending = None
    for l in reversed(range(n_layers)):
        xin, h1, u, tails, a1, ap, o, x1, h2, u2, u3, hid, o2 = saved[l]
        (w_in_t, cw, w_o), (fcw_l, down_l), up_l = mixers[l], lates[l], ups[l]
        a, b = cuts(l + 1) if pending else (0, 0)
        carried = dict(pay_layer=l + 1)
        (dx1, du2, do2, dmod_b, gd_b, dfcw), got = _ffn_bwd(
            l, dx, x1, mod, gains_ffn, u2, u3, o2, up_l, fcw_l, down_l, **kw, **carried,
            payload=(pending["up"], pending["w_in"], pending["w_out"], pending["conv_w"], pending["fcw"])
            if pending else (),
            recvs=(r_up, r_w_in, r_w_out, r_conv_w, r_fcw) if pending else (),
            rows=((a, b - a), None, None, None, None))
        if pending:
            r_up, r_w_in, r_w_out, r_conv_w, r_fcw = got
        p_up, got = _matmul_tn(f"dw_up_{l}", h2[None], du2, tk, act, **carried,
                               payload=(pending["up"],) if pending else (), recvs=(r_up,) if pending else (),
                               rows=((b, d - b),))
        if pending:
            (r_up,) = got
        p_down = _matmul_tn(f"dw_down_{l}", hid, do2[None], tk, act)[0].reshape(N_DEV, fc // 2, d)
        a, b = cuts(l)
        (dx, du, do, dmod_a, gd_a, gv, dcw, dpw), (r_down, r_up) = _mixer_bwd(
            l, dx1, xin, mod, gains_mix, u, tails, a1, o, w_in_t, cw, vec, pool_w_b, w_o, **kw,
            payload=(p_down, p_up), recvs=(r_down, r_up), rows=(None, (0, a)), pay_layer=l)
        last = dict(payload=(p_up,), recvs=(r_up,), pay_layer=0) if l == 0 else {}
        dw_in, got = _matmul_tn(f"dw_in_{l}", h1[None], du[None], tk, act, **last, rows=((a, b - a),))
        if l == 0:
            (r_up,) = got
            last["recvs"] = (r_up,)
        dw_out, got = _matmul_tn(f"dw_out_{l}", ap[None], do[None], tk, act, **last, rows=((b, d - b),))
        if l == 0:
            (r_up,) = got
        pending = dict(up=p_up, w_in=dw_in[0].reshape(d, N_DEV, d_in // N_DEV).transpose(1, 0, 2),
                       w_out=dw_out[0].reshape(N_DEV, d_mix // N_DEV, d),
                       conv_w=dcw[:n_taps].reshape(n_taps, N_DEV, dc // N_DEV).transpose(1, 0, 2),
                       fcw=dfcw[:, :3, :])
        dmods.append(jnp.concatenate([dmod_a[:, 0:3], dmod_b[:, 0:3]], axis=1).reshape(bl, n_mod * d))
        smalls.append(jnp.concatenate(
            [gd_a[0], gd_a[1], gv[0], gv[1], gv[2], gv[3], gd_b[0], gd_b[1], dfcw[:, 3, :].reshape(-1),
             dpw.reshape(-1)]))
    dmods.reverse()
    smalls.reverse()

    dmod_loc = jnp.stack(dmods)
    small_loc = jnp.stack(smalls)
    n_small = small_loc.shape[1]
    small_cols = 8 * LANES if (n_layers * n_small) % (8 * LANES) == 0 else LANES
    (r_w_in, r_w_out, r_conv_w, r_fcw), (g_dmod, g_small, g_loss) = _exchange_and_gather(
        (pending["w_in"], pending["w_out"], pending["conv_w"], pending["fcw"]),
        (r_w_in, r_w_out, r_conv_w, r_fcw), (None, None, None, None), 0,
        [dmod_loc.astype(act), small_loc.reshape(-1, small_cols).astype(act), loss_part], "exchange_tail")
    g_dmod = g_dmod.astype(F32)

    loss = _sum_parts(g_loss)[0, 0]

    def flat2(a):
        return a.reshape(-1, a.shape[-1])

    def update(name, parts, w, m, v):
        outs = _adam_reduce(name, parts.reshape(parts.shape[0], -1, w.shape[-1]), flat2(w), flat2(m), flat2(v))
        return [o_.reshape(w.shape) for o_ in outs]

    res = {}
    res["w_in"] = update("adam_w_in", r_w_in, w_in, m_w_in, v_w_in)
    res["w_out"] = update("adam_w_out", r_w_out, w_out, m_w_out, v_w_out)
    res["ffn_up"] = update("adam_ffn_up", r_up, ffn_up, m_ffn_up, v_ffn_up)
    res["ffn_down"] = update("adam_ffn_down", r_down, ffn_down, m_ffn_down, v_ffn_down)
    res["conv_w"] = update("adam_conv_w", r_conv_w, conv_w, m_conv_w, v_conv_w)
    res["ffn_conv_w"] = update("adam_ffn_conv_w", r_fcw, ffn_conv_w, m_ffn_conv_w, v_ffn_conv_w)

    dmod_all = g_dmod.transpose(1, 0, 2, 3).reshape(n_layers, N_DEV * bl, n_mod * d)
    dmod_cols = lax.dynamic_slice_in_dim(dmod_all, me * ada_cols, ada_cols, axis=2)
    res["ada_w"] = list(_ada_bwd(c_all.T, dmod_cols, ada_w, m_ada_w, v_ada_w))
    res["ada_b"] = update("adam_ada_b", dmod_all.transpose(1, 0, 2), ada_b, m_ada_b, v_ada_b)

    small_names = ["pre_mix_g", "post_mix_g", "conv_b", "conv_ln_g", "conv_ln_b", "pool_scale", "pre_ffn_g",
                   "post_ffn_g", "ffn_conv_b", "pool_w"]
    small_w = [pre_mix_g, post_mix_g, conv_b, conv_ln_g, conv_ln_b, pool_scale, pre_ffn_g, post_ffn_g,
               ffn_conv_b, pool_w]
    small_m = [m_pre_mix_g, m_post_mix_g, m_conv_b, m_conv_ln_g, m_conv_ln_b, m_pool_scale, m_pre_ffn_g,
               m_post_ffn_g, m_ffn_conv_b, m_pool_w]
    small_v = [v_pre_mix_g, v_post_mix_g, v_conv_b, v_conv_ln_g, v_conv_ln_b, v_pool_scale, v_pre_ffn_g,
               v_post_ffn_g, v_ffn_conv_b, v_pool_w]

    def pack(arrs):
        return jnp.concatenate([a.reshape(n_layers, -1) for a in arrs], axis=1).reshape(-1, small_cols)

    outs = _adam_reduce("adam_small", g_small, pack(small_w), pack(small_m), pack(small_v))
    outs = [o_.reshape(n_layers, n_small) for o_ in outs]
    off = 0
    for name, w in zip(small_names, small_w):
        size = w[0].size
        res[name] = [o_[:, off:off + size].reshape(w.shape) for o_ in outs]
        off += size

    order = ["ada_w", "ada_b", "pre_mix_g", "post_mix_g", "w_in", "conv_w", "conv_b", "conv_ln_g", "conv_ln_b",
             "pool_w", "pool_scale", "w_out", "pre_ffn_g", "post_ffn_g", "ffn_up", "ffn_conv_w", "ffn_conv_b",
             "ffn_down"]
    grad_x = tile_order(dx, inverse=True).reshape(bl, seq, d)
    return (loss, grad_x, *[res[n][0] for n in order], *[res[n][1] for n in order],
            *[res[n][2] for n in order], *[res[n][3] for n in order])
```

```python
import jax
import jax.numpy as jnp
from jax import lax
from jax.experimental import pallas as pl
from jax.experimental.pallas import tpu as pltpu

N_DEV = 8
EPS = 1e-6
POOL_WINDOWS = (2, 4, 8, 16)
POOL_GROUP = 128
TILE_TOKENS = 256
CONV_PREFIX = 8 * 30
FFN_PREFIX = 8 * 2
LANES = 128
ROW_CHUNK = 128
MXU_DTYPE = jnp.bfloat16
VMEM_LIMIT = 60 * 1024 * 1024

ADAM_LR = 0.001
ADAM_B1 = 0.9
ADAM_B2 = 0.999
ADAM_EPS = 1e-08
ADAM_WD = 0.01
ADAM_STEP = 10

MESH = pl.DeviceIdType.MESH
F32 = jnp.float32


def _mm(a, b):
    return jnp.dot(a.astype(MXU_DTYPE), b.astype(MXU_DTYPE), preferred_element_type=F32)


def _mm_nt(a, b):
    return lax.dot_general(a.astype(MXU_DTYPE), b.astype(MXU_DTYPE), (((1,), (1,)), ((), ())),
                           preferred_element_type=F32)


def _mm_tn(a, b):
    return lax.dot_general(a.astype(MXU_DTYPE), b.astype(MXU_DTYPE), (((0,), (0,)), ((), ())),
                           preferred_element_type=F32)


def _mean(v):
    return jnp.mean(v, axis=-1, keepdims=True)


def _colsum(v):
    return jnp.sum(v, axis=0, keepdims=True)


def _sigmoid(v):
    return jax.nn.sigmoid(v)


def _params(n_grid=1):
    return pltpu.CompilerParams(dimension_semantics=("arbitrary",) * n_grid, vmem_limit_bytes=VMEM_LIMIT)


def _const_spec(shape, index):
    return pl.BlockSpec(shape, lambda *_: index, pipeline_mode=pl.Buffered(1))


def _acc_spec(shape, index):
    return pl.BlockSpec(shape, lambda *_: index)


def _position():
    x, y, c = lax.axis_index("x"), lax.axis_index("y"), lax.axis_index("c")
    return x, y, c


def _gather_phase(ins, outs, send_sems, recv_sems, local_sems, phase):
    n = len(ins)
    x, y, c = _position()
    me, sibling = (x, y, c), (x, y, 1 - c)
    chips = [(1 - x, y), (x, 1 - y), (1 - x, 1 - y)]

    def slot(k, px, py, pc):
        return outs[k].at[4 * px + 2 * py + pc]

    def copy(k, s, block, to, src=None):
        return pltpu.make_async_remote_copy(
            src_ref=slot(k, *block) if src is None else src, dst_ref=slot(k, *block),
            send_sem=send_sems.at[k, s], recv_sem=recv_sems.at[k, s], device_id=to, device_id_type=MESH)

    mine = [pltpu.make_async_copy(ins[k], slot(k, *me), local_sems.at[k]) for k in range(n)]
    first = []
    for k in range(n):
        first.append(copy(k, 0, me, sibling, src=ins[k]))
        first += [copy(k, 1 + j, me, (*chip, c), src=ins[k]) for j, chip in enumerate(chips)]
    passed = [copy(k, 4 + j, (*chip, c), sibling) for j, chip in enumerate(chips) for k in range(n)]
    if phase == 0:
        for cp in mine + first:
            cp.start()
    elif phase == 1:
        for j, chip in enumerate(chips):
            for k in range(n):
                copy(k, 1 + j, (*chip, c), me).wait_recv()
                copy(k, 4 + j, (*chip, c), sibling).start()
    else:
        for k in range(n):
            copy(k, 0, sibling, me).wait_recv()
            for j, chip in enumerate(chips):
                copy(k, 4 + j, (*chip, 1 - c), me).wait_recv()
        for cp in first + passed:
            cp.wait_send()
        for cp in mine:
            cp.wait()


def _gather_scratch(n):
    return [pltpu.SemaphoreType.DMA((n, 7)), pltpu.SemaphoreType.DMA((n, 7)), pltpu.SemaphoreType.DMA((n,))]


def _layer_of(refs, layers):
    return [r if lay is None else r.at[lay] for r, lay in zip(refs, layers)]


def _gathered_shapes(arrs, layers):
    return [jax.ShapeDtypeStruct((N_DEV,) + (a.shape if lay is None else a.shape[1:]), a.dtype)
            for a, lay in zip(arrs, layers)]


def _all_gather(arrs, name, layers=None):
    n = len(arrs)
    layers = [None] * n if layers is None else layers

    def body(*refs):
        for phase in range(3):
            _gather_phase(_layer_of(refs[:n], layers), refs[n:2 * n], *refs[2 * n:], phase)

    any_spec = pl.BlockSpec(memory_space=pl.ANY)
    return pl.pallas_call(
        body, name=name,
        out_shape=_gathered_shapes(arrs, layers),
        in_specs=[any_spec] * n, out_specs=[any_spec] * n,
        scratch_shapes=_gather_scratch(n),
    )(*arrs)


def _exchange(payload, recvs, rows, layer, send_sems, recv_sems, local_sems, start):
    x, y, c = _position()
    me = 4 * x + 2 * y + c
    for k, (src, recv) in enumerate(zip(payload, recvs)):
        def block(ref, *index):
            return ref.at[index] if rows[k] is None else ref.at[(*index, pl.ds(*rows[k]))]

        local = pltpu.make_async_copy(block(src, me), block(recv, me, layer), local_sems.at[k])
        if start:
            local.start()
        else:
            local.wait()
        for j in range(1, N_DEV):
            px = (1 - x) if (j & 4) else x
            py = (1 - y) if (j & 2) else y
            pc = (1 - c) if (j & 1) else c
            peer = 4 * px + 2 * py + pc
            landing = block(recv, me, layer) if start else block(recv, peer, layer)
            cp = pltpu.make_async_remote_copy(
                src_ref=block(src, peer), dst_ref=landing, send_sem=send_sems.at[k, j - 1],
                recv_sem=recv_sems.at[k, j - 1], device_id=(px, py, pc), device_id_type=MESH)
            if start:
                cp.start()
            else:
                cp.wait()


def _exchange_scratch(n):
    return [pltpu.SemaphoreType.DMA((n, N_DEV - 1)), pltpu.SemaphoreType.DMA((n, N_DEV - 1)),
            pltpu.SemaphoreType.DMA((n,))]


def _exchange_and_gather(payload, recvs, rows, layer, arrs, name):
    n, m = len(payload), len(arrs)

    def body(*refs):
        pay, srcs = refs[:n], refs[2 * n:2 * n + m]
        outs, got = refs[2 * n + m:3 * n + m], refs[3 * n + m:3 * n + 2 * m]
        xsems, gsems = refs[3 * n + 2 * m:3 * n + 2 * m + 3], refs[3 * n + 2 * m + 3:]
        _exchange(pay, outs, rows, layer, *xsems, start=True)
        for phase in range(3):
            _gather_phase(srcs, got, *gsems, phase)
        _exchange(pay, outs, rows, layer, *xsems, start=False)

    any_spec = pl.BlockSpec(memory_space=pl.ANY)
    res = pl.pallas_call(
        body, name=name,
        out_shape=[jax.ShapeDtypeStruct(r.shape, r.dtype) for r in recvs]
        + [jax.ShapeDtypeStruct((N_DEV,) + a.shape, a.dtype) for a in arrs],
        in_specs=[any_spec] * (2 * n + m), out_specs=[any_spec] * (n + m),
        input_output_aliases={n + k: k for k in range(n)},
        scratch_shapes=_exchange_scratch(n) + _gather_scratch(m),
    )(*payload, *recvs, *arrs)
    return res[:n], res[n:]


def _carried_refs(refs, n_in, n_out, n_scratch, n_pay, aliased):
    ins = refs[:n_in]
    pay = refs[n_in:n_in + n_pay]
    o0 = n_in + (2 * n_pay if aliased else n_pay)
    outs = refs[o0:o0 + n_out]
    recvs = refs[o0 + n_out:o0 + n_out + n_pay]
    s0 = o0 + n_out + n_pay
    return ins, outs, refs[s0:s0 + n_scratch], pay, recvs, refs[s0 + n_scratch:]


def _carrier_call(body, name, grid, in_specs, out_specs, out_shape, scratch, operands, payload, recvs=None,
                  gather_layer=None):
    n_in, n_out, n_pay = len(in_specs), len(out_specs), len(payload)
    any_spec = pl.BlockSpec(memory_space=pl.ANY)
    if recvs is None:
        landing = _gathered_shapes(payload, [gather_layer] * n_pay)
        extra_in, aliases = list(payload), {}
        sems = _gather_scratch(n_pay) if n_pay else []
    else:
        landing = [jax.ShapeDtypeStruct(r.shape, r.dtype) for r in recvs]
        extra_in = list(payload) + list(recvs)
        aliases = {n_in + n_pay + k: n_out + k for k in range(n_pay)}
        sems = _exchange_scratch(n_pay) if n_pay else []
    res = pl.pallas_call(
        body, name=name, grid=grid,
        out_shape=list(out_shape) + landing,
        in_specs=list(in_specs) + [any_spec] * len(extra_in),
        out_specs=list(out_specs) + [any_spec] * n_pay,
        input_output_aliases=aliases,
        scratch_shapes=list(scratch) + sems,
        compiler_params=_params(len(grid)),
    )(*operands, *extra_in)
    return res[:n_out], res[n_out:]


def _adamw(w, g, m, v):
    m = ADAM_B1 * m + (1.0 - ADAM_B1) * g
    v = ADAM_B2 * v + (1.0 - ADAM_B2) * jnp.square(g)
    m_hat = m / (1.0 - ADAM_B1 ** ADAM_STEP)
    v_hat = v / (1.0 - ADAM_B2 ** ADAM_STEP)
    delta = -ADAM_LR * (m_hat / (jnp.sqrt(v_hat) + ADAM_EPS) + ADAM_WD * w)
    return delta, m, v


def _ada_fwd(c_all, ada_w, ada_b_cols):
    n_layers, d, cols = ada_w.shape
    b = c_all.shape[0]

    def body(c_ref, w_ref, b_ref, o_ref):
        cv = c_ref[...]
        act = cv * _sigmoid(cv)
        o_ref[...] = jnp.dot(act, w_ref[...], preferred_element_type=F32,
                             precision=lax.Precision.HIGHEST) + b_ref[...]

    return pl.pallas_call(
        body, name="ada_fwd", grid=(n_layers,),
        out_shape=jax.ShapeDtypeStruct((n_layers, b, cols), F32),
        in_specs=[pl.BlockSpec((b, d), lambda l: (0, 0)),
                  pl.BlockSpec((None, d, cols), lambda l: (l, 0, 0)),
                  pl.BlockSpec((None, 1, cols), lambda l: (l, 0, 0))],
        out_specs=pl.BlockSpec((None, b, cols), lambda l: (l, 0, 0)),
        compiler_params=_params(1),
    )(c_all, ada_w, ada_b_cols.reshape(n_layers, 1, cols))


def _ada_bwd(c_all_t, dmod_cols, w, m, v):
    n_layers, d, cols = w.shape
    b = c_all_t.shape[1]
    td = 256 if d % 256 == 0 else d

    def body(c_ref, dm_ref, w_ref, m_ref, v_ref, g_ref, dl_ref, nm_ref, nv_ref):
        cv = c_ref[...]
        act = cv * _sigmoid(cv)
        g = jnp.dot(act, dm_ref[...], preferred_element_type=F32, precision=lax.Precision.HIGHEST)
        delta, nm, nv = _adamw(w_ref[...], g, m_ref[...], v_ref[...])
        g_ref[...] = g
        dl_ref[...] = delta
        nm_ref[...] = nm
        nv_ref[...] = nv

    blk = pl.BlockSpec((None, td, cols), lambda l, i: (l, i, 0))
    shp = jax.ShapeDtypeStruct(w.shape, F32)
    return pl.pallas_call(
        body, name="ada_bwd", grid=(n_layers, d // td),
        out_shape=[shp] * 4,
        in_specs=[pl.BlockSpec((td, b), lambda l, i: (i, 0)),
                  pl.BlockSpec((None, b, cols), lambda l, i: (l, 0, 0)), blk, blk, blk],
        out_specs=[blk] * 4,
        compiler_params=_params(2),
    )(c_all_t, dmod_cols, w, m, v)


def _row_tile(rows, cols, budget=128 * 1024, step=8):
    best = None
    for t in range(step, rows + 1, step):
        if rows % t == 0 and t * cols <= budget:
            best = t
    return best if best is not None else rows


def _adam_reduce(name, parts, w, m, v):
    p, rows, cols = parts.shape
    tr = _row_tile(rows, cols, budget=(256 * 1024) // max(1, p // 4), step=8 if parts.dtype == F32 else 16)

    def body(p_ref, w_ref, m_ref, v_ref, g_ref, dl_ref, nm_ref, nv_ref):
        g = p_ref[0].astype(F32)
        for k in range(1, p):
            g = g + p_ref[k].astype(F32)
        delta, nm, nv = _adamw(w_ref[...], g, m_ref[...], v_ref[...])
        g_ref[...] = g
        dl_ref[...] = delta
        nm_ref[...] = nm
        nv_ref[...] = nv

    blk = pl.BlockSpec((tr, cols), lambda i: (i, 0))
    shp = jax.ShapeDtypeStruct((rows, cols), F32)
    return pl.pallas_call(
        body, name=name, grid=(rows // tr,),
        out_shape=[shp] * 4,
        in_specs=[pl.BlockSpec((p, tr, cols), lambda i: (0, i, 0)), blk, blk, blk],
        out_specs=[blk] * 4,
        compiler_params=_params(1),
    )(parts, w, m, v)


def _sum_parts(parts):
    p = parts.shape[0]

    def body(p_ref, o_ref):
        acc = p_ref[0]
        for k in range(1, p):
            acc = acc + p_ref[k]
        o_ref[...] = acc

    return pl.pallas_call(body, name="loss_sum", out_shape=jax.ShapeDtypeStruct(parts.shape[1:], F32))(parts)


def _matmul_tn(name, a, b, tk, out_dtype, payload=(), recvs=(), rows=(), pay_layer=0):
    ga, t, m = a.shape
    gb, _, n = b.shape
    g = max(ga, gb)
    n_k = t // tk
    n_pay = len(payload)

    def body(*refs):
        (a_ref, b_ref), (o_ref,), (acc_ref,), pay_refs, recv_refs, sems = _carried_refs(
            refs, 2, 1, 1, n_pay, aliased=True)
        gi, k = pl.program_id(0), pl.program_id(1)

        if n_pay:
            @pl.when(jnp.logical_and(gi == 0, k == 0))
            def _():
                _exchange(pay_refs, recv_refs, rows, pay_layer, *sems, start=True)

        @pl.when(k == 0)
        def _():
            acc_ref[...] = jnp.zeros_like(acc_ref)

        acc_ref[...] += _mm_tn(a_ref[...], b_ref[...])

        @pl.when(k == n_k - 1)
        def _():
            o_ref[...] = acc_ref[...].astype(out_dtype)

        if n_pay:
            @pl.when(jnp.logical_and(gi == g - 1, k == n_k - 1))
            def _():
                _exchange(pay_refs, recv_refs, rows, pay_layer, *sems, start=False)

    (out,), got = _carrier_call(
        body, name, (g, n_k),
        out_shape=[jax.ShapeDtypeStruct((g, m, n), out_dtype)],
        in_specs=[pl.BlockSpec((None, tk, m), (lambda gi, k: (gi, k, 0)) if ga > 1 else (lambda gi, k: (0, k, 0))),
                  pl.BlockSpec((None, tk, n), (lambda gi, k: (gi, k, 0)) if gb > 1 else (lambda gi, k: (0, k, 0)))],
        out_specs=[pl.BlockSpec((None, m, n), lambda gi, k: (gi, 0, 0))],
        scratch=[pltpu.VMEM((m, n), F32)],
        operands=(a, b), payload=payload, recvs=recvs)
    return out, got


def _time_of_row(tm):
    i = lax.broadcasted_iota(jnp.int32, (tm, 1), 0)
    return (i % 8) * (tm // 8) + i // 8


def _sublane_is(rows, s):
    return lax.broadcasted_iota(jnp.int32, (rows, 1), 0) % 8 == s


def _conv_taps(src_ref, col0, ncols, tm, tap_rows, w_ref, init_of, store):
    rc = min(ROW_CHUNK, tm)
    for cb in range(ncols // LANES):
        cs = slice(cb * LANES, (cb + 1) * LANES)
        ss = slice(col0 + cb * LANES, col0 + (cb + 1) * LANES)
        for r0 in range(0, tm, rc):
            acc = init_of(cs, rc)
            for k, row in enumerate(tap_rows):
                acc = acc + w_ref[k:k + 1, cs] * src_ref[r0 + row:r0 + row + rc, ss]
            store(r0, rc, cs, acc)


def _mixer_fwd(l, x, mod, gains, w_in_t, conv_w, vec, pool_w, w_out, *, seq, tm, act_dtype, payload=(),
               gather_layer=None):
    t, d = x.shape
    d_in = w_in_t.shape[0]
    dc = conv_w.shape[-1]
    d_mix = w_out.shape[0]
    n_taps = 31
    nt, tps = t // tm, seq // tm
    p = CONV_PREFIX
    n_pay = len(payload)

    def body(*refs):
        ins, outs, scr, pay_refs, got_refs, sems = _carried_refs(refs, 8, 7, 2, n_pay, aliased=False)
        x_ref, mod_ref, g_ref, win_ref, cw_ref, v_ref, pw_ref, wout_ref = ins
        x1_ref, h1_ref, u_ref, a1_ref, ap_ref, o_ref, tail_ref = outs
        ext_ref, car_ref = scr
        i = pl.program_id(0)
        first = (i % tps) == 0

        if n_pay:
            @pl.when(i == 0)
            def _():
                _gather_phase(_layer_of(pay_refs, [gather_layer] * n_pay), got_refs, *sems, 0)

        xv = x_ref[...]
        r = lax.rsqrt(_mean(xv * xv) + EPS)
        hv = (xv * r) * g_ref[0:1, :] * (1.0 + mod_ref[1:2, :]) + mod_ref[0:1, :]
        hb = hv.astype(act_dtype)
        h1_ref[...] = hb
        u = _mm_nt(hb, win_ref[...])
        u_ref[...] = u
        a0 = u[:, :dc] * _sigmoid(u[:, dc:2 * dc])

        @pl.when(first)
        def _():
            car_ref[...] = jnp.zeros_like(car_ref)

        @pl.when(i == 0)
        def _():
            ext_ref[p + tm:p + tm + 8, :] = jnp.zeros((8, ext_ref.shape[1]), F32)

        ext_ref[p:p + tm, 0:dc] = a0
        ext_ref[p:p + tm, dc:] = u[:, 2 * dc:]
        ext_ref[0:p, :] = jnp.where(_sublane_is(p, 0), car_ref[...], ext_ref[tm - 1:tm - 1 + p, :])

        def store(r0, rc, cs, acc):
            a1_ref[r0:r0 + rc, cs] = acc

        _conv_taps(ext_ref, 0, dc, tm, [p - 8 * (n_taps - 1 - k) for k in range(n_taps)],
                   cw_ref,
                   lambda cs, rc: jnp.broadcast_to(v_ref[0:1, cs], (rc, LANES)), store)
        a1 = a1_ref[...]
        mu = _mean(a1)
        xc = a1 - mu
        rstd = lax.rsqrt(_mean(xc * xc) + EPS)
        a2 = (xc * rstd) * v_ref[1:2, :] + v_ref[2:3, :]
        ap_ref[:, 0:dc] = (a2 * _sigmoid(a2)).astype(act_dtype)

        pos = (i % tps) * tm + _time_of_row(tm)
        for g, w in enumerate(POOL_WINDOWS):
            cs = slice(dc + g * POOL_GROUP, dc + (g + 1) * POOL_GROUP)
            s = ext_ref[p:p + tm, cs]
            for j in range(1, w):
                s = s + ext_ref[p - 8 * j:p - 8 * j + tm, cs]
            cnt = jnp.minimum(pos + 1, w).astype(F32)
            dv = s / cnt - ext_ref[p:p + tm, cs]
            q = _mm(dv, pw_ref[g])
            ap_ref[:, cs] = (q * v_ref[3:4, g * POOL_GROUP:(g + 1) * POOL_GROUP]).astype(act_dtype)

        o = _mm(ap_ref[...], wout_ref[...])
        o_ref[...] = o
        ro = lax.rsqrt(_mean(o * o) + EPS)
        x1_ref[...] = xv + (1.0 + mod_ref[2:3, :]) * ((o * ro) * g_ref[1:2, :])

        nxt = ext_ref[tm + 7:tm + 7 + p, :]
        car_ref[...] = nxt
        tail_ref[...] = nxt

        if n_pay:
            @pl.when(i == max(nt - 2, 0))
            def _():
                _gather_phase(_layer_of(pay_refs, [gather_layer] * n_pay), got_refs, *sems, 1)

            @pl.when(i == nt - 1)
            def _():
                _gather_phase(_layer_of(pay_refs, [gather_layer] * n_pay), got_refs, *sems, 2)

    row = lambda width: pl.BlockSpec((tm, width), lambda i: (i, 0))
    return _carrier_call(
        body, f"mixer_fwd_{l}", (nt,),
        out_shape=[jax.ShapeDtypeStruct((t, d), F32), jax.ShapeDtypeStruct((t, d), act_dtype),
                   jax.ShapeDtypeStruct((t, d_in), F32), jax.ShapeDtypeStruct((t, dc), F32),
                   jax.ShapeDtypeStruct((t, d_mix), act_dtype), jax.ShapeDtypeStruct((t, d), F32),
                   jax.ShapeDtypeStruct((nt, p, d_mix), F32)],
        in_specs=[row(d),
                  pl.BlockSpec((None, None, 8, d), lambda i: (l, i // tps, 0, 0)),
                  _const_spec((None, 8, d), (l, 0, 0)),
                  _const_spec((d_in, d), (0, 0)),
                  _const_spec((32, dc), (0, 0)),
                  _const_spec((None, 8, dc), (l, 0, 0)),
                  _const_spec((None, len(POOL_WINDOWS), POOL_GROUP, POOL_GROUP), (l, 0, 0, 0)),
                  _const_spec((d_mix, d), (0, 0))],
        out_specs=[row(d), row(d), row(d_in), row(dc), row(d_mix), row(d),
                   pl.BlockSpec((None, p, d_mix), lambda i: (i, 0, 0))],
        scratch=[pltpu.VMEM((p + tm + 8, d_mix), F32), pltpu.VMEM((p, d_mix), F32)],
        operands=(x, mod, gains, w_in_t, conv_w, vec, pool_w, w_out),
        payload=payload, gather_layer=gather_layer)


def _mixer_bwd(l, dx1, x, mod, gains, u, tails, a1, o, w_in_t, conv_w, vec, pool_w, w_out, *, seq, tm, act_dtype,
               payload=(), recvs=(), rows=(), pay_layer=0):
    t, d = x.shape
    d_in = w_in_t.shape[0]
    dc = conv_w.shape[-1]
    d_mix = w_out.shape[0]
    n_taps = 31
    nt, tps = t // tm, seq // tm
    p = CONV_PREFIX
    n_groups = len(POOL_WINDOWS)
    n_pay = len(payload)

    def body(*refs):
        ins, outs, scr, pay_refs, recv_refs, sems = _carried_refs(refs, 13, 8, 3, n_pay, aliased=True)
        dx1_ref, x_ref, mod_ref, g_ref, u_ref, tail_ref, a1_ref, o_ref, win_ref, cw_ref, v_ref, pw_ref, wout_ref = ins
        dx_ref, du_ref, do_ref, dmod_ref, gd_ref, gv_ref, dcw_ref, dpw_ref = outs
        ext_ref, fext_ref, fcar_ref = scr
        i = pl.program_id(0)
        j = nt - 1 - i
        first_in_seq = (j % tps) == 0
        last_in_seq = (j % tps) == tps - 1

        if n_pay:
            @pl.when(i == 0)
            def _():
                _exchange(pay_refs, recv_refs, rows, pay_layer, *sems, start=True)

        @pl.when(i == 0)
        def _():
            gd_ref[...] = jnp.zeros_like(gd_ref)
            gv_ref[...] = jnp.zeros_like(gv_ref)
            dcw_ref[...] = jnp.zeros_like(dcw_ref)
            dpw_ref[...] = jnp.zeros_like(dpw_ref)
            fext_ref[0:8, :] = jnp.zeros((8, fext_ref.shape[1]), F32)

        @pl.when(last_in_seq)
        def _():
            dmod_ref[...] = jnp.zeros_like(dmod_ref)
            fcar_ref[...] = jnp.zeros_like(fcar_ref)

        xv = x_ref[...]
        dx1v = dx1_ref[...]
        pre_g, post_g = g_ref[0:1, :], g_ref[1:2, :]

        ov = o_ref[...]
        ro = lax.rsqrt(_mean(ov * ov) + EPS)
        yo = ov * ro
        dmod_ref[2:3, :] += _colsum(dx1v * (yo * post_g))
        dn = dx1v * (1.0 + mod_ref[2:3, :])
        gd_ref[1:2, :] += _colsum(dn * yo)
        dyo = dn * post_g
        do = ro * (dyo - yo * _mean(dyo * yo))
        dob = do.astype(act_dtype)
        do_ref[...] = dob
        dap = _mm_nt(dob, wout_ref[...])

        uv = u_ref[...]
        val, gate = uv[:, :dc], uv[:, dc:2 * dc]
        sg = _sigmoid(gate)
        ext_ref[p:p + tm, 0:dc] = val * sg
        ext_ref[p:p + tm, dc:] = uv[:, 2 * dc:]
        keep = jnp.where(first_in_seq, 0.0, 1.0).astype(F32)
        ext_ref[0:p, :] = jnp.where(_sublane_is(p, 0), tail_ref[...] * keep, ext_ref[tm - 1:tm - 1 + p, :])

        a1v = a1_ref[...]
        mu = _mean(a1v)
        xc = a1v - mu
        rstd = lax.rsqrt(_mean(xc * xc) + EPS)
        xh = xc * rstd
        ln_g = v_ref[1:2, :]
        a2 = xh * ln_g + v_ref[2:3, :]
        s2 = _sigmoid(a2)
        da2 = dap[:, :dc] * (s2 * (1.0 + a2 * (1.0 - s2)))
        gv_ref[1:2, :] += _colsum(da2 * xh)
        gv_ref[2:3, :] += _colsum(da2)
        dxh = da2 * ln_g
        da1 = rstd * (dxh - _mean(dxh) - xh * _mean(dxh * xh))
        gv_ref[0:1, :] += _colsum(da1)
        last_sublane = _sublane_is(p, 7)

        def put(cs, value):
            fext_ref[8:8 + tm, cs] = value
            fext_ref[8 + tm:8 + tm + p, cs] = jnp.where(last_sublane, fcar_ref[:, cs], fext_ref[9:9 + p, cs])

        put(slice(0, dc), da1)

        for k in range(n_taps):
            row = p - 8 * (n_taps - 1 - k)
            dcw_ref[k:k + 1, :] += _colsum(fext_ref[8:8 + tm, 0:dc] * ext_ref[row:row + tm, 0:dc])

        def store(r0, rc, cs, acc):
            sgc = _sigmoid(u_ref[r0:r0 + rc, dc + cs.start:dc + cs.stop])
            vc = u_ref[r0:r0 + rc, cs]
            du_ref[r0:r0 + rc, cs] = (acc * sgc).astype(act_dtype)
            du_ref[r0:r0 + rc, dc + cs.start:dc + cs.stop] = (acc * vc * sgc * (1.0 - sgc)).astype(act_dtype)

        _conv_taps(fext_ref, 0, dc, tm, [8 + 8 * (n_taps - 1 - k) for k in range(n_taps)],
                   cw_ref,
                   lambda cs, rc: jnp.zeros((rc, LANES), F32), store)

        pos = (j % tps) * tm + _time_of_row(tm)
        for g, w in enumerate(POOL_WINDOWS):
            cs = slice(dc + g * POOL_GROUP, dc + (g + 1) * POOL_GROUP)
            gs = slice(g * POOL_GROUP, (g + 1) * POOL_GROUP)
            s = ext_ref[p:p + tm, cs]
            for jj in range(1, w):
                s = s + ext_ref[p - 8 * jj:p - 8 * jj + tm, cs]
            cnt = jnp.minimum(pos + 1, w).astype(F32)
            dv = (s / cnt - ext_ref[p:p + tm, cs]).astype(MXU_DTYPE)
            q = _mm(dv, pw_ref[g])
            dp = dap[:, cs]
            gv_ref[3:4, gs] += _colsum(dp * q)
            dq = (dp * v_ref[3:4, gs]).astype(MXU_DTYPE)
            dpw_ref[g] += _mm_tn(dv, dq)
            dd = _mm_nt(dq, pw_ref[g])
            put(cs, dd / cnt)
            dhp = fext_ref[8:8 + tm, cs]
            for jj in range(1, w):
                dhp = dhp + fext_ref[8 + 8 * jj:8 + 8 * jj + tm, cs]
            du_ref[:, dc + cs.start:dc + cs.stop] = (dhp - dd).astype(act_dtype)

        fcar_ref[...] = fext_ref[1:1 + p, :]

        dh = _mm(du_ref[...], win_ref[...])

        r = lax.rsqrt(_mean(xv * xv) + EPS)
        xn = xv * r
        dmod_ref[0:1, :] += _colsum(dh)
        dmod_ref[1:2, :] += _colsum(dh * (xn * pre_g))
        dy = dh * (1.0 + mod_ref[1:2, :])
        gd_ref[0:1, :] += _colsum(dy * xn)
        dxn = dy * pre_g
        dx_ref[...] = dx1v + r * (dxn - xn * _mean(dxn * xn))

        if n_pay:
            @pl.when(i == nt - 1)
            def _():
                _exchange(pay_refs, recv_refs, rows, pay_layer, *sems, start=False)

    rev = lambda width: pl.BlockSpec((tm, width), lambda i: (nt - 1 - i, 0))
    return _carrier_call(
        body, f"mixer_bwd_{l}", (nt,),
        out_shape=[jax.ShapeDtypeStruct((t, d), F32), jax.ShapeDtypeStruct((t, d_in), act_dtype),
                   jax.ShapeDtypeStruct((t, d), act_dtype),
                   jax.ShapeDtypeStruct((t // seq, 8, d), F32), jax.ShapeDtypeStruct((8, d), F32),
                   jax.ShapeDtypeStruct((8, dc), F32), jax.ShapeDtypeStruct((32, dc), F32),
                   jax.ShapeDtypeStruct((n_groups, POOL_GROUP, POOL_GROUP), F32)],
        in_specs=[rev(d), rev(d),
                  pl.BlockSpec((None, None, 8, d), lambda i: (l, (nt - 1 - i) // tps, 0, 0)),
                  _const_spec((None, 8, d), (l, 0, 0)),
                  rev(d_in),
                  pl.BlockSpec((None, p, d_mix), lambda i: (jnp.maximum(nt - 2 - i, 0), 0, 0)),
                  rev(dc), rev(d),
                  _const_spec((d_in, d), (0, 0)),
                  _const_spec((32, dc), (0, 0)),
                  _const_spec((None, 8, dc), (l, 0, 0)),
                  _const_spec((None, n_groups, POOL_GROUP, POOL_GROUP), (l, 0, 0, 0)),
                  _const_spec((d_mix, d), (0, 0))],
        out_specs=[rev(d), rev(d_in), rev(d),
                   pl.BlockSpec((None, 8, d), lambda i: ((nt - 1 - i) // tps, 0, 0)),
                   _acc_spec((8, d), (0, 0)), _acc_spec((8, dc), (0, 0)), _acc_spec((32, dc), (0, 0)),
                   _acc_spec((n_groups, POOL_GROUP, POOL_GROUP), (0, 0, 0))],
        scratch=[pltpu.VMEM((p + tm, d_mix), F32), pltpu.VMEM((8 + tm + p, d_mix), F32),
                 pltpu.VMEM((p, d_mix), F32)],
        operands=(dx1, x, mod, gains, u, tails, a1, o, w_in_t, conv_w, vec, pool_w, w_out),
        payload=payload, recvs=recvs)


def _chunked_weights(i, hbm_refs, vmem_refs, sem):
    offsets = [0]
    for ref in hbm_refs:
        offsets.append(offsets[-1] + ref.shape[0])

    def copy(r, k):
        return pltpu.make_async_copy(hbm_refs[r].at[k], vmem_refs[r].at[k], sem.at[offsets[r] + k])

    @pl.when(i == 0)
    def _():
        for r, ref in enumerate(hbm_refs):
            for k in range(ref.shape[0]):
                copy(r, k).start()

    def ready(r, k):
        @pl.when(i == 0)
        def _():
            copy(r, k).wait()

    return ready


def _ffn_fwd(l, x, mod, gains, up, fcw, down, *, seq, tm, act_dtype, payload=(), gather_layer=None, target=None):
    t, d = x.shape
    n_chunks, _, fc = up.shape
    half = n_chunks // 2
    nt, tps = t // tm, seq // tm
    p = FFN_PREFIX
    n_pay = len(payload)
    n_head = 0 if target is None else 1

    def body(*refs):
        ins, outs, scr, pay_refs, got_refs, sems = _carried_refs(refs, 6 + n_head, 6 + 2 * n_head, 5, n_pay,
                                                                 aliased=False)
        x_ref, mod_ref, g_ref, up_hbm, fcw_ref, down_hbm = ins[:6]
        x2_ref, h2_ref, u2_ref, u3_ref, hid_ref, o2_ref = outs[:6]
        ext_ref, car_ref, up_ref, down_ref, w_sem = scr
        i = pl.program_id(0)
        first = (i % tps) == 0
        ready = _chunked_weights(i, [up_hbm, down_hbm], [up_ref, down_ref], w_sem)

        if n_pay:
            @pl.when(i == 0)
            def _():
                _gather_phase(_layer_of(pay_refs, [gather_layer] * n_pay), got_refs, *sems, 0)

        xv = x_ref[...]
        r = lax.rsqrt(_mean(xv * xv) + EPS)
        hv = (xv * r) * g_ref[0:1, :] * (1.0 + mod_ref[4:5, :]) + mod_ref[3:4, :]
        hb = hv.astype(act_dtype)
        h2_ref[...] = hb

        @pl.when(first)
        def _():
            car_ref[...] = jnp.zeros_like(car_ref)

        @pl.when(i == 0)
        def _():
            for n in range(n_chunks):
                ext_ref[n, p + tm:p + tm + 8, :] = jnp.zeros((8, fc), F32)

        first_sublane = _sublane_is(p, 0)

        def project(n):
            ready(0, n)
            un = _mm(hb, up_ref[n])
            ext_ref[n, p:p + tm, :] = un
            u2_ref[n] = un.astype(act_dtype)
            ext_ref[n, 0:p, :] = jnp.where(first_sublane, car_ref[n], ext_ref[n, tm - 1:tm - 1 + p, :])
            car_ref[n] = ext_ref[n, tm + 7:tm + 7 + p, :]

        def conv(n):
            return (fcw_ref[n, 3:4, :] + fcw_ref[n, 0:1, :] * ext_ref[n, p - 16:p - 16 + tm, :]
                    + fcw_ref[n, 1:2, :] * ext_ref[n, p - 8:p - 8 + tm, :]
                    + fcw_ref[n, 2:3, :] * ext_ref[n, p:p + tm, :])

        for n in range(n_chunks):
            project(n)
        o2 = jnp.zeros((tm, d), F32)
        for n in range(half):
            gt = conv(n + half)
            v = conv(n)
            sg = _sigmoid(gt)
            silu = gt * sg
            u3_ref[n] = silu.astype(act_dtype)
            u3_ref[n + half] = (v * (sg * (1.0 + gt * (1.0 - sg)))).astype(act_dtype)
            hid = (silu * v).astype(act_dtype)
            hid_ref[n] = hid
            ready(1, n)
            o2 = o2 + _mm(hid, down_ref[n])
        o2_ref[...] = o2
        ro = lax.rsqrt(_mean(o2 * o2) + EPS)
        x2 = xv + (1.0 + mod_ref[5:6, :]) * ((o2 * ro) * g_ref[1:2, :])
        x2_ref[...] = x2

        if n_head:
            t_ref, dy_ref, loss_ref = ins[6], outs[6], outs[7]

            @pl.when(i == 0)
            def _():
                loss_ref[...] = jnp.zeros_like(loss_ref)

            diff = x2 - t_ref[...]
            dy_ref[...] = diff / d
            part = 0.5 * jnp.sum(_mean(diff * diff), axis=0, keepdims=True)
            loss_ref[...] += jnp.broadcast_to(part, loss_ref.shape)

        if n_pay:
            @pl.when(i == max(nt - 2, 0))
            def _():
                _gather_phase(_layer_of(pay_refs, [gather_layer] * n_pay), got_refs, *sems, 1)

            @pl.when(i == nt - 1)
            def _():
                _gather_phase(_layer_of(pay_refs, [gather_layer] * n_pay), got_refs, *sems, 2)

    row = lambda width: pl.BlockSpec((tm, width), lambda i: (i, 0))
    chunked = lambda n: pl.BlockSpec((n, tm, fc), lambda i: (0, i, 0))
    head = n_head * [None]
    return _carrier_call(
        body, f"ffn_fwd_{l}", (nt,),
        out_shape=[jax.ShapeDtypeStruct((t, d), F32), jax.ShapeDtypeStruct((t, d), act_dtype),
                   jax.ShapeDtypeStruct((n_chunks, t, fc), act_dtype),
                   jax.ShapeDtypeStruct((n_chunks, t, fc), act_dtype),
                   jax.ShapeDtypeStruct((half, t, fc), act_dtype), jax.ShapeDtypeStruct((t, d), F32)]
        + [jax.ShapeDtypeStruct((t, d), F32) for _ in head] + [jax.ShapeDtypeStruct((8, LANES), F32) for _ in head],
        in_specs=[row(d),
                  pl.BlockSpec((None, None, 8, d), lambda i: (l, i // tps, 0, 0)),
                  _const_spec((None, 8, d), (l, 0, 0)),
                  pl.BlockSpec(memory_space=pl.ANY),
                  _const_spec((n_chunks, 8, fc), (0, 0, 0)),
                  pl.BlockSpec(memory_space=pl.ANY)] + [row(d) for _ in head],
        out_specs=[row(d), row(d), chunked(n_chunks), chunked(n_chunks), chunked(half), row(d)]
        + [row(d) for _ in head] + [_acc_spec((8, LANES), (0, 0)) for _ in head],
        scratch=[pltpu.VMEM((n_chunks, p + tm + 8, fc), F32), pltpu.VMEM((n_chunks, p, fc), F32),
                 pltpu.VMEM(up.shape, up.dtype), pltpu.VMEM(down.shape, down.dtype),
                 pltpu.SemaphoreType.DMA((n_chunks + half,))],
        operands=(x, mod, gains, up, fcw, down) + ((target,) if n_head else ()),
        payload=payload, gather_layer=gather_layer)


def _ffn_bwd(l, dx2, x, mod, gains, u2, u3, o2, up, fcw, down, *, seq, tm, act_dtype,
             payload=(), recvs=(), rows=(), pay_layer=0):
    t, d = x.shape
    n_chunks, _, fc = up.shape
    half = n_chunks // 2
    nt, tps = t // tm, seq // tm
    p = FFN_PREFIX
    n_pay = len(payload)

    def body(*refs):
        ins, outs, scr, pay_refs, recv_refs, sems = _carried_refs(refs, 10, 6, 5, n_pay, aliased=True)
        dx2_ref, x_ref, mod_ref, g_ref, u2_ref, u3_ref, o2_ref, up_hbm, fcw_ref, down_hbm = ins
        dx1_ref, du2_ref, do2_ref, dmod_ref, gd_ref, dfcw_ref = outs
        fext_ref, fcar_ref, up_ref, down_ref, w_sem = scr
        i = pl.program_id(0)
        ready = _chunked_weights(i, [down_hbm, up_hbm], [down_ref, up_ref], w_sem)
        j = nt - 1 - i
        last_in_seq = (j % tps) == tps - 1

        if n_pay:
            @pl.when(i == 0)
            def _():
                _exchange(pay_refs, recv_refs, rows, pay_layer, *sems, start=True)

        @pl.when(i == 0)
        def _():
            gd_ref[...] = jnp.zeros_like(gd_ref)
            dfcw_ref[...] = jnp.zeros_like(dfcw_ref)
            for n in range(n_chunks):
                fext_ref[n, 0:8, :] = jnp.zeros((8, fc), F32)

        @pl.when(last_in_seq)
        def _():
            dmod_ref[...] = jnp.zeros_like(dmod_ref)
            fcar_ref[...] = jnp.zeros_like(fcar_ref)

        xv = x_ref[...]
        dx2v = dx2_ref[...]
        pre_g, post_g = g_ref[0:1, :], g_ref[1:2, :]

        ov = o2_ref[...]
        ro = lax.rsqrt(_mean(ov * ov) + EPS)
        yo = ov * ro
        dmod_ref[2:3, :] += _colsum(dx2v * (yo * post_g))
        dn = dx2v * (1.0 + mod_ref[5:6, :])
        gd_ref[1:2, :] += _colsum(dn * yo)
        dyo = dn * post_g
        do = ro * (dyo - yo * _mean(dyo * yo))
        dob = do.astype(act_dtype)
        do2_ref[...] = dob

        for n in range(half):
            ready(0, n)
            dhid = _mm_nt(dob, down_ref[n])
            fext_ref[n, 8:8 + tm, :] = dhid * u3_ref[n].astype(F32)
            fext_ref[n + half, 8:8 + tm, :] = dhid * u3_ref[n + half].astype(F32)

        last_sublane = _sublane_is(p, 7)
        dh = jnp.zeros((tm, d), F32)
        for n in range(n_chunks):
            fext_ref[n, 8 + tm:8 + tm + p, :] = jnp.where(last_sublane, fcar_ref[n], fext_ref[n, 9:9 + p, :])
            fcar_ref[n] = fext_ref[n, 1:1 + p, :]
            d2 = fext_ref[n, 8:8 + tm, :]
            d1 = fext_ref[n, 16:16 + tm, :]
            d0 = fext_ref[n, 24:24 + tm, :]
            u2v = u2_ref[n].astype(F32)
            dfcw_ref[n, 3:4, :] += _colsum(d2)
            dfcw_ref[n, 0:1, :] += _colsum(d0 * u2v)
            dfcw_ref[n, 1:2, :] += _colsum(d1 * u2v)
            dfcw_ref[n, 2:3, :] += _colsum(d2 * u2v)
            du2 = (fcw_ref[n, 0:1, :] * d0 + fcw_ref[n, 1:2, :] * d1 + fcw_ref[n, 2:3, :] * d2).astype(act_dtype)
            du2_ref[n] = du2
            ready(1, n)
            dh = dh + _mm_nt(du2, up_ref[n])

        r = lax.rsqrt(_mean(xv * xv) + EPS)
        xn = xv * r
        dmod_ref[0:1, :] += _colsum(dh)
        dmod_ref[1:2, :] += _colsum(dh * (xn * pre_g))
        dy = dh * (1.0 + mod_ref[4:5, :])
        gd_ref[0:1, :] += _colsum(dy * xn)
        dxn = dy * pre_g
        dx1_ref[...] = dx2v + r * (dxn - xn * _mean(dxn * xn))

        if n_pay:
            @pl.when(i == nt - 1)
            def _():
                _exchange(pay_refs, recv_refs, rows, pay_layer, *sems, start=False)

    rev = lambda width: pl.BlockSpec((tm, width), lambda i: (nt - 1 - i, 0))
    chunked = pl.BlockSpec((n_chunks, tm, fc), lambda i: (0, nt - 1 - i, 0))
    return _carrier_call(
        body, f"ffn_bwd_{l}", (nt,),
        out_shape=[jax.ShapeDtypeStruct((t, d), F32), jax.ShapeDtypeStruct((n_chunks, t, fc), act_dtype),
                   jax.ShapeDtypeStruct((t, d), act_dtype),
                   jax.ShapeDtypeStruct((t // seq, 8, d), F32), jax.ShapeDtypeStruct((8, d), F32),
                   jax.ShapeDtypeStruct((n_chunks, 8, fc), F32)],
        in_specs=[rev(d), rev(d),
                  pl.BlockSpec((None, None, 8, d), lambda i: (l, (nt - 1 - i) // tps, 0, 0)),
                  _const_spec((None, 8, d), (l, 0, 0)),
                  chunked, chunked, rev(d),
                  pl.BlockSpec(memory_space=pl.ANY),
                  _const_spec((n_chunks, 8, fc), (0, 0, 0)),
                  pl.BlockSpec(memory_space=pl.ANY)],
        out_specs=[rev(d), chunked, rev(d),
                   pl.BlockSpec((None, 8, d), lambda i: ((nt - 1 - i) // tps, 0, 0)),
                   _acc_spec((8, d), (0, 0)), _acc_spec((n_chunks, 8, fc), (0, 0, 0))],
        scratch=[pltpu.VMEM((n_chunks, 8 + tm + p, fc), F32), pltpu.VMEM((n_chunks, p, fc), F32),
                 pltpu.VMEM(up.shape, up.dtype), pltpu.VMEM(down.shape, down.dtype),
                 pltpu.SemaphoreType.DMA((n_chunks + half,))],
        operands=(dx2, x, mod, gains, u2, u3, o2, up, fcw, down),
        payload=payload, recvs=recvs)


def _pad_rows(a, rows):
    pad = [(0, 0)] * a.ndim
    pad[-2] = (0, rows - a.shape[-2])
    return jnp.pad(a, pad)


def kernel(x, c, ada_w, ada_b, pre_mix_g, post_mix_g, w_in, conv_w, conv_b, conv_ln_g, conv_ln_b, pool_w, pool_scale, w_out, pre_ffn_g, post_ffn_g, ffn_up, ffn_conv_w, ffn_conv_b, ffn_down, loss_target, m_ada_w, m_ada_b, m_pre_mix_g, m_post_mix_g, m_w_in, m_conv_w, m_conv_b, m_conv_ln_g, m_conv_ln_b, m_pool_w, m_pool_scale, m_w_out, m_pre_ffn_g, m_post_ffn_g, m_ffn_up, m_ffn_conv_w, m_ffn_conv_b, m_ffn_down, v_ada_w, v_ada_b, v_pre_mix_g, v_post_mix_g, v_w_in, v_conv_w, v_conv_b, v_conv_ln_g, v_conv_ln_b, v_pool_w, v_pool_scale, v_w_out, v_pre_ffn_g, v_post_ffn_g, v_ffn_up, v_ffn_conv_w, v_ffn_conv_b, v_ffn_down):
    bl, seq, d = x.shape
    n_layers = ada_w.shape[0]
    t = bl * seq
    dc = conv_b.shape[1]
    d_in = w_in.shape[2] * N_DEV
    d_mix = w_out.shape[1] * N_DEV
    n_taps = conv_w.shape[1]
    fc = ffn_up.shape[2]
    half = N_DEV // 2
    ada_cols = ada_w.shape[2]
    n_mod = ada_cols * N_DEV // d
    assert pool_scale.shape[1] == dc and n_taps == 31 and n_mod == 6 and ffn_conv_w.shape[1] == 3
    assert pool_w.shape[1:] == (len(POOL_WINDOWS), POOL_GROUP, POOL_GROUP)
    tm = TILE_TOKENS
    assert seq % tm == 0 and CONV_PREFIX <= tm - 8
    tk = 2048 if t % 2048 == 0 else tm
    act = MXU_DTYPE

    def tile_order(a, inverse=False):
        shape = (t // tm, tm // 8, 8, d) if inverse else (t // tm, 8, tm // 8, d)
        return a.reshape(shape).transpose(0, 2, 1, 3).reshape(t, d)

    ax = lax.axis_index
    me = 4 * ax("x") + 2 * ax("y") + ax("c")

    (c_all,) = _all_gather([c], "gather_c")
    c_all = c_all.reshape(N_DEV * bl, d)
    ada_b_cols = lax.dynamic_slice_in_dim(ada_b, me * ada_cols, ada_cols, axis=1)
    mod_cols = _ada_fwd(c_all, ada_w, ada_b_cols)

    w_in_s = w_in.astype(act).transpose(0, 2, 1)
    w_out_s, up_s, down_s = w_out.astype(act), ffn_up.astype(act), ffn_down.astype(act)
    conv_w_s = _pad_rows(conv_w, 32)
    fcw_s = _pad_rows(jnp.concatenate(
        [ffn_conv_w, lax.dynamic_slice_in_dim(ffn_conv_b, me * fc, fc, axis=1)[:, None, :]], axis=1), 8)

    mixer_shards = (w_in_s, w_out_s, conv_w_s)
    late_shards = (down_s, fcw_s)

    def mixer_weights(g_w_in, g_w_out, g_conv_w):
        return (g_w_in.reshape(d_in, d), g_conv_w.transpose(1, 0, 2).reshape(32, dc), g_w_out.reshape(d_mix, d))

    def late_weights(g_down, g_fcw):
        return (g_fcw, g_down.reshape(half, fc, d))

    g0 = _all_gather([mod_cols, *mixer_shards], "gather_w0", layers=[None, 0, 0, 0])
    mod_all = g0[0].transpose(1, 2, 0, 3).reshape(n_layers, N_DEV * bl, n_mod, d)
    mod = _pad_rows(lax.dynamic_slice_in_dim(mod_all, me * bl, bl, axis=1), 8)
    mixers, lates, ups = [mixer_weights(*g0[1:])], [], []
    gains_mix = _pad_rows(jnp.stack([pre_mix_g, post_mix_g], axis=1), 8)
    gains_ffn = _pad_rows(jnp.stack([pre_ffn_g, post_ffn_g], axis=1), 8)
    vec = _pad_rows(jnp.stack([conv_b, conv_ln_g, conv_ln_b, pool_scale], axis=1), 8)
    pool_w_b = pool_w.astype(act)

    kw = dict(seq=seq, tm=tm, act_dtype=act)
    xs = tile_order(x.reshape(t, d))
    saved = []
    for l in range(n_layers):
        w_in_t, cw, w_o = mixers[l]
        (x1, h1, u, a1, ap, o, tails), got = _mixer_fwd(
            l, xs, mod, gains_mix, w_in_t, cw, vec, pool_w_b, w_o, **kw,
            payload=(up_s, *late_shards) if l == 0 else (up_s,), gather_layer=l)
        ups.append(got[0])
        if l == 0:
            lates.append(late_weights(*got[1:]))
        up_l, (fcw_l, down_l) = ups[l], lates[l]
        last = l + 1 == n_layers
        (x2, h2, u2, u3, hid, o2, *head), got = _ffn_fwd(
            l, x1, mod, gains_ffn, up_l, fcw_l, down_l, **kw,
            payload=() if last else (*mixer_shards, *late_shards), gather_layer=l + 1,
            target=tile_order(loss_target.reshape(t, d)) if last else None)
        if got:
            mixers.append(mixer_weights(*got[:3]))
            lates.append(late_weights(*got[3:]))
        saved.append((xs, h1, u, tails, a1, ap, o, x1, h2, u2, u3, hid, o2))
        xs = x2

    dx, loss_part = head

    def landing(shard, dtype):
        return lax.empty((N_DEV, n_layers) + shard.shape[1:], dtype)

    r_up, r_down = landing(ffn_up, act), landing(ffn_down, act)
    r_w_in, r_w_out = landing(w_in, act), landing(w_out, act)
    r_conv_w, r_fcw = landing(conv_w, F32), landing(ffn_conv_w, F32)
    def cuts(l):
        first, second = (5.5, 7.8) if l == 0 else (2.5, 6)
        return int(first * d / 10) // 16 * 16, int(second * d / 10) // 16 * 16

    dmods, smalls = [], []
    pending = None
    for l in reversed(range(n_layers)):
        xin, h1, u, tails, a1, ap, o, x1, h2, u2, u3, hid, o2 = saved[l]
        (w_in_t, cw, w_o), (fcw_l, down_l), up_l = mixers[l], lates[l], ups[l]
        a, b = cuts(l + 1) if pending else (0, 0)
        carried = dict(pay_layer=l + 1)
        (dx1, du2, do2, dmod_b, gd_b, dfcw), got = _ffn_bwd(
            l, dx, x1, mod, gains_ffn, u2, u3, o2, up_l, fcw_l, down_l, **kw, **carried,
            payload=(pending["up"], pending["w_in"], pending["w_out"], pending["conv_w"], pending["fcw"])
            if pending else (),
            recvs=(r_up, r_w_in, r_w_out, r_conv_w, r_fcw) if pending else (),
            rows=((a, b - a), None, None, None, None))
        if pending:
            r_up, r_w_in, r_w_out, r_conv_w, r_fcw = got
        p_up, got = _matmul_tn(f"dw_up_{l}", h2[None], du2, tk, act, **carried,
                               payload=(pending["up"],) if pending else (), recvs=(r_up,) if pending else (),
                               rows=((b, d - b),))
        if pending:
            (r_up,) = got
        p_down = _matmul_tn(f"dw_down_{l}", hid, do2[None], tk, act)[0].reshape(N_DEV, fc // 2, d)
        a, b = cuts(l)
        (dx, du, do, dmod_a, gd_a, gv, dcw, dpw), (r_down, r_up) = _mixer_bwd(
            l, dx1, xin, mod, gains_mix, u, tails, a1, o, w_in_t, cw, vec, pool_w_b, w_o, **kw,
            payload=(p_down, p_up), recvs=(r_down, r_up), rows=(None, (0, a)), pay_layer=l)
        last = dict(payload=(p_up,), recvs=(r_up,), pay_layer=0) if l == 0 else {}
        dw_in, got = _matmul_tn(f"dw_in_{l}", h1[None], du[None], tk, act, **last, rows=((a, b - a),))
        if l == 0:
            (r_up,) = got
            last["recvs"] = (r_up,)
        dw_out, got = _matmul_tn(f"dw_out_{l}", ap[None], do[None], tk, act, **last, rows=((b, d - b),))
        if l == 0:
            (r_up,) = got
        pending = dict(up=p_up, w_in=dw_in[0].reshape(d, N_DEV, d_in // N_DEV).transpose(1, 0, 2),
                       w_out=dw_out[0].reshape(N_DEV, d_mix // N_DEV, d),
                       conv_w=dcw[:n_taps].reshape(n_taps, N_DEV, dc // N_DEV).transpose(1, 0, 2),
                       fcw=dfcw[:, :3, :])
        dmods.append(jnp.concatenate([dmod_a[:, 0:3], dmod_b[:, 0:3]], axis=1).reshape(bl, n_mod * d))
        smalls.append(jnp.concatenate(
            [gd_a[0], gd_a[1], gv[0], gv[1], gv[2], gv[3], gd_b[0], gd_b[1], dfcw[:, 3, :].reshape(-1),
             dpw.reshape(-1)]))
    dmods.reverse()
    smalls.reverse()

    dmod_loc = jnp.stack(dmods)
    small_loc = jnp.stack(smalls)
    n_small = small_loc.shape[1]
    small_cols = 8 * LANES if (n_layers * n_small) % (8 * LANES) == 0 else LANES
    (r_w_in, r_w_out, r_conv_w, r_fcw), (g_dmod, g_small, g_loss) = _exchange_and_gather(
        (pending["w_in"], pending["w_out"], pending["conv_w"], pending["fcw"]),
        (r_w_in, r_w_out, r_conv_w, r_fcw), (None, None, None, None), 0,
        [dmod_loc.astype(act), small_loc.reshape(-1, small_cols).astype(act), loss_part], "exchange_tail")
    g_dmod = g_dmod.astype(F32)

    loss = _sum_parts(g_loss)[0, 0]

    def flat2(a):
        return a.reshape(-1, a.shape[-1])

    def update(name, parts, w, m, v):
        outs = _adam_reduce(name, parts.reshape(parts.shape[0], -1, w.shape[-1]), flat2(w), flat2(m), flat2(v))
        return [o_.reshape(w.shape) for o_ in outs]

    res = {}
    res["w_in"] = update("adam_w_in", r_w_in, w_in, m_w_in, v_w_in)
    res["w_out"] = update("adam_w_out", r_w_out, w_out, m_w_out, v_w_out)
    res["ffn_up"] = update("adam_ffn_up", r_up, ffn_up, m_ffn_up, v_ffn_up)
    res["ffn_down"] = update("adam_ffn_down", r_down, ffn_down, m_ffn_down, v_ffn_down)
    res["conv_w"] = update("adam_conv_w", r_conv_w, conv_w, m_conv_w, v_conv_w)
    res["ffn_conv_w"] = update("adam_ffn_conv_w", r_fcw, ffn_conv_w, m_ffn_conv_w, v_ffn_conv_w)

    dmod_all = g_dmod.transpose(1, 0, 2, 3).reshape(n_layers, N_DEV * bl, n_mod * d)
    dmod_cols = lax.dynamic_slice_in_dim(dmod_all, me * ada_cols, ada_cols, axis=2)
    res["ada_w"] = list(_ada_bwd(c_all.T, dmod_cols, ada_w, m_ada_w, v_ada_w))
    res["ada_b"] = update("adam_ada_b", dmod_all.transpose(1, 0, 2), ada_b, m_ada_b, v_ada_b)

    small_names = ["pre_mix_g", "post_mix_g", "conv_b", "conv_ln_g", "conv_ln_b", "pool_scale", "pre_ffn_g",
                   "post_ffn_g", "ffn_conv_b", "pool_w"]
    small_w = [pre_mix_g, post_mix_g, conv_b, conv_ln_g, conv_ln_b, pool_scale, pre_ffn_g, post_ffn_g,
               ffn_conv_b, pool_w]
    small_m = [m_pre_mix_g, m_post_mix_g, m_conv_b, m_conv_ln_g, m_conv_ln_b, m_pool_scale, m_pre_ffn_g,
               m_post_ffn_g, m_ffn_conv_b, m_pool_w]
    small_v = [v_pre_mix_g, v_post_mix_g, v_conv_b, v_conv_ln_g, v_conv_ln_b, v_pool_scale, v_pre_ffn_g,
               v_post_ffn_g, v_ffn_conv_b, v_pool_w]

    def pack(arrs):
        return jnp.concatenate([a.reshape(n_layers, -1) for a in arrs], axis=1).reshape(-1, small_cols)

    outs = _adam_reduce("adam_small", g_small, pack(small_w), pack(small_m), pack(small_v))
    outs = [o_.reshape(n_layers, n_small) for o_ in outs]
    off = 0
    for name, w in zip(small_names, small_w):
        size = w[0].size
        res[name] = [o_[:, off:off + size].reshape(w.shape) for o_ in outs]
        off += size

    order = ["ada_w", "ada_b", "pre_mix_g", "post_mix_g", "w_in", "conv_w", "conv_b", "conv_ln_g", "conv_ln_b",
             "pool_w", "pool_scale", "w_out", "pre_ffn_g", "post_ffn_g", "ffn_up", "ffn_conv_w", "ffn_conv_b",
             "ffn_down"]
    grad_x = tile_order(dx, inverse=True).reshape(bl, seq, d)
    return (loss, grad_x, *[res[n][0] for n in order], *[res[n][1] for n in order],
            *[res[n][2] for n in order], *[res[n][3] for n in order])
```
